```python
import math
import jax
import jax.numpy as jnp
from jax import lax
import numpy as np

D_MODEL = 2048
BATCH = 16
SEQ = 256
DEPTH = 2
DEC_BATCH = 2
DEC_SEQ = 1024
PAST_LEN = 256

GRID_W = 64
HEAD_DIM = 128
C_CONV = D_MODEL // 4
CONV_K = 31
H_DIFF = (3 * D_MODEL // 8) // HEAD_DIM
H_NA = (3 * D_MODEL // 8) // HEAD_DIM
DQK = HEAD_DIM // 2
MIX_W = C_CONV + H_DIFF * HEAD_DIM + H_NA * HEAD_DIM
NA_KR = 8
NA_KC = 16
NA_QBW = 16
NA_KBW = 32
ROPE_BASE = 10000.0
N_GROUPS = 4
E_PER_GROUP = 4
N_EXPERTS = N_GROUPS * E_PER_GROUP
TOP_K_IN_GROUP = 2
D_FF_EXPERT = D_MODEL // 4
Q_BLOCK = 128

kernel_name = 'hybrid_dit_conv_diffattn_natten_hmoe_step'


def rms_norm(x, g, eps=1e-6):
    xf = x.astype(jnp.float32)
    y = xf * lax.rsqrt(jnp.mean(xf * xf, axis=-1, keepdims=True) + eps)
    return (y * g.astype(jnp.float32)).astype(x.dtype)


def layer_norm(x, g, b, eps=1e-5):
    xf = x.astype(jnp.float32)
    xc = xf - jnp.mean(xf, axis=-1, keepdims=True)
    y = xc * lax.rsqrt(jnp.mean(xc * xc, axis=-1, keepdims=True) + eps)
    return (y * g.astype(jnp.float32) + b.astype(jnp.float32)).astype(x.dtype)


def lambda_init(layer):
    return 0.8 - 0.6 * math.exp(-0.3 * layer)


def modulation(cvec, w_ada, b_ada):
    m = jax.nn.silu(cvec) @ w_ada + b_ada
    return jnp.split(m[:, None, :], 6, axis=-1)


def modulate(x, g, shift, scale):
    return rms_norm(x, g) * (1 + scale) + shift


def project(h, p):
    b, l, _ = h.shape
    dw = H_DIFF * HEAD_DIM
    nw = H_NA * HEAD_DIM
    cuts = [int(v) for v in np.cumsum([C_CONV, C_CONV, dw, dw, dw, nw, nw])]
    cv, cg, dq, dk, dv, nq, nk, nv = jnp.split(h @ p['w_in'], cuts, axis=-1)
    dq = rms_norm(dq.reshape(b, l, H_DIFF, 2, DQK), p['diff_qn_g'])
    dk = rms_norm(dk.reshape(b, l, H_DIFF, 2, DQK), p['diff_kn_g'])
    dv = dv.reshape(b, l, H_DIFF, HEAD_DIM)
    nq = rms_norm(nq.reshape(b, l, H_NA, HEAD_DIM), p['na_qn_g'])
    nk = rms_norm(nk.reshape(b, l, H_NA, HEAD_DIM), p['na_kn_g'])
    nv = nv.reshape(b, l, H_NA, HEAD_DIM)
    return cv, cg, dq, dk, dv, nq, nk, nv


def conformer_conv(val, gate, p):
    u = val * jax.nn.sigmoid(gate)
    y = lax.conv_general_dilated(
        u, p['conv_w'][:, None, :].astype(u.dtype), window_strides=(1,),
        padding=[(CONV_K // 2, CONV_K // 2)], dimension_numbers=('NWC', 'WIO', 'NWC'),
        feature_group_count=C_CONV)
    y = y + p['conv_b']
    return jax.nn.silu(layer_norm(y, p['conv_ln_g'], p['conv_ln_b']))


def axial_rope_tables(length):
    half = DQK // 2
    inv = 1.0 / (ROPE_BASE ** (jnp.arange(0, half, 2, dtype=jnp.float32) / half))
    t = jnp.arange(length)
    ang_r = (t // GRID_W).astype(jnp.float32)[:, None] * inv
    ang_c = (t % GRID_W).astype(jnp.float32)[:, None] * inv
    ang = jnp.concatenate([ang_r, ang_r, ang_c, ang_c], axis=-1)
    return jnp.cos(ang), jnp.sin(ang)


def apply_rope(x, cos, sin):
    xr = x.reshape(x.shape[:-1] + (2, 2, DQK // 4))
    rot = jnp.stack([-xr[..., 1, :], xr[..., 0, :]], axis=-2).reshape(x.shape)
    c = cos[None, :, None, None, :].astype(x.dtype)
    s = sin[None, :, None, None, :].astype(x.dtype)
    return x * c + rot * s


def _query_blocks(q):
    b, l = q.shape[:2]
    return jnp.moveaxis(q.reshape((b, l // Q_BLOCK, Q_BLOCK) + q.shape[2:]), 1, 0)


def _merge_blocks(o):
    o = jnp.moveaxis(o, 0, 1)
    return o.reshape((o.shape[0], o.shape[1] * o.shape[2]) + o.shape[3:])


def diff_lambda(p, lam_init):
    f = lambda a: a.astype(jnp.float32)
    return (jnp.exp(jnp.sum(f(p['diff_lam_q1']) * f(p['diff_lam_k1'])))
            - jnp.exp(jnp.sum(f(p['diff_lam_q2']) * f(p['diff_lam_k2']))) + lam_init)


def diff_attention(q, k, v, lam):
    scale = DQK ** -0.5

    def one_block(qb):
        s = jnp.einsum('bqhcd,bkhcd->bhcqk', qb, k, preferred_element_type=jnp.float32) * scale
        pr = jax.nn.softmax(s, axis=-1)
        a = pr[:, :, 0] - lam * pr[:, :, 1]
        return jnp.einsum('bhqk,bkhd->bqhd', a.astype(v.dtype), v)

    return _merge_blocks(lax.map(one_block, _query_blocks(q)))


def diff_out_norm(o, g, lam_init):
    return rms_norm(o, g, eps=1e-5) * (1 - lam_init)


def softmax_attention(q, k, v):
    scale = q.shape[-1] ** -0.5

    def one_block(qb):
        s = jnp.einsum('bqhd,bkhd->bhqk', qb, k, preferred_element_type=jnp.float32) * scale
        pr = jax.nn.softmax(s, axis=-1).astype(v.dtype)
        return jnp.einsum('bhqk,bkhd->bqhd', pr, v)

    return _merge_blocks(lax.map(one_block, _query_blocks(q)))


def neighbourhood_attention(q, k, v, ck, cv, rpb):
    b, l, h, d = q.shape
    rows = l // GRID_W
    kr = min(NA_KR, rows)
    ncb = GRID_W // NA_QBW
    r = np.arange(rows)
    key_rows = np.clip(r - kr // 2, 0, rows - kr)[:, None] + np.arange(kr)
    col_start = np.clip(np.arange(ncb) * NA_QBW - (NA_KBW - NA_QBW) // 2, 0, GRID_W - NA_KBW)
    key_cols = col_start[:, None] + np.arange(NA_KBW)
    q_cols = np.arange(ncb)[:, None] * NA_QBW + np.arange(NA_QBW)
    win_start = np.clip(q_cols - NA_KC // 2, 0, GRID_W - NA_KC)
    col_ok = ((key_cols[:, None, :] >= win_start[..., None])
              & (key_cols[:, None, :] < win_start[..., None] + NA_KC))
    n_loc = kr * NA_KBW
    mask = np.broadcast_to(col_ok[:, :, None, :], (ncb, NA_QBW, kr, NA_KBW)).reshape(ncb, NA_QBW, n_loc)
    dr = np.clip(key_rows - r[:, None] + NA_KR - 1, 0, 2 * NA_KR - 2)
    dc = np.clip(key_cols[:, None, :] - q_cols[..., None] + NA_KC - 1, 0, 2 * NA_KC - 2)
    bias = rpb[:, dr[:, None, None, :, None], dc[None, :, :, None, :]]
    bias = bias.reshape(h, rows, ncb, NA_QBW, n_loc).astype(jnp.float32)
    kidx = (key_rows[:, None, :, None] * GRID_W + key_cols[None, :, None, :]).reshape(rows, ncb, n_loc)
    kg = k[:, kidx]
    vg = v[:, kidx]
    qg = q.reshape(b, rows, ncb, NA_QBW, h, d)
    scale = d ** -0.5
    s_loc = jnp.einsum('brjqhd,brjkhd->bhrjqk', qg, kg, preferred_element_type=jnp.float32) * scale + bias[None]
    s_loc = jnp.where(mask, s_loc, -jnp.inf)
    s_ctx = jnp.einsum('brjqhd,bkhd->bhrjqk', qg, ck, preferred_element_type=jnp.float32) * scale
    pr = jax.nn.softmax(jnp.concatenate([s_loc, s_ctx], axis=-1), axis=-1).astype(v.dtype)
    o = (jnp.einsum('bhrjqk,brjkhd->brjqhd', pr[..., :n_loc], vg)
         + jnp.einsum('bhrjqk,bkhd->brjqhd', pr[..., n_loc:], cv))
    return o.reshape(b, l, h, d)


def hier_moe(x, p):
    b, l, dm = x.shape
    t = x.reshape(b * l, dm)
    lg = jnp.dot(t, p['moe_wr_g'], preferred_element_type=jnp.float32) + p['moe_br_g'].astype(jnp.float32)
    pg_top, g_idx = lax.top_k(jax.nn.softmax(lg, axis=-1), 1)
    g_onehot = jax.nn.one_hot(g_idx[:, 0], N_GROUPS, dtype=jnp.float32)
    le = (jnp.einsum('nd,dge->nge', t, p['moe_wr_e'], preferred_element_type=jnp.float32)
          + p['moe_br_e'].astype(jnp.float32))
    le_sel = jnp.einsum('nge,ng->ne', le, g_onehot)
    pe_top, e_idx = lax.top_k(jax.nn.softmax(le_sel, axis=-1), TOP_K_IN_GROUP)
    pe_top = pe_top / jnp.sum(pe_top, axis=-1, keepdims=True)
    w_grp = jnp.einsum('nk,nke->ne', pe_top, jax.nn.one_hot(e_idx, E_PER_GROUP, dtype=jnp.float32))
    gate_w = ((g_onehot * pg_top)[:, :, None] * w_grp[:, None, :]).reshape(-1, N_EXPERTS).astype(x.dtype)
    hid = (jax.nn.silu(jnp.einsum('nd,edf->nef', t, p['moe_w_gate']))
           * jnp.einsum('nd,edf->nef', t, p['moe_w_up']))
    out = jnp.einsum('nef,efd->nd', hid * gate_w[:, :, None], p['moe_w_down'])
    return out.reshape(b, l, dm)


def finish_layer(x, a_out, d_out, n_out, g1, sh2, sc2, g2, p):
    b, l, _ = x.shape
    mixed = jnp.concatenate([a_out, d_out.reshape(b, l, -1), n_out.reshape(b, l, -1)], axis=-1) @ p['w_out']
    x = x + g1 * mixed
    return x + g2 * hier_moe(modulate(x, p['norm_ffn_g'], sh2, sc2), p)


def context_layer(x, cvec, p, lam_init):
    b, l, _ = x.shape
    sh1, sc1, g1, sh2, sc2, g2 = modulation(cvec, p['w_ada'], p['b_ada'])
    cv, cg, dq, dk, dv, nq, nk, nv = project(modulate(x, p['norm_mix_g'], sh1, sc1), p)
    a_out = conformer_conv(cv, cg, p)
    d_out = diff_out_norm(diff_attention(dq, dk, dv, diff_lambda(p, lam_init)), p['diff_subln_g'], lam_init)
    n_out = softmax_attention(nq, nk, nv)
    x = finish_layer(x, a_out, d_out, n_out, g1, sh2, sc2, g2, p)
    return x, dk.reshape(b, l, H_DIFF, 2 * DQK), dv, nk, nv


def latent_layer(x, cvec, ck_diff, cv_diff, ck_na, cv_na, p, lam_init, cos, sin):
    b, l, _ = x.shape
    lc = ck_diff.shape[1]
    sh1, sc1, g1, sh2, sc2, g2 = modulation(cvec, p['w_ada'], p['b_ada'])
    cv, cg, dq, dk, dv, nq, nk, nv = project(modulate(x, p['norm_mix_g'], sh1, sc1), p)
    a_out = conformer_conv(cv, cg, p)
    dq = apply_rope(dq, cos, sin)
    dk = apply_rope(dk, cos, sin)
    kd = jnp.concatenate([dk, ck_diff.reshape(b, lc, H_DIFF, 2, DQK)], axis=1)
    vd = jnp.concatenate([dv, cv_diff], axis=1)
    d_out = diff_out_norm(diff_attention(dq, kd, vd, diff_lambda(p, lam_init)), p['diff_subln_g'], lam_init)
    n_out = neighbourhood_attention(nq, nk, nv, ck_na, cv_na, p['na_rpb'])
    return finish_layer(x, a_out, d_out, n_out, g1, sh2, sc2, g2, p)


def setup_inputs(seed: int = 0) -> dict:
    key = jax.random.key(seed)
    ks = iter(jax.random.split(key, 40))
    nrm = lambda shape, s: jax.random.normal(next(ks), shape, jnp.float32) * s
    gain = lambda shape: 1.0 + nrm(shape, 0.02)
    return {
        'x_prompt': nrm((BATCH, SEQ, D_MODEL), 1.0),
        'x_sample': nrm((DEC_BATCH, DEC_SEQ, D_MODEL), 1.0),
        'cache_diff_k': nrm((DEC_BATCH, DEPTH, PAST_LEN, H_DIFF, 2 * DQK), 1.0),
        'cache_diff_v': nrm((DEC_BATCH, DEPTH, PAST_LEN, H_DIFF, HEAD_DIM), 1.0),
        'cache_na_k': nrm((DEC_BATCH, DEPTH, PAST_LEN, H_NA, HEAD_DIM), 1.0),
        'cache_na_v': nrm((DEC_BATCH, DEPTH, PAST_LEN, H_NA, HEAD_DIM), 1.0),
        'c': nrm((DEC_BATCH, D_MODEL), 1.0),
        'c_ctx': nrm((D_MODEL,), 1.0),
        'norm_mix_g': gain((DEPTH, D_MODEL)),
        'norm_ffn_g': gain((DEPTH, D_MODEL)),
        'w_ada': nrm((DEPTH, D_MODEL, 6 * D_MODEL), 0.5 * D_MODEL ** -0.5),
        'b_ada': nrm((DEPTH, 6 * D_MODEL), 0.01),
        'w_in': nrm((DEPTH, D_MODEL, 2 * C_CONV + 3 * H_DIFF * HEAD_DIM + 3 * H_NA * HEAD_DIM), D_MODEL ** -0.5),
        'w_out': nrm((DEPTH, MIX_W, D_MODEL), MIX_W ** -0.5),
        'conv_w': nrm((DEPTH, CONV_K, C_CONV), CONV_K ** -0.5),
        'conv_b': nrm((DEPTH, C_CONV), 0.01),
        'conv_ln_g': gain((DEPTH, C_CONV)),
        'conv_ln_b': nrm((DEPTH, C_CONV), 0.01),
        'diff_qn_g': gain((DEPTH, DQK)),
        'diff_kn_g': gain((DEPTH, DQK)),
        'diff_lam_q1': nrm((DEPTH, DQK), 0.1),
        'diff_lam_k1': nrm((DEPTH, DQK), 0.1),
        'diff_lam_q2': nrm((DEPTH, DQK), 0.1),
        'diff_lam_k2': nrm((DEPTH, DQK), 0.1),
        'diff_subln_g': gain((DEPTH, HEAD_DIM)),
        'na_qn_g': gain((DEPTH, HEAD_DIM)),
        'na_kn_g': gain((DEPTH, HEAD_DIM)),
        'na_rpb': nrm((DEPTH, H_NA, 2 * NA_KR - 1, 2 * NA_KC - 1), 0.1),
        'moe_wr_g': nrm((DEPTH, D_MODEL, N_GROUPS), D_MODEL ** -0.5),
        'moe_br_g': nrm((DEPTH, N_GROUPS), 0.01),
        'moe_wr_e': nrm((DEPTH, D_MODEL, N_GROUPS, E_PER_GROUP), D_MODEL ** -0.5),
        'moe_br_e': nrm((DEPTH, N_GROUPS, E_PER_GROUP), 0.01),
        'moe_w_gate': nrm((DEPTH, N_EXPERTS, D_MODEL, D_FF_EXPERT), D_MODEL ** -0.5),
        'moe_w_up': nrm((DEPTH, N_EXPERTS, D_MODEL, D_FF_EXPERT), D_MODEL ** -0.5),
        'moe_w_down': nrm((DEPTH, N_EXPERTS, D_FF_EXPERT, D_MODEL), D_FF_EXPERT ** -0.5),
    }


def reference(x_prompt, x_sample, cache_diff_k, cache_diff_v, cache_na_k, cache_na_v, c, c_ctx,
              norm_mix_g, norm_ffn_g, w_ada, b_ada, w_in, w_out, conv_w, conv_b, conv_ln_g, conv_ln_b,
              diff_qn_g, diff_kn_g, diff_lam_q1, diff_lam_k1, diff_lam_q2, diff_lam_k2, diff_subln_g,
              na_qn_g, na_kn_g, na_rpb, moe_wr_g, moe_br_g, moe_wr_e, moe_br_e,
              moe_w_gate, moe_w_up, moe_w_down):
    cos, sin = axial_rope_tables(x_sample.shape[1])
    xp = x_prompt
    xs = x_sample
    new_dk, new_dv, new_nk, new_nv = [], [], [], []
    for l in range(DEPTH):
        p = {
            'norm_mix_g': norm_mix_g[l], 'norm_ffn_g': norm_ffn_g[l],
            'w_ada': w_ada[l], 'b_ada': b_ada[l], 'w_in': w_in[l], 'w_out': w_out[l],
            'conv_w': conv_w[l], 'conv_b': conv_b[l], 'conv_ln_g': conv_ln_g[l], 'conv_ln_b': conv_ln_b[l],
            'diff_qn_g': diff_qn_g[l], 'diff_kn_g': diff_kn_g[l],
            'diff_lam_q1': diff_lam_q1[l], 'diff_lam_k1': diff_lam_k1[l],
            'diff_lam_q2': diff_lam_q2[l], 'diff_lam_k2': diff_lam_k2[l], 'diff_subln_g': diff_subln_g[l],
            'na_qn_g': na_qn_g[l], 'na_kn_g': na_kn_g[l], 'na_rpb': na_rpb[l],
            'moe_wr_g': moe_wr_g[l], 'moe_br_g': moe_br_g[l], 'moe_wr_e': moe_wr_e[l], 'moe_br_e': moe_br_e[l],
            'moe_w_gate': moe_w_gate[l], 'moe_w_up': moe_w_up[l], 'moe_w_down': moe_w_down[l],
        }
        li = lambda_init(l)
        xp, dk, dv, nk, nv = context_layer(xp, c_ctx[None, :], p, li)
        new_dk.append(dk)
        new_dv.append(dv)
        new_nk.append(nk)
        new_nv.append(nv)
        xs = latent_layer(xs, c, cache_diff_k[:, l], cache_diff_v[:, l], cache_na_k[:, l], cache_na_v[:, l],
                          p, li, cos, sin)
    new_diff_k = jnp.stack(new_dk, axis=1)
    new_diff_v = jnp.stack(new_dv, axis=1)
    new_na_k = jnp.stack(new_nk, axis=1)
    new_na_v = jnp.stack(new_nv, axis=1)
    return (xp, xs, new_diff_k, new_diff_v, new_na_k, new_na_v)
```

```python
import functools
import math

import numpy as np
import jax
import jax.numpy as jnp
from jax import lax
from jax.experimental import pallas as pl
from jax.experimental.pallas import tpu as pltpu

F32 = jnp.float32
BF16 = jnp.bfloat16
I32 = jnp.int32

D_MODEL = 2048
BATCH = 16
SEQ = 256
DEPTH = 2
DEC_BATCH = 2
DEC_SEQ = 1024
PAST_LEN = 256
GRID_W = 64
GRID_ROWS = DEC_SEQ // GRID_W
HEAD_DIM = 128
C_CONV = 512
CONV_K = 31
N_HEADS = 6
DQK = 64
NA_KR = 8
NA_KC = 16
ROPE_BASE = 10000.0
N_GROUPS = 4
E_PER_GROUP = 4
N_EXPERTS = 16
D_FF = 512
N_ADA = 6 * D_MODEL
W_IN_COLS = 2 * C_CONV + 6 * N_HEADS * HEAD_DIM
T_CTX = BATCH * SEQ
T_LAT = DEC_BATCH * DEC_SEQ
T_ALL = T_CTX + T_LAT

COL_CV, COL_CG, COL_DQ, COL_DK, COL_DV, COL_NQ, COL_NK, COL_NV = 0, 4, 8, 14, 20, 26, 32, 38

LANES = 128
MOD_ROWS = 8
PROJ_TM = 1024
PROJ_TN = 512
MIX_TM = 512
MOE_TM = 256
MOE_ITEMS = 2 * T_ALL // MOE_TM + N_EXPERTS
ROUTE_ROWS = 32
CUM_BLK = 512
VMEM_LIMIT = 56 * 1024 * 1024
NEG_BIG = -1e30


def _cparams(sem):
    return pltpu.CompilerParams(dimension_semantics=sem, vmem_limit_bytes=VMEM_LIMIT)


def _sigmoid(x):
    return 1.0 / (1.0 + jnp.exp(-x))


def _silu(x):
    return x * _sigmoid(x)


def _lambda_init(layer):
    return 0.8 - 0.6 * math.exp(-0.3 * layer)


def _ada_kernel(c_ref, w_ref, b_ref, o_ref):
    c = c_ref[...]
    s = _silu(c).astype(BF16)
    o_ref[...] = jnp.dot(s, w_ref[...].astype(BF16), preferred_element_type=F32) + b_ref[...]


def _ada_modulation(cvec, w_ada, b_ada):
    tn = 1024
    return pl.pallas_call(
        _ada_kernel,
        grid=(DEPTH, N_ADA // tn),
        in_specs=[
            pl.BlockSpec((MOD_ROWS, D_MODEL), lambda l, j: (0, 0)),
            pl.BlockSpec((None, D_MODEL, tn), lambda l, j: (l, 0, j)),
            pl.BlockSpec((None, 1, tn), lambda l, j: (l, 0, j)),
        ],
        out_specs=pl.BlockSpec((None, MOD_ROWS, tn), lambda l, j: (l, 0, j)),
        out_shape=jax.ShapeDtypeStruct((DEPTH, MOD_ROWS, N_ADA), F32),
        compiler_params=_cparams(("arbitrary", "arbitrary")),
        name="ada_modulation",
    )(cvec, w_ada, b_ada.reshape(DEPTH, 1, N_ADA))


def _mod_spec(layer, chunk, row_of_block):
    return pl.BlockSpec((None, 1, D_MODEL),
                        lambda i, *_: (layer * MOD_ROWS + row_of_block(i), 0, chunk))


def _modulate(x, g, shift, scale):
    ms = jnp.mean(x * x, axis=-1, keepdims=True)
    return x * lax.rsqrt(ms + 1e-6) * (g * (1.0 + scale)) + shift


def _proj_kernel(x_ref, sh_ref, sc_ref, g_ref, w_ref, a_ref, gain_ref, flag_ref, *rest, rope):
    if rope:
        cos_ref, sa_ref, sb_ref, o_ref, h_ref = rest
    else:
        o_ref, h_ref = rest
    j = pl.program_id(1)

    @pl.when(j == 0)
    def _():
        h_ref[...] = _modulate(x_ref[...], g_ref[...], sh_ref[...], sc_ref[...]).astype(BF16)

    y = jnp.dot(h_ref[...], w_ref[...].astype(BF16), preferred_element_type=F32)

    def finish(apply_rope):
        for g in range(PROJ_TN // LANES):
            sl = slice(g * LANES, (g + 1) * LANES)
            yg = y[:, sl]
            ms = jnp.dot((yg * yg).astype(BF16), a_ref[g], preferred_element_type=F32)
            r = jnp.where(flag_ref[:, sl] > 0.0, lax.rsqrt(ms + 1e-6), 1.0)
            yg = yg * r * gain_ref[:, sl]
            if apply_rope:
                yg = (yg * cos_ref[...] + pltpu.roll(yg, LANES - 16, 1) * sa_ref[...]
                      + pltpu.roll(yg, 16, 1) * sb_ref[...])
            o_ref[:, sl] = yg

    if rope:
        first = COL_DQ * LANES // PROJ_TN
        last = COL_DV * LANES // PROJ_TN - 1
        is_rope = jnp.logical_and(j >= first, j <= last)
        pl.when(is_rope)(lambda: finish(True))
        pl.when(jnp.logical_not(is_rope))(lambda: finish(False))
    else:
        finish(False)


def _project(x, mod8, norm_g, w_in, amats, gains, flags, rope_tabs, layer, row_of_block):
    t = x.shape[0]
    rope = rope_tabs is not None
    in_specs = [
        pl.BlockSpec((PROJ_TM, D_MODEL), lambda i, j: (i, 0)),
        _mod_spec(layer, 0, row_of_block),
        _mod_spec(layer, 1, row_of_block),
        pl.BlockSpec((None, 1, D_MODEL), lambda i, j: (layer, 0, 0)),
        pl.BlockSpec((None, D_MODEL, PROJ_TN), lambda i, j: (layer, 0, j)),
        pl.BlockSpec((PROJ_TN // LANES, LANES, LANES), lambda i, j: (j, 0, 0)),
        pl.BlockSpec((None, 1, PROJ_TN), lambda i, j: (layer, 0, j)),
        pl.BlockSpec((1, PROJ_TN), lambda i, j: (0, j)),
    ]
    args = [x, mod8, mod8, norm_g, w_in, amats, gains, flags]
    if rope:
        in_specs += [pl.BlockSpec((DEC_SEQ, LANES), lambda i, j: (0, 0))] * 3
        args += list(rope_tabs)
    return pl.pallas_call(
        functools.partial(_proj_kernel, rope=rope),
        grid=(t // PROJ_TM, W_IN_COLS // PROJ_TN),
        in_specs=in_specs,
        out_specs=pl.BlockSpec((PROJ_TM, PROJ_TN), lambda i, j: (i, j)),
        out_shape=jax.ShapeDtypeStruct((t, W_IN_COLS), F32),
        scratch_shapes=[pltpu.VMEM((PROJ_TM, D_MODEL), BF16)],
        compiler_params=_cparams(("arbitrary", "arbitrary")),
        name="proj_lat" if rope else "proj_ctx",
    )(*args)


CONV_CH = 128
CONV_HALO = 16


def _conv_kernel(cv_ref, cg_ref, w_ref, b_ref, lg_ref, lb_ref, o_ref, win_ref, y_ref, *, seq):
    c = pl.program_id(1)
    n_chunks = seq // CONV_CH
    base = pl.multiple_of(c * CONV_CH, CONV_CH)

    def glu(start, size):
        return cv_ref[pl.ds(start, size), :] * _sigmoid(cg_ref[pl.ds(start, size), :])

    win_ref[CONV_HALO:CONV_HALO + CONV_CH, :] = glu(base, CONV_CH)
    lo_start = pl.multiple_of(jnp.maximum(base - CONV_HALO, 0), CONV_HALO)
    hi_start = pl.multiple_of(jnp.minimum(base + CONV_CH, seq - CONV_HALO), CONV_HALO)
    lo_keep = (c > 0).astype(F32)
    hi_keep = (c < n_chunks - 1).astype(F32)
    win_ref[0:CONV_HALO, :] = glu(lo_start, CONV_HALO) * lo_keep
    win_ref[CONV_HALO + CONV_CH:, :] = glu(hi_start, CONV_HALO) * hi_keep

    off = CONV_HALO - CONV_K // 2
    for g in range(C_CONV // LANES):
        sl = slice(g * LANES, (g + 1) * LANES)
        acc = jnp.zeros((CONV_CH, LANES), F32) + b_ref[:, sl]
        for k in range(CONV_K):
            acc = acc + win_ref[off + k:off + k + CONV_CH, sl] * w_ref[k:k + 1, sl]
        y_ref[:, sl] = acc

    y = y_ref[...]
    mu = jnp.mean(y, axis=-1, keepdims=True)
    yc = y - mu
    var = jnp.mean(yc * yc, axis=-1, keepdims=True)
    z = yc * lax.rsqrt(var + 1e-5) * lg_ref[...] + lb_ref[...]
    o_ref[...] = _silu(z)


def _conformer_conv(proj, conv_w, conv_b, ln_g, ln_b, layer, batch, seq):
    t = proj.shape[0]
    n_chunks = seq // CONV_CH
    vec = lambda: pl.BlockSpec((None, 1, C_CONV), lambda b, c: (layer, 0, 0))
    return pl.pallas_call(
        functools.partial(_conv_kernel, seq=seq),
        grid=(batch, n_chunks),
        in_specs=[
            pl.BlockSpec((seq, C_CONV), lambda b, c: (b, COL_CV * LANES // C_CONV)),
            pl.BlockSpec((seq, C_CONV), lambda b, c: (b, COL_CG * LANES // C_CONV)),
            pl.BlockSpec((None, CONV_K, C_CONV), lambda b, c: (layer, 0, 0)),
            vec(), vec(), vec(),
        ],
        out_specs=pl.BlockSpec((CONV_CH, C_CONV), lambda b, c: (b * n_chunks + c, 0)),
        out_shape=jax.ShapeDtypeStruct((t, C_CONV), F32),
        scratch_shapes=[pltpu.VMEM((CONV_CH + 2 * CONV_HALO, C_CONV), F32),
                        pltpu.VMEM((CONV_CH, C_CONV), F32)],
        compiler_params=_cparams(("arbitrary", "arbitrary")),
        name="conv_%d" % seq,
    )(proj, proj, conv_w, conv_b.reshape(DEPTH, 1, C_CONV), ln_g.reshape(DEPTH, 1, C_CONV),
      ln_b.reshape(DEPTH, 1, C_CONV))


_NT = (((1,), (1,)), ((), ()))


def _softmax_rows(s):
    m = jnp.max(s, axis=-1, keepdims=True)
    e = jnp.exp(s - m)
    return e / jnp.sum(e, axis=-1, keepdims=True)


def _diff_attn_kernel(lam_ref, g_ref, q_ref, k_ref, v_ref, *rest, lam_init, cached):
    if cached:
        ck_ref, cv_ref, o_ref = rest
        k = jnp.concatenate([k_ref[...], ck_ref[...]], axis=0).astype(BF16)
        v = jnp.concatenate([v_ref[...], cv_ref[...]], axis=0).astype(BF16)
    else:
        (o_ref,) = rest
        k = k_ref[...].astype(BF16)
        v = v_ref[...].astype(BF16)
    lv = lam_ref[...]
    lam = (jnp.exp(jnp.sum(lv[0:1] * lv[1:2], axis=-1, keepdims=True))
           - jnp.exp(jnp.sum(lv[2:3] * lv[3:4], axis=-1, keepdims=True)) + lam_init)
    q = q_ref[...]
    lane = lax.broadcasted_iota(I32, q.shape, 1)
    q0 = jnp.where(lane < DQK, q, 0.0).astype(BF16)
    q1 = jnp.where(lane >= DQK, q, 0.0).astype(BF16)
    scale = DQK ** -0.5
    p0 = _softmax_rows(lax.dot_general(q0, k, _NT, preferred_element_type=F32) * scale)
    p1 = _softmax_rows(lax.dot_general(q1, k, _NT, preferred_element_type=F32) * scale)
    a = (p0 - lam * p1).astype(BF16)
    o = jnp.dot(a, v, preferred_element_type=F32)
    ms = jnp.mean(o * o, axis=-1, keepdims=True)
    o_ref[...] = o * lax.rsqrt(ms + 1e-5) * g_ref[...] * (1.0 - lam_init)


def _diff_attention(proj, lam_params, subln_g, cache_k, cache_v, layer, batch, seq, tq):
    t = proj.shape[0]
    nq = seq // tq
    cached = cache_k is not None
    in_specs = [
        pl.BlockSpec((None, 4, DQK), lambda b, h, qi: (layer, 0, 0)),
        pl.BlockSpec((None, 1, HEAD_DIM), lambda b, h, qi: (layer, 0, 0)),
        pl.BlockSpec((tq, HEAD_DIM), lambda b, h, qi: (b * nq + qi, COL_DQ + h)),
        pl.BlockSpec((seq, HEAD_DIM), lambda b, h, qi: (b, COL_DK + h)),
        pl.BlockSpec((seq, HEAD_DIM), lambda b, h, qi: (b, COL_DV + h)),
    ]
    args = [lam_params, subln_g.reshape(DEPTH, 1, HEAD_DIM), proj, proj, proj]
    if cached:
        cspec = pl.BlockSpec((None, None, PAST_LEN, HEAD_DIM), lambda b, h, qi: (b, layer, 0, h))
        in_specs += [cspec, cspec]
        args += [cache_k, cache_v]
    return pl.pallas_call(
        functools.partial(_diff_attn_kernel, lam_init=_lambda_init(layer), cached=cached),
        grid=(batch, N_HEADS, nq),
        in_specs=in_specs,
        out_specs=pl.BlockSpec((tq, HEAD_DIM), lambda b, h, qi: (b * nq + qi, h)),
        out_shape=jax.ShapeDtypeStruct((t, N_HEADS * HEAD_DIM), F32),
        compiler_params=_cparams(("arbitrary", "arbitrary", "arbitrary")),
        name="diff_attn_lat" if cached else "diff_attn_ctx",
    )(*args)


def _ctx_attn_kernel(q_ref, k_ref, v_ref, o_ref):
    q = q_ref[...].astype(BF16)
    k = k_ref[...].astype(BF16)
    s = lax.dot_general(q, k, _NT, preferred_element_type=F32) * (HEAD_DIM ** -0.5)
    p = _softmax_rows(s).astype(BF16)
    o_ref[...] = jnp.dot(p, v_ref[...].astype(BF16), preferred_element_type=F32)


def _ctx_attention(proj):
    blk = lambda col: pl.BlockSpec((SEQ, HEAD_DIM), lambda b, h: (b, col + h))
    return pl.pallas_call(
        _ctx_attn_kernel,
        grid=(BATCH, N_HEADS),
        in_specs=[blk(COL_NQ), blk(COL_NK), blk(COL_NV)],
        out_specs=pl.BlockSpec((SEQ, HEAD_DIM), lambda b, h: (b, h)),
        out_shape=jax.ShapeDtypeStruct((T_CTX, N_HEADS * HEAD_DIM), F32),
        compiler_params=_cparams(("arbitrary", "arbitrary")),
        name="ctx_attn",
    )(proj, proj, proj)


NA_PAIRS = 2 * NA_KR - 2
NA_LOCAL = NA_KR * GRID_W


def _na_attn_kernel(src_ref, mask_ref, q_ref, k_ref, v_ref, ck_ref, cv_ref, o_ref, bias_ref):
    for d in range(NA_PAIRS):
        row = jnp.broadcast_to(src_ref[d:d + 1, :], (GRID_W, LANES))
        bias_ref[d] = pltpu.roll(row, LANES - (NA_KC - 1), 1, stride=1, stride_axis=0) + mask_ref[...]
    ck = ck_ref[...].astype(BF16)
    cv = cv_ref[...].astype(BF16)
    scale = HEAD_DIM ** -0.5
    for qr in range(GRID_ROWS):
        start = min(max(qr - NA_KR // 2, 0), GRID_ROWS - NA_KR)
        d0 = start - qr + NA_KR - 1
        q = q_ref[qr * GRID_W:(qr + 1) * GRID_W, :].astype(BF16)
        kl = k_ref[start * GRID_W:start * GRID_W + NA_LOCAL, :].astype(BF16)
        vl = v_ref[start * GRID_W:start * GRID_W + NA_LOCAL, :].astype(BF16)
        bias = jnp.concatenate([bias_ref[d0 + 2 * i] for i in range(NA_KR // 2)], axis=1)
        s_loc = lax.dot_general(q, kl, _NT, preferred_element_type=F32) * scale + bias
        s_ctx = lax.dot_general(q, ck, _NT, preferred_element_type=F32) * scale
        m = jnp.maximum(jnp.max(s_loc, axis=-1, keepdims=True), jnp.max(s_ctx, axis=-1, keepdims=True))
        e_loc = jnp.exp(s_loc - m)
        e_ctx = jnp.exp(s_ctx - m)
        den = jnp.sum(e_loc, axis=-1, keepdims=True) + jnp.sum(e_ctx, axis=-1, keepdims=True)
        p_loc = (e_loc / den).astype(BF16)
        p_ctx = (e_ctx / den).astype(BF16)
        o_ref[qr * GRID_W:(qr + 1) * GRID_W, :] = (
            jnp.dot(p_loc, vl, preferred_element_type=F32) + jnp.dot(p_ctx, cv, preferred_element_type=F32))


def _na_attention(proj, rpb_src, na_mask, cache_k, cache_v, layer):
    blk = lambda col: pl.BlockSpec((DEC_SEQ, HEAD_DIM), lambda b, h: (b, col + h))
    cspec = pl.BlockSpec((None, None, PAST_LEN, HEAD_DIM), lambda b, h: (b, layer, 0, h))
    return pl.pallas_call(
        _na_attn_kernel,
        grid=(DEC_BATCH, N_HEADS),
        in_specs=[
            pl.BlockSpec((None, None, 16, LANES), lambda b, h: (layer, h, 0, 0)),
            pl.BlockSpec((GRID_W, LANES), lambda b, h: (0, 0)),
            blk(COL_NQ), blk(COL_NK), blk(COL_NV), cspec, cspec,
        ],
        out_specs=pl.BlockSpec((DEC_SEQ, HEAD_DIM), lambda b, h: (b, h)),
        out_shape=jax.ShapeDtypeStruct((T_LAT, N_HEADS * HEAD_DIM), F32),
        scratch_shapes=[pltpu.VMEM((NA_PAIRS, GRID_W, LANES), F32)],
        compiler_params=_cparams(("arbitrary", "arbitrary")),
        name="na_attn_lat",
    )(rpb_src, na_mask, proj, proj, proj, cache_k, cache_v)


def _split3(x):
    hi = x.astype(BF16)
    r1 = x - hi.astype(F32)
    mid = r1.astype(BF16)
    lo = (r1 - mid.astype(F32)).astype(BF16)
    return hi, mid, lo


def _mix_kernel(a_ref, d_ref, n_ref, w_ref, x_ref, g1_ref, sh_ref, sc_ref, ng_ref, wr_ref, br_ref,
                x1_ref, lt_ref):
    k0, k1 = C_CONV, C_CONV + N_HEADS * HEAD_DIM
    mixed = (jnp.dot(a_ref[...].astype(BF16), w_ref[0:k0, :], preferred_element_type=F32)
             + jnp.dot(d_ref[...].astype(BF16), w_ref[k0:k1, :], preferred_element_type=F32)
             + jnp.dot(n_ref[...].astype(BF16), w_ref[k1:, :], preferred_element_type=F32))
    x1 = x_ref[...] + g1_ref[...] * mixed
    x1_ref[...] = x1
    h2 = _modulate(x1, ng_ref[...], sh_ref[...], sc_ref[...])
    hh, hm, hl = _split3(h2)
    wh, wm, wl = _split3(wr_ref[...])
    dot = lambda a, b: lax.dot_general(a, b, _NT, preferred_element_type=F32)
    lt = (dot(wh, hh) + (dot(wh, hm) + dot(wm, hh)) + (dot(wm, hm) + dot(wh, hl) + dot(wl, hh)))
    lt_ref[...] = lt + br_ref[...]


def _mix(a_out, d_out, n_out, w_out_bf, x, mod8, norm_g, wr_t, br, layer, row_of_block):
    t = x.shape[0]
    rows = lambda width: pl.BlockSpec((MIX_TM, width), lambda i: (i, 0))
    return pl.pallas_call(
        _mix_kernel,
        grid=(t // MIX_TM,),
        in_specs=[
            rows(C_CONV), rows(N_HEADS * HEAD_DIM), rows(N_HEADS * HEAD_DIM),
            pl.BlockSpec((None, D_MODEL, D_MODEL), lambda i: (layer, 0, 0)),
            rows(D_MODEL),
            _mod_spec(layer, 2, row_of_block),
            _mod_spec(layer, 3, row_of_block),
            _mod_spec(layer, 4, row_of_block),
            pl.BlockSpec((None, 1, D_MODEL), lambda i: (layer, 0, 0)),
            pl.BlockSpec((None, ROUTE_ROWS, D_MODEL), lambda i: (layer, 0, 0)),
            pl.BlockSpec((None, ROUTE_ROWS, 1), lambda i: (layer, 0, 0)),
        ],
        out_specs=[rows(D_MODEL), pl.BlockSpec((ROUTE_ROWS, MIX_TM), lambda i: (0, i))],
        out_shape=[jax.ShapeDtypeStruct((t, D_MODEL), F32),
                   jax.ShapeDtypeStruct((ROUTE_ROWS, t), F32)],
        compiler_params=_cparams(("arbitrary",)),
        name="mix_%d" % t,
    )(a_out, d_out, n_out, w_out_bf, x, mod8, mod8, mod8, norm_g, wr_t, br)


def _cast_kernel(x_ref, o_ref):
    o_ref[...] = x_ref[...].astype(BF16)


def _cast_bf16(w):
    tm = 512
    spec = pl.BlockSpec((None, tm, D_MODEL), lambda l, i: (l, i, 0))
    return pl.pallas_call(
        _cast_kernel, grid=(DEPTH, D_MODEL // tm), in_specs=[spec], out_specs=spec,
        out_shape=jax.ShapeDtypeStruct(w.shape, BF16),
        compiler_params=_cparams(("arbitrary", "arbitrary")), name="cast_w_out",
    )(w)


def _route_kernel(lt_ref, tri_ref, pos_ref, wt_ref, item_ref):
    t = T_ALL
    lg = lt_ref[0:N_GROUPS, :]
    eg = jnp.exp(lg - jnp.max(lg, axis=0, keepdims=True))
    pg = eg / jnp.sum(eg, axis=0, keepdims=True)
    pg_top = jnp.max(pg, axis=0, keepdims=True)
    gi = lax.broadcasted_iota(I32, pg.shape, 0).astype(F32)
    g_idx = jnp.min(jnp.where(pg == pg_top, gi, float(N_GROUPS)), axis=0, keepdims=True)

    le = jnp.zeros((E_PER_GROUP, t), F32)
    for g in range(N_GROUPS):
        rows = lt_ref[N_GROUPS + g * E_PER_GROUP:N_GROUPS + (g + 1) * E_PER_GROUP, :]
        le = jnp.where(g_idx == float(g), rows, le)
    ee = jnp.exp(le - jnp.max(le, axis=0, keepdims=True))
    pe = ee / jnp.sum(ee, axis=0, keepdims=True)
    ei = lax.broadcasted_iota(I32, pe.shape, 0).astype(F32)
    p1 = jnp.max(pe, axis=0, keepdims=True)
    i1 = jnp.min(jnp.where(pe == p1, ei, float(E_PER_GROUP)), axis=0, keepdims=True)
    pe_rest = jnp.where(ei == i1, -1.0, pe)
    p2 = jnp.max(pe_rest, axis=0, keepdims=True)
    i2 = jnp.min(jnp.where(pe_rest == p2, ei, float(E_PER_GROUP)), axis=0, keepdims=True)
    den = p1 + p2
    wt_ref[0:1, :] = pg_top * (p1 / den)
    wt_ref[1:2, :] = pg_top * (p2 / den)
    e1 = g_idx * E_PER_GROUP + i1
    e2 = g_idx * E_PER_GROUP + i2

    erow = lax.broadcasted_iota(I32, (N_EXPERTS, t), 0).astype(F32)
    oh1 = (erow == e1).astype(F32)
    oh2 = (erow == e2).astype(F32)
    cnt = oh1 + oh2
    carry = jnp.zeros((N_EXPERTS, 1), F32)
    ranks = []
    for b in range(t // CUM_BLK):
        blk = cnt[:, b * CUM_BLK:(b + 1) * CUM_BLK]
        ranks.append(jnp.dot(blk.astype(BF16), tri_ref[...], preferred_element_type=F32) + carry)
        carry = carry + jnp.sum(blk, axis=1, keepdims=True)
    rank = jnp.concatenate(ranks, axis=1)

    erow_l = lax.broadcasted_iota(I32, (N_EXPERTS, LANES), 0)

    def excl_scan(v):
        inc = v
        for s in (1, 2, 4, 8):
            inc = inc + jnp.where(erow_l >= s, pltpu.roll(inc, s, 0), 0.0)
        return inc - v

    total = jnp.broadcast_to(carry, (N_EXPERTS, LANES))
    offs = excl_scan(total)
    pos_ref[0:1, :] = jnp.sum(oh1 * (rank + offs[:, 0:1]), axis=0, keepdims=True).astype(I32)
    pos_ref[1:2, :] = jnp.sum(oh2 * (rank + offs[:, 0:1]), axis=0, keepdims=True).astype(I32)

    shift = int(math.log2(MOE_TM))
    offs_i = offs.astype(I32)
    total_i = total.astype(I32)
    first_tile = lax.shift_right_logical(offs_i, shift)
    last_tile = lax.shift_right_logical(offs_i + total_i - 1, shift)
    n_item = jnp.where(total_i > 0, last_tile - first_tile + 1, 0).astype(F32)
    item_start = excl_scan(n_item)
    item_end = item_start + n_item
    kk = lax.broadcasted_iota(I32, (N_EXPERTS, LANES), 1).astype(F32)
    item_e = jnp.minimum(jnp.sum((item_end <= kk).astype(F32), axis=0, keepdims=True),
                         float(N_EXPERTS - 1))
    sel = (erow_l.astype(F32) == item_e).astype(F32)
    pick = lambda v: jnp.sum(sel * v, axis=0, keepdims=True)
    item_ref[0:1, :] = (kk[0:1] + pick(first_tile.astype(F32) - item_start)).astype(I32)
    item_ref[1:2, :] = item_e.astype(I32)
    item_ref[2:3, :] = pick(offs).astype(I32)
    item_ref[3:4, :] = pick(offs + total).astype(I32)
    item_ref[4:5, :] = item_end[N_EXPERTS - 1:N_EXPERTS, :].astype(I32)
    item_ref[5:8, :] = jnp.zeros((3, LANES), I32)


def _route(lt_all, tri):
    full = lambda shape: pl.BlockSpec(shape, lambda i: (0,) * len(shape))
    return pl.pallas_call(
        _route_kernel,
        grid=(1,),
        in_specs=[full((ROUTE_ROWS, T_ALL)), full((CUM_BLK, CUM_BLK))],
        out_specs=[full((2, T_ALL)), full((2, T_ALL)), full((8, LANES))],
        out_shape=[jax.ShapeDtypeStruct((2, T_ALL), I32), jax.ShapeDtypeStruct((2, T_ALL), F32),
                   jax.ShapeDtypeStruct((8, LANES), I32)],
        compiler_params=_cparams(("arbitrary",)),
        name="moe_route",
    )(lt_all, tri)


N_TOK_TILES = T_ALL // MOE_TM
CTX_TILES = T_CTX // MOE_TM


def _tok_row(i):
    return jnp.where(i < CTX_TILES, 0, 1 + (i - CTX_TILES) // (DEC_SEQ // MOE_TM))


def _row_copy(src, src_row, dst, dst_row, sem):
    return pltpu.make_async_copy(src.at[pl.ds(src_row, 1)], dst.at[pl.ds(dst_row, 1)], sem)


def _dispatch_kernel(pos_ref, xc_ref, xl_ref, sh_ref, sc_ref, ng_ref, xs_ref, h_ref, sem):
    i = pl.program_id(0)

    @pl.when(i < CTX_TILES)
    def _():
        h_ref[...] = _modulate(xc_ref[...], ng_ref[...], sh_ref[...], sc_ref[...])

    @pl.when(i >= CTX_TILES)
    def _():
        h_ref[...] = _modulate(xl_ref[...], ng_ref[...], sh_ref[...], sc_ref[...])

    def issue(r, carry):
        tok = i * MOE_TM + r
        _row_copy(h_ref, r, xs_ref, pos_ref[tok], sem).start()
        _row_copy(h_ref, r, xs_ref, pos_ref[T_ALL + tok], sem).start()
        return carry

    lax.fori_loop(0, MOE_TM, issue, 0)

    def drain(r, carry):
        _row_copy(h_ref, 0, xs_ref, 0, sem).wait()
        return carry

    lax.fori_loop(0, 2 * MOE_TM, drain, 0)


def _dispatch(pos_flat, x1c, x1l, mod8, norm_g, layer):
    grid_spec = pltpu.PrefetchScalarGridSpec(
        num_scalar_prefetch=1,
        grid=(N_TOK_TILES,),
        in_specs=[
            pl.BlockSpec((MOE_TM, D_MODEL), lambda i, *_: (jnp.minimum(i, CTX_TILES - 1), 0)),
            pl.BlockSpec((MOE_TM, D_MODEL), lambda i, *_: (jnp.maximum(i - CTX_TILES, 0), 0)),
            _mod_spec(layer, 3, _tok_row),
            _mod_spec(layer, 4, _tok_row),
            pl.BlockSpec((None, 1, D_MODEL), lambda i, *_: (layer, 0, 0)),
        ],
        out_specs=pl.BlockSpec(memory_space=pl.ANY),
        scratch_shapes=[pltpu.VMEM((MOE_TM, D_MODEL), F32), pltpu.SemaphoreType.DMA(())],
    )
    return pl.pallas_call(
        _dispatch_kernel,
        grid_spec=grid_spec,
        out_shape=jax.ShapeDtypeStruct((2 * T_ALL, D_MODEL), F32),
        compiler_params=_cparams(("arbitrary",)),
        name="moe_dispatch",
    )(pos_flat, x1c, x1l, mod8, mod8, norm_g)


def _expert_kernel(tile_ref, exp_ref, lo_ref, hi_ref, n_ref, xs_ref, wg_ref, wu_ref, wd_ref, ys_ref):
    k = pl.program_id(0)

    @pl.when(k < n_ref[0])
    def _():
        x = xs_ref[...].astype(BF16)
        gate = jnp.dot(x, wg_ref[...].astype(BF16), preferred_element_type=F32)
        up = jnp.dot(x, wu_ref[...].astype(BF16), preferred_element_type=F32)
        hid = (_silu(gate) * up).astype(BF16)
        y = jnp.dot(hid, wd_ref[...].astype(BF16), preferred_element_type=F32)
        row0 = tile_ref[k] * MOE_TM
        row = row0 + lax.broadcasted_iota(I32, (MOE_TM, 1), 0)
        mine = jnp.logical_and(row >= lo_ref[k], row < hi_ref[k])
        first = lo_ref[k] <= row0

        @pl.when(first)
        def _():
            ys_ref[...] = jnp.where(mine, y, 0.0)

        @pl.when(jnp.logical_not(first))
        def _():
            ys_ref[...] = jnp.where(mine, y, ys_ref[...])


def _experts(items, xs, w_gate, w_up, w_down, layer):
    item_tile, item_exp, item_lo, item_hi, n_items = items
    cur = lambda k, n: jnp.minimum(k, n[0] - 1)
    rows = pl.BlockSpec((MOE_TM, D_MODEL), lambda k, t, e, lo, hi, n: (t[cur(k, n)], 0))
    w_in = pl.BlockSpec((None, None, D_MODEL, D_FF),
                        lambda k, t, e, lo, hi, n: (layer, e[cur(k, n)], 0, 0))
    w_dn = pl.BlockSpec((None, None, D_FF, D_MODEL),
                        lambda k, t, e, lo, hi, n: (layer, e[cur(k, n)], 0, 0))
    grid_spec = pltpu.PrefetchScalarGridSpec(
        num_scalar_prefetch=5, grid=(MOE_ITEMS,),
        in_specs=[rows, w_in, w_in, w_dn], out_specs=rows)
    return pl.pallas_call(
        _expert_kernel,
        grid_spec=grid_spec,
        out_shape=jax.ShapeDtypeStruct(xs.shape, F32),
        compiler_params=_cparams(("arbitrary",)),
        name="moe_experts",
    )(item_tile, item_exp, item_lo, item_hi, n_items, xs, w_gate, w_up, w_down)


def _combine_kernel(pos_ref, xc_ref, xl_ref, wt_ref, g2_ref, ys_ref, oc_ref, ol_ref, y_ref, sem):
    i = pl.program_id(0)

    def issue(r, carry):
        tok = i * MOE_TM + r
        _row_copy(ys_ref, pos_ref[tok], y_ref.at[0], r, sem).start()
        _row_copy(ys_ref, pos_ref[T_ALL + tok], y_ref.at[1], r, sem).start()
        return carry

    lax.fori_loop(0, MOE_TM, issue, 0)

    def drain(r, carry):
        _row_copy(ys_ref, 0, y_ref.at[0], 0, sem).wait()
        return carry

    lax.fori_loop(0, 2 * MOE_TM, drain, 0)

    moe = wt_ref[:, 0:1] * y_ref[0] + wt_ref[:, 1:2] * y_ref[1]

    @pl.when(i < CTX_TILES)
    def _():
        oc_ref[...] = xc_ref[...] + g2_ref[...] * moe

    @pl.when(i >= CTX_TILES)
    def _():
        ol_ref[...] = xl_ref[...] + g2_ref[...] * moe


def _combine(pos_flat, x1c, x1l, wts_t, mod8, ys, layer):
    cspec = pl.BlockSpec((MOE_TM, D_MODEL), lambda i, *_: (jnp.minimum(i, CTX_TILES - 1), 0))
    lspec = pl.BlockSpec((MOE_TM, D_MODEL), lambda i, *_: (jnp.maximum(i - CTX_TILES, 0), 0))
    grid_spec = pltpu.PrefetchScalarGridSpec(
        num_scalar_prefetch=1,
        grid=(N_TOK_TILES,),
        in_specs=[
            cspec, lspec,
            pl.BlockSpec((MOE_TM, 2), lambda i, *_: (i, 0)),
            _mod_spec(layer, 5, _tok_row),
            pl.BlockSpec(memory_space=pl.ANY),
        ],
        out_specs=[cspec, lspec],
        scratch_shapes=[pltpu.VMEM((2, MOE_TM, D_MODEL), F32), pltpu.SemaphoreType.DMA(())],
    )
    return pl.pallas_call(
        _combine_kernel,
        grid_spec=grid_spec,
        out_shape=[jax.ShapeDtypeStruct((T_CTX, D_MODEL), F32),
                   jax.ShapeDtypeStruct((T_LAT, D_MODEL), F32)],
        compiler_params=_cparams(("arbitrary",)),
        name="moe_combine",
    )(pos_flat, x1c, x1l, wts_t, mod8, ys)


def _norm_tables():
    amats = np.zeros((W_IN_COLS // LANES, LANES, LANES), np.float32)
    flags = np.zeros((1, W_IN_COLS), np.float32)
    half = np.kron(np.eye(2, dtype=np.float32), np.full((DQK, DQK), 1.0 / DQK, np.float32))
    full = np.full((LANES, LANES), 1.0 / LANES, np.float32)
    for g in range(W_IN_COLS // LANES):
        if COL_DQ <= g < COL_DV:
            amats[g] = half
        elif COL_NQ <= g < COL_NV:
            amats[g] = full
        else:
            continue
        flags[0, g * LANES:(g + 1) * LANES] = 1.0
    return jnp.asarray(amats, BF16), jnp.asarray(flags)


def _rope_tables():
    half = DQK // 2
    inv = 1.0 / (ROPE_BASE ** (np.arange(0, half, 2, dtype=np.float32) / half))
    t = np.arange(DEC_SEQ)
    ang_r = (t // GRID_W).astype(np.float32)[:, None] * inv
    ang_c = (t % GRID_W).astype(np.float32)[:, None] * inv
    ang = np.concatenate([ang_r, ang_r, ang_c, ang_c], axis=-1).astype(np.float32)
    ang = np.concatenate([ang, ang], axis=-1)
    first = (np.arange(LANES) % 32) < 16
    cos, sin = np.cos(ang), np.sin(ang)
    return (jnp.asarray(cos, F32), jnp.asarray(np.where(first, -sin, 0.0), F32),
            jnp.asarray(np.where(first, 0.0, sin), F32))


def _na_mask():
    qc = np.arange(GRID_W)[:, None]
    kc = (np.arange(LANES) % GRID_W)[None, :]
    ws = np.clip(qc - NA_KC // 2, 0, GRID_W - NA_KC)
    ok = (kc >= ws) & (kc < ws + NA_KC)
    return jnp.asarray(np.where(ok, 0.0, NEG_BIG), F32)


def kernel(x_prompt, x_sample, cache_diff_k, cache_diff_v, cache_na_k, cache_na_v, c, c_ctx, norm_mix_g, norm_ffn_g, w_ada, b_ada, w_in, w_out, conv_w, conv_b, conv_ln_g, conv_ln_b, diff_qn_g, diff_kn_g, diff_lam_q1, diff_lam_k1, diff_lam_q2, diff_lam_k2, diff_subln_g, na_qn_g, na_kn_g, na_rpb, moe_wr_g, moe_br_g, moe_wr_e, moe_br_e, moe_w_gate, moe_w_up, moe_w_down):
    cvec = jnp.zeros((MOD_ROWS, D_MODEL), F32).at[0].set(c_ctx).at[1:1 + DEC_BATCH].set(c)
    ones = lambda n: jnp.ones((DEPTH, n), F32)
    gains = jnp.concatenate(
        [ones(2 * C_CONV), jnp.tile(diff_qn_g, (1, 2 * N_HEADS)), jnp.tile(diff_kn_g, (1, 2 * N_HEADS)),
         ones(N_HEADS * HEAD_DIM), jnp.tile(na_qn_g, (1, N_HEADS)), jnp.tile(na_kn_g, (1, N_HEADS)),
         ones(N_HEADS * HEAD_DIM)], axis=1).reshape(DEPTH, 1, W_IN_COLS)
    amats, flags = _norm_tables()
    rope_tabs = _rope_tables()
    na_mask = _na_mask()
    lam_params = jnp.stack([diff_lam_q1, diff_lam_k1, diff_lam_q2, diff_lam_k2], axis=1)
    rpb_pad = jnp.pad(na_rpb, ((0, 0), (0, 0), (0, 0), (0, DQK - na_rpb.shape[-1])))
    rpb_src = jnp.concatenate([rpb_pad[:, :, :-1], rpb_pad[:, :, 1:]], axis=-1)
    rpb_src = jnp.pad(rpb_src, ((0, 0), (0, 0), (0, 16 - NA_PAIRS), (0, 0)))
    wr_t = jnp.concatenate([moe_wr_g, moe_wr_e.reshape(DEPTH, D_MODEL, N_EXPERTS)], axis=2)
    wr_t = jnp.pad(jnp.swapaxes(wr_t, 1, 2), ((0, 0), (0, ROUTE_ROWS - N_GROUPS - N_EXPERTS), (0, 0)))
    br = jnp.concatenate([moe_br_g, moe_br_e.reshape(DEPTH, N_EXPERTS)], axis=1)
    br = jnp.pad(br, ((0, 0), (0, ROUTE_ROWS - N_GROUPS - N_EXPERTS))).reshape(DEPTH, ROUTE_ROWS, 1)
    tri = jnp.asarray(np.triu(np.ones((CUM_BLK, CUM_BLK), np.float32), 1), BF16)
    norm_mix = norm_mix_g.reshape(DEPTH, 1, D_MODEL)
    norm_ffn = norm_ffn_g.reshape(DEPTH, 1, D_MODEL)
    ck_diff = cache_diff_k.reshape(DEC_BATCH, DEPTH, PAST_LEN, N_HEADS * HEAD_DIM)
    cv_diff = cache_diff_v.reshape(DEC_BATCH, DEPTH, PAST_LEN, N_HEADS * HEAD_DIM)
    ck_na = cache_na_k.reshape(DEC_BATCH, DEPTH, PAST_LEN, N_HEADS * HEAD_DIM)
    cv_na = cache_na_v.reshape(DEC_BATCH, DEPTH, PAST_LEN, N_HEADS * HEAD_DIM)

    mod8 = _ada_modulation(cvec, w_ada, b_ada).reshape(DEPTH * MOD_ROWS, 1, N_ADA)
    w_out_bf = _cast_bf16(w_out)

    ctx_row = lambda i: 0
    lat_row_proj = lambda i: 1 + i * PROJ_TM // DEC_SEQ
    lat_row_mix = lambda i: 1 + i * MIX_TM // DEC_SEQ

    xc = x_prompt.reshape(T_CTX, D_MODEL)
    xl = x_sample.reshape(T_LAT, D_MODEL)
    caches = [[], [], [], []]
    for layer in range(DEPTH):
        pc = _project(xc, mod8, norm_mix, w_in, amats, gains, flags, None, layer, ctx_row)
        plat = _project(xl, mod8, norm_mix, w_in, amats, gains, flags, rope_tabs, layer, lat_row_proj)
        for slot, col in zip(caches, (COL_DK, COL_DV, COL_NK, COL_NV)):
            slot.append(pc[:, col * LANES:(col + N_HEADS) * LANES].reshape(BATCH, SEQ, N_HEADS, HEAD_DIM))

        ac = _conformer_conv(pc, conv_w, conv_b, conv_ln_g, conv_ln_b, layer, BATCH, SEQ)
        al = _conformer_conv(plat, conv_w, conv_b, conv_ln_g, conv_ln_b, layer, DEC_BATCH, DEC_SEQ)
        dc = _diff_attention(pc, lam_params, diff_subln_g, None, None, layer, BATCH, SEQ, SEQ)
        dl = _diff_attention(plat, lam_params, diff_subln_g, ck_diff, cv_diff, layer, DEC_BATCH, DEC_SEQ, 256)
        nc = _ctx_attention(pc)
        nl = _na_attention(plat, rpb_src, na_mask, ck_na, cv_na, layer)

        x1c, ltc = _mix(ac, dc, nc, w_out_bf, xc, mod8, norm_ffn, wr_t, br, layer, ctx_row)
        x1l, ltl = _mix(al, dl, nl, w_out_bf, xl, mod8, norm_ffn, wr_t, br, layer, lat_row_mix)

        pos, wts, items = _route(jnp.concatenate([ltc, ltl], axis=1), tri)
        pos_flat = pos.reshape(2 * T_ALL)
        items = [items[r, :MOE_ITEMS] for r in range(4)] + [items[4, :1]]
        xs = _dispatch(pos_flat, x1c, x1l, mod8, norm_ffn, layer)
        ys = _experts(items, xs, moe_w_gate, moe_w_up, moe_w_down, layer)
        xc, xl = _combine(pos_flat, x1c, x1l, wts.T, mod8, ys, layer)

    outs = [jnp.stack(slot, axis=1) for slot in caches]
    return (xc.reshape(BATCH, SEQ, D_MODEL), xl.reshape(DEC_BATCH, DEC_SEQ, D_MODEL), *outs)
```

```python
import functools
import math

import numpy as np
import jax
import jax.numpy as jnp
from jax import lax
from jax.experimental import pallas as pl
from jax.experimental.pallas import tpu as pltpu

F32 = jnp.float32
BF16 = jnp.bfloat16
I32 = jnp.int32

D_MODEL = 2048
BATCH = 16
SEQ = 256
DEPTH = 2
DEC_BATCH = 2
DEC_SEQ = 1024
PAST_LEN = 256
GRID_W = 64
GRID_ROWS = DEC_SEQ // GRID_W
HEAD_DIM = 128
C_CONV = 512
CONV_K = 31
N_HEADS = 6
DQK = 64
NA_KR = 8
NA_KC = 16
ROPE_BASE = 10000.0
N_GROUPS = 4
E_PER_GROUP = 4
N_EXPERTS = 16
D_FF = 512
N_ADA = 6 * D_MODEL
W_IN_COLS = 2 * C_CONV + 6 * N_HEADS * HEAD_DIM
T_CTX = BATCH * SEQ
T_LAT = DEC_BATCH * DEC_SEQ
T_ALL = T_CTX + T_LAT

LANES = 128
MOD_ROWS = 8
PRE_TM = 512
HEADS_TM = 1024
MIX_TM = 512
MOE_TM = 256
MOE_ITEMS = 2 * T_ALL // MOE_TM + N_EXPERTS
ROUTE_ROWS = 32
CUM_BLK = 512
VMEM_LIMIT = 56 * 1024 * 1024
NEG_BIG = -1e30


def _cparams(sem):
    return pltpu.CompilerParams(dimension_semantics=sem, vmem_limit_bytes=VMEM_LIMIT)


def _sigmoid(x):
    return 1.0 / (1.0 + jnp.exp(-x))


def _silu(x):
    return x * _sigmoid(x)


def _lambda_init(layer):
    return 0.8 - 0.6 * math.exp(-0.3 * layer)


def _ada_kernel(c_ref, w_ref, b_ref, o_ref):
    c = c_ref[...]
    s = _silu(c).astype(BF16)
    o_ref[...] = jnp.dot(s, w_ref[...].astype(BF16), preferred_element_type=F32) + b_ref[...]


def _ada_modulation(cvec, w_ada, b_ada):
    tn = 1024
    return pl.pallas_call(
        _ada_kernel,
        grid=(DEPTH, N_ADA // tn),
        in_specs=[
            pl.BlockSpec((MOD_ROWS, D_MODEL), lambda l, j: (0, 0)),
            pl.BlockSpec((None, D_MODEL, tn), lambda l, j: (l, 0, j)),
            pl.BlockSpec((None, 1, tn), lambda l, j: (l, 0, j)),
        ],
        out_specs=pl.BlockSpec((None, MOD_ROWS, tn), lambda l, j: (l, 0, j)),
        out_shape=jax.ShapeDtypeStruct((DEPTH, MOD_ROWS, N_ADA), F32),
        compiler_params=_cparams(("arbitrary", "arbitrary")),
        name="ada_modulation",
    )(cvec, w_ada, b_ada.reshape(DEPTH, 1, N_ADA))


def _mod_spec(layer, chunk, row_of_block):
    return pl.BlockSpec((None, 1, D_MODEL),
                        lambda i, *_: (layer * MOD_ROWS + row_of_block(i), 0, chunk))


def _modulate(x, g, shift, scale):
    ms = jnp.mean(x * x, axis=-1, keepdims=True)
    return x * lax.rsqrt(ms + 1e-6) * (g * (1.0 + scale)) + shift


def _pre_kernel(x_ref, sh_ref, sc_ref, g_ref, w_ref, h_ref, u_ref, wbf_ref):
    @pl.when(pl.program_id(0) == 0)
    def _():
        wbf_ref[...] = w_ref[...].astype(BF16)

    h = _modulate(x_ref[...], g_ref[...], sh_ref[...], sc_ref[...]).astype(BF16)
    h_ref[...] = h
    y = jnp.dot(h, wbf_ref[...], preferred_element_type=F32)
    u_ref[...] = y[:, :C_CONV] * _sigmoid(y[:, C_CONV:])


def _pre_project(x, mod8, norm_g, w_in, layer, row_of_block, name):
    t = x.shape[0]
    return pl.pallas_call(
        _pre_kernel,
        grid=(t // PRE_TM,),
        in_specs=[
            pl.BlockSpec((PRE_TM, D_MODEL), lambda i: (i, 0)),
            _mod_spec(layer, 0, row_of_block),
            _mod_spec(layer, 1, row_of_block),
            pl.BlockSpec((None, 1, D_MODEL), lambda i: (layer, 0, 0)),
            pl.BlockSpec((None, D_MODEL, 2 * C_CONV), lambda i: (layer, 0, 0)),
        ],
        out_specs=[pl.BlockSpec((PRE_TM, D_MODEL), lambda i: (i, 0)),
                   pl.BlockSpec((PRE_TM, C_CONV), lambda i: (i, 0))],
        out_shape=[jax.ShapeDtypeStruct((t, D_MODEL), BF16), jax.ShapeDtypeStruct((t, C_CONV), F32)],
        scratch_shapes=[pltpu.VMEM((D_MODEL, 2 * C_CONV), BF16)],
        compiler_params=_cparams(("arbitrary",)),
        name=name,
    )(x, mod8, mod8, norm_g, w_in)


SEG_DQ, SEG_DK, SEG_DV, SEG_NQ, SEG_NK, SEG_NV = range(6)
SEG_W = N_HEADS * HEAD_DIM
HEADS_WBLK = 256


def _heads_kernel(h_ref, w0_ref, w1_ref, w2_ref, gain_ref, a64_ref, a128_ref, *rest, rope, per_batch):
    if rope:
        cos_ref, sa_ref, sb_ref, o_ref = rest
    else:
        (o_ref,) = rest
    j = pl.program_id(0)

    def finish(seg):
        for c, w_ref in enumerate((w0_ref, w1_ref, w2_ref)):
            y = jnp.dot(h_ref[...], w_ref[...].astype(BF16), preferred_element_type=F32)
            for g in range(HEADS_WBLK // LANES):
                hd = c * (HEADS_WBLK // LANES) + g
                yg = y[:, g * LANES:(g + 1) * LANES]
                if seg in (SEG_DQ, SEG_DK, SEG_NQ, SEG_NK):
                    a_ref = a64_ref if seg in (SEG_DQ, SEG_DK) else a128_ref
                    ms = jnp.dot((yg * yg).astype(BF16), a_ref[...], preferred_element_type=F32)
                    yg = yg * lax.rsqrt(ms + 1e-6) * gain_ref[:, hd * LANES:(hd + 1) * LANES]
                if rope and seg in (SEG_DQ, SEG_DK):
                    yg = (yg * cos_ref[...] + pltpu.roll(yg, LANES - 16, 1) * sa_ref[...]
                          + pltpu.roll(yg, 16, 1) * sb_ref[...])
                if per_batch == 1:
                    o_ref[hd] = yg
                else:
                    rows = yg.shape[0] // per_batch
                    for bb in range(per_batch):
                        o_ref[bb, hd] = yg[bb * rows:(bb + 1) * rows]

    for seg in range(6):
        pl.when(j == seg)(functools.partial(finish, seg))


def _head_project(h, w_in, gains, a64, a128, rope_tabs, layer, batch, seq, name):
    t = h.shape[0]
    rope = rope_tabs is not None
    per_batch = HEADS_TM // seq
    col0 = 2 * C_CONV // HEADS_WBLK
    wspec = lambda c: pl.BlockSpec((None, D_MODEL, HEADS_WBLK),
                                   lambda j, i: (layer, 0, col0 + (SEG_W // HEADS_WBLK) * j + c))
    in_specs = [
        pl.BlockSpec((HEADS_TM, D_MODEL), lambda j, i: (i, 0)),
        wspec(0), wspec(1), wspec(2),
        pl.BlockSpec((None, None, 1, SEG_W), lambda j, i: (layer, j, 0, 0)),
        pl.BlockSpec((LANES, LANES), lambda j, i: (0, 0)),
        pl.BlockSpec((LANES, LANES), lambda j, i: (0, 0)),
    ]
    args = [h, w_in, w_in, w_in, gains, a64, a128]
    if rope:
        in_specs += [pl.BlockSpec((DEC_SEQ, LANES), lambda j, i: (0, 0))] * 3
        args += list(rope_tabs)
    if per_batch == 1:
        out_spec = pl.BlockSpec((None, None, N_HEADS, seq, HEAD_DIM), lambda j, i: (j, i, 0, 0, 0))
    else:
        out_spec = pl.BlockSpec((None, per_batch, N_HEADS, seq, HEAD_DIM), lambda j, i: (j, i, 0, 0, 0))
    return pl.pallas_call(
        functools.partial(_heads_kernel, rope=rope, per_batch=per_batch),
        grid=(6, t // HEADS_TM),
        in_specs=in_specs,
        out_specs=out_spec,
        out_shape=jax.ShapeDtypeStruct((6, batch, N_HEADS, seq, HEAD_DIM), F32),
        compiler_params=_cparams(("arbitrary", "arbitrary")),
        name=name,
    )(*args)


CONV_CH = 128
CONV_HALO = 16


def _conv_kernel(u_ref, w_ref, b_ref, lg_ref, lb_ref, o_ref, win_ref, y_ref, *, seq):
    c = pl.program_id(1)
    n_chunks = seq // CONV_CH
    base = pl.multiple_of(c * CONV_CH, CONV_CH)

    win_ref[CONV_HALO:CONV_HALO + CONV_CH, :] = u_ref[pl.ds(base, CONV_CH), :]
    lo_start = pl.multiple_of(jnp.maximum(base - CONV_HALO, 0), CONV_HALO)
    hi_start = pl.multiple_of(jnp.minimum(base + CONV_CH, seq - CONV_HALO), CONV_HALO)
    lo_keep = (c > 0).astype(F32)
    hi_keep = (c < n_chunks - 1).astype(F32)
    win_ref[0:CONV_HALO, :] = u_ref[pl.ds(lo_start, CONV_HALO), :] * lo_keep
    win_ref[CONV_HALO + CONV_CH:, :] = u_ref[pl.ds(hi_start, CONV_HALO), :] * hi_keep

    off = CONV_HALO - CONV_K // 2
    for g in range(C_CONV // LANES):
        sl = slice(g * LANES, (g + 1) * LANES)
        acc = jnp.zeros((CONV_CH, LANES), F32) + b_ref[:, sl]
        for k in range(CONV_K):
            acc = acc + win_ref[off + k:off + k + CONV_CH, sl] * w_ref[k:k + 1, sl]
        y_ref[:, sl] = acc

    y = y_ref[...]
    mu = jnp.mean(y, axis=-1, keepdims=True)
    yc = y - mu
    var = jnp.mean(yc * yc, axis=-1, keepdims=True)
    z = yc * lax.rsqrt(var + 1e-5) * lg_ref[...] + lb_ref[...]
    o_ref[...] = _silu(z)


def _conformer_conv(u, conv_w, conv_b, ln_g, ln_b, layer, batch, seq):
    t = u.shape[0]
    n_chunks = seq // CONV_CH
    vec = lambda: pl.BlockSpec((None, 1, C_CONV), lambda b, c: (layer, 0, 0))
    return pl.pallas_call(
        functools.partial(_conv_kernel, seq=seq),
        grid=(batch, n_chunks),
        in_specs=[
            pl.BlockSpec((seq, C_CONV), lambda b, c: (b, 0)),
            pl.BlockSpec((None, CONV_K, C_CONV), lambda b, c: (layer, 0, 0)),
            vec(), vec(), vec(),
        ],
        out_specs=pl.BlockSpec((CONV_CH, C_CONV), lambda b, c: (b * n_chunks + c, 0)),
        out_shape=jax.ShapeDtypeStruct((t, C_CONV), F32),
        scratch_shapes=[pltpu.VMEM((CONV_CH + 2 * CONV_HALO, C_CONV), F32),
                        pltpu.VMEM((CONV_CH, C_CONV), F32)],
        compiler_params=_cparams(("arbitrary", "arbitrary")),
        name="conv_%d" % seq,
    )(u, conv_w, conv_b.reshape(DEPTH, 1, C_CONV), ln_g.reshape(DEPTH, 1, C_CONV),
      ln_b.reshape(DEPTH, 1, C_CONV))


_NT = (((1,), (1,)), ((), ()))


def _softmax_rows(s):
    m = jnp.max(s, axis=-1, keepdims=True)
    e = jnp.exp(s - m)
    return e / jnp.sum(e, axis=-1, keepdims=True)


def _diff_lambda(lam_ref, lam_init):
    lv = lam_ref[...]
    return (jnp.exp(jnp.sum(lv[0:1] * lv[1:2], axis=-1, keepdims=True))
            - jnp.exp(jnp.sum(lv[2:3] * lv[3:4], axis=-1, keepdims=True)) + lam_init)


def _diff_head(q, k, v, lam, g, lam_init):
    lane = lax.broadcasted_iota(I32, q.shape, 1)
    q0 = jnp.where(lane < DQK, q, 0.0).astype(BF16)
    q1 = jnp.where(lane >= DQK, q, 0.0).astype(BF16)
    scale = DQK ** -0.5
    p0 = _softmax_rows(lax.dot_general(q0, k, _NT, preferred_element_type=F32) * scale)
    p1 = _softmax_rows(lax.dot_general(q1, k, _NT, preferred_element_type=F32) * scale)
    a = (p0 - lam * p1).astype(BF16)
    o = jnp.dot(a, v, preferred_element_type=F32)
    ms = jnp.mean(o * o, axis=-1, keepdims=True)
    return o * lax.rsqrt(ms + 1e-5) * g * (1.0 - lam_init)


def _seg_spec(seg, rows, index):
    def imap(*ids):
        b, h, r = index(*ids)
        return (seg, b, h, r, 0)
    return pl.BlockSpec((None, None, None, rows, HEAD_DIM), imap)


def _diff_attn_lat_kernel(lam_ref, g_ref, q_ref, k_ref, v_ref, ck_ref, cv_ref, o_ref, *, lam_init):
    k = jnp.concatenate([k_ref[...], ck_ref[...]], axis=0).astype(BF16)
    v = jnp.concatenate([v_ref[...], cv_ref[...]], axis=0).astype(BF16)
    lam = _diff_lambda(lam_ref, lam_init)
    o_ref[...] = _diff_head(q_ref[...], k, v, lam, g_ref[...], lam_init)


def _diff_attention_lat(segs, lam_params, subln_g, cache_k, cache_v, layer, tq):
    nq = DEC_SEQ // tq
    cspec = pl.BlockSpec((None, None, None, PAST_LEN, HEAD_DIM), lambda b, h, qi: (b, layer, h, 0, 0))
    return pl.pallas_call(
        functools.partial(_diff_attn_lat_kernel, lam_init=_lambda_init(layer)),
        grid=(DEC_BATCH, N_HEADS, nq),
        in_specs=[
            pl.BlockSpec((None, 4, DQK), lambda b, h, qi: (layer, 0, 0)),
            pl.BlockSpec((None, 1, HEAD_DIM), lambda b, h, qi: (layer, 0, 0)),
            _seg_spec(SEG_DQ, tq, lambda b, h, qi: (b, h, qi)),
            _seg_spec(SEG_DK, DEC_SEQ, lambda b, h, qi: (b, h, 0)),
            _seg_spec(SEG_DV, DEC_SEQ, lambda b, h, qi: (b, h, 0)),
            cspec, cspec,
        ],
        out_specs=pl.BlockSpec((tq, HEAD_DIM), lambda b, h, qi: (b * nq + qi, h)),
        out_shape=jax.ShapeDtypeStruct((T_LAT, SEG_W), F32),
        compiler_params=_cparams(("arbitrary", "arbitrary", "arbitrary")),
        name="diff_attn_lat",
    )(lam_params, subln_g.reshape(DEPTH, 1, HEAD_DIM), segs, segs, segs, cache_k, cache_v)


def _ctx_attn_kernel(lam_ref, g_ref, dq_ref, dk_ref, dv_ref, nq_ref, nk_ref, nv_ref, d_ref, n_ref, *,
                     lam_init):
    lam = _diff_lambda(lam_ref, lam_init)
    for h in range(N_HEADS):
        sl = slice(h * HEAD_DIM, (h + 1) * HEAD_DIM)
        d_ref[:, sl] = _diff_head(dq_ref[h], dk_ref[h].astype(BF16), dv_ref[h].astype(BF16), lam,
                                  g_ref[...], lam_init)
        s = lax.dot_general(nq_ref[h].astype(BF16), nk_ref[h].astype(BF16), _NT,
                            preferred_element_type=F32) * (HEAD_DIM ** -0.5)
        n_ref[:, sl] = jnp.dot(_softmax_rows(s).astype(BF16), nv_ref[h].astype(BF16),
                               preferred_element_type=F32)


def _ctx_attention(segs, lam_params, subln_g, layer):
    seg = lambda s: pl.BlockSpec((None, None, N_HEADS, SEQ, HEAD_DIM), lambda b: (s, b, 0, 0, 0))
    out = pl.BlockSpec((SEQ, SEG_W), lambda b: (b, 0))
    return pl.pallas_call(
        functools.partial(_ctx_attn_kernel, lam_init=_lambda_init(layer)),
        grid=(BATCH,),
        in_specs=[pl.BlockSpec((None, 4, DQK), lambda b: (layer, 0, 0)),
                  pl.BlockSpec((None, 1, HEAD_DIM), lambda b: (layer, 0, 0))]
                 + [seg(s) for s in range(6)],
        out_specs=[out, out],
        out_shape=[jax.ShapeDtypeStruct((T_CTX, SEG_W), F32)] * 2,
        compiler_params=_cparams(("arbitrary",)),
        name="ctx_attn",
    )(lam_params, subln_g.reshape(DEPTH, 1, HEAD_DIM), *([segs] * 6))


NA_PAIRS = 2 * NA_KR - 2
NA_LOCAL = NA_KR * GRID_W


def _na_attn_kernel(src_ref, mask_ref, q_ref, k_ref, v_ref, ck_ref, cv_ref, o_ref, bias_ref):
    for d in range(NA_PAIRS):
        row = jnp.broadcast_to(src_ref[d:d + 1, :], (GRID_W, LANES))
        bias_ref[d] = pltpu.roll(row, LANES - (NA_KC - 1), 1, stride=1, stride_axis=0) + mask_ref[...]
    ck = ck_ref[...].astype(BF16)
    cv = cv_ref[...].astype(BF16)
    scale = HEAD_DIM ** -0.5
    for qr in range(GRID_ROWS):
        start = min(max(qr - NA_KR // 2, 0), GRID_ROWS - NA_KR)
        d0 = start - qr + NA_KR - 1
        q = q_ref[qr * GRID_W:(qr + 1) * GRID_W, :].astype(BF16)
        kl = k_ref[start * GRID_W:start * GRID_W + NA_LOCAL, :].astype(BF16)
        vl = v_ref[start * GRID_W:start * GRID_W + NA_LOCAL, :].astype(BF16)
        bias = jnp.concatenate([bias_ref[d0 + 2 * i] for i in range(NA_KR // 2)], axis=1)
        s_loc = lax.dot_general(q, kl, _NT, preferred_element_type=F32) * scale + bias
        s_ctx = lax.dot_general(q, ck, _NT, preferred_element_type=F32) * scale
        m = jnp.maximum(jnp.max(s_loc, axis=-1, keepdims=True), jnp.max(s_ctx, axis=-1, keepdims=True))
        e_loc = jnp.exp(s_loc - m)
        e_ctx = jnp.exp(s_ctx - m)
        den = jnp.sum(e_loc, axis=-1, keepdims=True) + jnp.sum(e_ctx, axis=-1, keepdims=True)
        p_loc = (e_loc / den).astype(BF16)
        p_ctx = (e_ctx / den).astype(BF16)
        o_ref[qr * GRID_W:(qr + 1) * GRID_W, :] = (
            jnp.dot(p_loc, vl, preferred_element_type=F32) + jnp.dot(p_ctx, cv, preferred_element_type=F32))


def _na_attention(segs, rpb_src, na_mask, cache_k, cache_v, layer):
    blk = lambda s: _seg_spec(s, DEC_SEQ, lambda b, h: (b, h, 0))
    cspec = pl.BlockSpec((None, None, None, PAST_LEN, HEAD_DIM), lambda b, h: (b, layer, h, 0, 0))
    return pl.pallas_call(
        _na_attn_kernel,
        grid=(DEC_BATCH, N_HEADS),
        in_specs=[
            pl.BlockSpec((None, None, 16, LANES), lambda b, h: (layer, h, 0, 0)),
            pl.BlockSpec((GRID_W, LANES), lambda b, h: (0, 0)),
            blk(SEG_NQ), blk(SEG_NK), blk(SEG_NV), cspec, cspec,
        ],
        out_specs=pl.BlockSpec((DEC_SEQ, HEAD_DIM), lambda b, h: (b, h)),
        out_shape=jax.ShapeDtypeStruct((T_LAT, SEG_W), F32),
        scratch_shapes=[pltpu.VMEM((NA_PAIRS, GRID_W, LANES), F32)],
        compiler_params=_cparams(("arbitrary", "arbitrary")),
        name="na_attn_lat",
    )(rpb_src, na_mask, segs, segs, segs, cache_k, cache_v)


def _split3(x):
    hi = x.astype(BF16)
    r1 = x - hi.astype(F32)
    mid = r1.astype(BF16)
    lo = (r1 - mid.astype(F32)).astype(BF16)
    return hi, mid, lo


def _mix_kernel(a_ref, d_ref, n_ref, w_ref, x_ref, g1_ref, sh_ref, sc_ref, ng_ref, wr_ref, br_ref,
                x1_ref, lt_ref):
    k0, k1 = C_CONV, C_CONV + N_HEADS * HEAD_DIM
    mixed = (jnp.dot(a_ref[...].astype(BF16), w_ref[0:k0, :], preferred_element_type=F32)
             + jnp.dot(d_ref[...].astype(BF16), w_ref[k0:k1, :], preferred_element_type=F32)
             + jnp.dot(n_ref[...].astype(BF16), w_ref[k1:, :], preferred_element_type=F32))
    x1 = x_ref[...] + g1_ref[...] * mixed
    x1_ref[...] = x1
    h2 = _modulate(x1, ng_ref[...], sh_ref[...], sc_ref[...])
    hh, hm, hl = _split3(h2)
    wh, wm, wl = _split3(wr_ref[...])
    dot = lambda a, b: lax.dot_general(a, b, _NT, preferred_element_type=F32)
    lt = (dot(wh, hh) + (dot(wh, hm) + dot(wm, hh)) + (dot(wm, hm) + dot(wh, hl) + dot(wl, hh)))
    lt_ref[...] = lt + br_ref[...]


def _mix(a_out, d_out, n_out, w_out_bf, x, mod8, norm_g, wr_t, br, layer, row_of_block):
    t = x.shape[0]
    rows = lambda width: pl.BlockSpec((MIX_TM, width), lambda i: (i, 0))
    return pl.pallas_call(
        _mix_kernel,
        grid=(t // MIX_TM,),
        in_specs=[
            rows(C_CONV), rows(N_HEADS * HEAD_DIM), rows(N_HEADS * HEAD_DIM),
            pl.BlockSpec((None, D_MODEL, D_MODEL), lambda i: (layer, 0, 0)),
            rows(D_MODEL),
            _mod_spec(layer, 2, row_of_block),
            _mod_spec(layer, 3, row_of_block),
            _mod_spec(layer, 4, row_of_block),
            pl.BlockSpec((None, 1, D_MODEL), lambda i: (layer, 0, 0)),
            pl.BlockSpec((None, ROUTE_ROWS, D_MODEL), lambda i: (layer, 0, 0)),
            pl.BlockSpec((None, ROUTE_ROWS, 1), lambda i: (layer, 0, 0)),
        ],
        out_specs=[rows(D_MODEL), pl.BlockSpec((ROUTE_ROWS, MIX_TM), lambda i: (0, i))],
        out_shape=[jax.ShapeDtypeStruct((t, D_MODEL), F32),
                   jax.ShapeDtypeStruct((ROUTE_ROWS, t), F32)],
        compiler_params=_cparams(("arbitrary",)),
        name="mix_%d" % t,
    )(a_out, d_out, n_out, w_out_bf, x, mod8, mod8, mod8, norm_g, wr_t, br)


def _cast_kernel(x_ref, o_ref):
    o_ref[...] = x_ref[...].astype(BF16)


def _cast_bf16(w):
    tm = 512
    spec = pl.BlockSpec((None, tm, D_MODEL), lambda l, i: (l, i, 0))
    return pl.pallas_call(
        _cast_kernel, grid=(DEPTH, D_MODEL // tm), in_specs=[spec], out_specs=spec,
        out_shape=jax.ShapeDtypeStruct(w.shape, BF16),
        compiler_params=_cparams(("arbitrary", "arbitrary")), name="cast_w_out",
    )(w)


def _route_kernel(lt_ref, tri_ref, pos_ref, wt_ref, item_ref):
    t = T_ALL
    lg = lt_ref[0:N_GROUPS, :]
    eg = jnp.exp(lg - jnp.max(lg, axis=0, keepdims=True))
    pg = eg / jnp.sum(eg, axis=0, keepdims=True)
    pg_top = jnp.max(pg, axis=0, keepdims=True)
    gi = lax.broadcasted_iota(I32, pg.shape, 0).astype(F32)
    g_idx = jnp.min(jnp.where(pg == pg_top, gi, float(N_GROUPS)), axis=0, keepdims=True)

    le = jnp.zeros((E_PER_GROUP, t), F32)
    for g in range(N_GROUPS):
        rows = lt_ref[N_GROUPS + g * E_PER_GROUP:N_GROUPS + (g + 1) * E_PER_GROUP, :]
        le = jnp.where(g_idx == float(g), rows, le)
    ee = jnp.exp(le - jnp.max(le, axis=0, keepdims=True))
    pe = ee / jnp.sum(ee, axis=0, keepdims=True)
    ei = lax.broadcasted_iota(I32, pe.shape, 0).astype(F32)
    p1 = jnp.max(pe, axis=0, keepdims=True)
    i1 = jnp.min(jnp.where(pe == p1, ei, float(E_PER_GROUP)), axis=0, keepdims=True)
    pe_rest = jnp.where(ei == i1, -1.0, pe)
    p2 = jnp.max(pe_rest, axis=0, keepdims=True)
    i2 = jnp.min(jnp.where(pe_rest == p2, ei, float(E_PER_GROUP)), axis=0, keepdims=True)
    den = p1 + p2
    wt_ref[0:1, :] = pg_top * (p1 / den)
    wt_ref[1:2, :] = pg_top * (p2 / den)
    e1 = g_idx * E_PER_GROUP + i1
    e2 = g_idx * E_PER_GROUP + i2

    erow = lax.broadcasted_iota(I32, (N_EXPERTS, t), 0).astype(F32)
    oh1 = (erow == e1).astype(F32)
    oh2 = (erow == e2).astype(F32)
    cnt = oh1 + oh2
    carry = jnp.zeros((N_EXPERTS, 1), F32)
    ranks = []
    for b in range(t // CUM_BLK):
        blk = cnt[:, b * CUM_BLK:(b + 1) * CUM_BLK]
        ranks.append(jnp.dot(blk.astype(BF16), tri_ref[...], preferred_element_type=F32) + carry)
        carry = carry + jnp.sum(blk, axis=1, keepdims=True)
    rank = jnp.concatenate(ranks, axis=1)

    erow_l = lax.broadcasted_iota(I32, (N_EXPERTS, LANES), 0)

    def excl_scan(v):
        inc = v
        for s in (1, 2, 4, 8):
            inc = inc + jnp.where(erow_l >= s, pltpu.roll(inc, s, 0), 0.0)
        return inc - v

    total = jnp.broadcast_to(carry, (N_EXPERTS, LANES))
    offs = excl_scan(total)
    pos_ref[0:1, :] = jnp.sum(oh1 * (rank + offs[:, 0:1]), axis=0, keepdims=True).astype(I32)
    pos_ref[1:2, :] = jnp.sum(oh2 * (rank + offs[:, 0:1]), axis=0, keepdims=True).astype(I32)

    shift = int(math.log2(MOE_TM))
    offs_i = offs.astype(I32)
    total_i = total.astype(I32)
    first_tile = lax.shift_right_logical(offs_i, shift)
    last_tile = lax.shift_right_logical(offs_i + total_i - 1, shift)
    n_item = jnp.where(total_i > 0, last_tile - first_tile + 1, 0).astype(F32)
    item_start = excl_scan(n_item)
    item_end = item_start + n_item
    kk = lax.broadcasted_iota(I32, (N_EXPERTS, LANES), 1).astype(F32)
    item_e = jnp.minimum(jnp.sum((item_end <= kk).astype(F32), axis=0, keepdims=True),
                         float(N_EXPERTS - 1))
    sel = (erow_l.astype(F32) == item_e).astype(F32)
    pick = lambda v: jnp.sum(sel * v, axis=0, keepdims=True)
    item_ref[0:1, :] = (kk[0:1] + pick(first_tile.astype(F32) - item_start)).astype(I32)
    item_ref[1:2, :] = item_e.astype(I32)
    item_ref[2:3, :] = pick(offs).astype(I32)
    item_ref[3:4, :] = pick(offs + total).astype(I32)
    item_ref[4:5, :] = item_end[N_EXPERTS - 1:N_EXPERTS, :].astype(I32)
    item_ref[5:8, :] = jnp.zeros((3, LANES), I32)


def _route(lt_all, tri):
    full = lambda shape: pl.BlockSpec(shape, lambda i: (0,) * len(shape))
    return pl.pallas_call(
        _route_kernel,
        grid=(1,),
        in_specs=[full((ROUTE_ROWS, T_ALL)), full((CUM_BLK, CUM_BLK))],
        out_specs=[full((2, T_ALL)), full((2, T_ALL)), full((8, LANES))],
        out_shape=[jax.ShapeDtypeStruct((2, T_ALL), I32), jax.ShapeDtypeStruct((2, T_ALL), F32),
                   jax.ShapeDtypeStruct((8, LANES), I32)],
        compiler_params=_cparams(("arbitrary",)),
        name="moe_route",
    )(lt_all, tri)


N_TOK_TILES = T_ALL // MOE_TM
CTX_TILES = T_CTX // MOE_TM


def _tok_row(i):
    return jnp.where(i < CTX_TILES, 0, 1 + (i - CTX_TILES) // (DEC_SEQ // MOE_TM))


def _row_copy(src, src_row, dst, dst_row, sem):
    return pltpu.make_async_copy(src.at[pl.ds(src_row, 1)], dst.at[pl.ds(dst_row, 1)], sem)


def _dispatch_kernel(pos_ref, xc_ref, xl_ref, sh_ref, sc_ref, ng_ref, xs_ref, h_ref, sem):
    i = pl.program_id(0)

    @pl.when(i < CTX_TILES)
    def _():
        h_ref[...] = _modulate(xc_ref[...], ng_ref[...], sh_ref[...], sc_ref[...])

    @pl.when(i >= CTX_TILES)
    def _():
        h_ref[...] = _modulate(xl_ref[...], ng_ref[...], sh_ref[...], sc_ref[...])

    def issue(r, carry):
        tok = i * MOE_TM + r
        _row_copy(h_ref, r, xs_ref, pos_ref[tok], sem).start()
        _row_copy(h_ref, r, xs_ref, pos_ref[T_ALL + tok], sem).start()
        return carry

    lax.fori_loop(0, MOE_TM, issue, 0)

    def drain(r, carry):
        _row_copy(h_ref, 0, xs_ref, 0, sem).wait()
        return carry

    lax.fori_loop(0, 2 * MOE_TM, drain, 0)


def _dispatch(pos_flat, x1c, x1l, mod8, norm_g, layer):
    grid_spec = pltpu.PrefetchScalarGridSpec(
        num_scalar_prefetch=1,
        grid=(N_TOK_TILES,),
        in_specs=[
            pl.BlockSpec((MOE_TM, D_MODEL), lambda i, *_: (jnp.minimum(i, CTX_TILES - 1), 0)),
            pl.BlockSpec((MOE_TM, D_MODEL), lambda i, *_: (jnp.maximum(i - CTX_TILES, 0), 0)),
            _mod_spec(layer, 3, _tok_row),
            _mod_spec(layer, 4, _tok_row),
            pl.BlockSpec((None, 1, D_MODEL), lambda i, *_: (layer, 0, 0)),
        ],
        out_specs=pl.BlockSpec(memory_space=pl.ANY),
        scratch_shapes=[pltpu.VMEM((MOE_TM, D_MODEL), F32), pltpu.SemaphoreType.DMA(())],
    )
    return pl.pallas_call(
        _dispatch_kernel,
        grid_spec=grid_spec,
        out_shape=jax.ShapeDtypeStruct((2 * T_ALL, D_MODEL), F32),
        compiler_params=_cparams(("arbitrary",)),
        name="moe_dispatch",
    )(pos_flat, x1c, x1l, mod8, mod8, norm_g)


def _expert_kernel(tile_ref, exp_ref, lo_ref, hi_ref, n_ref, xs_ref, wg_ref, wu_ref, wd_ref, ys_ref):
    k = pl.program_id(0)

    @pl.when(k < n_ref[0])
    def _():
        x = xs_ref[...].astype(BF16)
        gate = jnp.dot(x, wg_ref[...].astype(BF16), preferred_element_type=F32)
        up = jnp.dot(x, wu_ref[...].astype(BF16), preferred_element_type=F32)
        hid = (_silu(gate) * up).astype(BF16)
        y = jnp.dot(hid, wd_ref[...].astype(BF16), preferred_element_type=F32)
        row0 = tile_ref[k] * MOE_TM
        row = row0 + lax.broadcasted_iota(I32, (MOE_TM, 1), 0)
        mine = jnp.logical_and(row >= lo_ref[k], row < hi_ref[k])
        first = lo_ref[k] <= row0

        @pl.when(first)
        def _():
            ys_ref[...] = jnp.where(mine, y, 0.0)

        @pl.when(jnp.logical_not(first))
        def _():
            ys_ref[...] = jnp.where(mine, y, ys_ref[...])


def _experts(items, xs, w_gate, w_up, w_down, layer):
    item_tile, item_exp, item_lo, item_hi, n_items = items
    cur = lambda k, n: jnp.minimum(k, n[0] - 1)
    rows = pl.BlockSpec((MOE_TM, D_MODEL), lambda k, t, e, lo, hi, n: (t[cur(k, n)], 0))
    w_in = pl.BlockSpec((None, None, D_MODEL, D_FF),
                        lambda k, t, e, lo, hi, n: (layer, e[cur(k, n)], 0, 0))
    w_dn = pl.BlockSpec((None, None, D_FF, D_MODEL),
                        lambda k, t, e, lo, hi, n: (layer, e[cur(k, n)], 0, 0))
    grid_spec = pltpu.PrefetchScalarGridSpec(
        num_scalar_prefetch=5, grid=(MOE_ITEMS,),
        in_specs=[rows, w_in, w_in, w_dn], out_specs=rows)
    return pl.pallas_call(
        _expert_kernel,
        grid_spec=grid_spec,
        out_shape=jax.ShapeDtypeStruct(xs.shape, F32),
        compiler_params=_cparams(("arbitrary",)),
        name="moe_experts",
    )(item_tile, item_exp, item_lo, item_hi, n_items, xs, w_gate, w_up, w_down)


def _combine_kernel(pos_ref, xc_ref, xl_ref, wt_ref, g2_ref, ys_ref, oc_ref, ol_ref, y_ref, sem):
    i = pl.program_id(0)

    def issue(r, carry):
        tok = i * MOE_TM + r
        _row_copy(ys_ref, pos_ref[tok], y_ref.at[0], r, sem).start()
        _row_copy(ys_ref, pos_ref[T_ALL + tok], y_ref.at[1], r, sem).start()
        return carry

    lax.fori_loop(0, MOE_TM, issue, 0)

    def drain(r, carry):
        _row_copy(ys_ref, 0, y_ref.at[0], 0, sem).wait()
        return carry

    lax.fori_loop(0, 2 * MOE_TM, drain, 0)

    moe = wt_ref[:, 0:1] * y_ref[0] + wt_ref[:, 1:2] * y_ref[1]

    @pl.when(i < CTX_TILES)
    def _():
        oc_ref[...] = xc_ref[...] + g2_ref[...] * moe

    @pl.when(i >= CTX_TILES)
    def _():
        ol_ref[...] = xl_ref[...] + g2_ref[...] * moe


def _combine(pos_flat, x1c, x1l, wts_t, mod8, ys, layer):
    cspec = pl.BlockSpec((MOE_TM, D_MODEL), lambda i, *_: (jnp.minimum(i, CTX_TILES - 1), 0))
    lspec = pl.BlockSpec((MOE_TM, D_MODEL), lambda i, *_: (jnp.maximum(i - CTX_TILES, 0), 0))
    grid_spec = pltpu.PrefetchScalarGridSpec(
        num_scalar_prefetch=1,
        grid=(N_TOK_TILES,),
        in_specs=[
            cspec, lspec,
            pl.BlockSpec((MOE_TM, 2), lambda i, *_: (i, 0)),
            _mod_spec(layer, 5, _tok_row),
            pl.BlockSpec(memory_space=pl.ANY),
        ],
        out_specs=[cspec, lspec],
        scratch_shapes=[pltpu.VMEM((2, MOE_TM, D_MODEL), F32), pltpu.SemaphoreType.DMA(())],
    )
    return pl.pallas_call(
        _combine_kernel,
        grid_spec=grid_spec,
        out_shape=[jax.ShapeDtypeStruct((T_CTX, D_MODEL), F32),
                   jax.ShapeDtypeStruct((T_LAT, D_MODEL), F32)],
        compiler_params=_cparams(("arbitrary",)),
        name="moe_combine",
    )(pos_flat, x1c, x1l, wts_t, mod8, ys)


def _norm_tables():
    half = np.kron(np.eye(2, dtype=np.float32), np.full((DQK, DQK), 1.0 / DQK, np.float32))
    full = np.full((LANES, LANES), 1.0 / LANES, np.float32)
    return jnp.asarray(half, BF16), jnp.asarray(full, BF16)


def _rope_tables():
    half = DQK // 2
    inv = 1.0 / (ROPE_BASE ** (np.arange(0, half, 2, dtype=np.float32) / half))
    t = np.arange(DEC_SEQ)
    ang_r = (t // GRID_W).astype(np.float32)[:, None] * inv
    ang_c = (t % GRID_W).astype(np.float32)[:, None] * inv
    ang = np.concatenate([ang_r, ang_r, ang_c, ang_c], axis=-1).astype(np.float32)
    ang = np.concatenate([ang, ang], axis=-1)
    first = (np.arange(LANES) % 32) < 16
    cos, sin = np.cos(ang), np.sin(ang)
    return (jnp.asarray(cos, F32), jnp.asarray(np.where(first, -sin, 0.0), F32),
            jnp.asarray(np.where(first, 0.0, sin), F32))


def _na_mask():
    qc = np.arange(GRID_W)[:, None]
    kc = (np.arange(LANES) % GRID_W)[None, :]
    ws = np.clip(qc - NA_KC // 2, 0, GRID_W - NA_KC)
    ok = (kc >= ws) & (kc < ws + NA_KC)
    return jnp.asarray(np.where(ok, 0.0, NEG_BIG), F32)


def kernel(x_prompt, x_sample, cache_diff_k, cache_diff_v, cache_na_k, cache_na_v, c, c_ctx, norm_mix_g, norm_ffn_g, w_ada, b_ada, w_in, w_out, conv_w, conv_b, conv_ln_g, conv_ln_b, diff_qn_g, diff_kn_g, diff_lam_q1, diff_lam_k1, diff_lam_q2, diff_lam_k2, diff_subln_g, na_qn_g, na_kn_g, na_rpb, moe_wr_g, moe_br_g, moe_wr_e, moe_br_e, moe_w_gate, moe_w_up, moe_w_down):
    cvec = jnp.zeros((MOD_ROWS, D_MODEL), F32).at[0].set(c_ctx).at[1:1 + DEC_BATCH].set(c)
    ones = jnp.ones((DEPTH, SEG_W), F32)
    gains = jnp.stack(
        [jnp.tile(diff_qn_g, (1, 2 * N_HEADS)), jnp.tile(diff_kn_g, (1, 2 * N_HEADS)), ones,
         jnp.tile(na_qn_g, (1, N_HEADS)), jnp.tile(na_kn_g, (1, N_HEADS)), ones],
        axis=1).reshape(DEPTH, 6, 1, SEG_W)
    a64, a128 = _norm_tables()
    rope_tabs = _rope_tables()
    na_mask = _na_mask()
    lam_params = jnp.stack([diff_lam_q1, diff_lam_k1, diff_lam_q2, diff_lam_k2], axis=1)
    rpb_pad = jnp.pad(na_rpb, ((0, 0), (0, 0), (0, 0), (0, DQK - na_rpb.shape[-1])))
    rpb_src = jnp.concatenate([rpb_pad[:, :, :-1], rpb_pad[:, :, 1:]], axis=-1)
    rpb_src = jnp.pad(rpb_src, ((0, 0), (0, 0), (0, 16 - NA_PAIRS), (0, 0)))
    wr_t = jnp.concatenate([moe_wr_g, moe_wr_e.reshape(DEPTH, D_MODEL, N_EXPERTS)], axis=2)
    wr_t = jnp.pad(jnp.swapaxes(wr_t, 1, 2), ((0, 0), (0, ROUTE_ROWS - N_GROUPS - N_EXPERTS), (0, 0)))
    br = jnp.concatenate([moe_br_g, moe_br_e.reshape(DEPTH, N_EXPERTS)], axis=1)
    br = jnp.pad(br, ((0, 0), (0, ROUTE_ROWS - N_GROUPS - N_EXPERTS))).reshape(DEPTH, ROUTE_ROWS, 1)
    tri = jnp.asarray(np.triu(np.ones((CUM_BLK, CUM_BLK), np.float32), 1), BF16)
    norm_mix = norm_mix_g.reshape(DEPTH, 1, D_MODEL)
    norm_ffn = norm_ffn_g.reshape(DEPTH, 1, D_MODEL)
    head_major = lambda cache: jnp.transpose(cache, (0, 1, 3, 2, 4))
    ck_diff, cv_diff = head_major(cache_diff_k), head_major(cache_diff_v)
    ck_na, cv_na = head_major(cache_na_k), head_major(cache_na_v)

    mod8 = _ada_modulation(cvec, w_ada, b_ada).reshape(DEPTH * MOD_ROWS, 1, N_ADA)
    w_out_bf = _cast_bf16(w_out)

    ctx_row = lambda i: 0
    lat_row_pre = lambda i: 1 + i * PRE_TM // DEC_SEQ
    lat_row_mix = lambda i: 1 + i * MIX_TM // DEC_SEQ

    xc = x_prompt.reshape(T_CTX, D_MODEL)
    xl = x_sample.reshape(T_LAT, D_MODEL)
    caches = [[], [], [], []]
    for layer in range(DEPTH):
        hc, uc = _pre_project(xc, mod8, norm_mix, w_in, layer, ctx_row, "pre_ctx")
        hl, ul = _pre_project(xl, mod8, norm_mix, w_in, layer, lat_row_pre, "pre_lat")
        sc = _head_project(hc, w_in, gains, a64, a128, None, layer, BATCH, SEQ, "heads_ctx")
        sl = _head_project(hl, w_in, gains, a64, a128, rope_tabs, layer, DEC_BATCH, DEC_SEQ, "heads_lat")
        for slot, seg in zip(caches, (SEG_DK, SEG_DV, SEG_NK, SEG_NV)):
            slot.append(sc[seg])

        ac = _conformer_conv(uc, conv_w, conv_b, conv_ln_g, conv_ln_b, layer, BATCH, SEQ)
        al = _conformer_conv(ul, conv_w, conv_b, conv_ln_g, conv_ln_b, layer, DEC_BATCH, DEC_SEQ)
        dc, nc = _ctx_attention(sc, lam_params, diff_subln_g, layer)
        dl = _diff_attention_lat(sl, lam_params, diff_subln_g, ck_diff, cv_diff, layer, 256)
        nl = _na_attention(sl, rpb_src, na_mask, ck_na, cv_na, layer)

        x1c, ltc = _mix(ac, dc, nc, w_out_bf, xc, mod8, norm_ffn, wr_t, br, layer, ctx_row)
        x1l, ltl = _mix(al, dl, nl, w_out_bf, xl, mod8, norm_ffn, wr_t, br, layer, lat_row_mix)

        pos, wts, items = _route(jnp.concatenate([ltc, ltl], axis=1), tri)
        pos_flat = pos.reshape(2 * T_ALL)
        items = [items[r, :MOE_ITEMS] for r in range(4)] + [items[4, :1]]
        xs = _dispatch(pos_flat, x1c, x1l, mod8, norm_ffn, layer)
        ys = _experts(items, xs, moe_w_gate, moe_w_up, moe_w_down, layer)
        xc, xl = _combine(pos_flat, x1c, x1l, wts.T, mod8, ys, layer)

    outs = [jnp.transpose(jnp.stack(slot, axis=1), (0, 1, 3, 2, 4)) for slot in caches]
    return (xc.reshape(BATCH, SEQ, D_MODEL), xl.reshape(DEC_BATCH, DEC_SEQ, D_MODEL), *outs)
```

```python
import functools
import math

import numpy as np
import jax
import jax.numpy as jnp
from jax import lax
from jax.experimental import pallas as pl
from jax.experimental.pallas import tpu as pltpu

F32 = jnp.float32
BF16 = jnp.bfloat16
I32 = jnp.int32

D_MODEL = 2048
BATCH = 16
SEQ = 256
DEPTH = 2
DEC_BATCH = 2
DEC_SEQ = 1024
PAST_LEN = 256
GRID_W = 64
GRID_ROWS = DEC_SEQ // GRID_W
HEAD_DIM = 128
C_CONV = 512
CONV_K = 31
N_HEADS = 6
DQK = 64
NA_KR = 8
NA_KC = 16
ROPE_BASE = 10000.0
N_GROUPS = 4
E_PER_GROUP = 4
N_EXPERTS = 16
D_FF = 512
N_ADA = 6 * D_MODEL
W_IN_COLS = 2 * C_CONV + 6 * N_HEADS * HEAD_DIM
T_CTX = BATCH * SEQ
T_LAT = DEC_BATCH * DEC_SEQ
T_ALL = T_CTX + T_LAT

LANES = 128
MOD_ROWS = 8
PRE_TM = 512
HEADS_TM = 1024
MIX_TM = 512
MOE_TM = 256
MOE_ITEMS = 2 * T_ALL // MOE_TM + N_EXPERTS
ROUTE_ROWS = 32
CUM_BLK = 512
VMEM_LIMIT = 56 * 1024 * 1024
NEG_BIG = -1e30


def _cparams(sem):
    return pltpu.CompilerParams(dimension_semantics=sem, vmem_limit_bytes=VMEM_LIMIT)


def _sigmoid(x):
    return 1.0 / (1.0 + jnp.exp(-x))


def _silu(x):
    return x * _sigmoid(x)


def _lambda_init(layer):
    return 0.8 - 0.6 * math.exp(-0.3 * layer)


def _ada_kernel(c_ref, w_ref, b_ref, o_ref):
    c = c_ref[...]
    s = _silu(c).astype(BF16)
    o_ref[...] = jnp.dot(s, w_ref[...].astype(BF16), preferred_element_type=F32) + b_ref[...]


def _ada_modulation(cvec, w_ada, b_ada):
    tn = 1024
    return pl.pallas_call(
        _ada_kernel,
        grid=(DEPTH, N_ADA // tn),
        in_specs=[
            pl.BlockSpec((MOD_ROWS, D_MODEL), lambda l, j: (0, 0)),
            pl.BlockSpec((None, D_MODEL, tn), lambda l, j: (l, 0, j)),
            pl.BlockSpec((None, 1, tn), lambda l, j: (l, 0, j)),
        ],
        out_specs=pl.BlockSpec((None, MOD_ROWS, tn), lambda l, j: (l, 0, j)),
        out_shape=jax.ShapeDtypeStruct((DEPTH, MOD_ROWS, N_ADA), F32),
        compiler_params=_cparams(("arbitrary", "arbitrary")),
        name="ada_modulation",
    )(cvec, w_ada, b_ada.reshape(DEPTH, 1, N_ADA))


def _mod_spec(layer, chunk, row_of_block):
    return pl.BlockSpec((None, 1, D_MODEL),
                        lambda i, *_: (layer * MOD_ROWS + row_of_block(i), 0, chunk))


def _modulate(x, g, shift, scale):
    ms = jnp.mean(x * x, axis=-1, keepdims=True)
    return x * lax.rsqrt(ms + 1e-6) * (g * (1.0 + scale)) + shift


def _pre_kernel(x_ref, sh_ref, sc_ref, g_ref, w_ref, h_ref, u_ref, wbf_ref):
    @pl.when(pl.program_id(0) == 0)
    def _():
        wbf_ref[...] = w_ref[...].astype(BF16)

    h = _modulate(x_ref[...], g_ref[...], sh_ref[...], sc_ref[...]).astype(BF16)
    h_ref[...] = h
    y = jnp.dot(h, wbf_ref[...], preferred_element_type=F32)
    u_ref[...] = y[:, :C_CONV] * _sigmoid(y[:, C_CONV:])


def _pre_project(x, mod8, norm_g, w_in, layer, row_of_block, name):
    t = x.shape[0]
    return pl.pallas_call(
        _pre_kernel,
        grid=(t // PRE_TM,),
        in_specs=[
            pl.BlockSpec((PRE_TM, D_MODEL), lambda i: (i, 0)),
            _mod_spec(layer, 0, row_of_block),
            _mod_spec(layer, 1, row_of_block),
            pl.BlockSpec((None, 1, D_MODEL), lambda i: (layer, 0, 0)),
            pl.BlockSpec((None, D_MODEL, 2 * C_CONV), lambda i: (layer, 0, 0)),
        ],
        out_specs=[pl.BlockSpec((PRE_TM, D_MODEL), lambda i: (i, 0)),
                   pl.BlockSpec((PRE_TM, C_CONV), lambda i: (i, 0))],
        out_shape=[jax.ShapeDtypeStruct((t, D_MODEL), BF16), jax.ShapeDtypeStruct((t, C_CONV), F32)],
        scratch_shapes=[pltpu.VMEM((D_MODEL, 2 * C_CONV), BF16)],
        compiler_params=_cparams(("arbitrary",)),
        name=name,
    )(x, mod8, mod8, norm_g, w_in)


SEG_DQ, SEG_DK, SEG_DV, SEG_NQ, SEG_NK, SEG_NV = range(6)
SEG_W = N_HEADS * HEAD_DIM
HEADS_WBLK = 256


def _heads_kernel(h_ref, w0_ref, w1_ref, w2_ref, gain_ref, a64_ref, a128_ref, *rest, rope, per_batch):
    if rope:
        cos_ref, sa_ref, sb_ref, o_ref = rest
    else:
        (o_ref,) = rest
    j = pl.program_id(0)

    def finish(seg):
        for c, w_ref in enumerate((w0_ref, w1_ref, w2_ref)):
            y = jnp.dot(h_ref[...], w_ref[...].astype(BF16), preferred_element_type=F32)
            for g in range(HEADS_WBLK // LANES):
                hd = c * (HEADS_WBLK // LANES) + g
                yg = y[:, g * LANES:(g + 1) * LANES]
                if seg in (SEG_DQ, SEG_DK, SEG_NQ, SEG_NK):
                    a_ref = a64_ref if seg in (SEG_DQ, SEG_DK) else a128_ref
                    ms = jnp.dot((yg * yg).astype(BF16), a_ref[...], preferred_element_type=F32)
                    yg = yg * lax.rsqrt(ms + 1e-6) * gain_ref[:, hd * LANES:(hd + 1) * LANES]
                if rope and seg in (SEG_DQ, SEG_DK):
                    yg = (yg * cos_ref[...] + pltpu.roll(yg, LANES - 16, 1) * sa_ref[...]
                          + pltpu.roll(yg, 16, 1) * sb_ref[...])
                if per_batch == 1:
                    o_ref[hd] = yg
                else:
                    rows = yg.shape[0] // per_batch
                    for bb in range(per_batch):
                        o_ref[bb, hd] = yg[bb * rows:(bb + 1) * rows]

    for seg in range(6):
        pl.when(j == seg)(functools.partial(finish, seg))


def _head_project(h, w_in, gains, a64, a128, rope_tabs, layer, batch, seq, name):
    t = h.shape[0]
    rope = rope_tabs is not None
    per_batch = HEADS_TM // seq
    col0 = 2 * C_CONV // HEADS_WBLK
    wspec = lambda c: pl.BlockSpec((None, D_MODEL, HEADS_WBLK),
                                   lambda j, i: (layer, 0, col0 + (SEG_W // HEADS_WBLK) * j + c))
    in_specs = [
        pl.BlockSpec((HEADS_TM, D_MODEL), lambda j, i: (i, 0)),
        wspec(0), wspec(1), wspec(2),
        pl.BlockSpec((None, None, 1, SEG_W), lambda j, i: (layer, j, 0, 0)),
        pl.BlockSpec((LANES, LANES), lambda j, i: (0, 0)),
        pl.BlockSpec((LANES, LANES), lambda j, i: (0, 0)),
    ]
    args = [h, w_in, w_in, w_in, gains, a64, a128]
    if rope:
        in_specs += [pl.BlockSpec((DEC_SEQ, LANES), lambda j, i: (0, 0))] * 3
        args += list(rope_tabs)
    if per_batch == 1:
        out_spec = pl.BlockSpec((None, None, N_HEADS, seq, HEAD_DIM), lambda j, i: (j, i, 0, 0, 0))
    else:
        out_spec = pl.BlockSpec((None, per_batch, N_HEADS, seq, HEAD_DIM), lambda j, i: (j, i, 0, 0, 0))
    return pl.pallas_call(
        functools.partial(_heads_kernel, rope=rope, per_batch=per_batch),
        grid=(6, t // HEADS_TM),
        in_specs=in_specs,
        out_specs=out_spec,
        out_shape=jax.ShapeDtypeStruct((6, batch, N_HEADS, seq, HEAD_DIM), F32),
        compiler_params=_cparams(("arbitrary", "arbitrary")),
        name=name,
    )(*args)


CONV_CH = 128
CONV_HALO = 16


def _conv_kernel(u_ref, w_ref, b_ref, lg_ref, lb_ref, o_ref, win_ref, y_ref, *, seq):
    c = pl.program_id(1)
    n_chunks = seq // CONV_CH
    base = pl.multiple_of(c * CONV_CH, CONV_CH)

    win_ref[CONV_HALO:CONV_HALO + CONV_CH, :] = u_ref[pl.ds(base, CONV_CH), :]
    lo_start = pl.multiple_of(jnp.maximum(base - CONV_HALO, 0), CONV_HALO)
    hi_start = pl.multiple_of(jnp.minimum(base + CONV_CH, seq - CONV_HALO), CONV_HALO)
    lo_keep = (c > 0).astype(F32)
    hi_keep = (c < n_chunks - 1).astype(F32)
    win_ref[0:CONV_HALO, :] = u_ref[pl.ds(lo_start, CONV_HALO), :] * lo_keep
    win_ref[CONV_HALO + CONV_CH:, :] = u_ref[pl.ds(hi_start, CONV_HALO), :] * hi_keep

    off = CONV_HALO - CONV_K // 2
    for g in range(C_CONV // LANES):
        sl = slice(g * LANES, (g + 1) * LANES)
        acc = jnp.zeros((CONV_CH, LANES), F32) + b_ref[:, sl]
        for k in range(CONV_K):
            acc = acc + win_ref[off + k:off + k + CONV_CH, sl] * w_ref[k:k + 1, sl]
        y_ref[:, sl] = acc

    y = y_ref[...]
    mu = jnp.mean(y, axis=-1, keepdims=True)
    yc = y - mu
    var = jnp.mean(yc * yc, axis=-1, keepdims=True)
    z = yc * lax.rsqrt(var + 1e-5) * lg_ref[...] + lb_ref[...]
    o_ref[...] = _silu(z)


def _conformer_conv(u, conv_w, conv_b, ln_g, ln_b, layer, batch, seq):
    t = u.shape[0]
    n_chunks = seq // CONV_CH
    vec = lambda: pl.BlockSpec((None, 1, C_CONV), lambda b, c: (layer, 0, 0))
    return pl.pallas_call(
        functools.partial(_conv_kernel, seq=seq),
        grid=(batch, n_chunks),
        in_specs=[
            pl.BlockSpec((seq, C_CONV), lambda b, c: (b, 0)),
            pl.BlockSpec((None, CONV_K, C_CONV), lambda b, c: (layer, 0, 0)),
            vec(), vec(), vec(),
        ],
        out_specs=pl.BlockSpec((CONV_CH, C_CONV), lambda b, c: (b * n_chunks + c, 0)),
        out_shape=jax.ShapeDtypeStruct((t, C_CONV), F32),
        scratch_shapes=[pltpu.VMEM((CONV_CH + 2 * CONV_HALO, C_CONV), F32),
                        pltpu.VMEM((CONV_CH, C_CONV), F32)],
        compiler_params=_cparams(("arbitrary", "arbitrary")),
        name="conv_%d" % seq,
    )(u, conv_w, conv_b.reshape(DEPTH, 1, C_CONV), ln_g.reshape(DEPTH, 1, C_CONV),
      ln_b.reshape(DEPTH, 1, C_CONV))


_NT = (((1,), (1,)), ((), ()))


def _softmax_rows(s):
    m = jnp.max(s, axis=-1, keepdims=True)
    e = jnp.exp(s - m)
    return e / jnp.sum(e, axis=-1, keepdims=True)


def _diff_lambda(lam_ref, lam_init):
    lv = lam_ref[...]
    return (jnp.exp(jnp.sum(lv[0:1] * lv[1:2], axis=-1, keepdims=True))
            - jnp.exp(jnp.sum(lv[2:3] * lv[3:4], axis=-1, keepdims=True)) + lam_init)


def _diff_head(q, k, v, lam, g, lam_init):
    lane = lax.broadcasted_iota(I32, q.shape, 1)
    q0 = jnp.where(lane < DQK, q, 0.0).astype(BF16)
    q1 = jnp.where(lane >= DQK, q, 0.0).astype(BF16)
    scale = DQK ** -0.5
    p0 = _softmax_rows(lax.dot_general(q0, k, _NT, preferred_element_type=F32) * scale)
    p1 = _softmax_rows(lax.dot_general(q1, k, _NT, preferred_element_type=F32) * scale)
    a = (p0 - lam * p1).astype(BF16)
    o = jnp.dot(a, v, preferred_element_type=F32)
    ms = jnp.mean(o * o, axis=-1, keepdims=True)
    return o * lax.rsqrt(ms + 1e-5) * g * (1.0 - lam_init)


def _seg_spec(seg, rows, index):
    def imap(*ids):
        b, h, r = index(*ids)
        return (seg, b, h, r, 0)
    return pl.BlockSpec((None, None, None, rows, HEAD_DIM), imap)


def _diff_attn_lat_kernel(lam_ref, g_ref, q_ref, k_ref, v_ref, ck_ref, cv_ref, o_ref, *, lam_init):
    k = jnp.concatenate([k_ref[...], ck_ref[...]], axis=0).astype(BF16)
    v = jnp.concatenate([v_ref[...], cv_ref[...]], axis=0).astype(BF16)
    lam = _diff_lambda(lam_ref, lam_init)
    o_ref[...] = _diff_head(q_ref[...], k, v, lam, g_ref[...], lam_init)


def _diff_attention_lat(segs, lam_params, subln_g, cache_k, cache_v, layer, tq):
    nq = DEC_SEQ // tq
    cspec = pl.BlockSpec((None, None, None, PAST_LEN, HEAD_DIM), lambda b, h, qi: (b, layer, h, 0, 0))
    return pl.pallas_call(
        functools.partial(_diff_attn_lat_kernel, lam_init=_lambda_init(layer)),
        grid=(DEC_BATCH, N_HEADS, nq),
        in_specs=[
            pl.BlockSpec((None, 4, DQK), lambda b, h, qi: (layer, 0, 0)),
            pl.BlockSpec((None, 1, HEAD_DIM), lambda b, h, qi: (layer, 0, 0)),
            _seg_spec(SEG_DQ, tq, lambda b, h, qi: (b, h, qi)),
            _seg_spec(SEG_DK, DEC_SEQ, lambda b, h, qi: (b, h, 0)),
            _seg_spec(SEG_DV, DEC_SEQ, lambda b, h, qi: (b, h, 0)),
            cspec, cspec,
        ],
        out_specs=pl.BlockSpec((tq, HEAD_DIM), lambda b, h, qi: (b * nq + qi, h)),
        out_shape=jax.ShapeDtypeStruct((T_LAT, SEG_W), F32),
        compiler_params=_cparams(("arbitrary", "arbitrary", "arbitrary")),
        name="diff_attn_lat",
    )(lam_params, subln_g.reshape(DEPTH, 1, HEAD_DIM), segs, segs, segs, cache_k, cache_v)


def _ctx_attn_kernel(lam_ref, g_ref, dq_ref, dk_ref, dv_ref, nq_ref, nk_ref, nv_ref, d_ref, n_ref, *,
                     lam_init):
    lam = _diff_lambda(lam_ref, lam_init)
    for h in range(N_HEADS):
        sl = slice(h * HEAD_DIM, (h + 1) * HEAD_DIM)
        d_ref[:, sl] = _diff_head(dq_ref[h], dk_ref[h].astype(BF16), dv_ref[h].astype(BF16), lam,
                                  g_ref[...], lam_init)
        s = lax.dot_general(nq_ref[h].astype(BF16), nk_ref[h].astype(BF16), _NT,
                            preferred_element_type=F32) * (HEAD_DIM ** -0.5)
        n_ref[:, sl] = jnp.dot(_softmax_rows(s).astype(BF16), nv_ref[h].astype(BF16),
                               preferred_element_type=F32)


def _ctx_attention(segs, lam_params, subln_g, layer):
    seg = lambda s: pl.BlockSpec((None, None, N_HEADS, SEQ, HEAD_DIM), lambda b: (s, b, 0, 0, 0))
    out = pl.BlockSpec((SEQ, SEG_W), lambda b: (b, 0))
    return pl.pallas_call(
        functools.partial(_ctx_attn_kernel, lam_init=_lambda_init(layer)),
        grid=(BATCH,),
        in_specs=[pl.BlockSpec((None, 4, DQK), lambda b: (layer, 0, 0)),
                  pl.BlockSpec((None, 1, HEAD_DIM), lambda b: (layer, 0, 0))]
                 + [seg(s) for s in range(6)],
        out_specs=[out, out],
        out_shape=[jax.ShapeDtypeStruct((T_CTX, SEG_W), F32)] * 2,
        compiler_params=_cparams(("arbitrary",)),
        name="ctx_attn",
    )(lam_params, subln_g.reshape(DEPTH, 1, HEAD_DIM), *([segs] * 6))


NA_PAIRS = 2 * NA_KR - 2
NA_LOCAL = NA_KR * GRID_W


def _na_attn_kernel(src_ref, mask_ref, q_ref, k_ref, v_ref, ck_ref, cv_ref, o_ref, bias_ref):
    for d in range(NA_PAIRS):
        row = jnp.broadcast_to(src_ref[d:d + 1, :], (GRID_W, LANES))
        bias_ref[d] = pltpu.roll(row, LANES - (NA_KC - 1), 1, stride=1, stride_axis=0) + mask_ref[...]
    ck = ck_ref[...].astype(BF16)
    cv = cv_ref[...].astype(BF16)
    scale = HEAD_DIM ** -0.5
    for qr in range(GRID_ROWS):
        start = min(max(qr - NA_KR // 2, 0), GRID_ROWS - NA_KR)
        d0 = start - qr + NA_KR - 1
        q = q_ref[qr * GRID_W:(qr + 1) * GRID_W, :].astype(BF16)
        kl = k_ref[start * GRID_W:start * GRID_W + NA_LOCAL, :].astype(BF16)
        vl = v_ref[start * GRID_W:start * GRID_W + NA_LOCAL, :].astype(BF16)
        bias = jnp.concatenate([bias_ref[d0 + 2 * i] for i in range(NA_KR // 2)], axis=1)
        s_loc = lax.dot_general(q, kl, _NT, preferred_element_type=F32) * scale + bias
        s_ctx = lax.dot_general(q, ck, _NT, preferred_element_type=F32) * scale
        m = jnp.maximum(jnp.max(s_loc, axis=-1, keepdims=True), jnp.max(s_ctx, axis=-1, keepdims=True))
        e_loc = jnp.exp(s_loc - m)
        e_ctx = jnp.exp(s_ctx - m)
        den = jnp.sum(e_loc, axis=-1, keepdims=True) + jnp.sum(e_ctx, axis=-1, keepdims=True)
        p_loc = (e_loc / den).astype(BF16)
        p_ctx = (e_ctx / den).astype(BF16)
        o_ref[qr * GRID_W:(qr + 1) * GRID_W, :] = (
            jnp.dot(p_loc, vl, preferred_element_type=F32) + jnp.dot(p_ctx, cv, preferred_element_type=F32))


def _na_attention(segs, rpb_src, na_mask, cache_k, cache_v, layer):
    blk = lambda s: _seg_spec(s, DEC_SEQ, lambda b, h: (b, h, 0))
    cspec = pl.BlockSpec((None, None, None, PAST_LEN, HEAD_DIM), lambda b, h: (b, layer, h, 0, 0))
    return pl.pallas_call(
        _na_attn_kernel,
        grid=(DEC_BATCH, N_HEADS),
        in_specs=[
            pl.BlockSpec((None, None, 16, LANES), lambda b, h: (layer, h, 0, 0)),
            pl.BlockSpec((GRID_W, LANES), lambda b, h: (0, 0)),
            blk(SEG_NQ), blk(SEG_NK), blk(SEG_NV), cspec, cspec,
        ],
        out_specs=pl.BlockSpec((DEC_SEQ, HEAD_DIM), lambda b, h: (b, h)),
        out_shape=jax.ShapeDtypeStruct((T_LAT, SEG_W), F32),
        scratch_shapes=[pltpu.VMEM((NA_PAIRS, GRID_W, LANES), F32)],
        compiler_params=_cparams(("arbitrary", "arbitrary")),
        name="na_attn_lat",
    )(rpb_src, na_mask, segs, segs, segs, cache_k, cache_v)


def _split3(x):
    hi = x.astype(BF16)
    r1 = x - hi.astype(F32)
    mid = r1.astype(BF16)
    lo = (r1 - mid.astype(F32)).astype(BF16)
    return hi, mid, lo


def _mix_kernel(a_ref, d_ref, n_ref, w_ref, x_ref, g1_ref, sh_ref, sc_ref, ng_ref, wr_ref, br_ref,
                x1_ref, lt_ref):
    k0, k1 = C_CONV, C_CONV + N_HEADS * HEAD_DIM
    mixed = (jnp.dot(a_ref[...].astype(BF16), w_ref[0:k0, :], preferred_element_type=F32)
             + jnp.dot(d_ref[...].astype(BF16), w_ref[k0:k1, :], preferred_element_type=F32)
             + jnp.dot(n_ref[...].astype(BF16), w_ref[k1:, :], preferred_element_type=F32))
    x1 = x_ref[...] + g1_ref[...] * mixed
    x1_ref[...] = x1
    h2 = _modulate(x1, ng_ref[...], sh_ref[...], sc_ref[...])
    hh, hm, hl = _split3(h2)
    wh, wm, wl = _split3(wr_ref[...])
    dot = lambda a, b: lax.dot_general(a, b, _NT, preferred_element_type=F32)
    lt = (dot(wh, hh) + (dot(wh, hm) + dot(wm, hh)) + (dot(wm, hm) + dot(wh, hl) + dot(wl, hh)))
    lt_ref[...] = lt + br_ref[...]


def _mix(a_out, d_out, n_out, w_out_bf, x, mod8, norm_g, wr_t, br, layer, row_of_block):
    t = x.shape[0]
    rows = lambda width: pl.BlockSpec((MIX_TM, width), lambda i: (i, 0))
    return pl.pallas_call(
        _mix_kernel,
        grid=(t // MIX_TM,),
        in_specs=[
            rows(C_CONV), rows(N_HEADS * HEAD_DIM), rows(N_HEADS * HEAD_DIM),
            pl.BlockSpec((None, D_MODEL, D_MODEL), lambda i: (layer, 0, 0)),
            rows(D_MODEL),
            _mod_spec(layer, 2, row_of_block),
            _mod_spec(layer, 3, row_of_block),
            _mod_spec(layer, 4, row_of_block),
            pl.BlockSpec((None, 1, D_MODEL), lambda i: (layer, 0, 0)),
            pl.BlockSpec((None, ROUTE_ROWS, D_MODEL), lambda i: (layer, 0, 0)),
            pl.BlockSpec((None, ROUTE_ROWS, 1), lambda i: (layer, 0, 0)),
        ],
        out_specs=[rows(D_MODEL), pl.BlockSpec((ROUTE_ROWS, MIX_TM), lambda i: (0, i))],
        out_shape=[jax.ShapeDtypeStruct((t, D_MODEL), F32),
                   jax.ShapeDtypeStruct((ROUTE_ROWS, t), F32)],
        compiler_params=_cparams(("arbitrary",)),
        name="mix_%d" % t,
    )(a_out, d_out, n_out, w_out_bf, x, mod8, mod8, mod8, norm_g, wr_t, br)


def _cast_kernel(x_ref, o_ref):
    o_ref[...] = x_ref[...].astype(BF16)


def _cast_bf16(w):
    tm = 512
    spec = pl.BlockSpec((None, tm, D_MODEL), lambda l, i: (l, i, 0))
    return pl.pallas_call(
        _cast_kernel, grid=(DEPTH, D_MODEL // tm), in_specs=[spec], out_specs=spec,
        out_shape=jax.ShapeDtypeStruct(w.shape, BF16),
        compiler_params=_cparams(("arbitrary", "arbitrary")), name="cast_w_out",
    )(w)


def _route_kernel(lt_ref, tri_ref, pos_ref, wt_ref, item_ref):
    t = T_ALL
    lg = lt_ref[0:N_GROUPS, :]
    eg = jnp.exp(lg - jnp.max(lg, axis=0, keepdims=True))
    pg = eg / jnp.sum(eg, axis=0, keepdims=True)
    pg_top = jnp.max(pg, axis=0, keepdims=True)
    gi = lax.broadcasted_iota(I32, pg.shape, 0).astype(F32)
    g_idx = jnp.min(jnp.where(pg == pg_top, gi, float(N_GROUPS)), axis=0, keepdims=True)

    le = jnp.zeros((E_PER_GROUP, t), F32)
    for g in range(N_GROUPS):
        rows = lt_ref[N_GROUPS + g * E_PER_GROUP:N_GROUPS + (g + 1) * E_PER_GROUP, :]
        le = jnp.where(g_idx == float(g), rows, le)
    ee = jnp.exp(le - jnp.max(le, axis=0, keepdims=True))
    pe = ee / jnp.sum(ee, axis=0, keepdims=True)
    ei = lax.broadcasted_iota(I32, pe.shape, 0).astype(F32)
    p1 = jnp.max(pe, axis=0, keepdims=True)
    i1 = jnp.min(jnp.where(pe == p1, ei, float(E_PER_GROUP)), axis=0, keepdims=True)
    pe_rest = jnp.where(ei == i1, -1.0, pe)
    p2 = jnp.max(pe_rest, axis=0, keepdims=True)
    i2 = jnp.min(jnp.where(pe_rest == p2, ei, float(E_PER_GROUP)), axis=0, keepdims=True)
    den = p1 + p2
    wt_ref[0:1, :] = pg_top * (p1 / den)
    wt_ref[1:2, :] = pg_top * (p2 / den)
    e1 = g_idx * E_PER_GROUP + i1
    e2 = g_idx * E_PER_GROUP + i2

    erow = lax.broadcasted_iota(I32, (N_EXPERTS, t), 0).astype(F32)
    oh1 = (erow == e1).astype(F32)
    oh2 = (erow == e2).astype(F32)
    cnt = oh1 + oh2
    carry = jnp.zeros((N_EXPERTS, 1), F32)
    ranks = []
    for b in range(t // CUM_BLK):
        blk = cnt[:, b * CUM_BLK:(b + 1) * CUM_BLK]
        ranks.append(jnp.dot(blk.astype(BF16), tri_ref[...], preferred_element_type=F32) + carry)
        carry = carry + jnp.sum(blk, axis=1, keepdims=True)
    rank = jnp.concatenate(ranks, axis=1)

    erow_l = lax.broadcasted_iota(I32, (N_EXPERTS, LANES), 0)

    def excl_scan(v):
        inc = v
        for s in (1, 2, 4, 8):
            inc = inc + jnp.where(erow_l >= s, pltpu.roll(inc, s, 0), 0.0)
        return inc - v

    total = jnp.broadcast_to(carry, (N_EXPERTS, LANES))
    offs = excl_scan(total)
    pos_ref[0:1, :] = jnp.sum(oh1 * (rank + offs[:, 0:1]), axis=0, keepdims=True).astype(I32)
    pos_ref[1:2, :] = jnp.sum(oh2 * (rank + offs[:, 0:1]), axis=0, keepdims=True).astype(I32)

    shift = int(math.log2(MOE_TM))
    offs_i = offs.astype(I32)
    total_i = total.astype(I32)
    first_tile = lax.shift_right_logical(offs_i, shift)
    last_tile = lax.shift_right_logical(offs_i + total_i - 1, shift)
    n_item = jnp.where(total_i > 0, last_tile - first_tile + 1, 0).astype(F32)
    item_start = excl_scan(n_item)
    item_end = item_start + n_item
    kk = lax.broadcasted_iota(I32, (N_EXPERTS, LANES), 1).astype(F32)
    item_e = jnp.minimum(jnp.sum((item_end <= kk).astype(F32), axis=0, keepdims=True),
                         float(N_EXPERTS - 1))
    sel = (erow_l.astype(F32) == item_e).astype(F32)
    pick = lambda v: jnp.sum(sel * v, axis=0, keepdims=True)
    item_ref[0:1, :] = (kk[0:1] + pick(first_tile.astype(F32) - item_start)).astype(I32)
    item_ref[1:2, :] = item_e.astype(I32)
    item_ref[2:3, :] = pick(offs).astype(I32)
    item_ref[3:4, :] = pick(offs + total).astype(I32)
    item_ref[4:5, :] = item_end[N_EXPERTS - 1:N_EXPERTS, :].astype(I32)
    item_ref[5:8, :] = jnp.zeros((3, LANES), I32)


def _route(lt_all, tri):
    full = lambda shape: pl.BlockSpec(shape, lambda i: (0,) * len(shape))
    return pl.pallas_call(
        _route_kernel,
        grid=(1,),
        in_specs=[full((ROUTE_ROWS, T_ALL)), full((CUM_BLK, CUM_BLK))],
        out_specs=[full((2, T_ALL)), full((2, T_ALL)), full((8, LANES))],
        out_shape=[jax.ShapeDtypeStruct((2, T_ALL), I32), jax.ShapeDtypeStruct((2, T_ALL), F32),
                   jax.ShapeDtypeStruct((8, LANES), I32)],
        compiler_params=_cparams(("arbitrary",)),
        name="moe_route",
    )(lt_all, tri)


N_TOK_TILES = T_ALL // MOE_TM
CTX_TILES = T_CTX // MOE_TM


def _tok_row(i):
    return jnp.where(i < CTX_TILES, 0, 1 + (i - CTX_TILES) // (DEC_SEQ // MOE_TM))


N_CHUNK = D_MODEL // LANES
MOE_PITCH = N_CHUNK + 1
SLAB_ROWS = MOE_TM * MOE_PITCH
ISSUE_UNROLL = 4


def _slab_copy(src, src_tok, dst, dst_tok, sem):
    return pltpu.make_async_copy(src.at[pl.ds(src_tok * MOE_PITCH, MOE_PITCH)],
                                 dst.at[pl.ds(dst_tok * MOE_PITCH, MOE_PITCH)], sem)


def _to_slabs(ref, value):
    for c in range(N_CHUNK):
        ref[pl.ds(c, MOE_TM, stride=MOE_PITCH), :] = value[:, c * LANES:(c + 1) * LANES]
    ref[pl.ds(N_CHUNK, MOE_TM, stride=MOE_PITCH), :] = jnp.zeros((MOE_TM, LANES), F32)


def _slab_chunk(ref, c):
    return ref[pl.ds(c, MOE_TM, stride=MOE_PITCH), :]


def _dispatch_kernel(pos_ref, xc_ref, xl_ref, sh_ref, sc_ref, ng_ref, xs_ref, h_ref, sem):
    i = pl.program_id(0)
    slot = i % 2
    buf = h_ref.at[slot]

    def drain(s):
        def body(r, carry):
            _slab_copy(h_ref.at[s], 0, xs_ref, 0, sem.at[s]).wait()
            return carry
        lax.fori_loop(0, 2 * MOE_TM, body, 0, unroll=ISSUE_UNROLL)

    @pl.when(i >= 2)
    def _():
        drain(slot)

    @pl.when(i < CTX_TILES)
    def _():
        _to_slabs(buf, _modulate(xc_ref[...], ng_ref[...], sh_ref[...], sc_ref[...]))

    @pl.when(i >= CTX_TILES)
    def _():
        _to_slabs(buf, _modulate(xl_ref[...], ng_ref[...], sh_ref[...], sc_ref[...]))

    def issue(r, carry):
        tok = i * MOE_TM + r
        _slab_copy(buf, r, xs_ref, pos_ref[tok], sem.at[slot]).start(priority=0)
        _slab_copy(buf, r, xs_ref, pos_ref[T_ALL + tok], sem.at[slot]).start(priority=1)
        return carry

    lax.fori_loop(0, MOE_TM, issue, 0, unroll=ISSUE_UNROLL)

    @pl.when(i == N_TOK_TILES - 1)
    def _():
        drain(1 - slot)
        drain(slot)


def _dispatch(pos_flat, x1c, x1l, mod8, norm_g, layer):
    grid_spec = pltpu.PrefetchScalarGridSpec(
        num_scalar_prefetch=1,
        grid=(N_TOK_TILES,),
        in_specs=[
            pl.BlockSpec((MOE_TM, D_MODEL), lambda i, *_: (jnp.minimum(i, CTX_TILES - 1), 0)),
            pl.BlockSpec((MOE_TM, D_MODEL), lambda i, *_: (jnp.maximum(i - CTX_TILES, 0), 0)),
            _mod_spec(layer, 3, _tok_row),
            _mod_spec(layer, 4, _tok_row),
            pl.BlockSpec((None, 1, D_MODEL), lambda i, *_: (layer, 0, 0)),
        ],
        out_specs=pl.BlockSpec(memory_space=pl.ANY),
        scratch_shapes=[pltpu.VMEM((2, SLAB_ROWS, LANES), F32), pltpu.SemaphoreType.DMA((2,))],
    )
    return pl.pallas_call(
        _dispatch_kernel,
        grid_spec=grid_spec,
        out_shape=jax.ShapeDtypeStruct((2 * T_ALL * MOE_PITCH, LANES), F32),
        compiler_params=_cparams(("arbitrary",)),
        name="moe_dispatch",
    )(pos_flat, x1c, x1l, mod8, mod8, norm_g)


def _expert_kernel(tile_ref, exp_ref, lo_ref, hi_ref, n_ref, xs_ref, wg_hbm, wu_hbm, wd_hbm, ys_ref,
                   wg_buf, wu_buf, wd_buf, wg_bf, wu_bf, wd_bf, slot_ref, sem, *, layer):
    k = pl.program_id(0)
    n = n_ref[0]

    def weight_copies(e, s):
        return (pltpu.make_async_copy(wg_hbm.at[layer, e], wg_buf.at[s], sem.at[s, 0]),
                pltpu.make_async_copy(wu_hbm.at[layer, e], wu_buf.at[s], sem.at[s, 1]),
                pltpu.make_async_copy(wd_hbm.at[layer, e], wd_buf.at[s], sem.at[s, 2]))

    @pl.when(k == 0)
    def _():
        slot_ref[0] = 0
        for cp in weight_copies(exp_ref[0], 0):
            cp.start()

    @pl.when(k < n)
    def _():
        e = exp_ref[k]
        new_expert = jnp.logical_or(k == 0, exp_ref[jnp.maximum(k - 1, 0)] != e)

        @pl.when(new_expert)
        def _():
            s = slot_ref[0]
            for cp in weight_copies(e, s):
                cp.wait()
            wg_bf[...] = wg_buf[s].astype(BF16)
            wu_bf[...] = wu_buf[s].astype(BF16)
            wd_bf[...] = wd_buf[s].astype(BF16)
            nxt = lax.while_loop(lambda j: jnp.logical_and(j < n, exp_ref[jnp.minimum(j, n - 1)] == e),
                                 lambda j: j + 1, k + 1)

            @pl.when(nxt < n)
            def _():
                for cp in weight_copies(exp_ref[jnp.minimum(nxt, n - 1)], 1 - s):
                    cp.start()
            slot_ref[0] = 1 - s

        x = jnp.concatenate([_slab_chunk(xs_ref, c) for c in range(N_CHUNK)], axis=1).astype(BF16)
        gate = jnp.dot(x, wg_bf[...], preferred_element_type=F32)
        up = jnp.dot(x, wu_bf[...], preferred_element_type=F32)
        hid = (_silu(gate) * up).astype(BF16)
        y = jnp.dot(hid, wd_bf[...], preferred_element_type=F32)
        row0 = tile_ref[k] * MOE_TM
        row = row0 + lax.broadcasted_iota(I32, (MOE_TM, 1), 0)
        mine = jnp.logical_and(row >= lo_ref[k], row < hi_ref[k])
        first = lo_ref[k] <= row0

        @pl.when(first)
        def _():
            _to_slabs(ys_ref, jnp.where(mine, y, 0.0))

        @pl.when(jnp.logical_not(first))
        def _():
            for c in range(N_CHUNK):
                old = _slab_chunk(ys_ref, c)
                ys_ref[pl.ds(c, MOE_TM, stride=MOE_PITCH), :] = jnp.where(
                    mine, y[:, c * LANES:(c + 1) * LANES], old)


def _experts(items, xs, w_gate, w_up, w_down, layer):
    item_tile, item_exp, item_lo, item_hi, n_items = items
    cur = lambda k, n: jnp.minimum(k, n[0] - 1)
    rows = pl.BlockSpec((SLAB_ROWS, LANES), lambda k, t, e, lo, hi, n: (t[cur(k, n)], 0))
    hbm = pl.BlockSpec(memory_space=pl.ANY)
    grid_spec = pltpu.PrefetchScalarGridSpec(
        num_scalar_prefetch=5, grid=(MOE_ITEMS,),
        in_specs=[rows, hbm, hbm, hbm], out_specs=rows,
        scratch_shapes=[
            pltpu.VMEM((2, D_MODEL, D_FF), F32), pltpu.VMEM((2, D_MODEL, D_FF), F32),
            pltpu.VMEM((2, D_FF, D_MODEL), F32),
            pltpu.VMEM((D_MODEL, D_FF), BF16), pltpu.VMEM((D_MODEL, D_FF), BF16),
            pltpu.VMEM((D_FF, D_MODEL), BF16),
            pltpu.SMEM((1,), I32), pltpu.SemaphoreType.DMA((2, 3)),
        ])
    return pl.pallas_call(
        functools.partial(_expert_kernel, layer=layer),
        grid_spec=grid_spec,
        out_shape=jax.ShapeDtypeStruct(xs.shape, F32),
        compiler_params=_cparams(("arbitrary",)),
        name="moe_experts",
    )(item_tile, item_exp, item_lo, item_hi, n_items, xs, w_gate, w_up, w_down)


def _combine_kernel(pos_ref, xc_ref, xl_ref, wt_ref, g2_ref, ys_ref, oc_ref, ol_ref, y_ref, sem):
    i = pl.program_id(0)
    slot = i % 2

    def gather(tile, s):
        def body(r, carry):
            tok = tile * MOE_TM + r
            _slab_copy(ys_ref, pos_ref[tok], y_ref.at[s, 0], r, sem.at[s]).start(priority=0)
            _slab_copy(ys_ref, pos_ref[T_ALL + tok], y_ref.at[s, 1], r, sem.at[s]).start(priority=1)
            return carry
        lax.fori_loop(0, MOE_TM, body, 0, unroll=ISSUE_UNROLL)

    @pl.when(i == 0)
    def _():
        gather(0, 0)

    @pl.when(i + 1 < N_TOK_TILES)
    def _():
        gather(i + 1, 1 - slot)

    def drain(r, carry):
        _slab_copy(ys_ref, 0, y_ref.at[slot, 0], 0, sem.at[slot]).wait()
        return carry

    lax.fori_loop(0, 2 * MOE_TM, drain, 0, unroll=ISSUE_UNROLL)

    def write(x_ref, o_ref):
        w0 = wt_ref[:, 0:1]
        w1 = wt_ref[:, 1:2]
        for c in range(N_CHUNK):
            sl = slice(c * LANES, (c + 1) * LANES)
            moe = w0 * _slab_chunk(y_ref.at[slot, 0], c) + w1 * _slab_chunk(y_ref.at[slot, 1], c)
            o_ref[:, sl] = x_ref[:, sl] + g2_ref[:, sl] * moe

    pl.when(i < CTX_TILES)(lambda: write(xc_ref, oc_ref))
    pl.when(i >= CTX_TILES)(lambda: write(xl_ref, ol_ref))


def _combine(pos_flat, x1c, x1l, wts_t, mod8, ys, layer):
    cspec = pl.BlockSpec((MOE_TM, D_MODEL), lambda i, *_: (jnp.minimum(i, CTX_TILES - 1), 0))
    lspec = pl.BlockSpec((MOE_TM, D_MODEL), lambda i, *_: (jnp.maximum(i - CTX_TILES, 0), 0))
    grid_spec = pltpu.PrefetchScalarGridSpec(
        num_scalar_prefetch=1,
        grid=(N_TOK_TILES,),
        in_specs=[
            cspec, lspec,
            pl.BlockSpec((MOE_TM, 2), lambda i, *_: (i, 0)),
            _mod_spec(layer, 5, _tok_row),
            pl.BlockSpec(memory_space=pl.ANY),
        ],
        out_specs=[cspec, lspec],
        scratch_shapes=[pltpu.VMEM((2, 2, SLAB_ROWS, LANES), F32), pltpu.SemaphoreType.DMA((2,))],
    )
    return pl.pallas_call(
        _combine_kernel,
        grid_spec=grid_spec,
        out_shape=[jax.ShapeDtypeStruct((T_CTX, D_MODEL), F32),
                   jax.ShapeDtypeStruct((T_LAT, D_MODEL), F32)],
        compiler_params=_cparams(("arbitrary",)),
        name="moe_combine",
    )(pos_flat, x1c, x1l, wts_t, mod8, ys)


def _norm_tables():
    half = np.kron(np.eye(2, dtype=np.float32), np.full((DQK, DQK), 1.0 / DQK, np.float32))
    full = np.full((LANES, LANES), 1.0 / LANES, np.float32)
    return jnp.asarray(half, BF16), jnp.asarray(full, BF16)


def _rope_tables():
    half = DQK // 2
    inv = 1.0 / (ROPE_BASE ** (np.arange(0, half, 2, dtype=np.float32) / half))
    t = np.arange(DEC_SEQ)
    ang_r = (t // GRID_W).astype(np.float32)[:, None] * inv
    ang_c = (t % GRID_W).astype(np.float32)[:, None] * inv
    ang = np.concatenate([ang_r, ang_r, ang_c, ang_c], axis=-1).astype(np.float32)
    ang = np.concatenate([ang, ang], axis=-1)
    first = (np.arange(LANES) % 32) < 16
    cos, sin = np.cos(ang), np.sin(ang)
    return (jnp.asarray(cos, F32), jnp.asarray(np.where(first, -sin, 0.0), F32),
            jnp.asarray(np.where(first, 0.0, sin), F32))


def _na_mask():
    qc = np.arange(GRID_W)[:, None]
    kc = (np.arange(LANES) % GRID_W)[None, :]
    ws = np.clip(qc - NA_KC // 2, 0, GRID_W - NA_KC)
    ok = (kc >= ws) & (kc < ws + NA_KC)
    return jnp.asarray(np.where(ok, 0.0, NEG_BIG), F32)


def kernel(x_prompt, x_sample, cache_diff_k, cache_diff_v, cache_na_k, cache_na_v, c, c_ctx, norm_mix_g, norm_ffn_g, w_ada, b_ada, w_in, w_out, conv_w, conv_b, conv_ln_g, conv_ln_b, diff_qn_g, diff_kn_g, diff_lam_q1, diff_lam_k1, diff_lam_q2, diff_lam_k2, diff_subln_g, na_qn_g, na_kn_g, na_rpb, moe_wr_g, moe_br_g, moe_wr_e, moe_br_e, moe_w_gate, moe_w_up, moe_w_down):
    cvec = jnp.zeros((MOD_ROWS, D_MODEL), F32).at[0].set(c_ctx).at[1:1 + DEC_BATCH].set(c)
    ones = jnp.ones((DEPTH, SEG_W), F32)
    gains = jnp.stack(
        [jnp.tile(diff_qn_g, (1, 2 * N_HEADS)), jnp.tile(diff_kn_g, (1, 2 * N_HEADS)), ones,
         jnp.tile(na_qn_g, (1, N_HEADS)), jnp.tile(na_kn_g, (1, N_HEADS)), ones],
        axis=1).reshape(DEPTH, 6, 1, SEG_W)
    a64, a128 = _norm_tables()
    rope_tabs = _rope_tables()
    na_mask = _na_mask()
    lam_params = jnp.stack([diff_lam_q1, diff_lam_k1, diff_lam_q2, diff_lam_k2], axis=1)
    rpb_pad = jnp.pad(na_rpb, ((0, 0), (0, 0), (0, 0), (0, DQK - na_rpb.shape[-1])))
    rpb_src = jnp.concatenate([rpb_pad[:, :, :-1], rpb_pad[:, :, 1:]], axis=-1)
    rpb_src = jnp.pad(rpb_src, ((0, 0), (0, 0), (0, 16 - NA_PAIRS), (0, 0)))
    wr_t = jnp.concatenate([moe_wr_g, moe_wr_e.reshape(DEPTH, D_MODEL, N_EXPERTS)], axis=2)
    wr_t = jnp.pad(jnp.swapaxes(wr_t, 1, 2), ((0, 0), (0, ROUTE_ROWS - N_GROUPS - N_EXPERTS), (0, 0)))
    br = jnp.concatenate([moe_br_g, moe_br_e.reshape(DEPTH, N_EXPERTS)], axis=1)
    br = jnp.pad(br, ((0, 0), (0, ROUTE_ROWS - N_GROUPS - N_EXPERTS))).reshape(DEPTH, ROUTE_ROWS, 1)
    tri = jnp.asarray(np.triu(np.ones((CUM_BLK, CUM_BLK), np.float32), 1), BF16)
    norm_mix = norm_mix_g.reshape(DEPTH, 1, D_MODEL)
    norm_ffn = norm_ffn_g.reshape(DEPTH, 1, D_MODEL)
    head_major = lambda cache: jnp.transpose(cache, (0, 1, 3, 2, 4))
    ck_diff, cv_diff = head_major(cache_diff_k), head_major(cache_diff_v)
    ck_na, cv_na = head_major(cache_na_k), head_major(cache_na_v)

    mod8 = _ada_modulation(cvec, w_ada, b_ada).reshape(DEPTH * MOD_ROWS, 1, N_ADA)
    w_out_bf = _cast_bf16(w_out)

    ctx_row = lambda i: 0
    lat_row_pre = lambda i: 1 + i * PRE_TM // DEC_SEQ
    lat_row_mix = lambda i: 1 + i * MIX_TM // DEC_SEQ

    xc = x_prompt.reshape(T_CTX, D_MODEL)
    xl = x_sample.reshape(T_LAT, D_MODEL)
    caches = [[], [], [], []]
    for layer in range(DEPTH):
        hc, uc = _pre_project(xc, mod8, norm_mix, w_in, layer, ctx_row, "pre_ctx")
        hl, ul = _pre_project(xl, mod8, norm_mix, w_in, layer, lat_row_pre, "pre_lat")
        sc = _head_project(hc, w_in, gains, a64, a128, None, layer, BATCH, SEQ, "heads_ctx")
        sl = _head_project(hl, w_in, gains, a64, a128, rope_tabs, layer, DEC_BATCH, DEC_SEQ, "heads_lat")
        for slot, seg in zip(caches, (SEG_DK, SEG_DV, SEG_NK, SEG_NV)):
            slot.append(sc[seg])

        ac = _conformer_conv(uc, conv_w, conv_b, conv_ln_g, conv_ln_b, layer, BATCH, SEQ)
        al = _conformer_conv(ul, conv_w, conv_b, conv_ln_g, conv_ln_b, layer, DEC_BATCH, DEC_SEQ)
        dc, nc = _ctx_attention(sc, lam_params, diff_subln_g, layer)
        dl = _diff_attention_lat(sl, lam_params, diff_subln_g, ck_diff, cv_diff, layer, 256)
        nl = _na_attention(sl, rpb_src, na_mask, ck_na, cv_na, layer)

        x1c, ltc = _mix(ac, dc, nc, w_out_bf, xc, mod8, norm_ffn, wr_t, br, layer, ctx_row)
        x1l, ltl = _mix(al, dl, nl, w_out_bf, xl, mod8, norm_ffn, wr_t, br, layer, lat_row_mix)

        pos, wts, items = _route(jnp.concatenate([ltc, ltl], axis=1), tri)
        pos_flat = pos.reshape(2 * T_ALL)
        items = [items[r, :MOE_ITEMS] for r in range(4)] + [items[4, :1]]
        xs = _dispatch(pos_flat, x1c, x1l, mod8, norm_ffn, layer)
        ys = _experts(items, xs, moe_w_gate, moe_w_up, moe_w_down, layer)
        xc, xl = _combine(pos_flat, x1c, x1l, wts.T, mod8, ys, layer)

    outs = [jnp.transpose(jnp.stack(slot, axis=1), (0, 1, 3, 2, 4)) for slot in caches]
    return (xc.reshape(BATCH, SEQ, D_MODEL), xl.reshape(DEC_BATCH, DEC_SEQ, D_MODEL), *outs)
```

```python
import functools
import math

import numpy as np
import jax
import jax.numpy as jnp
from jax import lax
from jax.experimental import pallas as pl
from jax.experimental.pallas import tpu as pltpu

F32 = jnp.float32
BF16 = jnp.bfloat16
I32 = jnp.int32

D_MODEL = 2048
BATCH = 16
SEQ = 256
DEPTH = 2
DEC_BATCH = 2
DEC_SEQ = 1024
PAST_LEN = 256
GRID_W = 64
GRID_ROWS = DEC_SEQ // GRID_W
HEAD_DIM = 128
C_CONV = 512
CONV_K = 31
N_HEADS = 6
DQK = 64
NA_KR = 8
NA_KC = 16
ROPE_BASE = 10000.0
N_GROUPS = 4
E_PER_GROUP = 4
N_EXPERTS = 16
D_FF = 512
N_ADA = 6 * D_MODEL
W_IN_COLS = 2 * C_CONV + 6 * N_HEADS * HEAD_DIM
T_CTX = BATCH * SEQ
T_LAT = DEC_BATCH * DEC_SEQ
T_ALL = T_CTX + T_LAT

LANES = 128
MOD_ROWS = 8
PRE_TM = 512
HEADS_TM = 1024
MIX_TM = 512
MOE_TM = 256
MOE_ITEMS = 2 * T_ALL // MOE_TM + N_EXPERTS
ROUTE_ROWS = 32
CUM_BLK = 512
VMEM_LIMIT = 56 * 1024 * 1024
NEG_BIG = -1e30


def _cparams(sem):
    return pltpu.CompilerParams(dimension_semantics=sem, vmem_limit_bytes=VMEM_LIMIT)


def _sigmoid(x):
    return 1.0 / (1.0 + jnp.exp(-x))


def _silu(x):
    return x * _sigmoid(x)


def _lambda_init(layer):
    return 0.8 - 0.6 * math.exp(-0.3 * layer)


def _ada_kernel(c_ref, w_ref, b_ref, o_ref):
    c = c_ref[...]
    s = _silu(c).astype(BF16)
    o_ref[...] = jnp.dot(s, w_ref[...].astype(BF16), preferred_element_type=F32) + b_ref[...]


def _ada_modulation(cvec, w_ada, b_ada):
    tn = 1024
    return pl.pallas_call(
        _ada_kernel,
        grid=(DEPTH, N_ADA // tn),
        in_specs=[
            pl.BlockSpec((MOD_ROWS, D_MODEL), lambda l, j: (0, 0)),
            pl.BlockSpec((None, D_MODEL, tn), lambda l, j: (l, 0, j)),
            pl.BlockSpec((None, 1, tn), lambda l, j: (l, 0, j)),
        ],
        out_specs=pl.BlockSpec((None, MOD_ROWS, tn), lambda l, j: (l, 0, j)),
        out_shape=jax.ShapeDtypeStruct((DEPTH, MOD_ROWS, N_ADA), F32),
        compiler_params=_cparams(("arbitrary", "arbitrary")),
        name="ada_modulation",
    )(cvec, w_ada, b_ada.reshape(DEPTH, 1, N_ADA))


def _mod_spec(layer, chunk, row_of_block):
    return pl.BlockSpec((None, 1, D_MODEL),
                        lambda i, *_: (layer * MOD_ROWS + row_of_block(i), 0, chunk))


def _modulate(x, g, shift, scale):
    ms = jnp.mean(x * x, axis=-1, keepdims=True)
    return x * lax.rsqrt(ms + 1e-6) * (g * (1.0 + scale)) + shift


def _pre_kernel(x_ref, sh_ref, sc_ref, g_ref, w_ref, h_ref, u_ref, wbf_ref):
    @pl.when(pl.program_id(0) == 0)
    def _():
        wbf_ref[...] = w_ref[...].astype(BF16)

    h = _modulate(x_ref[...], g_ref[...], sh_ref[...], sc_ref[...]).astype(BF16)
    h_ref[...] = h
    y = jnp.dot(h, wbf_ref[...], preferred_element_type=F32)
    u_ref[...] = y[:, :C_CONV] * _sigmoid(y[:, C_CONV:])


def _pre_project(x, mod8, norm_g, w_in, layer, row_of_block, name):
    t = x.shape[0]
    return pl.pallas_call(
        _pre_kernel,
        grid=(t // PRE_TM,),
        in_specs=[
            pl.BlockSpec((PRE_TM, D_MODEL), lambda i: (i, 0)),
            _mod_spec(layer, 0, row_of_block),
            _mod_spec(layer, 1, row_of_block),
            pl.BlockSpec((None, 1, D_MODEL), lambda i: (layer, 0, 0)),
            pl.BlockSpec((None, D_MODEL, 2 * C_CONV), lambda i: (layer, 0, 0)),
        ],
        out_specs=[pl.BlockSpec((PRE_TM, D_MODEL), lambda i: (i, 0)),
                   pl.BlockSpec((PRE_TM, C_CONV), lambda i: (i, 0))],
        out_shape=[jax.ShapeDtypeStruct((t, D_MODEL), BF16), jax.ShapeDtypeStruct((t, C_CONV), F32)],
        scratch_shapes=[pltpu.VMEM((D_MODEL, 2 * C_CONV), BF16)],
        compiler_params=_cparams(("arbitrary",)),
        name=name,
    )(x, mod8, mod8, norm_g, w_in)


SEG_DQ, SEG_DK, SEG_DV, SEG_NQ, SEG_NK, SEG_NV = range(6)
SEG_W = N_HEADS * HEAD_DIM
HEADS_WBLK = 256
HEADS_RC = 512


def _heads_kernel(h_ref, w0_ref, w1_ref, w2_ref, gain_ref, a64_ref, a128_ref, *rest, rope, per_batch):
    if rope:
        cos_ref, sa_ref, sb_ref, o_ref = rest
    else:
        (o_ref,) = rest
    j = pl.program_id(0)

    def finish(seg):
        n_g = HEADS_WBLK // LANES
        for c, w_ref in enumerate((w0_ref, w1_ref, w2_ref)):
            w = w_ref[...].astype(BF16)
            gain = gain_ref[:, c * HEADS_WBLK:(c + 1) * HEADS_WBLK]
            for r0 in range(0, HEADS_TM, HEADS_RC):
                y = jnp.dot(h_ref[r0:r0 + HEADS_RC, :], w, preferred_element_type=F32)
                for g in range(n_g):
                    hd = c * n_g + g
                    yg = y[:, g * LANES:(g + 1) * LANES]
                    if seg in (SEG_NQ, SEG_NK):
                        ms = jnp.mean(yg * yg, axis=-1, keepdims=True)
                        yg = yg * lax.rsqrt(ms + 1e-6) * gain[:, g * LANES:(g + 1) * LANES]
                    elif seg in (SEG_DQ, SEG_DK):
                        sq = yg * yg
                        low = lax.broadcasted_iota(I32, sq.shape, 1) < DQK
                        s_all = jnp.sum(sq, axis=-1, keepdims=True)
                        s_lo = jnp.sum(jnp.where(low, sq, 0.0), axis=-1, keepdims=True)
                        ms = jnp.where(low, s_lo, s_all - s_lo) * (1.0 / DQK)
                        yg = yg * lax.rsqrt(ms + 1e-6) * gain[:, g * LANES:(g + 1) * LANES]
                    if rope and seg in (SEG_DQ, SEG_DK):
                        rs = slice(r0, r0 + HEADS_RC)
                        yg = (yg * cos_ref[rs, :] + pltpu.roll(yg, LANES - 16, 1) * sa_ref[rs, :]
                              + pltpu.roll(yg, 16, 1) * sb_ref[rs, :])
                    if per_batch == 1:
                        o_ref[hd, r0:r0 + HEADS_RC, :] = yg
                    else:
                        rows = HEADS_TM // per_batch
                        for q in range(HEADS_RC // rows):
                            o_ref[r0 // rows + q, hd] = yg[q * rows:(q + 1) * rows]

    for seg in range(6):
        pl.when(j == seg)(functools.partial(finish, seg))


def _head_project(h, w_in, gains, a64, a128, rope_tabs, layer, batch, seq, name):
    t = h.shape[0]
    rope = rope_tabs is not None
    per_batch = HEADS_TM // seq
    col0 = 2 * C_CONV // HEADS_WBLK
    wspec = lambda c: pl.BlockSpec((None, D_MODEL, HEADS_WBLK),
                                   lambda j, i: (layer, 0, col0 + (SEG_W // HEADS_WBLK) * j + c))
    in_specs = [
        pl.BlockSpec((HEADS_TM, D_MODEL), lambda j, i: (i, 0)),
        wspec(0), wspec(1), wspec(2),
        pl.BlockSpec((None, None, 1, SEG_W), lambda j, i: (layer, j, 0, 0)),
        pl.BlockSpec((HEADS_WBLK, HEADS_WBLK), lambda j, i: (0, 0)),
        pl.BlockSpec((HEADS_WBLK, HEADS_WBLK), lambda j, i: (0, 0)),
    ]
    args = [h, w_in, w_in, w_in, gains, a64, a128]
    if rope:
        in_specs += [pl.BlockSpec((DEC_SEQ, LANES), lambda j, i: (0, 0))] * 3
        args += list(rope_tabs)
    if per_batch == 1:
        out_spec = pl.BlockSpec((None, None, N_HEADS, seq, HEAD_DIM), lambda j, i: (j, i, 0, 0, 0))
    else:
        out_spec = pl.BlockSpec((None, per_batch, N_HEADS, seq, HEAD_DIM), lambda j, i: (j, i, 0, 0, 0))
    return pl.pallas_call(
        functools.partial(_heads_kernel, rope=rope, per_batch=per_batch),
        grid=(6, t // HEADS_TM),
        in_specs=in_specs,
        out_specs=out_spec,
        out_shape=jax.ShapeDtypeStruct((6, batch, N_HEADS, seq, HEAD_DIM), F32),
        compiler_params=_cparams(("arbitrary", "arbitrary")),
        name=name,
    )(*args)


CONV_CH = 128
CONV_HALO = 16


def _conv_kernel(u_ref, w_ref, b_ref, lg_ref, lb_ref, o_ref, win_ref, y_ref, *, seq):
    c = pl.program_id(1)
    n_chunks = seq // CONV_CH
    base = pl.multiple_of(c * CONV_CH, CONV_CH)

    win_ref[CONV_HALO:CONV_HALO + CONV_CH, :] = u_ref[pl.ds(base, CONV_CH), :]
    lo_start = pl.multiple_of(jnp.maximum(base - CONV_HALO, 0), CONV_HALO)
    hi_start = pl.multiple_of(jnp.minimum(base + CONV_CH, seq - CONV_HALO), CONV_HALO)
    lo_keep = (c > 0).astype(F32)
    hi_keep = (c < n_chunks - 1).astype(F32)
    win_ref[0:CONV_HALO, :] = u_ref[pl.ds(lo_start, CONV_HALO), :] * lo_keep
    win_ref[CONV_HALO + CONV_CH:, :] = u_ref[pl.ds(hi_start, CONV_HALO), :] * hi_keep

    off = CONV_HALO - CONV_K // 2
    sub = 8
    win_rows = CONV_CH + 2 * CONV_HALO
    for g in range(C_CONV // LANES):
        sl = slice(g * LANES, (g + 1) * LANES)
        window = win_ref[:, sl]
        acc = jnp.zeros((CONV_CH, LANES), F32) + b_ref[:, sl]
        for phase in range(sub):
            taps = [k for k in range(CONV_K) if (off + k) % sub == phase]
            if not taps:
                continue
            shifted = window if phase == 0 else pltpu.roll(window, win_rows - phase, 0)
            for k in taps:
                a = (off + k) // sub * sub
                acc = acc + shifted[a:a + CONV_CH, :] * w_ref[k:k + 1, sl]
        y_ref[:, sl] = acc

    y = y_ref[...]
    mu = jnp.mean(y, axis=-1, keepdims=True)
    yc = y - mu
    var = jnp.mean(yc * yc, axis=-1, keepdims=True)
    z = yc * lax.rsqrt(var + 1e-5) * lg_ref[...] + lb_ref[...]
    o_ref[...] = _silu(z)


def _conformer_conv(u, conv_w, conv_b, ln_g, ln_b, layer, batch, seq):
    t = u.shape[0]
    n_chunks = seq // CONV_CH
    vec = lambda: pl.BlockSpec((None, 1, C_CONV), lambda b, c: (layer, 0, 0))
    return pl.pallas_call(
        functools.partial(_conv_kernel, seq=seq),
        grid=(batch, n_chunks),
        in_specs=[
            pl.BlockSpec((seq, C_CONV), lambda b, c: (b, 0)),
            pl.BlockSpec((None, CONV_K, C_CONV), lambda b, c: (layer, 0, 0)),
            vec(), vec(), vec(),
        ],
        out_specs=pl.BlockSpec((CONV_CH, C_CONV), lambda b, c: (b * n_chunks + c, 0)),
        out_shape=jax.ShapeDtypeStruct((t, C_CONV), F32),
        scratch_shapes=[pltpu.VMEM((CONV_CH + 2 * CONV_HALO, C_CONV), F32),
                        pltpu.VMEM((CONV_CH, C_CONV), F32)],
        compiler_params=_cparams(("arbitrary", "arbitrary")),
        name="conv_%d" % seq,
    )(u, conv_w, conv_b.reshape(DEPTH, 1, C_CONV), ln_g.reshape(DEPTH, 1, C_CONV),
      ln_b.reshape(DEPTH, 1, C_CONV))


_NT = (((1,), (1,)), ((), ()))


LOG2E = 1.4426950408889634


def _softmax_rows(s):
    m = jnp.max(s, axis=-1, keepdims=True)
    e = jnp.exp(s - m)
    return e / jnp.sum(e, axis=-1, keepdims=True)


def _exp2_rows(s2):
    e = jnp.exp2(s2 - jnp.max(s2, axis=-1, keepdims=True))
    return e, 1.0 / jnp.sum(e, axis=-1, keepdims=True)


def _diff_lambda(lam_ref, lam_init):
    lv = lam_ref[...]
    return (jnp.exp(jnp.sum(lv[0:1] * lv[1:2], axis=-1, keepdims=True))
            - jnp.exp(jnp.sum(lv[2:3] * lv[3:4], axis=-1, keepdims=True)) + lam_init)


def _diff_head(q, k, v, lam, g, lam_init, long_keys):
    lane = lax.broadcasted_iota(I32, q.shape, 1)
    if long_keys:
        qs = q * (DQK ** -0.5 * LOG2E)
        q0 = jnp.where(lane < DQK, qs, 0.0).astype(BF16)
        q1 = jnp.where(lane >= DQK, qs, 0.0).astype(BF16)
        e0, r0 = _exp2_rows(lax.dot_general(q0, k, _NT, preferred_element_type=F32))
        e1, r1 = _exp2_rows(lax.dot_general(q1, k, _NT, preferred_element_type=F32))
        o = (jnp.dot(e0.astype(BF16), v, preferred_element_type=F32) * r0
             - jnp.dot(e1.astype(BF16), v, preferred_element_type=F32) * (lam * r1))
    else:
        q0 = jnp.where(lane < DQK, q, 0.0).astype(BF16)
        q1 = jnp.where(lane >= DQK, q, 0.0).astype(BF16)
        scale = DQK ** -0.5
        p0 = _softmax_rows(lax.dot_general(q0, k, _NT, preferred_element_type=F32) * scale)
        p1 = _softmax_rows(lax.dot_general(q1, k, _NT, preferred_element_type=F32) * scale)
        o = jnp.dot((p0 - lam * p1).astype(BF16), v, preferred_element_type=F32)
    ms = jnp.mean(o * o, axis=-1, keepdims=True)
    return o * lax.rsqrt(ms + 1e-5) * g * (1.0 - lam_init)


def _seg_spec(seg, rows, index):
    def imap(*ids):
        b, h, r = index(*ids)
        return (seg, b, h, r, 0)
    return pl.BlockSpec((None, None, None, rows, HEAD_DIM), imap)


def _diff_attn_lat_kernel(lam_ref, g_ref, q_ref, k_ref, v_ref, ck_ref, cv_ref, o_ref, *, lam_init):
    k = jnp.concatenate([k_ref[...], ck_ref[...]], axis=0).astype(BF16)
    v = jnp.concatenate([v_ref[...], cv_ref[...]], axis=0).astype(BF16)
    lam = _diff_lambda(lam_ref, lam_init)
    o_ref[...] = _diff_head(q_ref[...], k, v, lam, g_ref[...], lam_init, True)


def _diff_attention_lat(segs, lam_params, subln_g, cache_k, cache_v, layer, tq):
    nq = DEC_SEQ // tq
    cspec = pl.BlockSpec((None, None, None, PAST_LEN, HEAD_DIM), lambda b, h, qi: (b, layer, h, 0, 0))
    return pl.pallas_call(
        functools.partial(_diff_attn_lat_kernel, lam_init=_lambda_init(layer)),
        grid=(DEC_BATCH, N_HEADS, nq),
        in_specs=[
            pl.BlockSpec((None, 4, DQK), lambda b, h, qi: (layer, 0, 0)),
            pl.BlockSpec((None, 1, HEAD_DIM), lambda b, h, qi: (layer, 0, 0)),
            _seg_spec(SEG_DQ, tq, lambda b, h, qi: (b, h, qi)),
            _seg_spec(SEG_DK, DEC_SEQ, lambda b, h, qi: (b, h, 0)),
            _seg_spec(SEG_DV, DEC_SEQ, lambda b, h, qi: (b, h, 0)),
            cspec, cspec,
        ],
        out_specs=pl.BlockSpec((tq, HEAD_DIM), lambda b, h, qi: (b * nq + qi, h)),
        out_shape=jax.ShapeDtypeStruct((T_LAT, SEG_W), F32),
        compiler_params=_cparams(("arbitrary", "arbitrary", "arbitrary")),
        name="diff_attn_lat",
    )(lam_params, subln_g.reshape(DEPTH, 1, HEAD_DIM), segs, segs, segs, cache_k, cache_v)


def _ctx_attn_kernel(lam_ref, g_ref, dq_ref, dk_ref, dv_ref, nq_ref, nk_ref, nv_ref, d_ref, n_ref, *,
                     lam_init):
    lam = _diff_lambda(lam_ref, lam_init)
    for h in range(N_HEADS):
        sl = slice(h * HEAD_DIM, (h + 1) * HEAD_DIM)
        d_ref[:, sl] = _diff_head(dq_ref[h], dk_ref[h].astype(BF16), dv_ref[h].astype(BF16), lam,
                                  g_ref[...], lam_init, False)
        s = lax.dot_general(nq_ref[h].astype(BF16), nk_ref[h].astype(BF16), _NT,
                            preferred_element_type=F32) * (HEAD_DIM ** -0.5)
        n_ref[:, sl] = jnp.dot(_softmax_rows(s).astype(BF16), nv_ref[h].astype(BF16),
                               preferred_element_type=F32)


def _ctx_attention(segs, lam_params, subln_g, layer):
    seg = lambda s: pl.BlockSpec((None, None, N_HEADS, SEQ, HEAD_DIM), lambda b: (s, b, 0, 0, 0))
    out = pl.BlockSpec((SEQ, SEG_W), lambda b: (b, 0))
    return pl.pallas_call(
        functools.partial(_ctx_attn_kernel, lam_init=_lambda_init(layer)),
        grid=(BATCH,),
        in_specs=[pl.BlockSpec((None, 4, DQK), lambda b: (layer, 0, 0)),
                  pl.BlockSpec((None, 1, HEAD_DIM), lambda b: (layer, 0, 0))]
                 + [seg(s) for s in range(6)],
        out_specs=[out, out],
        out_shape=[jax.ShapeDtypeStruct((T_CTX, SEG_W), F32)] * 2,
        compiler_params=_cparams(("arbitrary",)),
        name="ctx_attn",
    )(lam_params, subln_g.reshape(DEPTH, 1, HEAD_DIM), *([segs] * 6))


NA_PAIRS = 2 * NA_KR - 2
NA_QROWS = 4
NA_KROWS = 12


def _na_key_block(chunk):
    first = min(max(chunk * NA_QROWS - NA_KR // 2, 0), GRID_ROWS - NA_KR)
    return min(first - first % 2, GRID_ROWS - NA_KROWS)


def _na_attn_kernel(src_ref, mask_ref, q_ref, k_ref, v_ref, ck_ref, cv_ref, o_ref, bias_ref):
    for d in range(NA_PAIRS):
        row = jnp.broadcast_to(src_ref[d:d + 1, :] * LOG2E, (GRID_W, LANES))
        tile = pltpu.roll(row, LANES - (NA_KC - 1), 1, stride=1, stride_axis=0)
        for v in range(3):
            bias_ref[v, d] = tile + mask_ref[v]
    dead = jnp.full((GRID_W, LANES), NEG_BIG, F32)
    ck = ck_ref[...].astype(BF16)
    cv = cv_ref[...].astype(BF16)
    scale = HEAD_DIM ** -0.5 * LOG2E
    for chunk in range(GRID_ROWS // NA_QROWS):
        kb = _na_key_block(chunk)
        rows = []
        for qr in range(chunk * NA_QROWS, (chunk + 1) * NA_QROWS):
            start = min(max(qr - NA_KR // 2, 0), GRID_ROWS - NA_KR)
            assert kb <= start and start + NA_KR <= kb + NA_KROWS
            tiles = []
            for m in range(NA_KROWS // 2):
                r0 = kb + 2 * m
                live0 = start <= r0 < start + NA_KR
                live1 = start <= r0 + 1 < start + NA_KR
                d = r0 - qr + NA_KR - 1
                assert not (live0 or live1) or 0 <= d < NA_PAIRS
                if live0 and live1:
                    tiles.append(bias_ref[0, d])
                elif live0:
                    tiles.append(bias_ref[1, d])
                elif live1:
                    tiles.append(bias_ref[2, d])
                else:
                    tiles.append(dead)
            rows.append(jnp.concatenate(tiles, axis=1))
        bias = jnp.concatenate(rows, axis=0)
        qs = slice(chunk * NA_QROWS * GRID_W, (chunk + 1) * NA_QROWS * GRID_W)
        ks = slice(kb * GRID_W, (kb + NA_KROWS) * GRID_W)
        q = (q_ref[qs, :] * scale).astype(BF16)
        kl = k_ref[ks, :].astype(BF16)
        vl = v_ref[ks, :].astype(BF16)
        s_loc = lax.dot_general(q, kl, _NT, preferred_element_type=F32) + bias
        s_ctx = lax.dot_general(q, ck, _NT, preferred_element_type=F32)
        m = jnp.maximum(jnp.max(s_loc, axis=-1, keepdims=True), jnp.max(s_ctx, axis=-1, keepdims=True))
        e_loc = jnp.exp2(s_loc - m)
        e_ctx = jnp.exp2(s_ctx - m)
        r = 1.0 / (jnp.sum(e_loc, axis=-1, keepdims=True) + jnp.sum(e_ctx, axis=-1, keepdims=True))
        o_ref[qs, :] = (jnp.dot(e_loc.astype(BF16), vl, preferred_element_type=F32)
                        + jnp.dot(e_ctx.astype(BF16), cv, preferred_element_type=F32)) * r


def _na_attention(segs, rpb_src, na_mask, cache_k, cache_v, layer):
    blk = lambda s: _seg_spec(s, DEC_SEQ, lambda b, h: (b, h, 0))
    cspec = pl.BlockSpec((None, None, None, PAST_LEN, HEAD_DIM), lambda b, h: (b, layer, h, 0, 0))
    return pl.pallas_call(
        _na_attn_kernel,
        grid=(DEC_BATCH, N_HEADS),
        in_specs=[
            pl.BlockSpec((None, None, 16, LANES), lambda b, h: (layer, h, 0, 0)),
            pl.BlockSpec((3, GRID_W, LANES), lambda b, h: (0, 0, 0)),
            blk(SEG_NQ), blk(SEG_NK), blk(SEG_NV), cspec, cspec,
        ],
        out_specs=pl.BlockSpec((DEC_SEQ, HEAD_DIM), lambda b, h: (b, h)),
        out_shape=jax.ShapeDtypeStruct((T_LAT, SEG_W), F32),
        scratch_shapes=[pltpu.VMEM((3, NA_PAIRS, GRID_W, LANES), F32)],
        compiler_params=_cparams(("arbitrary", "arbitrary")),
        name="na_attn_lat",
    )(rpb_src, na_mask, segs, segs, segs, cache_k, cache_v)


def _split_f32(x, n):
    terms = []
    for _ in range(n):
        t = x.astype(BF16).astype(F32)
        terms.append(t)
        x = x - t
    return terms


def _mix_kernel(a_ref, d_ref, n_ref, w_ref, x_ref, g1_ref, sh_ref, sc_ref, ng_ref, wr_ref, br_ref,
                x1_ref, lt_ref):
    k0, k1 = C_CONV, C_CONV + N_HEADS * HEAD_DIM
    mixed = (jnp.dot(a_ref[...].astype(BF16), w_ref[0:k0, :], preferred_element_type=F32)
             + jnp.dot(d_ref[...].astype(BF16), w_ref[k0:k1, :], preferred_element_type=F32)
             + jnp.dot(n_ref[...].astype(BF16), w_ref[k1:, :], preferred_element_type=F32))
    x1 = x_ref[...] + g1_ref[...] * mixed
    x1_ref[...] = x1
    h2 = _modulate(x1, ng_ref[...], sh_ref[...], sc_ref[...])
    wh, wm, wl = _split_f32(wr_ref[...], 3)
    w_cat = (wh + pltpu.roll(wm, ROUTE_ROWS, 1) + pltpu.roll(wl, 2 * ROUTE_ROWS, 1)).astype(BF16)
    hh, hm = _split_f32(h2, 2)
    r = jnp.dot(jnp.concatenate([hh, hm], axis=0).astype(BF16), w_cat, preferred_element_type=F32)
    rh, rm = r[:MIX_TM], r[MIX_TM:]
    back = lambda v, groups: pltpu.roll(v, LANES - groups * ROUTE_ROWS, 1)
    small = (back(rm, 1) + back(rh, 2)) + (back(rh, 1) + rm)
    logits = (small + rh).T[:ROUTE_ROWS, :]
    lt_ref[...] = logits + br_ref[...]


def _mix(a_out, d_out, n_out, w_out_bf, x, mod8, norm_g, wr_t, br, layer, row_of_block):
    t = x.shape[0]
    rows = lambda width: pl.BlockSpec((MIX_TM, width), lambda i: (i, 0))
    return pl.pallas_call(
        _mix_kernel,
        grid=(t // MIX_TM,),
        in_specs=[
            rows(C_CONV), rows(N_HEADS * HEAD_DIM), rows(N_HEADS * HEAD_DIM),
            pl.BlockSpec((None, D_MODEL, D_MODEL), lambda i: (layer, 0, 0)),
            rows(D_MODEL),
            _mod_spec(layer, 2, row_of_block),
            _mod_spec(layer, 3, row_of_block),
            _mod_spec(layer, 4, row_of_block),
            pl.BlockSpec((None, 1, D_MODEL), lambda i: (layer, 0, 0)),
            pl.BlockSpec((None, D_MODEL, LANES), lambda i: (layer, 0, 0)),
            pl.BlockSpec((None, ROUTE_ROWS, 1), lambda i: (layer, 0, 0)),
        ],
        out_specs=[rows(D_MODEL), pl.BlockSpec((ROUTE_ROWS, MIX_TM), lambda i: (0, i))],
        out_shape=[jax.ShapeDtypeStruct((t, D_MODEL), F32),
                   jax.ShapeDtypeStruct((ROUTE_ROWS, t), F32)],
        compiler_params=_cparams(("arbitrary",)),
        name="mix_%d" % t,
    )(a_out, d_out, n_out, w_out_bf, x, mod8, mod8, mod8, norm_g, wr_t, br)


CACHE_SEGS = (SEG_DK, SEG_DV, SEG_NK, SEG_NV)


def _stack_kernel(*refs):
    ins, outs = refs[:2 * len(CACHE_SEGS)], refs[2 * len(CACHE_SEGS):]
    for n, o_ref in enumerate(outs):
        for layer in range(DEPTH):
            o_ref[layer] = ins[DEPTH * n + layer][...]


def _stack_caches(seg_arrays):
    in_specs, args = [], []
    for seg in CACHE_SEGS:
        for layer in range(DEPTH):
            in_specs.append(pl.BlockSpec((None, None, N_HEADS, SEQ, HEAD_DIM),
                                         lambda b, seg=seg: (seg, b, 0, 0, 0)))
            args.append(seg_arrays[layer])
    out_spec = pl.BlockSpec((None, DEPTH, N_HEADS, SEQ, HEAD_DIM), lambda b: (b, 0, 0, 0, 0))
    return pl.pallas_call(
        _stack_kernel,
        grid=(BATCH,),
        in_specs=in_specs,
        out_specs=[out_spec] * len(CACHE_SEGS),
        out_shape=[jax.ShapeDtypeStruct((BATCH, DEPTH, N_HEADS, SEQ, HEAD_DIM), F32)] * len(CACHE_SEGS),
        compiler_params=_cparams(("arbitrary",)),
        name="stack_caches",
    )(*args)


def _cast_kernel(x_ref, o_ref):
    o_ref[...] = x_ref[...].astype(BF16)


def _cast_bf16(w):
    tm = 512
    spec = pl.BlockSpec((None, tm, D_MODEL), lambda l, i: (l, i, 0))
    return pl.pallas_call(
        _cast_kernel, grid=(DEPTH, D_MODEL // tm), in_specs=[spec], out_specs=spec,
        out_shape=jax.ShapeDtypeStruct(w.shape, BF16),
        compiler_params=_cparams(("arbitrary", "arbitrary")), name="cast_w_out",
    )(w)


def _route_kernel(lt_ref, tri_ref, pos_ref, wt_ref, item_ref):
    t = T_ALL
    lg = lt_ref[0:N_GROUPS, :]
    eg = jnp.exp(lg - jnp.max(lg, axis=0, keepdims=True))
    pg = eg / jnp.sum(eg, axis=0, keepdims=True)
    pg_top = jnp.max(pg, axis=0, keepdims=True)
    gi = lax.broadcasted_iota(I32, pg.shape, 0).astype(F32)
    g_idx = jnp.min(jnp.where(pg == pg_top, gi, float(N_GROUPS)), axis=0, keepdims=True)

    le = jnp.zeros((E_PER_GROUP, t), F32)
    for g in range(N_GROUPS):
        rows = lt_ref[N_GROUPS + g * E_PER_GROUP:N_GROUPS + (g + 1) * E_PER_GROUP, :]
        le = jnp.where(g_idx == float(g), rows, le)
    ee = jnp.exp(le - jnp.max(le, axis=0, keepdims=True))
    pe = ee / jnp.sum(ee, axis=0, keepdims=True)
    ei = lax.broadcasted_iota(I32, pe.shape, 0).astype(F32)
    p1 = jnp.max(pe, axis=0, keepdims=True)
    i1 = jnp.min(jnp.where(pe == p1, ei, float(E_PER_GROUP)), axis=0, keepdims=True)
    pe_rest = jnp.where(ei == i1, -1.0, pe)
    p2 = jnp.max(pe_rest, axis=0, keepdims=True)
    i2 = jnp.min(jnp.where(pe_rest == p2, ei, float(E_PER_GROUP)), axis=0, keepdims=True)
    den = p1 + p2
    wt_ref[0:1, :] = pg_top * (p1 / den)
    wt_ref[1:2, :] = pg_top * (p2 / den)
    e1 = g_idx * E_PER_GROUP + i1
    e2 = g_idx * E_PER_GROUP + i2

    erow = lax.broadcasted_iota(I32, (N_EXPERTS, t), 0).astype(F32)
    oh1 = (erow == e1).astype(F32)
    oh2 = (erow == e2).astype(F32)
    cnt = oh1 + oh2
    carry = jnp.zeros((N_EXPERTS, 1), F32)
    ranks = []
    for b in range(t // CUM_BLK):
        blk = cnt[:, b * CUM_BLK:(b + 1) * CUM_BLK]
        ranks.append(jnp.dot(blk.astype(BF16), tri_ref[...], preferred_element_type=F32) + carry)
        carry = carry + jnp.sum(blk, axis=1, keepdims=True)
    rank = jnp.concatenate(ranks, axis=1)

    erow_l = lax.broadcasted_iota(I32, (N_EXPERTS, LANES), 0)

    def excl_scan(v):
        inc = v
        for s in (1, 2, 4, 8):
            inc = inc + jnp.where(erow_l >= s, pltpu.roll(inc, s, 0), 0.0)
        return inc - v

    total = jnp.broadcast_to(carry, (N_EXPERTS, LANES))
    offs = excl_scan(total)
    pos_ref[0:1, :] = jnp.sum(oh1 * (rank + offs[:, 0:1]), axis=0, keepdims=True).astype(I32)
    pos_ref[1:2, :] = jnp.sum(oh2 * (rank + offs[:, 0:1]), axis=0, keepdims=True).astype(I32)

    shift = int(math.log2(MOE_TM))
    offs_i = offs.astype(I32)
    total_i = total.astype(I32)
    first_tile = lax.shift_right_logical(offs_i, shift)
    last_tile = lax.shift_right_logical(offs_i + total_i - 1, shift)
    n_item = jnp.where(total_i > 0, last_tile - first_tile + 1, 0).astype(F32)
    item_start = excl_scan(n_item)
    item_end = item_start + n_item
    kk = lax.broadcasted_iota(I32, (N_EXPERTS, LANES), 1).astype(F32)
    item_e = jnp.minimum(jnp.sum((item_end <= kk).astype(F32), axis=0, keepdims=True),
                         float(N_EXPERTS - 1))
    sel = (erow_l.astype(F32) == item_e).astype(F32)
    pick = lambda v: jnp.sum(sel * v, axis=0, keepdims=True)
    item_ref[0:1, :] = (kk[0:1] + pick(first_tile.astype(F32) - item_start)).astype(I32)
    item_ref[1:2, :] = item_e.astype(I32)
    item_ref[2:3, :] = pick(offs).astype(I32)
    item_ref[3:4, :] = pick(offs + total).astype(I32)
    item_ref[4:5, :] = item_end[N_EXPERTS - 1:N_EXPERTS, :].astype(I32)
    item_ref[5:8, :] = jnp.zeros((3, LANES), I32)


def _route(lt_all, tri):
    full = lambda shape: pl.BlockSpec(shape, lambda i: (0,) * len(shape))
    return pl.pallas_call(
        _route_kernel,
        grid=(1,),
        in_specs=[full((ROUTE_ROWS, T_ALL)), full((CUM_BLK, CUM_BLK))],
        out_specs=[full((2, T_ALL)), full((2, T_ALL)), full((8, LANES))],
        out_shape=[jax.ShapeDtypeStruct((2, T_ALL), I32), jax.ShapeDtypeStruct((2, T_ALL), F32),
                   jax.ShapeDtypeStruct((8, LANES), I32)],
        compiler_params=_cparams(("arbitrary",)),
        name="moe_route",
    )(lt_all, tri)


N_TOK_TILES = T_ALL // MOE_TM
CTX_TILES = T_CTX // MOE_TM


def _tok_row(i):
    return jnp.where(i < CTX_TILES, 0, 1 + (i - CTX_TILES) // (DEC_SEQ // MOE_TM))


N_CHUNK = D_MODEL // LANES
MOE_PITCH = N_CHUNK + 1
SLAB_ROWS = MOE_TM * MOE_PITCH
ISSUE_UNROLL = 4


def _slab_copy(src, src_tok, dst, dst_tok, sem, pitch):
    return pltpu.make_async_copy(src.at[pl.ds(src_tok * pitch, pitch)],
                                 dst.at[pl.ds(dst_tok * pitch, pitch)], sem)


def _to_slabs(ref, value):
    for c in range(N_CHUNK):
        ref[pl.ds(c, MOE_TM, stride=MOE_PITCH), :] = value[:, c * LANES:(c + 1) * LANES]
    ref[pl.ds(N_CHUNK, MOE_TM, stride=MOE_PITCH), :] = jnp.zeros((MOE_TM, LANES), F32)


def _slab_chunk(ref, c):
    return ref[pl.ds(c, MOE_TM, stride=MOE_PITCH), :]


def _dispatch_kernel(pos_ref, xc_ref, xl_ref, sh_ref, sc_ref, ng_ref, xs_ref, h_ref, sem):
    i = pl.program_id(0)
    slot = i % 2
    buf = h_ref.at[slot]

    def drain(s):
        def body(r, carry):
            _slab_copy(h_ref.at[s], 0, xs_ref, 0, sem.at[s], MOE_PITCH).wait()
            return carry
        lax.fori_loop(0, 2 * MOE_TM, body, 0, unroll=ISSUE_UNROLL)

    @pl.when(i >= 2)
    def _():
        drain(slot)

    @pl.when(i < CTX_TILES)
    def _():
        _to_slabs(buf, _modulate(xc_ref[...], ng_ref[...], sh_ref[...], sc_ref[...]))

    @pl.when(i >= CTX_TILES)
    def _():
        _to_slabs(buf, _modulate(xl_ref[...], ng_ref[...], sh_ref[...], sc_ref[...]))

    def issue(r, carry):
        tok = i * MOE_TM + r
        _slab_copy(buf, r, xs_ref, pos_ref[tok], sem.at[slot], MOE_PITCH).start(priority=0)
        _slab_copy(buf, r, xs_ref, pos_ref[T_ALL + tok], sem.at[slot], MOE_PITCH).start(priority=1)
        return carry

    lax.fori_loop(0, MOE_TM, issue, 0, unroll=ISSUE_UNROLL)

    @pl.when(i == N_TOK_TILES - 1)
    def _():
        drain(1 - slot)
        drain(slot)


def _dispatch(pos_flat, x1c, x1l, mod8, norm_g, layer):
    grid_spec = pltpu.PrefetchScalarGridSpec(
        num_scalar_prefetch=1,
        grid=(N_TOK_TILES,),
        in_specs=[
            pl.BlockSpec((MOE_TM, D_MODEL), lambda i, *_: (jnp.minimum(i, CTX_TILES - 1), 0)),
            pl.BlockSpec((MOE_TM, D_MODEL), lambda i, *_: (jnp.maximum(i - CTX_TILES, 0), 0)),
            _mod_spec(layer, 3, _tok_row),
            _mod_spec(layer, 4, _tok_row),
            pl.BlockSpec((None, 1, D_MODEL), lambda i, *_: (layer, 0, 0)),
        ],
        out_specs=pl.BlockSpec(memory_space=pl.ANY),
        scratch_shapes=[pltpu.VMEM((2, SLAB_ROWS, LANES), F32), pltpu.SemaphoreType.DMA((2,))],
    )
    return pl.pallas_call(
        _dispatch_kernel,
        grid_spec=grid_spec,
        out_shape=jax.ShapeDtypeStruct((2 * T_ALL * MOE_PITCH, LANES), F32),
        compiler_params=_cparams(("arbitrary",)),
        name="moe_dispatch",
    )(pos_flat, x1c, x1l, mod8, mod8, norm_g)


def _expert_kernel(tile_ref, exp_ref, lo_ref, hi_ref, n_ref, xs_ref, wg_hbm, wu_hbm, wd_hbm, ys_ref,
                   wg_buf, wu_buf, wd_buf, wg_bf, wu_bf, wd_bf, slot_ref, sem, *, layer):
    k = pl.program_id(0)
    n = n_ref[0]

    def weight_copies(e, s):
        return (pltpu.make_async_copy(wg_hbm.at[layer, e], wg_buf.at[s], sem.at[s, 0]),
                pltpu.make_async_copy(wu_hbm.at[layer, e], wu_buf.at[s], sem.at[s, 1]),
                pltpu.make_async_copy(wd_hbm.at[layer, e], wd_buf.at[s], sem.at[s, 2]))

    @pl.when(k == 0)
    def _():
        slot_ref[0] = 0
        for cp in weight_copies(exp_ref[0], 0):
            cp.start()

    @pl.when(k < n)
    def _():
        e = exp_ref[k]
        new_expert = jnp.logical_or(k == 0, exp_ref[jnp.maximum(k - 1, 0)] != e)

        @pl.when(new_expert)
        def _():
            s = slot_ref[0]
            for cp in weight_copies(e, s):
                cp.wait()
            wg_bf[...] = wg_buf[s].astype(BF16)
            wu_bf[...] = wu_buf[s].astype(BF16)
            wd_bf[...] = wd_buf[s].astype(BF16)
            nxt = lax.while_loop(lambda j: jnp.logical_and(j < n, exp_ref[jnp.minimum(j, n - 1)] == e),
                                 lambda j: j + 1, k + 1)

            @pl.when(nxt < n)
            def _():
                for cp in weight_copies(exp_ref[jnp.minimum(nxt, n - 1)], 1 - s):
                    cp.start()
            slot_ref[0] = 1 - s

        x = jnp.concatenate([_slab_chunk(xs_ref, c) for c in range(N_CHUNK)], axis=1).astype(BF16)
        gate = jnp.dot(x, wg_bf[...], preferred_element_type=F32)
        up = jnp.dot(x, wu_bf[...], preferred_element_type=F32)
        hid = (_silu(gate) * up).astype(BF16)
        y = jnp.dot(hid, wd_bf[...], preferred_element_type=F32)
        row0 = tile_ref[k] * MOE_TM
        row = row0 + lax.broadcasted_iota(I32, (MOE_TM, 1), 0)
        mine = jnp.logical_and(row >= lo_ref[k], row < hi_ref[k])
        first = lo_ref[k] <= row0

        @pl.when(first)
        def _():
            _to_slabs(ys_ref, jnp.where(mine, y, 0.0))

        @pl.when(jnp.logical_not(first))
        def _():
            for c in range(N_CHUNK):
                old = _slab_chunk(ys_ref, c)
                ys_ref[pl.ds(c, MOE_TM, stride=MOE_PITCH), :] = jnp.where(
                    mine, y[:, c * LANES:(c + 1) * LANES], old)


def _experts(items, xs, w_gate, w_up, w_down, layer):
    item_tile, item_exp, item_lo, item_hi, n_items = items
    cur = lambda k, n: jnp.minimum(k, n[0] - 1)
    rows = pl.BlockSpec((SLAB_ROWS, LANES), lambda k, t, e, lo, hi, n: (t[cur(k, n)], 0))
    hbm = pl.BlockSpec(memory_space=pl.ANY)
    grid_spec = pltpu.PrefetchScalarGridSpec(
        num_scalar_prefetch=5, grid=(MOE_ITEMS,),
        in_specs=[rows, hbm, hbm, hbm], out_specs=rows,
        scratch_shapes=[
            pltpu.VMEM((2, D_MODEL, D_FF), F32), pltpu.VMEM((2, D_MODEL, D_FF), F32),
            pltpu.VMEM((2, D_FF, D_MODEL), F32),
            pltpu.VMEM((D_MODEL, D_FF), BF16), pltpu.VMEM((D_MODEL, D_FF), BF16),
            pltpu.VMEM((D_FF, D_MODEL), BF16),
            pltpu.SMEM((1,), I32), pltpu.SemaphoreType.DMA((2, 3)),
        ])
    return pl.pallas_call(
        functools.partial(_expert_kernel, layer=layer),
        grid_spec=grid_spec,
        out_shape=jax.ShapeDtypeStruct((2 * T_ALL * MOE_PITCH, LANES), F32),
        compiler_params=_cparams(("arbitrary",)),
        name="moe_experts",
    )(item_tile, item_exp, item_lo, item_hi, n_items, xs, w_gate, w_up, w_down)


def _combine_kernel(pos_ref, xc_ref, xl_ref, wt_ref, g2_ref, ys_ref, oc_ref, ol_ref, y_ref, sem):
    i = pl.program_id(0)
    slot = i % 2

    def gather(tile, s):
        def body(r, carry):
            tok = tile * MOE_TM + r
            _slab_copy(ys_ref, pos_ref[tok], y_ref.at[s, 0], r, sem.at[s], MOE_PITCH).start(priority=0)
            _slab_copy(ys_ref, pos_ref[T_ALL + tok], y_ref.at[s, 1], r, sem.at[s], MOE_PITCH).start(priority=1)
            return carry
        lax.fori_loop(0, MOE_TM, body, 0, unroll=ISSUE_UNROLL)

    @pl.when(i == 0)
    def _():
        gather(0, 0)

    @pl.when(i + 1 < N_TOK_TILES)
    def _():
        gather(i + 1, 1 - slot)

    def drain(r, carry):
        _slab_copy(ys_ref, 0, y_ref.at[slot, 0], 0, sem.at[slot], MOE_PITCH).wait()
        return carry

    lax.fori_loop(0, 2 * MOE_TM, drain, 0, unroll=ISSUE_UNROLL)

    def write(x_ref, o_ref):
        w0 = wt_ref[:, 0:1]
        w1 = wt_ref[:, 1:2]
        for c in range(N_CHUNK):
            sl = slice(c * LANES, (c + 1) * LANES)
            moe = w0 * _slab_chunk(y_ref.at[slot, 0], c) + w1 * _slab_chunk(y_ref.at[slot, 1], c)
            o_ref[:, sl] = x_ref[:, sl] + g2_ref[:, sl] * moe

    pl.when(i < CTX_TILES)(lambda: write(xc_ref, oc_ref))
    pl.when(i >= CTX_TILES)(lambda: write(xl_ref, ol_ref))


def _combine(pos_flat, x1c, x1l, wts_t, mod8, ys, layer):
    cspec = pl.BlockSpec((MOE_TM, D_MODEL), lambda i, *_: (jnp.minimum(i, CTX_TILES - 1), 0))
    lspec = pl.BlockSpec((MOE_TM, D_MODEL), lambda i, *_: (jnp.maximum(i - CTX_TILES, 0), 0))
    grid_spec = pltpu.PrefetchScalarGridSpec(
        num_scalar_prefetch=1,
        grid=(N_TOK_TILES,),
        in_specs=[
            cspec, lspec,
            pl.BlockSpec((MOE_TM, 2), lambda i, *_: (i, 0)),
            _mod_spec(layer, 5, _tok_row),
            pl.BlockSpec(memory_space=pl.ANY),
        ],
        out_specs=[cspec, lspec],
        scratch_shapes=[pltpu.VMEM((2, 2, SLAB_ROWS, LANES), F32), pltpu.SemaphoreType.DMA((2,))],
    )
    return pl.pallas_call(
        _combine_kernel,
        grid_spec=grid_spec,
        out_shape=[jax.ShapeDtypeStruct((T_CTX, D_MODEL), F32),
                   jax.ShapeDtypeStruct((T_LAT, D_MODEL), F32)],
        compiler_params=_cparams(("arbitrary",)),
        name="moe_combine",
    )(pos_flat, x1c, x1l, wts_t, mod8, ys)


def _norm_tables():
    mean_blocks = lambda w: np.kron(np.eye(HEADS_WBLK // w, dtype=np.float32),
                                    np.full((w, w), 1.0 / w, np.float32))
    return jnp.asarray(mean_blocks(DQK), BF16), jnp.asarray(mean_blocks(HEAD_DIM), BF16)


def _rope_tables():
    half = DQK // 2
    inv = 1.0 / (ROPE_BASE ** (np.arange(0, half, 2, dtype=np.float32) / half))
    t = np.arange(DEC_SEQ)
    ang_r = (t // GRID_W).astype(np.float32)[:, None] * inv
    ang_c = (t % GRID_W).astype(np.float32)[:, None] * inv
    ang = np.concatenate([ang_r, ang_r, ang_c, ang_c], axis=-1).astype(np.float32)
    ang = np.concatenate([ang, ang], axis=-1)
    first = (np.arange(LANES) % 32) < 16
    cos, sin = np.cos(ang), np.sin(ang)
    return (jnp.asarray(cos, F32), jnp.asarray(np.where(first, -sin, 0.0), F32),
            jnp.asarray(np.where(first, 0.0, sin), F32))


def _na_mask():
    qc = np.arange(GRID_W)[:, None]
    kc = (np.arange(LANES) % GRID_W)[None, :]
    ws = np.clip(qc - NA_KC // 2, 0, GRID_W - NA_KC)
    ok = (kc >= ws) & (kc < ws + NA_KC)
    first = (np.arange(LANES) < GRID_W)[None, :]
    masks = [ok, ok & first, ok & ~first]
    return jnp.asarray(np.stack([np.where(m, 0.0, NEG_BIG) for m in masks]), F32)


def kernel(x_prompt, x_sample, cache_diff_k, cache_diff_v, cache_na_k, cache_na_v, c, c_ctx, norm_mix_g, norm_ffn_g, w_ada, b_ada, w_in, w_out, conv_w, conv_b, conv_ln_g, conv_ln_b, diff_qn_g, diff_kn_g, diff_lam_q1, diff_lam_k1, diff_lam_q2, diff_lam_k2, diff_subln_g, na_qn_g, na_kn_g, na_rpb, moe_wr_g, moe_br_g, moe_wr_e, moe_br_e, moe_w_gate, moe_w_up, moe_w_down):
    cvec = jnp.zeros((MOD_ROWS, D_MODEL), F32).at[0].set(c_ctx).at[1:1 + DEC_BATCH].set(c)
    ones = jnp.ones((DEPTH, SEG_W), F32)
    gains = jnp.stack(
        [jnp.tile(diff_qn_g, (1, 2 * N_HEADS)), jnp.tile(diff_kn_g, (1, 2 * N_HEADS)), ones,
         jnp.tile(na_qn_g, (1, N_HEADS)), jnp.tile(na_kn_g, (1, N_HEADS)), ones],
        axis=1).reshape(DEPTH, 6, 1, SEG_W)
    a64, a128 = _norm_tables()
    rope_tabs = _rope_tables()
    na_mask = _na_mask()
    lam_params = jnp.stack([diff_lam_q1, diff_lam_k1, diff_lam_q2, diff_lam_k2], axis=1)
    rpb_pad = jnp.pad(na_rpb, ((0, 0), (0, 0), (0, 0), (0, DQK - na_rpb.shape[-1])))
    rpb_src = jnp.concatenate([rpb_pad[:, :, :-1], rpb_pad[:, :, 1:]], axis=-1)
    rpb_src = jnp.pad(rpb_src, ((0, 0), (0, 0), (0, 16 - NA_PAIRS), (0, 0)))
    wr_t = jnp.concatenate([moe_wr_g, moe_wr_e.reshape(DEPTH, D_MODEL, N_EXPERTS)], axis=2)
    wr_t = jnp.pad(wr_t, ((0, 0), (0, 0), (0, LANES - N_GROUPS - N_EXPERTS)))
    br = jnp.concatenate([moe_br_g, moe_br_e.reshape(DEPTH, N_EXPERTS)], axis=1)
    br = jnp.pad(br, ((0, 0), (0, ROUTE_ROWS - N_GROUPS - N_EXPERTS))).reshape(DEPTH, ROUTE_ROWS, 1)
    tri = jnp.asarray(np.triu(np.ones((CUM_BLK, CUM_BLK), np.float32), 1), BF16)
    norm_mix = norm_mix_g.reshape(DEPTH, 1, D_MODEL)
    norm_ffn = norm_ffn_g.reshape(DEPTH, 1, D_MODEL)
    head_major = lambda cache: jnp.transpose(cache, (0, 1, 3, 2, 4))
    ck_diff, cv_diff = head_major(cache_diff_k), head_major(cache_diff_v)
    ck_na, cv_na = head_major(cache_na_k), head_major(cache_na_v)

    mod8 = _ada_modulation(cvec, w_ada, b_ada).reshape(DEPTH * MOD_ROWS, 1, N_ADA)
    w_out_bf = _cast_bf16(w_out)

    ctx_row = lambda i: 0
    lat_row_pre = lambda i: 1 + i * PRE_TM // DEC_SEQ
    lat_row_mix = lambda i: 1 + i * MIX_TM // DEC_SEQ

    xc = x_prompt.reshape(T_CTX, D_MODEL)
    xl = x_sample.reshape(T_LAT, D_MODEL)
    ctx_segs = []
    for layer in range(DEPTH):
        hc, uc = _pre_project(xc, mod8, norm_mix, w_in, layer, ctx_row, "pre_ctx")
        hl, ul = _pre_project(xl, mod8, norm_mix, w_in, layer, lat_row_pre, "pre_lat")
        sc = _head_project(hc, w_in, gains, a64, a128, None, layer, BATCH, SEQ, "heads_ctx")
        sl = _head_project(hl, w_in, gains, a64, a128, rope_tabs, layer, DEC_BATCH, DEC_SEQ, "heads_lat")
        ctx_segs.append(sc)

        ac = _conformer_conv(uc, conv_w, conv_b, conv_ln_g, conv_ln_b, layer, BATCH, SEQ)
        al = _conformer_conv(ul, conv_w, conv_b, conv_ln_g, conv_ln_b, layer, DEC_BATCH, DEC_SEQ)
        dc, nc = _ctx_attention(sc, lam_params, diff_subln_g, layer)
        dl = _diff_attention_lat(sl, lam_params, diff_subln_g, ck_diff, cv_diff, layer, 256)
        nl = _na_attention(sl, rpb_src, na_mask, ck_na, cv_na, layer)

        x1c, ltc = _mix(ac, dc, nc, w_out_bf, xc, mod8, norm_ffn, wr_t, br, layer, ctx_row)
        x1l, ltl = _mix(al, dl, nl, w_out_bf, xl, mod8, norm_ffn, wr_t, br, layer, lat_row_mix)

        pos, wts, items = _route(jnp.concatenate([ltc, ltl], axis=1), tri)
        pos_flat = pos.reshape(2 * T_ALL)
        items = [items[r, :MOE_ITEMS] for r in range(4)] + [items[4, :1]]
        xs = _dispatch(pos_flat, x1c, x1l, mod8, norm_ffn, layer)
        ys = _experts(items, xs, moe_w_gate, moe_w_up, moe_w_down, layer)
        xc, xl = _combine(pos_flat, x1c, x1l, wts.T, mod8, ys, layer)

    outs = [jnp.transpose(cache, (0, 1, 3, 2, 4)) for cache in _stack_caches(ctx_segs)]
    return (xc.reshape(BATCH, SEQ, D_MODEL), xl.reshape(DEC_BATCH, DEC_SEQ, D_MODEL), *outs)
```

```python
import functools
import math

import numpy as np
import jax
import jax.numpy as jnp
from jax import lax
from jax.experimental import pallas as pl
from jax.experimental.pallas import tpu as pltpu

F32 = jnp.float32
BF16 = jnp.bfloat16
I32 = jnp.int32

D_MODEL = 2048
BATCH = 16
SEQ = 256
DEPTH = 2
DEC_BATCH = 2
DEC_SEQ = 1024
PAST_LEN = 256
GRID_W = 64
GRID_ROWS = DEC_SEQ // GRID_W
HEAD_DIM = 128
C_CONV = 512
CONV_K = 31
N_HEADS = 6
DQK = 64
NA_KR = 8
NA_KC = 16
ROPE_BASE = 10000.0
N_GROUPS = 4
E_PER_GROUP = 4
N_EXPERTS = 16
D_FF = 512
N_ADA = 6 * D_MODEL
W_IN_COLS = 2 * C_CONV + 6 * N_HEADS * HEAD_DIM
T_CTX = BATCH * SEQ
T_LAT = DEC_BATCH * DEC_SEQ
T_ALL = T_CTX + T_LAT

LANES = 128
MOD_ROWS = 8
PRE_TM = 512
HEADS_TM = 1024
MIX_TM = 512
MOE_TM = 256
MOE_ITEMS = 2 * T_ALL // MOE_TM + N_EXPERTS
ROUTE_ROWS = 32
CUM_BLK = 512
VMEM_LIMIT = 56 * 1024 * 1024
NEG_BIG = -1e30


def _cparams(sem):
    return pltpu.CompilerParams(dimension_semantics=sem, vmem_limit_bytes=VMEM_LIMIT)


def _sigmoid(x):
    return 1.0 / (1.0 + jnp.exp(-x))


def _silu(x):
    return x * _sigmoid(x)


def _lambda_init(layer):
    return 0.8 - 0.6 * math.exp(-0.3 * layer)


def _ada_kernel(c_ref, w_ref, b_ref, o_ref):
    c = c_ref[...]
    s = _silu(c).astype(BF16)
    o_ref[...] = jnp.dot(s, w_ref[...].astype(BF16), preferred_element_type=F32) + b_ref[...]


def _ada_modulation(cvec, w_ada, b_ada):
    tn = 1024
    return pl.pallas_call(
        _ada_kernel,
        grid=(DEPTH, N_ADA // tn),
        in_specs=[
            pl.BlockSpec((MOD_ROWS, D_MODEL), lambda l, j: (0, 0)),
            pl.BlockSpec((None, D_MODEL, tn), lambda l, j: (l, 0, j)),
            pl.BlockSpec((None, 1, tn), lambda l, j: (l, 0, j)),
        ],
        out_specs=pl.BlockSpec((None, MOD_ROWS, tn), lambda l, j: (l, 0, j)),
        out_shape=jax.ShapeDtypeStruct((DEPTH, MOD_ROWS, N_ADA), F32),
        compiler_params=_cparams(("arbitrary", "arbitrary")),
        name="ada_modulation",
    )(cvec, w_ada, b_ada.reshape(DEPTH, 1, N_ADA))


def _mod_spec(layer, chunk, row_of_block):
    return pl.BlockSpec((None, 1, D_MODEL),
                        lambda i, *_: (layer * MOD_ROWS + row_of_block(i), 0, chunk))


def _modulate(x, g, shift, scale):
    ms = jnp.mean(x * x, axis=-1, keepdims=True)
    return x * lax.rsqrt(ms + 1e-6) * (g * (1.0 + scale)) + shift


def _pre_kernel(x_ref, sh_ref, sc_ref, g_ref, w_ref, h_ref, u_ref, wbf_ref):
    @pl.when(pl.program_id(0) == 0)
    def _():
        wbf_ref[...] = w_ref[...].astype(BF16)

    h = _modulate(x_ref[...], g_ref[...], sh_ref[...], sc_ref[...]).astype(BF16)
    h_ref[...] = h
    y = jnp.dot(h, wbf_ref[...], preferred_element_type=F32)
    u_ref[...] = y[:, :C_CONV] * _sigmoid(y[:, C_CONV:])


def _pre_project(x, mod8, norm_g, w_in, layer, row_of_block, name):
    t = x.shape[0]
    return pl.pallas_call(
        _pre_kernel,
        grid=(t // PRE_TM,),
        in_specs=[
            pl.BlockSpec((PRE_TM, D_MODEL), lambda i: (i, 0)),
            _mod_spec(layer, 0, row_of_block),
            _mod_spec(layer, 1, row_of_block),
            pl.BlockSpec((None, 1, D_MODEL), lambda i: (layer, 0, 0)),
            pl.BlockSpec((None, D_MODEL, 2 * C_CONV), lambda i: (layer, 0, 0)),
        ],
        out_specs=[pl.BlockSpec((PRE_TM, D_MODEL), lambda i: (i, 0)),
                   pl.BlockSpec((PRE_TM, C_CONV), lambda i: (i, 0))],
        out_shape=[jax.ShapeDtypeStruct((t, D_MODEL), BF16), jax.ShapeDtypeStruct((t, C_CONV), F32)],
        scratch_shapes=[pltpu.VMEM((D_MODEL, 2 * C_CONV), BF16)],
        compiler_params=_cparams(("arbitrary",)),
        name=name,
    )(x, mod8, mod8, norm_g, w_in)


SEG_DQ, SEG_DK, SEG_DV, SEG_NQ, SEG_NK, SEG_NV = range(6)
SEG_W = N_HEADS * HEAD_DIM
HEADS_WBLK = 256
HEADS_RC = 512


def _heads_kernel(h_ref, w0_ref, w1_ref, w2_ref, gain_ref, a64_ref, a128_ref, *rest, rope, per_batch):
    if rope:
        cos_ref, sa_ref, sb_ref, o_ref = rest
    else:
        (o_ref,) = rest
    j = pl.program_id(0)

    def finish(seg):
        n_g = HEADS_WBLK // LANES
        for c, w_ref in enumerate((w0_ref, w1_ref, w2_ref)):
            w = w_ref[...].astype(BF16)
            gain = gain_ref[:, c * HEADS_WBLK:(c + 1) * HEADS_WBLK]
            for r0 in range(0, HEADS_TM, HEADS_RC):
                y = jnp.dot(h_ref[r0:r0 + HEADS_RC, :], w, preferred_element_type=F32)
                for g in range(n_g):
                    hd = c * n_g + g
                    yg = y[:, g * LANES:(g + 1) * LANES]
                    if seg in (SEG_NQ, SEG_NK):
                        ms = jnp.mean(yg * yg, axis=-1, keepdims=True)
                        yg = yg * lax.rsqrt(ms + 1e-6) * gain[:, g * LANES:(g + 1) * LANES]
                    elif seg in (SEG_DQ, SEG_DK):
                        sq = yg * yg
                        low = lax.broadcasted_iota(I32, sq.shape, 1) < DQK
                        s_all = jnp.sum(sq, axis=-1, keepdims=True)
                        s_lo = jnp.sum(jnp.where(low, sq, 0.0), axis=-1, keepdims=True)
                        ms = jnp.where(low, s_lo, s_all - s_lo) * (1.0 / DQK)
                        yg = yg * lax.rsqrt(ms + 1e-6) * gain[:, g * LANES:(g + 1) * LANES]
                    if rope and seg in (SEG_DQ, SEG_DK):
                        rs = slice(r0, r0 + HEADS_RC)
                        yg = (yg * cos_ref[rs, :] + pltpu.roll(yg, LANES - 16, 1) * sa_ref[rs, :]
                              + pltpu.roll(yg, 16, 1) * sb_ref[rs, :])
                    yg = yg.astype(o_ref.dtype)
                    if per_batch == 1:
                        o_ref[hd, r0:r0 + HEADS_RC, :] = yg
                    else:
                        rows = HEADS_TM // per_batch
                        for q in range(HEADS_RC // rows):
                            o_ref[r0 // rows + q, hd] = yg[q * rows:(q + 1) * rows]

    for seg in range(6):
        pl.when(j == seg)(functools.partial(finish, seg))


def _head_project(h, w_in, gains, a64, a128, rope_tabs, layer, batch, seq, name):
    t = h.shape[0]
    rope = rope_tabs is not None
    per_batch = HEADS_TM // seq
    col0 = 2 * C_CONV // HEADS_WBLK
    wspec = lambda c: pl.BlockSpec((None, D_MODEL, HEADS_WBLK),
                                   lambda j, i: (layer, 0, col0 + (SEG_W // HEADS_WBLK) * j + c))
    in_specs = [
        pl.BlockSpec((HEADS_TM, D_MODEL), lambda j, i: (i, 0)),
        wspec(0), wspec(1), wspec(2),
        pl.BlockSpec((None, None, 1, SEG_W), lambda j, i: (layer, j, 0, 0)),
        pl.BlockSpec((HEADS_WBLK, HEADS_WBLK), lambda j, i: (0, 0)),
        pl.BlockSpec((HEADS_WBLK, HEADS_WBLK), lambda j, i: (0, 0)),
    ]
    args = [h, w_in, w_in, w_in, gains, a64, a128]
    if rope:
        in_specs += [pl.BlockSpec((DEC_SEQ, LANES), lambda j, i: (0, 0))] * 3
        args += list(rope_tabs)
    if per_batch == 1:
        out_spec = pl.BlockSpec((None, None, N_HEADS, seq, HEAD_DIM), lambda j, i: (j, i, 0, 0, 0))
    else:
        out_spec = pl.BlockSpec((None, per_batch, N_HEADS, seq, HEAD_DIM), lambda j, i: (j, i, 0, 0, 0))
    return pl.pallas_call(
        functools.partial(_heads_kernel, rope=rope, per_batch=per_batch),
        grid=(6, t // HEADS_TM),
        in_specs=in_specs,
        out_specs=out_spec,
        out_shape=jax.ShapeDtypeStruct((6, batch, N_HEADS, seq, HEAD_DIM), BF16 if rope else F32),
        compiler_params=_cparams(("arbitrary", "arbitrary")),
        name=name,
    )(*args)


CONV_CH = 128
CONV_HALO = 16


def _conv_kernel(u_ref, w_ref, b_ref, lg_ref, lb_ref, o_ref, win_ref, y_ref, *, seq):
    c = pl.program_id(1)
    n_chunks = seq // CONV_CH
    base = pl.multiple_of(c * CONV_CH, CONV_CH)

    win_ref[CONV_HALO:CONV_HALO + CONV_CH, :] = u_ref[pl.ds(base, CONV_CH), :]
    lo_start = pl.multiple_of(jnp.maximum(base - CONV_HALO, 0), CONV_HALO)
    hi_start = pl.multiple_of(jnp.minimum(base + CONV_CH, seq - CONV_HALO), CONV_HALO)
    lo_keep = (c > 0).astype(F32)
    hi_keep = (c < n_chunks - 1).astype(F32)
    win_ref[0:CONV_HALO, :] = u_ref[pl.ds(lo_start, CONV_HALO), :] * lo_keep
    win_ref[CONV_HALO + CONV_CH:, :] = u_ref[pl.ds(hi_start, CONV_HALO), :] * hi_keep

    off = CONV_HALO - CONV_K // 2
    sub = 8
    win_rows = CONV_CH + 2 * CONV_HALO
    for g in range(C_CONV // LANES):
        sl = slice(g * LANES, (g + 1) * LANES)
        window = win_ref[:, sl]
        acc = jnp.zeros((CONV_CH, LANES), F32) + b_ref[:, sl]
        for phase in range(sub):
            taps = [k for k in range(CONV_K) if (off + k) % sub == phase]
            if not taps:
                continue
            shifted = window if phase == 0 else pltpu.roll(window, win_rows - phase, 0)
            for k in taps:
                a = (off + k) // sub * sub
                acc = acc + shifted[a:a + CONV_CH, :] * w_ref[k:k + 1, sl]
        y_ref[:, sl] = acc

    y = y_ref[...]
    mu = jnp.mean(y, axis=-1, keepdims=True)
    yc = y - mu
    var = jnp.mean(yc * yc, axis=-1, keepdims=True)
    z = yc * lax.rsqrt(var + 1e-5) * lg_ref[...] + lb_ref[...]
    o_ref[...] = _silu(z).astype(o_ref.dtype)


def _conformer_conv(u, conv_w, conv_b, ln_g, ln_b, layer, batch, seq):
    t = u.shape[0]
    n_chunks = seq // CONV_CH
    vec = lambda: pl.BlockSpec((None, 1, C_CONV), lambda b, c: (layer, 0, 0))
    return pl.pallas_call(
        functools.partial(_conv_kernel, seq=seq),
        grid=(batch, n_chunks),
        in_specs=[
            pl.BlockSpec((seq, C_CONV), lambda b, c: (b, 0)),
            pl.BlockSpec((None, CONV_K, C_CONV), lambda b, c: (layer, 0, 0)),
            vec(), vec(), vec(),
        ],
        out_specs=pl.BlockSpec((CONV_CH, C_CONV), lambda b, c: (b * n_chunks + c, 0)),
        out_shape=jax.ShapeDtypeStruct((t, C_CONV), BF16),
        scratch_shapes=[pltpu.VMEM((CONV_CH + 2 * CONV_HALO, C_CONV), F32),
                        pltpu.VMEM((CONV_CH, C_CONV), F32)],
        compiler_params=_cparams(("arbitrary", "arbitrary")),
        name="conv_%d" % seq,
    )(u, conv_w, conv_b.reshape(DEPTH, 1, C_CONV), ln_g.reshape(DEPTH, 1, C_CONV),
      ln_b.reshape(DEPTH, 1, C_CONV))


_NT = (((1,), (1,)), ((), ()))


LOG2E = 1.4426950408889634


def _softmax_rows(s):
    m = jnp.max(s, axis=-1, keepdims=True)
    e = jnp.exp(s - m)
    return e / jnp.sum(e, axis=-1, keepdims=True)


def _exp2_rows(s2):
    e = jnp.exp2(s2 - jnp.max(s2, axis=-1, keepdims=True))
    return e, 1.0 / jnp.sum(e, axis=-1, keepdims=True)


def _diff_lambda(lam_ref, lam_init):
    lv = lam_ref[...]
    return (jnp.exp(jnp.sum(lv[0:1] * lv[1:2], axis=-1, keepdims=True))
            - jnp.exp(jnp.sum(lv[2:3] * lv[3:4], axis=-1, keepdims=True)) + lam_init)


def _diff_head(q, k, v, lam, g, lam_init, long_keys):
    lane = lax.broadcasted_iota(I32, q.shape, 1)
    if long_keys:
        qs = q * (DQK ** -0.5 * LOG2E)
        q0 = jnp.where(lane < DQK, qs, 0.0).astype(BF16)
        q1 = jnp.where(lane >= DQK, qs, 0.0).astype(BF16)
        e0, r0 = _exp2_rows(lax.dot_general(q0, k, _NT, preferred_element_type=F32))
        e1, r1 = _exp2_rows(lax.dot_general(q1, k, _NT, preferred_element_type=F32))
        o = (jnp.dot(e0.astype(BF16), v, preferred_element_type=F32) * r0
             - jnp.dot(e1.astype(BF16), v, preferred_element_type=F32) * (lam * r1))
    else:
        q0 = jnp.where(lane < DQK, q, 0.0).astype(BF16)
        q1 = jnp.where(lane >= DQK, q, 0.0).astype(BF16)
        scale = DQK ** -0.5
        p0 = _softmax_rows(lax.dot_general(q0, k, _NT, preferred_element_type=F32) * scale)
        p1 = _softmax_rows(lax.dot_general(q1, k, _NT, preferred_element_type=F32) * scale)
        o = jnp.dot((p0 - lam * p1).astype(BF16), v, preferred_element_type=F32)
    ms = jnp.mean(o * o, axis=-1, keepdims=True)
    return o * lax.rsqrt(ms + 1e-5) * g * (1.0 - lam_init)


def _seg_spec(seg, rows, index):
    def imap(*ids):
        b, h, r = index(*ids)
        return (seg, b, h, r, 0)
    return pl.BlockSpec((None, None, None, rows, HEAD_DIM), imap)


def _diff_attn_lat_kernel(lam_ref, g_ref, q_ref, k_ref, v_ref, ck_ref, cv_ref, o_ref, *, lam_init):
    k = jnp.concatenate([k_ref[...].astype(BF16), ck_ref[...].astype(BF16)], axis=0)
    v = jnp.concatenate([v_ref[...].astype(BF16), cv_ref[...].astype(BF16)], axis=0)
    lam = _diff_lambda(lam_ref, lam_init)
    o_ref[...] = _diff_head(q_ref[...].astype(F32), k, v, lam, g_ref[...], lam_init,
                            True).astype(o_ref.dtype)


def _diff_attention_lat(segs, lam_params, subln_g, cache_k, cache_v, layer, tq):
    nq = DEC_SEQ // tq
    cspec = pl.BlockSpec((None, None, None, PAST_LEN, HEAD_DIM), lambda b, h, qi: (b, layer, h, 0, 0))
    return pl.pallas_call(
        functools.partial(_diff_attn_lat_kernel, lam_init=_lambda_init(layer)),
        grid=(DEC_BATCH, N_HEADS, nq),
        in_specs=[
            pl.BlockSpec((None, 4, DQK), lambda b, h, qi: (layer, 0, 0)),
            pl.BlockSpec((None, 1, HEAD_DIM), lambda b, h, qi: (layer, 0, 0)),
            _seg_spec(SEG_DQ, tq, lambda b, h, qi: (b, h, qi)),
            _seg_spec(SEG_DK, DEC_SEQ, lambda b, h, qi: (b, h, 0)),
            _seg_spec(SEG_DV, DEC_SEQ, lambda b, h, qi: (b, h, 0)),
            cspec, cspec,
        ],
        out_specs=pl.BlockSpec((tq, HEAD_DIM), lambda b, h, qi: (b * nq + qi, h)),
        out_shape=jax.ShapeDtypeStruct((T_LAT, SEG_W), BF16),
        compiler_params=_cparams(("arbitrary", "arbitrary", "arbitrary")),
        name="diff_attn_lat",
    )(lam_params, subln_g.reshape(DEPTH, 1, HEAD_DIM), segs, segs, segs, cache_k, cache_v)


def _ctx_attn_kernel(lam_ref, g_ref, dq_ref, dk_ref, dv_ref, nq_ref, nk_ref, nv_ref, d_ref, n_ref, *,
                     lam_init):
    lam = _diff_lambda(lam_ref, lam_init)
    for h in range(N_HEADS):
        sl = slice(h * HEAD_DIM, (h + 1) * HEAD_DIM)
        d_ref[:, sl] = _diff_head(dq_ref[h], dk_ref[h].astype(BF16), dv_ref[h].astype(BF16), lam,
                                  g_ref[...], lam_init, False).astype(d_ref.dtype)
        s = lax.dot_general(nq_ref[h].astype(BF16), nk_ref[h].astype(BF16), _NT,
                            preferred_element_type=F32) * (HEAD_DIM ** -0.5)
        n_ref[:, sl] = jnp.dot(_softmax_rows(s).astype(BF16), nv_ref[h].astype(BF16),
                               preferred_element_type=F32).astype(n_ref.dtype)


def _ctx_attention(segs, lam_params, subln_g, layer):
    seg = lambda s: pl.BlockSpec((None, None, N_HEADS, SEQ, HEAD_DIM), lambda b: (s, b, 0, 0, 0))
    out = pl.BlockSpec((SEQ, SEG_W), lambda b: (b, 0))
    return pl.pallas_call(
        functools.partial(_ctx_attn_kernel, lam_init=_lambda_init(layer)),
        grid=(BATCH,),
        in_specs=[pl.BlockSpec((None, 4, DQK), lambda b: (layer, 0, 0)),
                  pl.BlockSpec((None, 1, HEAD_DIM), lambda b: (layer, 0, 0))]
                 + [seg(s) for s in range(6)],
        out_specs=[out, out],
        out_shape=[jax.ShapeDtypeStruct((T_CTX, SEG_W), BF16)] * 2,
        compiler_params=_cparams(("arbitrary",)),
        name="ctx_attn",
    )(lam_params, subln_g.reshape(DEPTH, 1, HEAD_DIM), *([segs] * 6))


NA_PAIRS = 2 * NA_KR - 2
NA_QROWS = 4
NA_KROWS = 12


def _na_key_block(chunk):
    first = min(max(chunk * NA_QROWS - NA_KR // 2, 0), GRID_ROWS - NA_KR)
    return min(first - first % 2, GRID_ROWS - NA_KROWS)


def _na_attn_kernel(src_ref, mask_ref, q_ref, k_ref, v_ref, ck_ref, cv_ref, o_ref, bias_ref):
    for d in range(NA_PAIRS):
        row = jnp.broadcast_to(src_ref[d:d + 1, :] * LOG2E, (GRID_W, LANES))
        tile = pltpu.roll(row, LANES - (NA_KC - 1), 1, stride=1, stride_axis=0)
        for v in range(3):
            bias_ref[v, d] = tile + mask_ref[v]
    dead = jnp.full((GRID_W, LANES), NEG_BIG, F32)
    ck = ck_ref[...].astype(BF16)
    cv = cv_ref[...].astype(BF16)
    scale = HEAD_DIM ** -0.5 * LOG2E
    for chunk in range(GRID_ROWS // NA_QROWS):
        kb = _na_key_block(chunk)
        rows = []
        for qr in range(chunk * NA_QROWS, (chunk + 1) * NA_QROWS):
            start = min(max(qr - NA_KR // 2, 0), GRID_ROWS - NA_KR)
            assert kb <= start and start + NA_KR <= kb + NA_KROWS
            tiles = []
            for m in range(NA_KROWS // 2):
                r0 = kb + 2 * m
                live0 = start <= r0 < start + NA_KR
                live1 = start <= r0 + 1 < start + NA_KR
                d = r0 - qr + NA_KR - 1
                assert not (live0 or live1) or 0 <= d < NA_PAIRS
                if live0 and live1:
                    tiles.append(bias_ref[0, d])
                elif live0:
                    tiles.append(bias_ref[1, d])
                elif live1:
                    tiles.append(bias_ref[2, d])
                else:
                    tiles.append(dead)
            rows.append(jnp.concatenate(tiles, axis=1))
        bias = jnp.concatenate(rows, axis=0)
        qs = slice(chunk * NA_QROWS * GRID_W, (chunk + 1) * NA_QROWS * GRID_W)
        ks = slice(kb * GRID_W, (kb + NA_KROWS) * GRID_W)
        q = (q_ref[qs, :].astype(F32) * scale).astype(BF16)
        kl = k_ref[ks, :].astype(BF16)
        vl = v_ref[ks, :].astype(BF16)
        s_loc = lax.dot_general(q, kl, _NT, preferred_element_type=F32) + bias
        s_ctx = lax.dot_general(q, ck, _NT, preferred_element_type=F32)
        m = jnp.maximum(jnp.max(s_loc, axis=-1, keepdims=True), jnp.max(s_ctx, axis=-1, keepdims=True))
        e_loc = jnp.exp2(s_loc - m)
        e_ctx = jnp.exp2(s_ctx - m)
        r = 1.0 / (jnp.sum(e_loc, axis=-1, keepdims=True) + jnp.sum(e_ctx, axis=-1, keepdims=True))
        o_ref[qs, :] = ((jnp.dot(e_loc.astype(BF16), vl, preferred_element_type=F32)
                         + jnp.dot(e_ctx.astype(BF16), cv, preferred_element_type=F32)) * r
                        ).astype(o_ref.dtype)


def _na_attention(segs, rpb_src, na_mask, cache_k, cache_v, layer):
    blk = lambda s: _seg_spec(s, DEC_SEQ, lambda b, h: (b, h, 0))
    cspec = pl.BlockSpec((None, None, None, PAST_LEN, HEAD_DIM), lambda b, h: (b, layer, h, 0, 0))
    return pl.pallas_call(
        _na_attn_kernel,
        grid=(DEC_BATCH, N_HEADS),
        in_specs=[
            pl.BlockSpec((None, None, 16, LANES), lambda b, h: (layer, h, 0, 0)),
            pl.BlockSpec((3, GRID_W, LANES), lambda b, h: (0, 0, 0)),
            blk(SEG_NQ), blk(SEG_NK), blk(SEG_NV), cspec, cspec,
        ],
        out_specs=pl.BlockSpec((DEC_SEQ, HEAD_DIM), lambda b, h: (b, h)),
        out_shape=jax.ShapeDtypeStruct((T_LAT, SEG_W), BF16),
        scratch_shapes=[pltpu.VMEM((3, NA_PAIRS, GRID_W, LANES), F32)],
        compiler_params=_cparams(("arbitrary", "arbitrary")),
        name="na_attn_lat",
    )(rpb_src, na_mask, segs, segs, segs, cache_k, cache_v)


def _split_f32(x, n):
    terms = []
    for _ in range(n):
        t = x.astype(BF16).astype(F32)
        terms.append(t)
        x = x - t
    return terms


def _mix_kernel(a_ref, d_ref, n_ref, w_ref, x_ref, g1_ref, sh_ref, sc_ref, ng_ref, wr_ref, br_ref,
                x1_ref, lt_ref):
    k0, k1 = C_CONV, C_CONV + N_HEADS * HEAD_DIM
    mixed = (jnp.dot(a_ref[...].astype(BF16), w_ref[0:k0, :], preferred_element_type=F32)
             + jnp.dot(d_ref[...].astype(BF16), w_ref[k0:k1, :], preferred_element_type=F32)
             + jnp.dot(n_ref[...].astype(BF16), w_ref[k1:, :], preferred_element_type=F32))
    x1 = x_ref[...] + g1_ref[...] * mixed
    x1_ref[...] = x1
    h2 = _modulate(x1, ng_ref[...], sh_ref[...], sc_ref[...])
    wh, wm, wl = _split_f32(wr_ref[...], 3)
    w_cat = (wh + pltpu.roll(wm, ROUTE_ROWS, 1) + pltpu.roll(wl, 2 * ROUTE_ROWS, 1)).astype(BF16)
    hh, hm = _split_f32(h2, 2)
    r = jnp.dot(jnp.concatenate([hh, hm], axis=0).astype(BF16), w_cat, preferred_element_type=F32)
    rh, rm = r[:MIX_TM], r[MIX_TM:]
    back = lambda v, groups: pltpu.roll(v, LANES - groups * ROUTE_ROWS, 1)
    small = (back(rm, 1) + back(rh, 2)) + (back(rh, 1) + rm)
    logits = (small + rh).T[:ROUTE_ROWS, :]
    lt_ref[...] = logits + br_ref[...]


def _mix(a_out, d_out, n_out, w_out_bf, x, mod8, norm_g, wr_t, br, layer, row_of_block):
    t = x.shape[0]
    rows = lambda width: pl.BlockSpec((MIX_TM, width), lambda i: (i, 0))
    return pl.pallas_call(
        _mix_kernel,
        grid=(t // MIX_TM,),
        in_specs=[
            rows(C_CONV), rows(N_HEADS * HEAD_DIM), rows(N_HEADS * HEAD_DIM),
            pl.BlockSpec((None, D_MODEL, D_MODEL), lambda i: (layer, 0, 0)),
            rows(D_MODEL),
            _mod_spec(layer, 2, row_of_block),
            _mod_spec(layer, 3, row_of_block),
            _mod_spec(layer, 4, row_of_block),
            pl.BlockSpec((None, 1, D_MODEL), lambda i: (layer, 0, 0)),
            pl.BlockSpec((None, D_MODEL, LANES), lambda i: (layer, 0, 0)),
            pl.BlockSpec((None, ROUTE_ROWS, 1), lambda i: (layer, 0, 0)),
        ],
        out_specs=[rows(D_MODEL), pl.BlockSpec((ROUTE_ROWS, MIX_TM), lambda i: (0, i))],
        out_shape=[jax.ShapeDtypeStruct((t, D_MODEL), F32),
                   jax.ShapeDtypeStruct((ROUTE_ROWS, t), F32)],
        compiler_params=_cparams(("arbitrary",)),
        name="mix_%d" % t,
    )(a_out, d_out, n_out, w_out_bf, x, mod8, mod8, mod8, norm_g, wr_t, br)


CACHE_SEGS = (SEG_DK, SEG_DV, SEG_NK, SEG_NV)


def _stack_kernel(*refs):
    ins, outs = refs[:2 * len(CACHE_SEGS)], refs[2 * len(CACHE_SEGS):]
    for n, o_ref in enumerate(outs):
        for layer in range(DEPTH):
            o_ref[layer] = ins[DEPTH * n + layer][...]


def _stack_caches(seg_arrays):
    in_specs, args = [], []
    for seg in CACHE_SEGS:
        for layer in range(DEPTH):
            in_specs.append(pl.BlockSpec((None, None, N_HEADS, SEQ, HEAD_DIM),
                                         lambda b, seg=seg: (seg, b, 0, 0, 0)))
            args.append(seg_arrays[layer])
    out_spec = pl.BlockSpec((None, DEPTH, N_HEADS, SEQ, HEAD_DIM), lambda b: (b, 0, 0, 0, 0))
    return pl.pallas_call(
        _stack_kernel,
        grid=(BATCH,),
        in_specs=in_specs,
        out_specs=[out_spec] * len(CACHE_SEGS),
        out_shape=[jax.ShapeDtypeStruct((BATCH, DEPTH, N_HEADS, SEQ, HEAD_DIM), F32)] * len(CACHE_SEGS),
        compiler_params=_cparams(("arbitrary",)),
        name="stack_caches",
    )(*args)


def _cast_kernel(x_ref, o_ref):
    o_ref[...] = x_ref[...].astype(BF16)


def _cast_bf16(w):
    tm = 512
    spec = pl.BlockSpec((None, tm, D_MODEL), lambda l, i: (l, i, 0))
    return pl.pallas_call(
        _cast_kernel, grid=(DEPTH, D_MODEL // tm), in_specs=[spec], out_specs=spec,
        out_shape=jax.ShapeDtypeStruct(w.shape, BF16),
        compiler_params=_cparams(("arbitrary", "arbitrary")), name="cast_w_out",
    )(w)


def _route_kernel(lt_ref, tri_ref, pos_ref, wt_ref, item_ref):
    t = T_ALL
    lg = lt_ref[0:N_GROUPS, :]
    eg = jnp.exp(lg - jnp.max(lg, axis=0, keepdims=True))
    pg = eg / jnp.sum(eg, axis=0, keepdims=True)
    pg_top = jnp.max(pg, axis=0, keepdims=True)
    gi = lax.broadcasted_iota(I32, pg.shape, 0).astype(F32)
    g_idx = jnp.min(jnp.where(pg == pg_top, gi, float(N_GROUPS)), axis=0, keepdims=True)

    le = jnp.zeros((E_PER_GROUP, t), F32)
    for g in range(N_GROUPS):
        rows = lt_ref[N_GROUPS + g * E_PER_GROUP:N_GROUPS + (g + 1) * E_PER_GROUP, :]
        le = jnp.where(g_idx == float(g), rows, le)
    ee = jnp.exp(le - jnp.max(le, axis=0, keepdims=True))
    pe = ee / jnp.sum(ee, axis=0, keepdims=True)
    ei = lax.broadcasted_iota(I32, pe.shape, 0).astype(F32)
    p1 = jnp.max(pe, axis=0, keepdims=True)
    i1 = jnp.min(jnp.where(pe == p1, ei, float(E_PER_GROUP)), axis=0, keepdims=True)
    pe_rest = jnp.where(ei == i1, -1.0, pe)
    p2 = jnp.max(pe_rest, axis=0, keepdims=True)
    i2 = jnp.min(jnp.where(pe_rest == p2, ei, float(E_PER_GROUP)), axis=0, keepdims=True)
    den = p1 + p2
    wt_ref[0:1, :] = pg_top * (p1 / den)
    wt_ref[1:2, :] = pg_top * (p2 / den)
    e1 = g_idx * E_PER_GROUP + i1
    e2 = g_idx * E_PER_GROUP + i2

    erow = lax.broadcasted_iota(I32, (N_EXPERTS, t), 0).astype(F32)
    oh1 = (erow == e1).astype(F32)
    oh2 = (erow == e2).astype(F32)
    cnt = oh1 + oh2
    carry = jnp.zeros((N_EXPERTS, 1), F32)
    ranks = []
    for b in range(t // CUM_BLK):
        blk = cnt[:, b * CUM_BLK:(b + 1) * CUM_BLK]
        ranks.append(jnp.dot(blk.astype(BF16), tri_ref[...], preferred_element_type=F32) + carry)
        carry = carry + jnp.sum(blk, axis=1, keepdims=True)
    rank = jnp.concatenate(ranks, axis=1)

    erow_l = lax.broadcasted_iota(I32, (N_EXPERTS, LANES), 0)

    def excl_scan(v):
        inc = v
        for s in (1, 2, 4, 8):
            inc = inc + jnp.where(erow_l >= s, pltpu.roll(inc, s, 0), 0.0)
        return inc - v

    total = jnp.broadcast_to(carry, (N_EXPERTS, LANES))
    offs = excl_scan(total)
    pos_ref[0:1, :] = jnp.sum(oh1 * (rank + offs[:, 0:1]), axis=0, keepdims=True).astype(I32)
    pos_ref[1:2, :] = jnp.sum(oh2 * (rank + offs[:, 0:1]), axis=0, keepdims=True).astype(I32)

    shift = int(math.log2(MOE_TM))
    offs_i = offs.astype(I32)
    total_i = total.astype(I32)
    first_tile = lax.shift_right_logical(offs_i, shift)
    last_tile = lax.shift_right_logical(offs_i + total_i - 1, shift)
    n_item = jnp.where(total_i > 0, last_tile - first_tile + 1, 0).astype(F32)
    item_start = excl_scan(n_item)
    item_end = item_start + n_item
    kk = lax.broadcasted_iota(I32, (N_EXPERTS, LANES), 1).astype(F32)
    item_e = jnp.minimum(jnp.sum((item_end <= kk).astype(F32), axis=0, keepdims=True),
                         float(N_EXPERTS - 1))
    sel = (erow_l.astype(F32) == item_e).astype(F32)
    pick = lambda v: jnp.sum(sel * v, axis=0, keepdims=True)
    item_ref[0:1, :] = (kk[0:1] + pick(first_tile.astype(F32) - item_start)).astype(I32)
    item_ref[1:2, :] = item_e.astype(I32)
    item_ref[2:3, :] = pick(offs).astype(I32)
    item_ref[3:4, :] = pick(offs + total).astype(I32)
    item_ref[4:5, :] = item_end[N_EXPERTS - 1:N_EXPERTS, :].astype(I32)
    item_ref[5:8, :] = jnp.zeros((3, LANES), I32)


def _route(lt_all, tri):
    full = lambda shape: pl.BlockSpec(shape, lambda i: (0,) * len(shape))
    return pl.pallas_call(
        _route_kernel,
        grid=(1,),
        in_specs=[full((ROUTE_ROWS, T_ALL)), full((CUM_BLK, CUM_BLK))],
        out_specs=[full((2, T_ALL)), full((2, T_ALL)), full((8, LANES))],
        out_shape=[jax.ShapeDtypeStruct((2, T_ALL), I32), jax.ShapeDtypeStruct((2, T_ALL), F32),
                   jax.ShapeDtypeStruct((8, LANES), I32)],
        compiler_params=_cparams(("arbitrary",)),
        name="moe_route",
    )(lt_all, tri)


N_TOK_TILES = T_ALL // MOE_TM
CTX_TILES = T_CTX // MOE_TM


def _tok_row(i):
    return jnp.where(i < CTX_TILES, 0, 1 + (i - CTX_TILES) // (DEC_SEQ // MOE_TM))


N_CHUNK = D_MODEL // LANES
MOE_PITCH = N_CHUNK + 1
SLAB_ROWS = MOE_TM * MOE_PITCH
ISSUE_UNROLL = 4


def _slab_copy(src, src_tok, dst, dst_tok, sem, pitch):
    return pltpu.make_async_copy(src.at[pl.ds(src_tok * pitch, pitch)],
                                 dst.at[pl.ds(dst_tok * pitch, pitch)], sem)


def _to_slabs(ref, value):
    for c in range(N_CHUNK):
        ref[pl.ds(c, MOE_TM, stride=MOE_PITCH), :] = value[:, c * LANES:(c + 1) * LANES]
    ref[pl.ds(N_CHUNK, MOE_TM, stride=MOE_PITCH), :] = jnp.zeros((MOE_TM, LANES), F32)


def _slab_chunk(ref, c):
    return ref[pl.ds(c, MOE_TM, stride=MOE_PITCH), :]


def _dispatch_kernel(pos_ref, xc_ref, xl_ref, sh_ref, sc_ref, ng_ref, xs_ref, h_ref, sem):
    i = pl.program_id(0)
    slot = i % 2
    buf = h_ref.at[slot]

    def drain(s):
        def body(r, carry):
            _slab_copy(h_ref.at[s], 0, xs_ref, 0, sem.at[s], MOE_PITCH).wait()
            return carry
        lax.fori_loop(0, 2 * MOE_TM, body, 0, unroll=ISSUE_UNROLL)

    @pl.when(i >= 2)
    def _():
        drain(slot)

    @pl.when(i < CTX_TILES)
    def _():
        _to_slabs(buf, _modulate(xc_ref[...], ng_ref[...], sh_ref[...], sc_ref[...]))

    @pl.when(i >= CTX_TILES)
    def _():
        _to_slabs(buf, _modulate(xl_ref[...], ng_ref[...], sh_ref[...], sc_ref[...]))

    def issue(r, carry):
        tok = i * MOE_TM + r
        _slab_copy(buf, r, xs_ref, pos_ref[tok], sem.at[slot], MOE_PITCH).start(priority=0)
        _slab_copy(buf, r, xs_ref, pos_ref[T_ALL + tok], sem.at[slot], MOE_PITCH).start(priority=1)
        return carry

    lax.fori_loop(0, MOE_TM, issue, 0, unroll=ISSUE_UNROLL)

    @pl.when(i == N_TOK_TILES - 1)
    def _():
        drain(1 - slot)
        drain(slot)


def _dispatch(pos_flat, x1c, x1l, mod8, norm_g, layer):
    grid_spec = pltpu.PrefetchScalarGridSpec(
        num_scalar_prefetch=1,
        grid=(N_TOK_TILES,),
        in_specs=[
            pl.BlockSpec((MOE_TM, D_MODEL), lambda i, *_: (jnp.minimum(i, CTX_TILES - 1), 0)),
            pl.BlockSpec((MOE_TM, D_MODEL), lambda i, *_: (jnp.maximum(i - CTX_TILES, 0), 0)),
            _mod_spec(layer, 3, _tok_row),
            _mod_spec(layer, 4, _tok_row),
            pl.BlockSpec((None, 1, D_MODEL), lambda i, *_: (layer, 0, 0)),
        ],
        out_specs=pl.BlockSpec(memory_space=pl.ANY),
        scratch_shapes=[pltpu.VMEM((2, SLAB_ROWS, LANES), F32), pltpu.SemaphoreType.DMA((2,))],
    )
    return pl.pallas_call(
        _dispatch_kernel,
        grid_spec=grid_spec,
        out_shape=jax.ShapeDtypeStruct((2 * T_ALL * MOE_PITCH, LANES), F32),
        compiler_params=_cparams(("arbitrary",)),
        name="moe_dispatch",
    )(pos_flat, x1c, x1l, mod8, mod8, norm_g)


def _expert_kernel(tile_ref, exp_ref, lo_ref, hi_ref, n_ref, xs_ref, wg_hbm, wu_hbm, wd_hbm, ys_ref,
                   wg_buf, wu_buf, wd_buf, wg_bf, wu_bf, wd_bf, slot_ref, sem, *, layer):
    k = pl.program_id(0)
    n = n_ref[0]

    def weight_copies(e, s):
        return (pltpu.make_async_copy(wg_hbm.at[layer, e], wg_buf.at[s], sem.at[s, 0]),
                pltpu.make_async_copy(wu_hbm.at[layer, e], wu_buf.at[s], sem.at[s, 1]),
                pltpu.make_async_copy(wd_hbm.at[layer, e], wd_buf.at[s], sem.at[s, 2]))

    @pl.when(k == 0)
    def _():
        slot_ref[0] = 0
        for cp in weight_copies(exp_ref[0], 0):
            cp.start()

    @pl.when(k < n)
    def _():
        e = exp_ref[k]
        new_expert = jnp.logical_or(k == 0, exp_ref[jnp.maximum(k - 1, 0)] != e)

        @pl.when(new_expert)
        def _():
            s = slot_ref[0]
            for cp in weight_copies(e, s):
                cp.wait()
            wg_bf[...] = wg_buf[s].astype(BF16)
            wu_bf[...] = wu_buf[s].astype(BF16)
            wd_bf[...] = wd_buf[s].astype(BF16)
            nxt = lax.while_loop(lambda j: jnp.logical_and(j < n, exp_ref[jnp.minimum(j, n - 1)] == e),
                                 lambda j: j + 1, k + 1)

            @pl.when(nxt < n)
            def _():
                for cp in weight_copies(exp_ref[jnp.minimum(nxt, n - 1)], 1 - s):
                    cp.start()
            slot_ref[0] = 1 - s

        x = jnp.concatenate([_slab_chunk(xs_ref, c) for c in range(N_CHUNK)], axis=1).astype(BF16)
        gate = jnp.dot(x, wg_bf[...], preferred_element_type=F32)
        up = jnp.dot(x, wu_bf[...], preferred_element_type=F32)
        hid = (_silu(gate) * up).astype(BF16)
        y = jnp.dot(hid, wd_bf[...], preferred_element_type=F32)
        row0 = tile_ref[k] * MOE_TM
        row = row0 + lax.broadcasted_iota(I32, (MOE_TM, 1), 0)
        mine = jnp.logical_and(row >= lo_ref[k], row < hi_ref[k])
        first = lo_ref[k] <= row0

        @pl.when(first)
        def _():
            _to_slabs(ys_ref, jnp.where(mine, y, 0.0))

        @pl.when(jnp.logical_not(first))
        def _():
            for c in range(N_CHUNK):
                old = _slab_chunk(ys_ref, c)
                ys_ref[pl.ds(c, MOE_TM, stride=MOE_PITCH), :] = jnp.where(
                    mine, y[:, c * LANES:(c + 1) * LANES], old)


def _experts(items, xs, w_gate, w_up, w_down, layer):
    item_tile, item_exp, item_lo, item_hi, n_items = items
    cur = lambda k, n: jnp.minimum(k, n[0] - 1)
    rows = pl.BlockSpec((SLAB_ROWS, LANES), lambda k, t, e, lo, hi, n: (t[cur(k, n)], 0))
    hbm = pl.BlockSpec(memory_space=pl.ANY)
    grid_spec = pltpu.PrefetchScalarGridSpec(
        num_scalar_prefetch=5, grid=(MOE_ITEMS,),
        in_specs=[rows, hbm, hbm, hbm], out_specs=rows,
        scratch_shapes=[
            pltpu.VMEM((2, D_MODEL, D_FF), F32), pltpu.VMEM((2, D_MODEL, D_FF), F32),
            pltpu.VMEM((2, D_FF, D_MODEL), F32),
            pltpu.VMEM((D_MODEL, D_FF), BF16), pltpu.VMEM((D_MODEL, D_FF), BF16),
            pltpu.VMEM((D_FF, D_MODEL), BF16),
            pltpu.SMEM((1,), I32), pltpu.SemaphoreType.DMA((2, 3)),
        ])
    return pl.pallas_call(
        functools.partial(_expert_kernel, layer=layer),
        grid_spec=grid_spec,
        out_shape=jax.ShapeDtypeStruct((2 * T_ALL * MOE_PITCH, LANES), F32),
        compiler_params=_cparams(("arbitrary",)),
        name="moe_experts",
    )(item_tile, item_exp, item_lo, item_hi, n_items, xs, w_gate, w_up, w_down)


def _combine_kernel(pos_ref, xc_ref, xl_ref, wt_ref, g2_ref, ys_ref, oc_ref, ol_ref, y_ref, sem):
    i = pl.program_id(0)
    slot = i % 2

    def gather(tile, s):
        def body(r, carry):
            tok = tile * MOE_TM + r
            _slab_copy(ys_ref, pos_ref[tok], y_ref.at[s, 0], r, sem.at[s], MOE_PITCH).start(priority=0)
            _slab_copy(ys_ref, pos_ref[T_ALL + tok], y_ref.at[s, 1], r, sem.at[s], MOE_PITCH).start(priority=1)
            return carry
        lax.fori_loop(0, MOE_TM, body, 0, unroll=ISSUE_UNROLL)

    @pl.when(i == 0)
    def _():
        gather(0, 0)

    @pl.when(i + 1 < N_TOK_TILES)
    def _():
        gather(i + 1, 1 - slot)

    def drain(r, carry):
        _slab_copy(ys_ref, 0, y_ref.at[slot, 0], 0, sem.at[slot], MOE_PITCH).wait()
        return carry

    lax.fori_loop(0, 2 * MOE_TM, drain, 0, unroll=ISSUE_UNROLL)

    def write(x_ref, o_ref):
        w0 = wt_ref[:, 0:1]
        w1 = wt_ref[:, 1:2]
        for c in range(N_CHUNK):
            sl = slice(c * LANES, (c + 1) * LANES)
            moe = w0 * _slab_chunk(y_ref.at[slot, 0], c) + w1 * _slab_chunk(y_ref.at[slot, 1], c)
            o_ref[:, sl] = x_ref[:, sl] + g2_ref[:, sl] * moe

    pl.when(i < CTX_TILES)(lambda: write(xc_ref, oc_ref))
    pl.when(i >= CTX_TILES)(lambda: write(xl_ref, ol_ref))


def _combine(pos_flat, x1c, x1l, wts_t, mod8, ys, layer):
    cspec = pl.BlockSpec((MOE_TM, D_MODEL), lambda i, *_: (jnp.minimum(i, CTX_TILES - 1), 0))
    lspec = pl.BlockSpec((MOE_TM, D_MODEL), lambda i, *_: (jnp.maximum(i - CTX_TILES, 0), 0))
    grid_spec = pltpu.PrefetchScalarGridSpec(
        num_scalar_prefetch=1,
        grid=(N_TOK_TILES,),
        in_specs=[
            cspec, lspec,
            pl.BlockSpec((MOE_TM, 2), lambda i, *_: (i, 0)),
            _mod_spec(layer, 5, _tok_row),
            pl.BlockSpec(memory_space=pl.ANY),
        ],
        out_specs=[cspec, lspec],
        scratch_shapes=[pltpu.VMEM((2, 2, SLAB_ROWS, LANES), F32), pltpu.SemaphoreType.DMA((2,))],
    )
    return pl.pallas_call(
        _combine_kernel,
        grid_spec=grid_spec,
        out_shape=[jax.ShapeDtypeStruct((T_CTX, D_MODEL), F32),
                   jax.ShapeDtypeStruct((T_LAT, D_MODEL), F32)],
        compiler_params=_cparams(("arbitrary",)),
        name="moe_combine",
    )(pos_flat, x1c, x1l, wts_t, mod8, ys)


def _norm_tables():
    mean_blocks = lambda w: np.kron(np.eye(HEADS_WBLK // w, dtype=np.float32),
                                    np.full((w, w), 1.0 / w, np.float32))
    return jnp.asarray(mean_blocks(DQK), BF16), jnp.asarray(mean_blocks(HEAD_DIM), BF16)


def _rope_tables():
    half = DQK // 2
    inv = 1.0 / (ROPE_BASE ** (np.arange(0, half, 2, dtype=np.float32) / half))
    t = np.arange(DEC_SEQ)
    ang_r = (t // GRID_W).astype(np.float32)[:, None] * inv
    ang_c = (t % GRID_W).astype(np.float32)[:, None] * inv
    ang = np.concatenate([ang_r, ang_r, ang_c, ang_c], axis=-1).astype(np.float32)
    ang = np.concatenate([ang, ang], axis=-1)
    first = (np.arange(LANES) % 32) < 16
    cos, sin = np.cos(ang), np.sin(ang)
    return (jnp.asarray(cos, F32), jnp.asarray(np.where(first, -sin, 0.0), F32),
            jnp.asarray(np.where(first, 0.0, sin), F32))


def _na_mask():
    qc = np.arange(GRID_W)[:, None]
    kc = (np.arange(LANES) % GRID_W)[None, :]
    ws = np.clip(qc - NA_KC // 2, 0, GRID_W - NA_KC)
    ok = (kc >= ws) & (kc < ws + NA_KC)
    first = (np.arange(LANES) < GRID_W)[None, :]
    masks = [ok, ok & first, ok & ~first]
    return jnp.asarray(np.stack([np.where(m, 0.0, NEG_BIG) for m in masks]), F32)


def kernel(x_prompt, x_sample, cache_diff_k, cache_diff_v, cache_na_k, cache_na_v, c, c_ctx, norm_mix_g, norm_ffn_g, w_ada, b_ada, w_in, w_out, conv_w, conv_b, conv_ln_g, conv_ln_b, diff_qn_g, diff_kn_g, diff_lam_q1, diff_lam_k1, diff_lam_q2, diff_lam_k2, diff_subln_g, na_qn_g, na_kn_g, na_rpb, moe_wr_g, moe_br_g, moe_wr_e, moe_br_e, moe_w_gate, moe_w_up, moe_w_down):
    cvec = jnp.zeros((MOD_ROWS, D_MODEL), F32).at[0].set(c_ctx).at[1:1 + DEC_BATCH].set(c)
    ones = jnp.ones((DEPTH, SEG_W), F32)
    gains = jnp.stack(
        [jnp.tile(diff_qn_g, (1, 2 * N_HEADS)), jnp.tile(diff_kn_g, (1, 2 * N_HEADS)), ones,
         jnp.tile(na_qn_g, (1, N_HEADS)), jnp.tile(na_kn_g, (1, N_HEADS)), ones],
        axis=1).reshape(DEPTH, 6, 1, SEG_W)
    a64, a128 = _norm_tables()
    rope_tabs = _rope_tables()
    na_mask = _na_mask()
    lam_params = jnp.stack([diff_lam_q1, diff_lam_k1, diff_lam_q2, diff_lam_k2], axis=1)
    rpb_pad = jnp.pad(na_rpb, ((0, 0), (0, 0), (0, 0), (0, DQK - na_rpb.shape[-1])))
    rpb_src = jnp.concatenate([rpb_pad[:, :, :-1], rpb_pad[:, :, 1:]], axis=-1)
    rpb_src = jnp.pad(rpb_src, ((0, 0), (0, 0), (0, 16 - NA_PAIRS), (0, 0)))
    wr_t = jnp.concatenate([moe_wr_g, moe_wr_e.reshape(DEPTH, D_MODEL, N_EXPERTS)], axis=2)
    wr_t = jnp.pad(wr_t, ((0, 0), (0, 0), (0, LANES - N_GROUPS - N_EXPERTS)))
    br = jnp.concatenate([moe_br_g, moe_br_e.reshape(DEPTH, N_EXPERTS)], axis=1)
    br = jnp.pad(br, ((0, 0), (0, ROUTE_ROWS - N_GROUPS - N_EXPERTS))).reshape(DEPTH, ROUTE_ROWS, 1)
    tri = jnp.asarray(np.triu(np.ones((CUM_BLK, CUM_BLK), np.float32), 1), BF16)
    norm_mix = norm_mix_g.reshape(DEPTH, 1, D_MODEL)
    norm_ffn = norm_ffn_g.reshape(DEPTH, 1, D_MODEL)
    head_major = lambda cache: jnp.transpose(cache, (0, 1, 3, 2, 4))
    ck_diff, cv_diff = head_major(cache_diff_k), head_major(cache_diff_v)
    ck_na, cv_na = head_major(cache_na_k), head_major(cache_na_v)

    mod8 = _ada_modulation(cvec, w_ada, b_ada).reshape(DEPTH * MOD_ROWS, 1, N_ADA)
    w_out_bf = _cast_bf16(w_out)

    ctx_row = lambda i: 0
    lat_row_pre = lambda i: 1 + i * PRE_TM // DEC_SEQ
    lat_row_mix = lambda i: 1 + i * MIX_TM // DEC_SEQ

    xc = x_prompt.reshape(T_CTX, D_MODEL)
    xl = x_sample.reshape(T_LAT, D_MODEL)
    ctx_segs = []
    for layer in range(DEPTH):
        hc, uc = _pre_project(xc, mod8, norm_mix, w_in, layer, ctx_row, "pre_ctx")
        hl, ul = _pre_project(xl, mod8, norm_mix, w_in, layer, lat_row_pre, "pre_lat")
        sc = _head_project(hc, w_in, gains, a64, a128, None, layer, BATCH, SEQ, "heads_ctx")
        sl = _head_project(hl, w_in, gains, a64, a128, rope_tabs, layer, DEC_BATCH, DEC_SEQ, "heads_lat")
        ctx_segs.append(sc)

        ac = _conformer_conv(uc, conv_w, conv_b, conv_ln_g, conv_ln_b, layer, BATCH, SEQ)
        al = _conformer_conv(ul, conv_w, conv_b, conv_ln_g, conv_ln_b, layer, DEC_BATCH, DEC_SEQ)
        dc, nc = _ctx_attention(sc, lam_params, diff_subln_g, layer)
        dl = _diff_attention_lat(sl, lam_params, diff_subln_g, ck_diff, cv_diff, layer, 256)
        nl = _na_attention(sl, rpb_src, na_mask, ck_na, cv_na, layer)

        x1c, ltc = _mix(ac, dc, nc, w_out_bf, xc, mod8, norm_ffn, wr_t, br, layer, ctx_row)
        x1l, ltl = _mix(al, dl, nl, w_out_bf, xl, mod8, norm_ffn, wr_t, br, layer, lat_row_mix)

        pos, wts, items = _route(jnp.concatenate([ltc, ltl], axis=1), tri)
        pos_flat = pos.reshape(2 * T_ALL)
        items = [items[r, :MOE_ITEMS] for r in range(4)] + [items[4, :1]]
        xs = _dispatch(pos_flat, x1c, x1l, mod8, norm_ffn, layer)
        ys = _experts(items, xs, moe_w_gate, moe_w_up, moe_w_down, layer)
        xc, xl = _combine(pos_flat, x1c, x1l, wts.T, mod8, ys, layer)

    outs = [jnp.transpose(cache, (0, 1, 3, 2, 4)) for cache in _stack_caches(ctx_segs)]
    return (xc.reshape(BATCH, SEQ, D_MODEL), xl.reshape(DEC_BATCH, DEC_SEQ, D_MODEL), *outs)
```

```python
import functools
import math

import numpy as np
import jax
import jax.numpy as jnp
from jax import lax
from jax.experimental import pallas as pl
from jax.experimental.pallas import tpu as pltpu

F32 = jnp.float32
BF16 = jnp.bfloat16
I32 = jnp.int32

D_MODEL = 2048
BATCH = 16
SEQ = 256
DEPTH = 2
DEC_BATCH = 2
DEC_SEQ = 1024
PAST_LEN = 256
GRID_W = 64
GRID_ROWS = DEC_SEQ // GRID_W
HEAD_DIM = 128
C_CONV = 512
CONV_K = 31
N_HEADS = 6
DQK = 64
NA_KR = 8
NA_KC = 16
ROPE_BASE = 10000.0
N_GROUPS = 4
E_PER_GROUP = 4
N_EXPERTS = 16
D_FF = 512
N_ADA = 6 * D_MODEL
W_IN_COLS = 2 * C_CONV + 6 * N_HEADS * HEAD_DIM
T_CTX = BATCH * SEQ
T_LAT = DEC_BATCH * DEC_SEQ
T_ALL = T_CTX + T_LAT

LANES = 128
MOD_ROWS = 8
PRE_TM = 512
HEADS_TM = 1024
MIX_TM = 512
MOE_TM = 256
MOE_ITEMS = 2 * T_ALL // MOE_TM + N_EXPERTS
ROUTE_ROWS = 32
CUM_BLK = 512
VMEM_LIMIT = 56 * 1024 * 1024
NEG_BIG = -1e30


def _cparams(sem):
    return pltpu.CompilerParams(dimension_semantics=sem, vmem_limit_bytes=VMEM_LIMIT)


def _sigmoid(x):
    return 1.0 / (1.0 + jnp.exp(-x))


def _silu(x):
    return x * _sigmoid(x)


def _lambda_init(layer):
    return 0.8 - 0.6 * math.exp(-0.3 * layer)


def _ada_kernel(c_ref, w_ref, b_ref, o_ref):
    c = c_ref[...]
    s = _silu(c).astype(BF16)
    o_ref[...] = jnp.dot(s, w_ref[...].astype(BF16), preferred_element_type=F32) + b_ref[...]


def _ada_modulation(cvec, w_ada, b_ada):
    tn = 1024
    return pl.pallas_call(
        _ada_kernel,
        grid=(DEPTH, N_ADA // tn),
        in_specs=[
            pl.BlockSpec((MOD_ROWS, D_MODEL), lambda l, j: (0, 0)),
            pl.BlockSpec((None, D_MODEL, tn), lambda l, j: (l, 0, j)),
            pl.BlockSpec((None, 1, tn), lambda l, j: (l, 0, j)),
        ],
        out_specs=pl.BlockSpec((None, MOD_ROWS, tn), lambda l, j: (l, 0, j)),
        out_shape=jax.ShapeDtypeStruct((DEPTH, MOD_ROWS, N_ADA), F32),
        compiler_params=_cparams(("arbitrary", "arbitrary")),
        name="ada_modulation",
    )(cvec, w_ada, b_ada.reshape(DEPTH, 1, N_ADA))


def _mod_spec(layer, chunk, row_of_block):
    return pl.BlockSpec((None, 1, D_MODEL),
                        lambda i, *_: (layer * MOD_ROWS + row_of_block(i), 0, chunk))


def _modulate(x, g, shift, scale):
    ms = jnp.mean(x * x, axis=-1, keepdims=True)
    return x * lax.rsqrt(ms + 1e-6) * (g * (1.0 + scale)) + shift


def _pre_kernel(x_ref, sh_ref, sc_ref, g_ref, w_ref, h_ref, u_ref, wbf_ref):
    @pl.when(pl.program_id(0) == 0)
    def _():
        wbf_ref[...] = w_ref[...].astype(BF16)

    h = _modulate(x_ref[...], g_ref[...], sh_ref[...], sc_ref[...]).astype(BF16)
    h_ref[...] = h
    y = jnp.dot(h, wbf_ref[...], preferred_element_type=F32)
    u_ref[...] = y[:, :C_CONV] * _sigmoid(y[:, C_CONV:])


def _pre_project(x, mod8, norm_g, w_in, layer, row_of_block, name):
    t = x.shape[0]
    return pl.pallas_call(
        _pre_kernel,
        grid=(t // PRE_TM,),
        in_specs=[
            pl.BlockSpec((PRE_TM, D_MODEL), lambda i: (i, 0)),
            _mod_spec(layer, 0, row_of_block),
            _mod_spec(layer, 1, row_of_block),
            pl.BlockSpec((None, 1, D_MODEL), lambda i: (layer, 0, 0)),
            pl.BlockSpec((None, D_MODEL, 2 * C_CONV), lambda i: (layer, 0, 0)),
        ],
        out_specs=[pl.BlockSpec((PRE_TM, D_MODEL), lambda i: (i, 0)),
                   pl.BlockSpec((PRE_TM, C_CONV), lambda i: (i, 0))],
        out_shape=[jax.ShapeDtypeStruct((t, D_MODEL), BF16), jax.ShapeDtypeStruct((t, C_CONV), F32)],
        scratch_shapes=[pltpu.VMEM((D_MODEL, 2 * C_CONV), BF16)],
        compiler_params=_cparams(("arbitrary",)),
        name=name,
    )(x, mod8, mod8, norm_g, w_in)


SEG_DQ, SEG_DK, SEG_DV, SEG_NQ, SEG_NK, SEG_NV = range(6)
SEG_W = N_HEADS * HEAD_DIM
HEADS_WBLK = 256
HEADS_RC = 512


def _heads_kernel(h_ref, w0_ref, w1_ref, w2_ref, gain_ref, a64_ref, a128_ref, *rest, rope, per_batch):
    if rope:
        cos_ref, sa_ref, sb_ref, o_ref = rest
    else:
        (o_ref,) = rest
    j = pl.program_id(0)

    def finish(seg):
        n_g = HEADS_WBLK // LANES
        for c, w_ref in enumerate((w0_ref, w1_ref, w2_ref)):
            w = w_ref[...].astype(BF16)
            gain = gain_ref[:, c * HEADS_WBLK:(c + 1) * HEADS_WBLK]
            for r0 in range(0, HEADS_TM, HEADS_RC):
                y = jnp.dot(h_ref[r0:r0 + HEADS_RC, :], w, preferred_element_type=F32)
                for g in range(n_g):
                    hd = c * n_g + g
                    yg = y[:, g * LANES:(g + 1) * LANES]
                    if seg in (SEG_NQ, SEG_NK):
                        ms = jnp.mean(yg * yg, axis=-1, keepdims=True)
                        yg = yg * lax.rsqrt(ms + 1e-6) * gain[:, g * LANES:(g + 1) * LANES]
                    elif seg in (SEG_DQ, SEG_DK):
                        sq = yg * yg
                        low = lax.broadcasted_iota(I32, sq.shape, 1) < DQK
                        s_all = jnp.sum(sq, axis=-1, keepdims=True)
                        s_lo = jnp.sum(jnp.where(low, sq, 0.0), axis=-1, keepdims=True)
                        ms = jnp.where(low, s_lo, s_all - s_lo) * (1.0 / DQK)
                        yg = yg * lax.rsqrt(ms + 1e-6) * gain[:, g * LANES:(g + 1) * LANES]
                    if rope and seg in (SEG_DQ, SEG_DK):
                        rs = slice(r0, r0 + HEADS_RC)
                        yg = (yg * cos_ref[rs, :] + pltpu.roll(yg, LANES - 16, 1) * sa_ref[rs, :]
                              + pltpu.roll(yg, 16, 1) * sb_ref[rs, :])
                    yg = yg.astype(o_ref.dtype)
                    if per_batch == 1:
                        o_ref[hd, r0:r0 + HEADS_RC, :] = yg
                    else:
                        rows = HEADS_TM // per_batch
                        for q in range(HEADS_RC // rows):
                            o_ref[r0 // rows + q, hd] = yg[q * rows:(q + 1) * rows]

    for seg in range(6):
        pl.when(j == seg)(functools.partial(finish, seg))


def _head_project(h, w_in, gains, a64, a128, rope_tabs, layer, batch, seq, name):
    t = h.shape[0]
    rope = rope_tabs is not None
    per_batch = HEADS_TM // seq
    col0 = 2 * C_CONV // HEADS_WBLK
    wspec = lambda c: pl.BlockSpec((None, D_MODEL, HEADS_WBLK),
                                   lambda j, i: (layer, 0, col0 + (SEG_W // HEADS_WBLK) * j + c))
    in_specs = [
        pl.BlockSpec((HEADS_TM, D_MODEL), lambda j, i: (i, 0)),
        wspec(0), wspec(1), wspec(2),
        pl.BlockSpec((None, None, 1, SEG_W), lambda j, i: (layer, j, 0, 0)),
        pl.BlockSpec((HEADS_WBLK, HEADS_WBLK), lambda j, i: (0, 0)),
        pl.BlockSpec((HEADS_WBLK, HEADS_WBLK), lambda j, i: (0, 0)),
    ]
    args = [h, w_in, w_in, w_in, gains, a64, a128]
    if rope:
        in_specs += [pl.BlockSpec((DEC_SEQ, LANES), lambda j, i: (0, 0))] * 3
        args += list(rope_tabs)
    if per_batch == 1:
        out_spec = pl.BlockSpec((None, None, N_HEADS, seq, HEAD_DIM), lambda j, i: (j, i, 0, 0, 0))
    else:
        out_spec = pl.BlockSpec((None, per_batch, N_HEADS, seq, HEAD_DIM), lambda j, i: (j, i, 0, 0, 0))
    return pl.pallas_call(
        functools.partial(_heads_kernel, rope=rope, per_batch=per_batch),
        grid=(6, t // HEADS_TM),
        in_specs=in_specs,
        out_specs=out_spec,
        out_shape=jax.ShapeDtypeStruct((6, batch, N_HEADS, seq, HEAD_DIM), BF16 if rope else F32),
        compiler_params=_cparams(("arbitrary", "arbitrary")),
        name=name,
    )(*args)


CONV_CH = 128
CONV_HALO = 16


def _conv_kernel(u_ref, w_ref, b_ref, lg_ref, lb_ref, o_ref, win_ref, y_ref, *, seq):
    c = pl.program_id(1)
    n_chunks = seq // CONV_CH
    base = pl.multiple_of(c * CONV_CH, CONV_CH)

    win_ref[CONV_HALO:CONV_HALO + CONV_CH, :] = u_ref[pl.ds(base, CONV_CH), :]
    lo_start = pl.multiple_of(jnp.maximum(base - CONV_HALO, 0), CONV_HALO)
    hi_start = pl.multiple_of(jnp.minimum(base + CONV_CH, seq - CONV_HALO), CONV_HALO)
    lo_keep = (c > 0).astype(F32)
    hi_keep = (c < n_chunks - 1).astype(F32)
    win_ref[0:CONV_HALO, :] = u_ref[pl.ds(lo_start, CONV_HALO), :] * lo_keep
    win_ref[CONV_HALO + CONV_CH:, :] = u_ref[pl.ds(hi_start, CONV_HALO), :] * hi_keep

    off = CONV_HALO - CONV_K // 2
    sub = 8
    win_rows = CONV_CH + 2 * CONV_HALO
    for g in range(C_CONV // LANES):
        sl = slice(g * LANES, (g + 1) * LANES)
        window = win_ref[:, sl]
        acc = jnp.zeros((CONV_CH, LANES), F32) + b_ref[:, sl]
        for phase in range(sub):
            taps = [k for k in range(CONV_K) if (off + k) % sub == phase]
            if not taps:
                continue
            shifted = window if phase == 0 else pltpu.roll(window, win_rows - phase, 0)
            for k in taps:
                a = (off + k) // sub * sub
                acc = acc + shifted[a:a + CONV_CH, :] * w_ref[k:k + 1, sl]
        y_ref[:, sl] = acc

    y = y_ref[...]
    mu = jnp.mean(y, axis=-1, keepdims=True)
    yc = y - mu
    var = jnp.mean(yc * yc, axis=-1, keepdims=True)
    z = yc * lax.rsqrt(var + 1e-5) * lg_ref[...] + lb_ref[...]
    o_ref[...] = _silu(z).astype(o_ref.dtype)


def _conformer_conv(u, conv_w, conv_b, ln_g, ln_b, layer, batch, seq):
    t = u.shape[0]
    n_chunks = seq // CONV_CH
    vec = lambda: pl.BlockSpec((None, 1, C_CONV), lambda b, c: (layer, 0, 0))
    return pl.pallas_call(
        functools.partial(_conv_kernel, seq=seq),
        grid=(batch, n_chunks),
        in_specs=[
            pl.BlockSpec((seq, C_CONV), lambda b, c: (b, 0)),
            pl.BlockSpec((None, CONV_K, C_CONV), lambda b, c: (layer, 0, 0)),
            vec(), vec(), vec(),
        ],
        out_specs=pl.BlockSpec((CONV_CH, C_CONV), lambda b, c: (b * n_chunks + c, 0)),
        out_shape=jax.ShapeDtypeStruct((t, C_CONV), BF16),
        scratch_shapes=[pltpu.VMEM((CONV_CH + 2 * CONV_HALO, C_CONV), F32),
                        pltpu.VMEM((CONV_CH, C_CONV), F32)],
        compiler_params=_cparams(("arbitrary", "arbitrary")),
        name="conv_%d" % seq,
    )(u, conv_w, conv_b.reshape(DEPTH, 1, C_CONV), ln_g.reshape(DEPTH, 1, C_CONV),
      ln_b.reshape(DEPTH, 1, C_CONV))


_NT = (((1,), (1,)), ((), ()))


LOG2E = 1.4426950408889634


def _softmax_rows(s):
    m = jnp.max(s, axis=-1, keepdims=True)
    e = jnp.exp(s - m)
    return e / jnp.sum(e, axis=-1, keepdims=True)


def _exp2_rows(s2):
    e = jnp.exp2(s2 - jnp.max(s2, axis=-1, keepdims=True))
    return e, 1.0 / jnp.sum(e, axis=-1, keepdims=True)


def _diff_lambda(lam_ref, lam_init):
    lv = lam_ref[...]
    return (jnp.exp(jnp.sum(lv[0:1] * lv[1:2], axis=-1, keepdims=True))
            - jnp.exp(jnp.sum(lv[2:3] * lv[3:4], axis=-1, keepdims=True)) + lam_init)


def _diff_head(q, k, v, lam, g, lam_init, long_keys):
    lane = lax.broadcasted_iota(I32, q.shape, 1)
    if long_keys:
        qs = q * (DQK ** -0.5 * LOG2E)
        q0 = jnp.where(lane < DQK, qs, 0.0).astype(BF16)
        q1 = jnp.where(lane >= DQK, qs, 0.0).astype(BF16)
        e0, r0 = _exp2_rows(lax.dot_general(q0, k, _NT, preferred_element_type=F32))
        e1, r1 = _exp2_rows(lax.dot_general(q1, k, _NT, preferred_element_type=F32))
        o = (jnp.dot(e0.astype(BF16), v, preferred_element_type=F32) * r0
             - jnp.dot(e1.astype(BF16), v, preferred_element_type=F32) * (lam * r1))
    else:
        q0 = jnp.where(lane < DQK, q, 0.0).astype(BF16)
        q1 = jnp.where(lane >= DQK, q, 0.0).astype(BF16)
        scale = DQK ** -0.5
        p0 = _softmax_rows(lax.dot_general(q0, k, _NT, preferred_element_type=F32) * scale)
        p1 = _softmax_rows(lax.dot_general(q1, k, _NT, preferred_element_type=F32) * scale)
        o = jnp.dot((p0 - lam * p1).astype(BF16), v, preferred_element_type=F32)
    ms = jnp.mean(o * o, axis=-1, keepdims=True)
    return o * lax.rsqrt(ms + 1e-5) * g * (1.0 - lam_init)


def _seg_spec(seg, rows, index):
    def imap(*ids):
        b, h, r = index(*ids)
        return (seg, b, h, r, 0)
    return pl.BlockSpec((None, None, None, rows, HEAD_DIM), imap)


def _diff_attn_lat_kernel(lam_ref, g_ref, q_ref, k_ref, v_ref, ck_ref, cv_ref, o_ref, *, lam_init):
    k = jnp.concatenate([k_ref[...].astype(BF16), ck_ref[...].astype(BF16)], axis=0)
    v = jnp.concatenate([v_ref[...].astype(BF16), cv_ref[...].astype(BF16)], axis=0)
    lam = _diff_lambda(lam_ref, lam_init)
    o_ref[...] = _diff_head(q_ref[...].astype(F32), k, v, lam, g_ref[...], lam_init,
                            True).astype(o_ref.dtype)


def _diff_attention_lat(segs, lam_params, subln_g, cache_k, cache_v, layer, tq):
    nq = DEC_SEQ // tq
    cspec = pl.BlockSpec((None, None, None, PAST_LEN, HEAD_DIM), lambda b, h, qi: (b, layer, h, 0, 0))
    return pl.pallas_call(
        functools.partial(_diff_attn_lat_kernel, lam_init=_lambda_init(layer)),
        grid=(DEC_BATCH, N_HEADS, nq),
        in_specs=[
            pl.BlockSpec((None, 4, DQK), lambda b, h, qi: (layer, 0, 0)),
            pl.BlockSpec((None, 1, HEAD_DIM), lambda b, h, qi: (layer, 0, 0)),
            _seg_spec(SEG_DQ, tq, lambda b, h, qi: (b, h, qi)),
            _seg_spec(SEG_DK, DEC_SEQ, lambda b, h, qi: (b, h, 0)),
            _seg_spec(SEG_DV, DEC_SEQ, lambda b, h, qi: (b, h, 0)),
            cspec, cspec,
        ],
        out_specs=pl.BlockSpec((tq, HEAD_DIM), lambda b, h, qi: (b * nq + qi, h)),
        out_shape=jax.ShapeDtypeStruct((T_LAT, SEG_W), BF16),
        compiler_params=_cparams(("arbitrary", "arbitrary", "arbitrary")),
        name="diff_attn_lat",
    )(lam_params, subln_g.reshape(DEPTH, 1, HEAD_DIM), segs, segs, segs, cache_k, cache_v)


CACHE_SEGS = (SEG_DK, SEG_DV, SEG_NK, SEG_NV)


def _ctx_attn_kernel(lam_ref, g_ref, *refs, lam_init, n_prev):
    seg_refs, refs = refs[:6], refs[6:]
    dq_ref, dk_ref, dv_ref, nq_ref, nk_ref, nv_ref = seg_refs
    prev_refs, refs = refs[:n_prev * len(CACHE_SEGS)], refs[n_prev * len(CACHE_SEGS):]
    d_ref, n_ref = refs[:2]
    cache_refs = refs[2:]
    lam = _diff_lambda(lam_ref, lam_init)
    for h in range(N_HEADS):
        sl = slice(h * HEAD_DIM, (h + 1) * HEAD_DIM)
        d_ref[:, sl] = _diff_head(dq_ref[h], dk_ref[h].astype(BF16), dv_ref[h].astype(BF16), lam,
                                  g_ref[...], lam_init, False).astype(d_ref.dtype)
        s = lax.dot_general(nq_ref[h].astype(BF16), nk_ref[h].astype(BF16), _NT,
                            preferred_element_type=F32) * (HEAD_DIM ** -0.5)
        n_ref[:, sl] = jnp.dot(_softmax_rows(s).astype(BF16), nv_ref[h].astype(BF16),
                               preferred_element_type=F32).astype(n_ref.dtype)
    for n, (seg, c_ref) in enumerate(zip(CACHE_SEGS, cache_refs)):
        for layer in range(n_prev):
            c_ref[layer] = prev_refs[layer * len(CACHE_SEGS) + n][...]
        c_ref[n_prev] = seg_refs[seg][...]


def _ctx_attention(segs, prev_segs, lam_params, subln_g, layer):
    seg = lambda s: pl.BlockSpec((None, None, N_HEADS, SEQ, HEAD_DIM), lambda b: (s, b, 0, 0, 0))
    out = pl.BlockSpec((SEQ, SEG_W), lambda b: (b, 0))
    in_specs = [pl.BlockSpec((None, 4, DQK), lambda b: (layer, 0, 0)),
                pl.BlockSpec((None, 1, HEAD_DIM), lambda b: (layer, 0, 0))] + [seg(s) for s in range(6)]
    args = [lam_params, subln_g.reshape(DEPTH, 1, HEAD_DIM)] + [segs] * 6
    out_specs = [out, out]
    out_shape = [jax.ShapeDtypeStruct((T_CTX, SEG_W), BF16)] * 2
    for prev in prev_segs:
        in_specs += [seg(s) for s in CACHE_SEGS]
        args += [prev] * len(CACHE_SEGS)
    if prev_segs:
        assert len(prev_segs) == DEPTH - 1
        cache = pl.BlockSpec((None, DEPTH, N_HEADS, SEQ, HEAD_DIM), lambda b: (b, 0, 0, 0, 0))
        out_specs += [cache] * len(CACHE_SEGS)
        out_shape += [jax.ShapeDtypeStruct((BATCH, DEPTH, N_HEADS, SEQ, HEAD_DIM), F32)] * len(CACHE_SEGS)
    return pl.pallas_call(
        functools.partial(_ctx_attn_kernel, lam_init=_lambda_init(layer), n_prev=len(prev_segs)),
        grid=(BATCH,),
        in_specs=in_specs,
        out_specs=out_specs,
        out_shape=out_shape,
        compiler_params=_cparams(("arbitrary",)),
        name="ctx_attn_caches" if prev_segs else "ctx_attn",
    )(*args)


NA_PAIRS = 2 * NA_KR - 2
NA_QROWS = 4
NA_KROWS = 12


def _na_key_block(chunk):
    first = min(max(chunk * NA_QROWS - NA_KR // 2, 0), GRID_ROWS - NA_KR)
    return min(first - first % 2, GRID_ROWS - NA_KROWS)


def _na_attn_kernel(src_ref, mask_ref, q_ref, k_ref, v_ref, ck_ref, cv_ref, o_ref, bias_ref):
    for d in range(NA_PAIRS):
        row = jnp.broadcast_to(src_ref[d:d + 1, :] * LOG2E, (GRID_W, LANES))
        tile = pltpu.roll(row, LANES - (NA_KC - 1), 1, stride=1, stride_axis=0)
        for v in range(3):
            bias_ref[v, d] = tile + mask_ref[v]
    dead = jnp.full((GRID_W, LANES), NEG_BIG, F32)
    ck = ck_ref[...].astype(BF16)
    cv = cv_ref[...].astype(BF16)
    scale = HEAD_DIM ** -0.5 * LOG2E
    for chunk in range(GRID_ROWS // NA_QROWS):
        kb = _na_key_block(chunk)
        rows = []
        for qr in range(chunk * NA_QROWS, (chunk + 1) * NA_QROWS):
            start = min(max(qr - NA_KR // 2, 0), GRID_ROWS - NA_KR)
            assert kb <= start and start + NA_KR <= kb + NA_KROWS
            tiles = []
            for m in range(NA_KROWS // 2):
                r0 = kb + 2 * m
                live0 = start <= r0 < start + NA_KR
                live1 = start <= r0 + 1 < start + NA_KR
                d = r0 - qr + NA_KR - 1
                assert not (live0 or live1) or 0 <= d < NA_PAIRS
                if live0 and live1:
                    tiles.append(bias_ref[0, d])
                elif live0:
                    tiles.append(bias_ref[1, d])
                elif live1:
                    tiles.append(bias_ref[2, d])
                else:
                    tiles.append(dead)
            rows.append(jnp.concatenate(tiles, axis=1))
        bias = jnp.concatenate(rows, axis=0)
        qs = slice(chunk * NA_QROWS * GRID_W, (chunk + 1) * NA_QROWS * GRID_W)
        ks = slice(kb * GRID_W, (kb + NA_KROWS) * GRID_W)
        q = (q_ref[qs, :].astype(F32) * scale).astype(BF16)
        kl = k_ref[ks, :].astype(BF16)
        vl = v_ref[ks, :].astype(BF16)
        s_loc = lax.dot_general(q, kl, _NT, preferred_element_type=F32) + bias
        s_ctx = lax.dot_general(q, ck, _NT, preferred_element_type=F32)
        m = jnp.maximum(jnp.max(s_loc, axis=-1, keepdims=True), jnp.max(s_ctx, axis=-1, keepdims=True))
        e_loc = jnp.exp2(s_loc - m)
        e_ctx = jnp.exp2(s_ctx - m)
        r = 1.0 / (jnp.sum(e_loc, axis=-1, keepdims=True) + jnp.sum(e_ctx, axis=-1, keepdims=True))
        o_ref[qs, :] = ((jnp.dot(e_loc.astype(BF16), vl, preferred_element_type=F32)
                         + jnp.dot(e_ctx.astype(BF16), cv, preferred_element_type=F32)) * r
                        ).astype(o_ref.dtype)


def _na_attention(segs, rpb_src, na_mask, cache_k, cache_v, layer):
    blk = lambda s: _seg_spec(s, DEC_SEQ, lambda b, h: (b, h, 0))
    cspec = pl.BlockSpec((None, None, None, PAST_LEN, HEAD_DIM), lambda b, h: (b, layer, h, 0, 0))
    return pl.pallas_call(
        _na_attn_kernel,
        grid=(DEC_BATCH, N_HEADS),
        in_specs=[
            pl.BlockSpec((None, None, 16, LANES), lambda b, h: (layer, h, 0, 0)),
            pl.BlockSpec((3, GRID_W, LANES), lambda b, h: (0, 0, 0)),
            blk(SEG_NQ), blk(SEG_NK), blk(SEG_NV), cspec, cspec,
        ],
        out_specs=pl.BlockSpec((DEC_SEQ, HEAD_DIM), lambda b, h: (b, h)),
        out_shape=jax.ShapeDtypeStruct((T_LAT, SEG_W), BF16),
        scratch_shapes=[pltpu.VMEM((3, NA_PAIRS, GRID_W, LANES), F32)],
        compiler_params=_cparams(("arbitrary", "arbitrary")),
        name="na_attn_lat",
    )(rpb_src, na_mask, segs, segs, segs, cache_k, cache_v)


def _split_f32(x, n):
    terms = []
    for _ in range(n):
        t = x.astype(BF16).astype(F32)
        terms.append(t)
        x = x - t
    return terms


def _mix_kernel(a_ref, d_ref, n_ref, w_ref, x_ref, g1_ref, sh_ref, sc_ref, ng_ref, wr_ref, br_ref,
                x1_ref, lt_ref):
    k0, k1 = C_CONV, C_CONV + N_HEADS * HEAD_DIM
    mixed = (jnp.dot(a_ref[...].astype(BF16), w_ref[0:k0, :], preferred_element_type=F32)
             + jnp.dot(d_ref[...].astype(BF16), w_ref[k0:k1, :], preferred_element_type=F32)
             + jnp.dot(n_ref[...].astype(BF16), w_ref[k1:, :], preferred_element_type=F32))
    x1 = x_ref[...] + g1_ref[...] * mixed
    x1_ref[...] = x1
    h2 = _modulate(x1, ng_ref[...], sh_ref[...], sc_ref[...])
    wh, wm, wl = _split_f32(wr_ref[...], 3)
    w_cat = (wh + pltpu.roll(wm, ROUTE_ROWS, 1) + pltpu.roll(wl, 2 * ROUTE_ROWS, 1)).astype(BF16)
    hh, hm = _split_f32(h2, 2)
    r = jnp.dot(jnp.concatenate([hh, hm], axis=0).astype(BF16), w_cat, preferred_element_type=F32)
    rh, rm = r[:MIX_TM], r[MIX_TM:]
    back = lambda v, groups: pltpu.roll(v, LANES - groups * ROUTE_ROWS, 1)
    small = (back(rm, 1) + back(rh, 2)) + (back(rh, 1) + rm)
    logits = (small + rh).T[:ROUTE_ROWS, :]
    lt_ref[...] = logits + br_ref[...]


def _mix(a_out, d_out, n_out, w_out_bf, x, mod8, norm_g, wr_t, br, layer, row_of_block):
    t = x.shape[0]
    rows = lambda width: pl.BlockSpec((MIX_TM, width), lambda i: (i, 0))
    return pl.pallas_call(
        _mix_kernel,
        grid=(t // MIX_TM,),
        in_specs=[
            rows(C_CONV), rows(N_HEADS * HEAD_DIM), rows(N_HEADS * HEAD_DIM),
            pl.BlockSpec((None, D_MODEL, D_MODEL), lambda i: (layer, 0, 0)),
            rows(D_MODEL),
            _mod_spec(layer, 2, row_of_block),
            _mod_spec(layer, 3, row_of_block),
            _mod_spec(layer, 4, row_of_block),
            pl.BlockSpec((None, 1, D_MODEL), lambda i: (layer, 0, 0)),
            pl.BlockSpec((None, D_MODEL, LANES), lambda i: (layer, 0, 0)),
            pl.BlockSpec((None, ROUTE_ROWS, 1), lambda i: (layer, 0, 0)),
        ],
        out_specs=[rows(D_MODEL), pl.BlockSpec((ROUTE_ROWS, MIX_TM), lambda i: (0, i))],
        out_shape=[jax.ShapeDtypeStruct((t, D_MODEL), F32),
                   jax.ShapeDtypeStruct((ROUTE_ROWS, t), F32)],
        compiler_params=_cparams(("arbitrary",)),
        name="mix_%d" % t,
    )(a_out, d_out, n_out, w_out_bf, x, mod8, mod8, mod8, norm_g, wr_t, br)


def _cast_kernel(x_ref, o_ref):
    o_ref[...] = x_ref[...].astype(BF16)


def _cast_bf16(w):
    tm = 512
    spec = pl.BlockSpec((None, tm, D_MODEL), lambda l, i: (l, i, 0))
    return pl.pallas_call(
        _cast_kernel, grid=(DEPTH, D_MODEL // tm), in_specs=[spec], out_specs=spec,
        out_shape=jax.ShapeDtypeStruct(w.shape, BF16),
        compiler_params=_cparams(("arbitrary", "arbitrary")), name="cast_w_out",
    )(w)


def _route_kernel(lt_ref, tri_ref, pos_ref, wt_ref, item_ref):
    t = T_ALL
    lg = lt_ref[0:N_GROUPS, :]
    eg = jnp.exp(lg - jnp.max(lg, axis=0, keepdims=True))
    pg = eg / jnp.sum(eg, axis=0, keepdims=True)
    pg_top = jnp.max(pg, axis=0, keepdims=True)
    gi = lax.broadcasted_iota(I32, pg.shape, 0).astype(F32)
    g_idx = jnp.min(jnp.where(pg == pg_top, gi, float(N_GROUPS)), axis=0, keepdims=True)

    le = jnp.zeros((E_PER_GROUP, t), F32)
    for g in range(N_GROUPS):
        rows = lt_ref[N_GROUPS + g * E_PER_GROUP:N_GROUPS + (g + 1) * E_PER_GROUP, :]
        le = jnp.where(g_idx == float(g), rows, le)
    ee = jnp.exp(le - jnp.max(le, axis=0, keepdims=True))
    pe = ee / jnp.sum(ee, axis=0, keepdims=True)
    ei = lax.broadcasted_iota(I32, pe.shape, 0).astype(F32)
    p1 = jnp.max(pe, axis=0, keepdims=True)
    i1 = jnp.min(jnp.where(pe == p1, ei, float(E_PER_GROUP)), axis=0, keepdims=True)
    pe_rest = jnp.where(ei == i1, -1.0, pe)
    p2 = jnp.max(pe_rest, axis=0, keepdims=True)
    i2 = jnp.min(jnp.where(pe_rest == p2, ei, float(E_PER_GROUP)), axis=0, keepdims=True)
    den = p1 + p2
    wt_ref[0:1, :] = pg_top * (p1 / den)
    wt_ref[1:2, :] = pg_top * (p2 / den)
    e1 = g_idx * E_PER_GROUP + i1
    e2 = g_idx * E_PER_GROUP + i2

    erow = lax.broadcasted_iota(I32, (N_EXPERTS, t), 0).astype(F32)
    oh1 = (erow == e1).astype(F32)
    oh2 = (erow == e2).astype(F32)
    cnt = oh1 + oh2
    carry = jnp.zeros((N_EXPERTS, 1), F32)
    ranks = []
    for b in range(t // CUM_BLK):
        blk = cnt[:, b * CUM_BLK:(b + 1) * CUM_BLK]
        ranks.append(jnp.dot(blk.astype(BF16), tri_ref[...], preferred_element_type=F32) + carry)
        carry = carry + jnp.sum(blk, axis=1, keepdims=True)
    rank = jnp.concatenate(ranks, axis=1)

    erow_l = lax.broadcasted_iota(I32, (N_EXPERTS, LANES), 0)

    def excl_scan(v):
        inc = v
        for s in (1, 2, 4, 8):
            inc = inc + jnp.where(erow_l >= s, pltpu.roll(inc, s, 0), 0.0)
        return inc - v

    total = jnp.broadcast_to(carry, (N_EXPERTS, LANES))
    offs = excl_scan(total)
    pos_ref[0:1, :] = jnp.sum(oh1 * (rank + offs[:, 0:1]), axis=0, keepdims=True).astype(I32)
    pos_ref[1:2, :] = jnp.sum(oh2 * (rank + offs[:, 0:1]), axis=0, keepdims=True).astype(I32)

    shift = int(math.log2(MOE_TM))
    offs_i = offs.astype(I32)
    total_i = total.astype(I32)
    first_tile = lax.shift_right_logical(offs_i, shift)
    last_tile = lax.shift_right_logical(offs_i + total_i - 1, shift)
    n_item = jnp.where(total_i > 0, last_tile - first_tile + 1, 0).astype(F32)
    item_start = excl_scan(n_item)
    item_end = item_start + n_item
    kk = lax.broadcasted_iota(I32, (N_EXPERTS, LANES), 1).astype(F32)
    item_e = jnp.minimum(jnp.sum((item_end <= kk).astype(F32), axis=0, keepdims=True),
                         float(N_EXPERTS - 1))
    sel = (erow_l.astype(F32) == item_e).astype(F32)
    pick = lambda v: jnp.sum(sel * v, axis=0, keepdims=True)
    item_ref[0:1, :] = (kk[0:1] + pick(first_tile.astype(F32) - item_start)).astype(I32)
    item_ref[1:2, :] = item_e.astype(I32)
    item_ref[2:3, :] = pick(offs).astype(I32)
    item_ref[3:4, :] = pick(offs + total).astype(I32)
    item_ref[4:5, :] = item_end[N_EXPERTS - 1:N_EXPERTS, :].astype(I32)
    item_ref[5:8, :] = jnp.zeros((3, LANES), I32)


def _route(lt_all, tri):
    full = lambda shape: pl.BlockSpec(shape, lambda i: (0,) * len(shape))
    return pl.pallas_call(
        _route_kernel,
        grid=(1,),
        in_specs=[full((ROUTE_ROWS, T_ALL)), full((CUM_BLK, CUM_BLK))],
        out_specs=[full((2, T_ALL)), full((2, T_ALL)), full((8, LANES))],
        out_shape=[jax.ShapeDtypeStruct((2, T_ALL), I32), jax.ShapeDtypeStruct((2, T_ALL), F32),
                   jax.ShapeDtypeStruct((8, LANES), I32)],
        compiler_params=_cparams(("arbitrary",)),
        name="moe_route",
    )(lt_all, tri)


N_TOK_TILES = T_ALL // MOE_TM
CTX_TILES = T_CTX // MOE_TM


def _tok_row(i):
    return jnp.where(i < CTX_TILES, 0, 1 + (i - CTX_TILES) // (DEC_SEQ // MOE_TM))


N_CHUNK = D_MODEL // LANES
MOE_PITCH = N_CHUNK + 1
SLAB_ROWS = MOE_TM * MOE_PITCH
ISSUE_UNROLL = 4


def _slab_copy(src, src_tok, dst, dst_tok, sem, pitch):
    return pltpu.make_async_copy(src.at[pl.ds(src_tok * pitch, pitch)],
                                 dst.at[pl.ds(dst_tok * pitch, pitch)], sem)


def _to_slabs(ref, value):
    for c in range(N_CHUNK):
        ref[pl.ds(c, MOE_TM, stride=MOE_PITCH), :] = value[:, c * LANES:(c + 1) * LANES]
    ref[pl.ds(N_CHUNK, MOE_TM, stride=MOE_PITCH), :] = jnp.zeros((MOE_TM, LANES), F32)


def _slab_chunk(ref, c):
    return ref[pl.ds(c, MOE_TM, stride=MOE_PITCH), :]


def _dispatch_kernel(pos_ref, xc_ref, xl_ref, sh_ref, sc_ref, ng_ref, xs_ref, h_ref, sem):
    i = pl.program_id(0)
    slot = i % 2
    buf = h_ref.at[slot]

    def drain(s):
        def body(r, carry):
            _slab_copy(h_ref.at[s], 0, xs_ref, 0, sem.at[s], MOE_PITCH).wait()
            return carry
        lax.fori_loop(0, 2 * MOE_TM, body, 0, unroll=ISSUE_UNROLL)

    @pl.when(i >= 2)
    def _():
        drain(slot)

    @pl.when(i < CTX_TILES)
    def _():
        _to_slabs(buf, _modulate(xc_ref[...], ng_ref[...], sh_ref[...], sc_ref[...]))

    @pl.when(i >= CTX_TILES)
    def _():
        _to_slabs(buf, _modulate(xl_ref[...], ng_ref[...], sh_ref[...], sc_ref[...]))

    def issue(r, carry):
        tok = i * MOE_TM + r
        _slab_copy(buf, r, xs_ref, pos_ref[tok], sem.at[slot], MOE_PITCH).start(priority=0)
        _slab_copy(buf, r, xs_ref, pos_ref[T_ALL + tok], sem.at[slot], MOE_PITCH).start(priority=1)
        return carry

    lax.fori_loop(0, MOE_TM, issue, 0, unroll=ISSUE_UNROLL)

    @pl.when(i == N_TOK_TILES - 1)
    def _():
        drain(1 - slot)
        drain(slot)


def _dispatch(pos_flat, x1c, x1l, mod8, norm_g, layer):
    grid_spec = pltpu.PrefetchScalarGridSpec(
        num_scalar_prefetch=1,
        grid=(N_TOK_TILES,),
        in_specs=[
            pl.BlockSpec((MOE_TM, D_MODEL), lambda i, *_: (jnp.minimum(i, CTX_TILES - 1), 0)),
            pl.BlockSpec((MOE_TM, D_MODEL), lambda i, *_: (jnp.maximum(i - CTX_TILES, 0), 0)),
            _mod_spec(layer, 3, _tok_row),
            _mod_spec(layer, 4, _tok_row),
            pl.BlockSpec((None, 1, D_MODEL), lambda i, *_: (layer, 0, 0)),
        ],
        out_specs=pl.BlockSpec(memory_space=pl.ANY),
        scratch_shapes=[pltpu.VMEM((2, SLAB_ROWS, LANES), F32), pltpu.SemaphoreType.DMA((2,))],
    )
    return pl.pallas_call(
        _dispatch_kernel,
        grid_spec=grid_spec,
        out_shape=jax.ShapeDtypeStruct((2 * T_ALL * MOE_PITCH, LANES), F32),
        compiler_params=_cparams(("arbitrary",)),
        name="moe_dispatch",
    )(pos_flat, x1c, x1l, mod8, mod8, norm_g)


N_ROW_TILES = 2 * T_ALL // MOE_TM
X_SLOTS = 3
Y_SLOTS = 2


def _expert_kernel(tile_ref, exp_ref, lo_ref, hi_ref, n_ref, xs_hbm, wg_hbm, wu_hbm, wd_hbm, ys_hbm,
                   x_buf, y_buf, wg_buf, wu_buf, wd_buf, wg_bf, wu_bf, wd_bf, slot_ref, sem, x_sem, y_sem,
                   *, layer):
    k = pl.program_id(0)
    n = n_ref[0]

    def weight_copies(e, s):
        return (pltpu.make_async_copy(wg_hbm.at[layer, e], wg_buf.at[s], sem.at[s, 0]),
                pltpu.make_async_copy(wu_hbm.at[layer, e], wu_buf.at[s], sem.at[s, 1]),
                pltpu.make_async_copy(wd_hbm.at[layer, e], wd_buf.at[s], sem.at[s, 2]))

    def x_copy(t):
        s = t % X_SLOTS
        return pltpu.make_async_copy(xs_hbm.at[pl.ds(t * SLAB_ROWS, SLAB_ROWS)], x_buf.at[s], x_sem.at[s])

    def y_copy(t):
        s = t % Y_SLOTS
        return pltpu.make_async_copy(y_buf.at[s], ys_hbm.at[pl.ds(t * SLAB_ROWS, SLAB_ROWS)], y_sem.at[s])

    @pl.when(k == 0)
    def _():
        slot_ref[0] = 0
        for cp in weight_copies(exp_ref[0], 0):
            cp.start()
        for t in range(X_SLOTS - 1):
            x_copy(t).start()

    @pl.when(k < n)
    def _():
        e = exp_ref[k]
        tile = tile_ref[k]
        prev_tile = tile_ref[jnp.maximum(k - 1, 0)]
        new_expert = jnp.logical_or(k == 0, exp_ref[jnp.maximum(k - 1, 0)] != e)

        @pl.when(jnp.logical_or(k == 0, tile != prev_tile))
        def _():
            x_copy(tile).wait()

            @pl.when(tile + (X_SLOTS - 1) < N_ROW_TILES)
            def _():
                x_copy(tile + (X_SLOTS - 1)).start()

            @pl.when(k > 0)
            def _():
                y_copy(prev_tile).start()

            @pl.when(tile >= Y_SLOTS)
            def _():
                y_copy(tile - Y_SLOTS).wait()

        xs_ref = x_buf.at[tile % X_SLOTS]
        ys_ref = y_buf.at[tile % Y_SLOTS]

        @pl.when(new_expert)
        def _():
            s = slot_ref[0]
            for cp in weight_copies(e, s):
                cp.wait()
            wg_bf[...] = wg_buf[s].astype(BF16)
            wu_bf[...] = wu_buf[s].astype(BF16)
            wd_bf[...] = wd_buf[s].astype(BF16)
            nxt = lax.while_loop(lambda j: jnp.logical_and(j < n, exp_ref[jnp.minimum(j, n - 1)] == e),
                                 lambda j: j + 1, k + 1)

            @pl.when(nxt < n)
            def _():
                for cp in weight_copies(exp_ref[jnp.minimum(nxt, n - 1)], 1 - s):
                    cp.start()
            slot_ref[0] = 1 - s

        x = jnp.concatenate([_slab_chunk(xs_ref, c) for c in range(N_CHUNK)], axis=1).astype(BF16)
        gate = jnp.dot(x, wg_bf[...], preferred_element_type=F32)
        up = jnp.dot(x, wu_bf[...], preferred_element_type=F32)
        hid = (_silu(gate) * up).astype(BF16)
        y = jnp.dot(hid, wd_bf[...], preferred_element_type=F32)
        row0 = tile * MOE_TM
        row = row0 + lax.broadcasted_iota(I32, (MOE_TM, 1), 0)
        mine = jnp.logical_and(row >= lo_ref[k], row < hi_ref[k])
        first = lo_ref[k] <= row0

        @pl.when(first)
        def _():
            _to_slabs(ys_ref, jnp.where(mine, y, 0.0))

        @pl.when(jnp.logical_not(first))
        def _():
            for c in range(N_CHUNK):
                old = _slab_chunk(ys_ref, c)
                ys_ref[pl.ds(c, MOE_TM, stride=MOE_PITCH), :] = jnp.where(
                    mine, y[:, c * LANES:(c + 1) * LANES], old)

        @pl.when(k == n - 1)
        def _():
            y_copy(tile).start()
            y_copy(tile).wait()

            @pl.when(tile >= 1)
            def _():
                y_copy(tile - 1).wait()


def _experts(items, xs, w_gate, w_up, w_down, layer):
    item_tile, item_exp, item_lo, item_hi, n_items = items
    hbm = pl.BlockSpec(memory_space=pl.ANY)
    grid_spec = pltpu.PrefetchScalarGridSpec(
        num_scalar_prefetch=5, grid=(MOE_ITEMS,),
        in_specs=[hbm, hbm, hbm, hbm], out_specs=hbm,
        scratch_shapes=[
            pltpu.VMEM((X_SLOTS, SLAB_ROWS, LANES), F32), pltpu.VMEM((Y_SLOTS, SLAB_ROWS, LANES), F32),
            pltpu.VMEM((2, D_MODEL, D_FF), F32), pltpu.VMEM((2, D_MODEL, D_FF), F32),
            pltpu.VMEM((2, D_FF, D_MODEL), F32),
            pltpu.VMEM((D_MODEL, D_FF), BF16), pltpu.VMEM((D_MODEL, D_FF), BF16),
            pltpu.VMEM((D_FF, D_MODEL), BF16),
            pltpu.SMEM((1,), I32), pltpu.SemaphoreType.DMA((2, 3)),
            pltpu.SemaphoreType.DMA((X_SLOTS,)), pltpu.SemaphoreType.DMA((Y_SLOTS,)),
        ])
    return pl.pallas_call(
        functools.partial(_expert_kernel, layer=layer),
        grid_spec=grid_spec,
        out_shape=jax.ShapeDtypeStruct((2 * T_ALL * MOE_PITCH, LANES), F32),
        compiler_params=_cparams(("arbitrary",)),
        name="moe_experts",
    )(item_tile, item_exp, item_lo, item_hi, n_items, xs, w_gate, w_up, w_down)


def _combine_kernel(pos_ref, xc_ref, xl_ref, wt_ref, g2_ref, ys_ref, oc_ref, ol_ref, y_ref, sem):
    i = pl.program_id(0)
    slot = i % 2

    def gather(tile, s):
        def body(r, carry):
            tok = tile * MOE_TM + r
            _slab_copy(ys_ref, pos_ref[tok], y_ref.at[s, 0], r, sem.at[s], MOE_PITCH).start(priority=0)
            _slab_copy(ys_ref, pos_ref[T_ALL + tok], y_ref.at[s, 1], r, sem.at[s], MOE_PITCH).start(priority=1)
            return carry
        lax.fori_loop(0, MOE_TM, body, 0, unroll=ISSUE_UNROLL)

    @pl.when(i == 0)
    def _():
        gather(0, 0)

    @pl.when(i + 1 < N_TOK_TILES)
    def _():
        gather(i + 1, 1 - slot)

    def drain(r, carry):
        _slab_copy(ys_ref, 0, y_ref.at[slot, 0], 0, sem.at[slot], MOE_PITCH).wait()
        return carry

    lax.fori_loop(0, 2 * MOE_TM, drain, 0, unroll=ISSUE_UNROLL)

    def write(x_ref, o_ref):
        w0 = wt_ref[:, 0:1]
        w1 = wt_ref[:, 1:2]
        for c in range(N_CHUNK):
            sl = slice(c * LANES, (c + 1) * LANES)
            moe = w0 * _slab_chunk(y_ref.at[slot, 0], c) + w1 * _slab_chunk(y_ref.at[slot, 1], c)
            o_ref[:, sl] = x_ref[:, sl] + g2_ref[:, sl] * moe

    pl.when(i < CTX_TILES)(lambda: write(xc_ref, oc_ref))
    pl.when(i >= CTX_TILES)(lambda: write(xl_ref, ol_ref))


def _combine(pos_flat, x1c, x1l, wts_t, mod8, ys, layer):
    cspec = pl.BlockSpec((MOE_TM, D_MODEL), lambda i, *_: (jnp.minimum(i, CTX_TILES - 1), 0))
    lspec = pl.BlockSpec((MOE_TM, D_MODEL), lambda i, *_: (jnp.maximum(i - CTX_TILES, 0), 0))
    grid_spec = pltpu.PrefetchScalarGridSpec(
        num_scalar_prefetch=1,
        grid=(N_TOK_TILES,),
        in_specs=[
            cspec, lspec,
            pl.BlockSpec((MOE_TM, 2), lambda i, *_: (i, 0)),
            _mod_spec(layer, 5, _tok_row),
            pl.BlockSpec(memory_space=pl.ANY),
        ],
        out_specs=[cspec, lspec],
        scratch_shapes=[pltpu.VMEM((2, 2, SLAB_ROWS, LANES), F32), pltpu.SemaphoreType.DMA((2,))],
    )
    return pl.pallas_call(
        _combine_kernel,
        grid_spec=grid_spec,
        out_shape=[jax.ShapeDtypeStruct((T_CTX, D_MODEL), F32),
                   jax.ShapeDtypeStruct((T_LAT, D_MODEL), F32)],
        compiler_params=_cparams(("arbitrary",)),
        name="moe_combine",
    )(pos_flat, x1c, x1l, wts_t, mod8, ys)


def _norm_tables():
    mean_blocks = lambda w: np.kron(np.eye(HEADS_WBLK // w, dtype=np.float32),
                                    np.full((w, w), 1.0 / w, np.float32))
    return jnp.asarray(mean_blocks(DQK), BF16), jnp.asarray(mean_blocks(HEAD_DIM), BF16)


def _rope_tables():
    half = DQK // 2
    inv = 1.0 / (ROPE_BASE ** (np.arange(0, half, 2, dtype=np.float32) / half))
    t = np.arange(DEC_SEQ)
    ang_r = (t // GRID_W).astype(np.float32)[:, None] * inv
    ang_c = (t % GRID_W).astype(np.float32)[:, None] * inv
    ang = np.concatenate([ang_r, ang_r, ang_c, ang_c], axis=-1).astype(np.float32)
    ang = np.concatenate([ang, ang], axis=-1)
    first = (np.arange(LANES) % 32) < 16
    cos, sin = np.cos(ang), np.sin(ang)
    return (jnp.asarray(cos, F32), jnp.asarray(np.where(first, -sin, 0.0), F32),
            jnp.asarray(np.where(first, 0.0, sin), F32))


def _na_mask():
    qc = np.arange(GRID_W)[:, None]
    kc = (np.arange(LANES) % GRID_W)[None, :]
    ws = np.clip(qc - NA_KC // 2, 0, GRID_W - NA_KC)
    ok = (kc >= ws) & (kc < ws + NA_KC)
    first = (np.arange(LANES) < GRID_W)[None, :]
    masks = [ok, ok & first, ok & ~first]
    return jnp.asarray(np.stack([np.where(m, 0.0, NEG_BIG) for m in masks]), F32)


def kernel(x_prompt, x_sample, cache_diff_k, cache_diff_v, cache_na_k, cache_na_v, c, c_ctx, norm_mix_g, norm_ffn_g, w_ada, b_ada, w_in, w_out, conv_w, conv_b, conv_ln_g, conv_ln_b, diff_qn_g, diff_kn_g, diff_lam_q1, diff_lam_k1, diff_lam_q2, diff_lam_k2, diff_subln_g, na_qn_g, na_kn_g, na_rpb, moe_wr_g, moe_br_g, moe_wr_e, moe_br_e, moe_w_gate, moe_w_up, moe_w_down):
    cvec = jnp.zeros((MOD_ROWS, D_MODEL), F32).at[0].set(c_ctx).at[1:1 + DEC_BATCH].set(c)
    ones = jnp.ones((DEPTH, SEG_W), F32)
    gains = jnp.stack(
        [jnp.tile(diff_qn_g, (1, 2 * N_HEADS)), jnp.tile(diff_kn_g, (1, 2 * N_HEADS)), ones,
         jnp.tile(na_qn_g, (1, N_HEADS)), jnp.tile(na_kn_g, (1, N_HEADS)), ones],
        axis=1).reshape(DEPTH, 6, 1, SEG_W)
    a64, a128 = _norm_tables()
    rope_tabs = _rope_tables()
    na_mask = _na_mask()
    lam_params = jnp.stack([diff_lam_q1, diff_lam_k1, diff_lam_q2, diff_lam_k2], axis=1)
    rpb_pad = jnp.pad(na_rpb, ((0, 0), (0, 0), (0, 0), (0, DQK - na_rpb.shape[-1])))
    rpb_src = jnp.concatenate([rpb_pad[:, :, :-1], rpb_pad[:, :, 1:]], axis=-1)
    rpb_src = jnp.pad(rpb_src, ((0, 0), (0, 0), (0, 16 - NA_PAIRS), (0, 0)))
    wr_t = jnp.concatenate([moe_wr_g, moe_wr_e.reshape(DEPTH, D_MODEL, N_EXPERTS)], axis=2)
    wr_t = jnp.pad(wr_t, ((0, 0), (0, 0), (0, LANES - N_GROUPS - N_EXPERTS)))
    br = jnp.concatenate([moe_br_g, moe_br_e.reshape(DEPTH, N_EXPERTS)], axis=1)
    br = jnp.pad(br, ((0, 0), (0, ROUTE_ROWS - N_GROUPS - N_EXPERTS))).reshape(DEPTH, ROUTE_ROWS, 1)
    tri = jnp.asarray(np.triu(np.ones((CUM_BLK, CUM_BLK), np.float32), 1), BF16)
    norm_mix = norm_mix_g.reshape(DEPTH, 1, D_MODEL)
    norm_ffn = norm_ffn_g.reshape(DEPTH, 1, D_MODEL)
    head_major = lambda cache: jnp.transpose(cache, (0, 1, 3, 2, 4))
    ck_diff, cv_diff = head_major(cache_diff_k), head_major(cache_diff_v)
    ck_na, cv_na = head_major(cache_na_k), head_major(cache_na_v)

    mod8 = _ada_modulation(cvec, w_ada, b_ada).reshape(DEPTH * MOD_ROWS, 1, N_ADA)
    w_out_bf = _cast_bf16(w_out)

    ctx_row = lambda i: 0
    lat_row_pre = lambda i: 1 + i * PRE_TM // DEC_SEQ
    lat_row_mix = lambda i: 1 + i * MIX_TM // DEC_SEQ

    xc = x_prompt.reshape(T_CTX, D_MODEL)
    xl = x_sample.reshape(T_LAT, D_MODEL)
    ctx_segs = []
    for layer in range(DEPTH):
        hc, uc = _pre_project(xc, mod8, norm_mix, w_in, layer, ctx_row, "pre_ctx")
        hl, ul = _pre_project(xl, mod8, norm_mix, w_in, layer, lat_row_pre, "pre_lat")
        sc = _head_project(hc, w_in, gains, a64, a128, None, layer, BATCH, SEQ, "heads_ctx")
        sl = _head_project(hl, w_in, gains, a64, a128, rope_tabs, layer, DEC_BATCH, DEC_SEQ, "heads_lat")
        ctx_segs.append(sc)

        ac = _conformer_conv(uc, conv_w, conv_b, conv_ln_g, conv_ln_b, layer, BATCH, SEQ)
        al = _conformer_conv(ul, conv_w, conv_b, conv_ln_g, conv_ln_b, layer, DEC_BATCH, DEC_SEQ)
        last = layer == DEPTH - 1
        dc, nc, *caches = _ctx_attention(sc, ctx_segs[:-1] if last else [], lam_params, diff_subln_g, layer)
        dl = _diff_attention_lat(sl, lam_params, diff_subln_g, ck_diff, cv_diff, layer, 256)
        nl = _na_attention(sl, rpb_src, na_mask, ck_na, cv_na, layer)

        x1c, ltc = _mix(ac, dc, nc, w_out_bf, xc, mod8, norm_ffn, wr_t, br, layer, ctx_row)
        x1l, ltl = _mix(al, dl, nl, w_out_bf, xl, mod8, norm_ffn, wr_t, br, layer, lat_row_mix)

        pos, wts, items = _route(jnp.concatenate([ltc, ltl], axis=1), tri)
        pos_flat = pos.reshape(2 * T_ALL)
        items = [items[r, :MOE_ITEMS] for r in range(4)] + [items[4, :1]]
        xs = _dispatch(pos_flat, x1c, x1l, mod8, norm_ffn, layer)
        ys = _experts(items, xs, moe_w_gate, moe_w_up, moe_w_down, layer)
        xc, xl = _combine(pos_flat, x1c, x1l, wts.T, mod8, ys, layer)

    outs = [jnp.transpose(cache, (0, 1, 3, 2, 4)) for cache in caches]
    return (xc.reshape(BATCH, SEQ, D_MODEL), xl.reshape(DEC_BATCH, DEC_SEQ, D_MODEL), *outs)
```

```python
import functools
import math

import numpy as np
import jax
import jax.numpy as jnp
from jax import lax
from jax.experimental import pallas as pl
from jax.experimental.pallas import tpu as pltpu

F32 = jnp.float32
BF16 = jnp.bfloat16
I32 = jnp.int32

D_MODEL = 2048
BATCH = 16
SEQ = 256
DEPTH = 2
DEC_BATCH = 2
DEC_SEQ = 1024
PAST_LEN = 256
GRID_W = 64
GRID_ROWS = DEC_SEQ // GRID_W
HEAD_DIM = 128
C_CONV = 512
CONV_K = 31
N_HEADS = 6
DQK = 64
NA_KR = 8
NA_KC = 16
ROPE_BASE = 10000.0
N_GROUPS = 4
E_PER_GROUP = 4
N_EXPERTS = 16
D_FF = 512
N_ADA = 6 * D_MODEL
W_IN_COLS = 2 * C_CONV + 6 * N_HEADS * HEAD_DIM
T_CTX = BATCH * SEQ
T_LAT = DEC_BATCH * DEC_SEQ
T_ALL = T_CTX + T_LAT

LANES = 128
MOD_ROWS = 8
PRE_TM = 512
HEADS_TM = 1024
MIX_TM = 512
MOE_TM = 256
MOE_ITEMS = 2 * T_ALL // MOE_TM + N_EXPERTS
ROUTE_ROWS = 32
CUM_BLK = 512
VMEM_LIMIT = 56 * 1024 * 1024
NEG_BIG = -1e30


def _cparams(sem):
    return pltpu.CompilerParams(dimension_semantics=sem, vmem_limit_bytes=VMEM_LIMIT)


def _sigmoid(x):
    return 1.0 / (1.0 + jnp.exp(-x))


def _silu(x):
    return x * _sigmoid(x)


def _lambda_init(layer):
    return 0.8 - 0.6 * math.exp(-0.3 * layer)


def _ada_kernel(c_ref, w_ref, b_ref, o_ref):
    c = c_ref[...]
    s = _silu(c).astype(BF16)
    o_ref[...] = jnp.dot(s, w_ref[...].astype(BF16), preferred_element_type=F32) + b_ref[...]


def _ada_modulation(cvec, w_ada, b_ada):
    tn = 1024
    return pl.pallas_call(
        _ada_kernel,
        grid=(DEPTH, N_ADA // tn),
        in_specs=[
            pl.BlockSpec((MOD_ROWS, D_MODEL), lambda l, j: (0, 0)),
            pl.BlockSpec((None, D_MODEL, tn), lambda l, j: (l, 0, j)),
            pl.BlockSpec((None, 1, tn), lambda l, j: (l, 0, j)),
        ],
        out_specs=pl.BlockSpec((None, MOD_ROWS, tn), lambda l, j: (l, 0, j)),
        out_shape=jax.ShapeDtypeStruct((DEPTH, MOD_ROWS, N_ADA), F32),
        compiler_params=_cparams(("arbitrary", "arbitrary")),
        name="ada_modulation",
    )(cvec, w_ada, b_ada.reshape(DEPTH, 1, N_ADA))


def _mod_spec(layer, chunk, row_of_block):
    return pl.BlockSpec((None, 1, D_MODEL),
                        lambda i, *_: (layer * MOD_ROWS + row_of_block(i), 0, chunk))


def _modulate(x, g, shift, scale):
    ms = jnp.mean(x * x, axis=-1, keepdims=True)
    return x * lax.rsqrt(ms + 1e-6) * (g * (1.0 + scale)) + shift


def _pre_kernel(x_ref, sh_ref, sc_ref, g_ref, w_ref, h_ref, u_ref, wbf_ref):
    @pl.when(pl.program_id(0) == 0)
    def _():
        wbf_ref[...] = w_ref[...].astype(BF16)

    h = _modulate(x_ref[...], g_ref[...], sh_ref[...], sc_ref[...]).astype(BF16)
    h_ref[...] = h
    y = jnp.dot(h, wbf_ref[...], preferred_element_type=F32)
    u_ref[...] = y[:, :C_CONV] * _sigmoid(y[:, C_CONV:])


def _pre_project(x, mod8, norm_g, w_in, layer, row_of_block, name):
    t = x.shape[0]
    return pl.pallas_call(
        _pre_kernel,
        grid=(t // PRE_TM,),
        in_specs=[
            pl.BlockSpec((PRE_TM, D_MODEL), lambda i: (i, 0)),
            _mod_spec(layer, 0, row_of_block),
            _mod_spec(layer, 1, row_of_block),
            pl.BlockSpec((None, 1, D_MODEL), lambda i: (layer, 0, 0)),
            pl.BlockSpec((None, D_MODEL, 2 * C_CONV), lambda i: (layer, 0, 0)),
        ],
        out_specs=[pl.BlockSpec((PRE_TM, D_MODEL), lambda i: (i, 0)),
                   pl.BlockSpec((PRE_TM, C_CONV), lambda i: (i, 0))],
        out_shape=[jax.ShapeDtypeStruct((t, D_MODEL), BF16), jax.ShapeDtypeStruct((t, C_CONV), F32)],
        scratch_shapes=[pltpu.VMEM((D_MODEL, 2 * C_CONV), BF16)],
        compiler_params=_cparams(("arbitrary",)),
        name=name,
    )(x, mod8, mod8, norm_g, w_in)


SEG_DQ, SEG_DK, SEG_DV, SEG_NQ, SEG_NK, SEG_NV = range(6)
SEG_W = N_HEADS * HEAD_DIM
HEADS_WBLK = 256
HEADS_RC = 512


def _heads_kernel(h_ref, w0_ref, w1_ref, w2_ref, gain_ref, a64_ref, a128_ref, *rest, rope, per_batch):
    if rope:
        cos_ref, sa_ref, sb_ref, o_ref = rest
    else:
        (o_ref,) = rest
    j = pl.program_id(0)

    def finish(seg):
        n_g = HEADS_WBLK // LANES
        for c, w_ref in enumerate((w0_ref, w1_ref, w2_ref)):
            w = w_ref[...].astype(BF16)
            gain = gain_ref[:, c * HEADS_WBLK:(c + 1) * HEADS_WBLK]
            for r0 in range(0, HEADS_TM, HEADS_RC):
                y = jnp.dot(h_ref[r0:r0 + HEADS_RC, :], w, preferred_element_type=F32)
                for g in range(n_g):
                    hd = c * n_g + g
                    yg = y[:, g * LANES:(g + 1) * LANES]
                    if seg in (SEG_NQ, SEG_NK):
                        ms = jnp.mean(yg * yg, axis=-1, keepdims=True)
                        yg = yg * lax.rsqrt(ms + 1e-6) * gain[:, g * LANES:(g + 1) * LANES]
                    elif seg in (SEG_DQ, SEG_DK):
                        sq = yg * yg
                        low = lax.broadcasted_iota(I32, sq.shape, 1) < DQK
                        s_all = jnp.sum(sq, axis=-1, keepdims=True)
                        s_lo = jnp.sum(jnp.where(low, sq, 0.0), axis=-1, keepdims=True)
                        ms = jnp.where(low, s_lo, s_all - s_lo) * (1.0 / DQK)
                        yg = yg * lax.rsqrt(ms + 1e-6) * gain[:, g * LANES:(g + 1) * LANES]
                    if rope and seg in (SEG_DQ, SEG_DK):
                        rs = slice(r0, r0 + HEADS_RC)
                        yg = (yg * cos_ref[rs, :] + pltpu.roll(yg, LANES - 16, 1) * sa_ref[rs, :]
                              + pltpu.roll(yg, 16, 1) * sb_ref[rs, :])
                    yg = yg.astype(o_ref.dtype)
                    if per_batch == 1:
                        o_ref[hd, r0:r0 + HEADS_RC, :] = yg
                    else:
                        rows = HEADS_TM // per_batch
                        for q in range(HEADS_RC // rows):
                            o_ref[r0 // rows + q, hd] = yg[q * rows:(q + 1) * rows]

    for seg in range(6):
        pl.when(j == seg)(functools.partial(finish, seg))


def _head_project(h, w_in, gains, a64, a128, rope_tabs, layer, batch, seq, name):
    t = h.shape[0]
    rope = rope_tabs is not None
    per_batch = HEADS_TM // seq
    col0 = 2 * C_CONV // HEADS_WBLK
    wspec = lambda c: pl.BlockSpec((None, D_MODEL, HEADS_WBLK),
                                   lambda j, i: (layer, 0, col0 + (SEG_W // HEADS_WBLK) * j + c))
    in_specs = [
        pl.BlockSpec((HEADS_TM, D_MODEL), lambda j, i: (i, 0)),
        wspec(0), wspec(1), wspec(2),
        pl.BlockSpec((None, None, 1, SEG_W), lambda j, i: (layer, j, 0, 0)),
        pl.BlockSpec((HEADS_WBLK, HEADS_WBLK), lambda j, i: (0, 0)),
        pl.BlockSpec((HEADS_WBLK, HEADS_WBLK), lambda j, i: (0, 0)),
    ]
    args = [h, w_in, w_in, w_in, gains, a64, a128]
    if rope:
        in_specs += [pl.BlockSpec((DEC_SEQ, LANES), lambda j, i: (0, 0))] * 3
        args += list(rope_tabs)
    if per_batch == 1:
        out_spec = pl.BlockSpec((None, None, N_HEADS, seq, HEAD_DIM), lambda j, i: (j, i, 0, 0, 0))
    else:
        out_spec = pl.BlockSpec((None, per_batch, N_HEADS, seq, HEAD_DIM), lambda j, i: (j, i, 0, 0, 0))
    return pl.pallas_call(
        functools.partial(_heads_kernel, rope=rope, per_batch=per_batch),
        grid=(6, t // HEADS_TM),
        in_specs=in_specs,
        out_specs=out_spec,
        out_shape=jax.ShapeDtypeStruct((6, batch, N_HEADS, seq, HEAD_DIM), BF16 if rope else F32),
        compiler_params=_cparams(("arbitrary", "arbitrary")),
        name=name,
    )(*args)


CONV_CH = 128
CONV_HALO = 16


def _conv_window(win_ref, y_ref, w_ref, b_ref, lg_ref, lb_ref):
    off = CONV_HALO - CONV_K // 2
    sub = 8
    win_rows = CONV_CH + 2 * CONV_HALO
    for g in range(C_CONV // LANES):
        sl = slice(g * LANES, (g + 1) * LANES)
        window = win_ref[:, sl]
        acc = jnp.zeros((CONV_CH, LANES), F32) + b_ref[:, sl]
        for phase in range(sub):
            taps = [k for k in range(CONV_K) if (off + k) % sub == phase]
            if not taps:
                continue
            shifted = window if phase == 0 else pltpu.roll(window, win_rows - phase, 0)
            for k in taps:
                a = (off + k) // sub * sub
                acc = acc + shifted[a:a + CONV_CH, :] * w_ref[k:k + 1, sl]
        y_ref[:, sl] = acc

    y = y_ref[...]
    mu = jnp.mean(y, axis=-1, keepdims=True)
    yc = y - mu
    var = jnp.mean(yc * yc, axis=-1, keepdims=True)
    z = yc * lax.rsqrt(var + 1e-5) * lg_ref[...] + lb_ref[...]
    return _silu(z)


def _conv_block(i, u_ref, ulo_ref, uhi_ref, conv_refs, win_ref, y_ref, a_ref, seq):
    rows = u_ref.shape[0]
    for c in range(rows // CONV_CH):
        r0 = c * CONV_CH
        first = i * rows + r0
        lo_keep = ((first % seq) != 0).astype(F32)
        hi_keep = (((first + CONV_CH) % seq) != 0).astype(F32)
        lo = u_ref[r0 - CONV_HALO:r0, :] if c > 0 else ulo_ref[...]
        hi = u_ref[r0 + CONV_CH:r0 + CONV_CH + CONV_HALO, :] if r0 + CONV_CH < rows else uhi_ref[...]
        win_ref[0:CONV_HALO, :] = lo * lo_keep
        win_ref[CONV_HALO:CONV_HALO + CONV_CH, :] = u_ref[r0:r0 + CONV_CH, :]
        win_ref[CONV_HALO + CONV_CH:, :] = hi * hi_keep
        a_ref[r0:r0 + CONV_CH, :] = _conv_window(win_ref, y_ref, *conv_refs).astype(a_ref.dtype)


_NT = (((1,), (1,)), ((), ()))


LOG2E = 1.4426950408889634


def _softmax_rows(s):
    m = jnp.max(s, axis=-1, keepdims=True)
    e = jnp.exp(s - m)
    return e / jnp.sum(e, axis=-1, keepdims=True)


def _exp2_rows(s2):
    e = jnp.exp2(s2 - jnp.max(s2, axis=-1, keepdims=True))
    return e, 1.0 / jnp.sum(e, axis=-1, keepdims=True)


def _diff_lambda(lam_ref, lam_init):
    lv = lam_ref[...]
    return (jnp.exp(jnp.sum(lv[0:1] * lv[1:2], axis=-1, keepdims=True))
            - jnp.exp(jnp.sum(lv[2:3] * lv[3:4], axis=-1, keepdims=True)) + lam_init)


def _diff_head(q, k, v, lam, g, lam_init, long_keys):
    lane = lax.broadcasted_iota(I32, q.shape, 1)
    if long_keys:
        qs = q * (DQK ** -0.5 * LOG2E)
        q0 = jnp.where(lane < DQK, qs, 0.0).astype(BF16)
        q1 = jnp.where(lane >= DQK, qs, 0.0).astype(BF16)
        e0, r0 = _exp2_rows(lax.dot_general(q0, k, _NT, preferred_element_type=F32))
        e1, r1 = _exp2_rows(lax.dot_general(q1, k, _NT, preferred_element_type=F32))
        o = (jnp.dot(e0.astype(BF16), v, preferred_element_type=F32) * r0
             - jnp.dot(e1.astype(BF16), v, preferred_element_type=F32) * (lam * r1))
    else:
        q0 = jnp.where(lane < DQK, q, 0.0).astype(BF16)
        q1 = jnp.where(lane >= DQK, q, 0.0).astype(BF16)
        scale = DQK ** -0.5
        p0 = _softmax_rows(lax.dot_general(q0, k, _NT, preferred_element_type=F32) * scale)
        p1 = _softmax_rows(lax.dot_general(q1, k, _NT, preferred_element_type=F32) * scale)
        o = jnp.dot((p0 - lam * p1).astype(BF16), v, preferred_element_type=F32)
    ms = jnp.mean(o * o, axis=-1, keepdims=True)
    return o * lax.rsqrt(ms + 1e-5) * g * (1.0 - lam_init)


def _seg_spec(seg, rows, index):
    def imap(*ids):
        b, h, r = index(*ids)
        return (seg, b, h, r, 0)
    return pl.BlockSpec((None, None, None, rows, HEAD_DIM), imap)


def _diff_attn_lat_kernel(lam_ref, g_ref, q_ref, k_ref, v_ref, ck_ref, cv_ref, o_ref, *, lam_init):
    k = jnp.concatenate([k_ref[...].astype(BF16), ck_ref[...].astype(BF16)], axis=0)
    v = jnp.concatenate([v_ref[...].astype(BF16), cv_ref[...].astype(BF16)], axis=0)
    lam = _diff_lambda(lam_ref, lam_init)
    o_ref[...] = _diff_head(q_ref[...].astype(F32), k, v, lam, g_ref[...], lam_init,
                            True).astype(o_ref.dtype)


def _diff_attention_lat(segs, lam_params, subln_g, cache_k, cache_v, layer, tq):
    nq = DEC_SEQ // tq
    cspec = pl.BlockSpec((None, None, None, PAST_LEN, HEAD_DIM), lambda b, h, qi: (b, layer, h, 0, 0))
    return pl.pallas_call(
        functools.partial(_diff_attn_lat_kernel, lam_init=_lambda_init(layer)),
        grid=(DEC_BATCH, N_HEADS, nq),
        in_specs=[
            pl.BlockSpec((None, 4, DQK), lambda b, h, qi: (layer, 0, 0)),
            pl.BlockSpec((None, 1, HEAD_DIM), lambda b, h, qi: (layer, 0, 0)),
            _seg_spec(SEG_DQ, tq, lambda b, h, qi: (b, h, qi)),
            _seg_spec(SEG_DK, DEC_SEQ, lambda b, h, qi: (b, h, 0)),
            _seg_spec(SEG_DV, DEC_SEQ, lambda b, h, qi: (b, h, 0)),
            cspec, cspec,
        ],
        out_specs=pl.BlockSpec((tq, HEAD_DIM), lambda b, h, qi: (b * nq + qi, h)),
        out_shape=jax.ShapeDtypeStruct((T_LAT, SEG_W), BF16),
        compiler_params=_cparams(("arbitrary", "arbitrary", "arbitrary")),
        name="diff_attn_lat",
    )(lam_params, subln_g.reshape(DEPTH, 1, HEAD_DIM), segs, segs, segs, cache_k, cache_v)


CACHE_SEGS = (SEG_DK, SEG_DV, SEG_NK, SEG_NV)


def _ctx_attn_kernel(lam_ref, g_ref, *refs, lam_init, n_prev):
    seg_refs, refs = refs[:6], refs[6:]
    dq_ref, dk_ref, dv_ref, nq_ref, nk_ref, nv_ref = seg_refs
    prev_refs, refs = refs[:n_prev * len(CACHE_SEGS)], refs[n_prev * len(CACHE_SEGS):]
    d_ref, n_ref = refs[:2]
    cache_refs = refs[2:]
    lam = _diff_lambda(lam_ref, lam_init)
    for h in range(N_HEADS):
        sl = slice(h * HEAD_DIM, (h + 1) * HEAD_DIM)
        d_ref[:, sl] = _diff_head(dq_ref[h], dk_ref[h].astype(BF16), dv_ref[h].astype(BF16), lam,
                                  g_ref[...], lam_init, False).astype(d_ref.dtype)
        s = lax.dot_general(nq_ref[h].astype(BF16), nk_ref[h].astype(BF16), _NT,
                            preferred_element_type=F32) * (HEAD_DIM ** -0.5)
        n_ref[:, sl] = jnp.dot(_softmax_rows(s).astype(BF16), nv_ref[h].astype(BF16),
                               preferred_element_type=F32).astype(n_ref.dtype)
    for n, (seg, c_ref) in enumerate(zip(CACHE_SEGS, cache_refs)):
        for layer in range(n_prev):
            c_ref[layer] = prev_refs[layer * len(CACHE_SEGS) + n][...]
        c_ref[n_prev] = seg_refs[seg][...]


def _ctx_attention(segs, prev_segs, lam_params, subln_g, layer):
    seg = lambda s: pl.BlockSpec((None, None, N_HEADS, SEQ, HEAD_DIM), lambda b: (s, b, 0, 0, 0))
    out = pl.BlockSpec((SEQ, SEG_W), lambda b: (b, 0))
    in_specs = [pl.BlockSpec((None, 4, DQK), lambda b: (layer, 0, 0)),
                pl.BlockSpec((None, 1, HEAD_DIM), lambda b: (layer, 0, 0))] + [seg(s) for s in range(6)]
    args = [lam_params, subln_g.reshape(DEPTH, 1, HEAD_DIM)] + [segs] * 6
    out_specs = [out, out]
    out_shape = [jax.ShapeDtypeStruct((T_CTX, SEG_W), BF16)] * 2
    for prev in prev_segs:
        in_specs += [seg(s) for s in CACHE_SEGS]
        args += [prev] * len(CACHE_SEGS)
    if prev_segs:
        assert len(prev_segs) == DEPTH - 1
        cache = pl.BlockSpec((None, DEPTH, N_HEADS, SEQ, HEAD_DIM), lambda b: (b, 0, 0, 0, 0))
        out_specs += [cache] * len(CACHE_SEGS)
        out_shape += [jax.ShapeDtypeStruct((BATCH, DEPTH, N_HEADS, SEQ, HEAD_DIM), F32)] * len(CACHE_SEGS)
    return pl.pallas_call(
        functools.partial(_ctx_attn_kernel, lam_init=_lambda_init(layer), n_prev=len(prev_segs)),
        grid=(BATCH,),
        in_specs=in_specs,
        out_specs=out_specs,
        out_shape=out_shape,
        compiler_params=_cparams(("arbitrary",)),
        name="ctx_attn_caches" if prev_segs else "ctx_attn",
    )(*args)


NA_PAIRS = 2 * NA_KR - 2
NA_QROWS = 4
NA_KROWS = 12


def _na_key_block(chunk):
    first = min(max(chunk * NA_QROWS - NA_KR // 2, 0), GRID_ROWS - NA_KR)
    return min(first - first % 2, GRID_ROWS - NA_KROWS)


def _na_attn_kernel(src_ref, mask_ref, q_ref, k_ref, v_ref, ck_ref, cv_ref, o_ref, bias_ref):
    for d in range(NA_PAIRS):
        row = jnp.broadcast_to(src_ref[d:d + 1, :] * LOG2E, (GRID_W, LANES))
        tile = pltpu.roll(row, LANES - (NA_KC - 1), 1, stride=1, stride_axis=0)
        for v in range(3):
            bias_ref[v, d] = tile + mask_ref[v]
    dead = jnp.full((GRID_W, LANES), NEG_BIG, F32)
    ck = ck_ref[...].astype(BF16)
    cv = cv_ref[...].astype(BF16)
    scale = HEAD_DIM ** -0.5 * LOG2E
    for chunk in range(GRID_ROWS // NA_QROWS):
        kb = _na_key_block(chunk)
        rows = []
        for qr in range(chunk * NA_QROWS, (chunk + 1) * NA_QROWS):
            start = min(max(qr - NA_KR // 2, 0), GRID_ROWS - NA_KR)
            assert kb <= start and start + NA_KR <= kb + NA_KROWS
            tiles = []
            for m in range(NA_KROWS // 2):
                r0 = kb + 2 * m
                live0 = start <= r0 < start + NA_KR
                live1 = start <= r0 + 1 < start + NA_KR
                d = r0 - qr + NA_KR - 1
                assert not (live0 or live1) or 0 <= d < NA_PAIRS
                if live0 and live1:
                    tiles.append(bias_ref[0, d])
                elif live0:
                    tiles.append(bias_ref[1, d])
                elif live1:
                    tiles.append(bias_ref[2, d])
                else:
                    tiles.append(dead)
            rows.append(jnp.concatenate(tiles, axis=1))
        bias = jnp.concatenate(rows, axis=0)
        qs = slice(chunk * NA_QROWS * GRID_W, (chunk + 1) * NA_QROWS * GRID_W)
        ks = slice(kb * GRID_W, (kb + NA_KROWS) * GRID_W)
        q = (q_ref[qs, :].astype(F32) * scale).astype(BF16)
        kl = k_ref[ks, :].astype(BF16)
        vl = v_ref[ks, :].astype(BF16)
        s_loc = lax.dot_general(q, kl, _NT, preferred_element_type=F32) + bias
        s_ctx = lax.dot_general(q, ck, _NT, preferred_element_type=F32)
        m = jnp.maximum(jnp.max(s_loc, axis=-1, keepdims=True), jnp.max(s_ctx, axis=-1, keepdims=True))
        e_loc = jnp.exp2(s_loc - m)
        e_ctx = jnp.exp2(s_ctx - m)
        r = 1.0 / (jnp.sum(e_loc, axis=-1, keepdims=True) + jnp.sum(e_ctx, axis=-1, keepdims=True))
        o_ref[qs, :] = ((jnp.dot(e_loc.astype(BF16), vl, preferred_element_type=F32)
                         + jnp.dot(e_ctx.astype(BF16), cv, preferred_element_type=F32)) * r
                        ).astype(o_ref.dtype)


def _na_attention(segs, rpb_src, na_mask, cache_k, cache_v, layer):
    blk = lambda s: _seg_spec(s, DEC_SEQ, lambda b, h: (b, h, 0))
    cspec = pl.BlockSpec((None, None, None, PAST_LEN, HEAD_DIM), lambda b, h: (b, layer, h, 0, 0))
    return pl.pallas_call(
        _na_attn_kernel,
        grid=(DEC_BATCH, N_HEADS),
        in_specs=[
            pl.BlockSpec((None, None, 16, LANES), lambda b, h: (layer, h, 0, 0)),
            pl.BlockSpec((3, GRID_W, LANES), lambda b, h: (0, 0, 0)),
            blk(SEG_NQ), blk(SEG_NK), blk(SEG_NV), cspec, cspec,
        ],
        out_specs=pl.BlockSpec((DEC_SEQ, HEAD_DIM), lambda b, h: (b, h)),
        out_shape=jax.ShapeDtypeStruct((T_LAT, SEG_W), BF16),
        scratch_shapes=[pltpu.VMEM((3, NA_PAIRS, GRID_W, LANES), F32)],
        compiler_params=_cparams(("arbitrary", "arbitrary")),
        name="na_attn_lat",
    )(rpb_src, na_mask, segs, segs, segs, cache_k, cache_v)


def _split_f32(x, n):
    terms = []
    for _ in range(n):
        t = x.astype(BF16).astype(F32)
        terms.append(t)
        x = x - t
    return terms


def _mix_kernel(u_ref, ulo_ref, uhi_ref, cw_ref, cb_ref, clg_ref, clb_ref, d_ref, n_ref, w_ref, x_ref,
                g1_ref, sh_ref, sc_ref, ng_ref, wr_ref, br_ref, x1_ref, lt_ref, win_ref, y_ref, a_ref, *, seq):
    k0, k1 = C_CONV, C_CONV + N_HEADS * HEAD_DIM
    _conv_block(pl.program_id(0), u_ref, ulo_ref, uhi_ref, (cw_ref, cb_ref, clg_ref, clb_ref),
                win_ref, y_ref, a_ref, seq)
    mixed = (jnp.dot(d_ref[...].astype(BF16), w_ref[k0:k1, :], preferred_element_type=F32)
             + jnp.dot(n_ref[...].astype(BF16), w_ref[k1:, :], preferred_element_type=F32)
             + jnp.dot(a_ref[...], w_ref[0:k0, :], preferred_element_type=F32))
    x1 = x_ref[...] + g1_ref[...] * mixed
    x1_ref[...] = x1
    h2 = _modulate(x1, ng_ref[...], sh_ref[...], sc_ref[...])
    wh, wm, wl = _split_f32(wr_ref[...], 3)
    w_cat = (wh + pltpu.roll(wm, ROUTE_ROWS, 1) + pltpu.roll(wl, 2 * ROUTE_ROWS, 1)).astype(BF16)
    hh, hm = _split_f32(h2, 2)
    r = jnp.dot(jnp.concatenate([hh, hm], axis=0).astype(BF16), w_cat, preferred_element_type=F32)
    rh, rm = r[:MIX_TM], r[MIX_TM:]
    back = lambda v, groups: pltpu.roll(v, LANES - groups * ROUTE_ROWS, 1)
    small = (back(rm, 1) + back(rh, 2)) + (back(rh, 1) + rm)
    logits = (small + rh).T[:ROUTE_ROWS, :]
    lt_ref[...] = logits + br_ref[...]


def _mix(u, conv_params, d_out, n_out, w_out_bf, x, mod8, norm_g, wr_t, br, layer, row_of_block, seq):
    t = x.shape[0]
    rows = lambda width: pl.BlockSpec((MIX_TM, width), lambda i: (i, 0))
    halo_blocks = MIX_TM // CONV_HALO
    cvec = lambda: pl.BlockSpec((None, 1, C_CONV), lambda i: (layer, 0, 0))
    conv_w, conv_b, ln_g, ln_b = conv_params
    return pl.pallas_call(
        functools.partial(_mix_kernel, seq=seq),
        grid=(t // MIX_TM,),
        in_specs=[
            rows(C_CONV),
            pl.BlockSpec((CONV_HALO, C_CONV), lambda i: (jnp.maximum(i * halo_blocks - 1, 0), 0)),
            pl.BlockSpec((CONV_HALO, C_CONV),
                         lambda i: (jnp.minimum((i + 1) * halo_blocks, t // CONV_HALO - 1), 0)),
            pl.BlockSpec((None, CONV_K, C_CONV), lambda i: (layer, 0, 0)),
            cvec(), cvec(), cvec(),
            rows(N_HEADS * HEAD_DIM), rows(N_HEADS * HEAD_DIM),
            pl.BlockSpec((None, D_MODEL, D_MODEL), lambda i: (layer, 0, 0)),
            rows(D_MODEL),
            _mod_spec(layer, 2, row_of_block),
            _mod_spec(layer, 3, row_of_block),
            _mod_spec(layer, 4, row_of_block),
            pl.BlockSpec((None, 1, D_MODEL), lambda i: (layer, 0, 0)),
            pl.BlockSpec((None, D_MODEL, LANES), lambda i: (layer, 0, 0)),
            pl.BlockSpec((None, ROUTE_ROWS, 1), lambda i: (layer, 0, 0)),
        ],
        out_specs=[rows(D_MODEL), pl.BlockSpec((ROUTE_ROWS, MIX_TM), lambda i: (0, i))],
        out_shape=[jax.ShapeDtypeStruct((t, D_MODEL), F32),
                   jax.ShapeDtypeStruct((ROUTE_ROWS, t), F32)],
        scratch_shapes=[pltpu.VMEM((CONV_CH + 2 * CONV_HALO, C_CONV), F32),
                        pltpu.VMEM((CONV_CH, C_CONV), F32),
                        pltpu.VMEM((MIX_TM, C_CONV), BF16)],
        compiler_params=_cparams(("arbitrary",)),
        name="mix_%d" % t,
    )(u, u, u, conv_w, conv_b.reshape(DEPTH, 1, C_CONV), ln_g.reshape(DEPTH, 1, C_CONV),
      ln_b.reshape(DEPTH, 1, C_CONV), d_out, n_out, w_out_bf, x, mod8, mod8, mod8, norm_g, wr_t, br)


def _cast_kernel(x_ref, o_ref):
    o_ref[...] = x_ref[...].astype(BF16)


def _cast_bf16(w):
    tm = 512
    spec = pl.BlockSpec((None, tm, D_MODEL), lambda l, i: (l, i, 0))
    return pl.pallas_call(
        _cast_kernel, grid=(DEPTH, D_MODEL // tm), in_specs=[spec], out_specs=spec,
        out_shape=jax.ShapeDtypeStruct(w.shape, BF16),
        compiler_params=_cparams(("arbitrary", "arbitrary")), name="cast_w_out",
    )(w)


def _route_kernel(lt_ref, tri_ref, pos_ref, wt_ref, item_ref):
    t = T_ALL
    lg = lt_ref[0:N_GROUPS, :]
    eg = jnp.exp(lg - jnp.max(lg, axis=0, keepdims=True))
    pg = eg / jnp.sum(eg, axis=0, keepdims=True)
    pg_top = jnp.max(pg, axis=0, keepdims=True)
    gi = lax.broadcasted_iota(I32, pg.shape, 0).astype(F32)
    g_idx = jnp.min(jnp.where(pg == pg_top, gi, float(N_GROUPS)), axis=0, keepdims=True)

    le = jnp.zeros((E_PER_GROUP, t), F32)
    for g in range(N_GROUPS):
        rows = lt_ref[N_GROUPS + g * E_PER_GROUP:N_GROUPS + (g + 1) * E_PER_GROUP, :]
        le = jnp.where(g_idx == float(g), rows, le)
    ee = jnp.exp(le - jnp.max(le, axis=0, keepdims=True))
    pe = ee / jnp.sum(ee, axis=0, keepdims=True)
    ei = lax.broadcasted_iota(I32, pe.shape, 0).astype(F32)
    p1 = jnp.max(pe, axis=0, keepdims=True)
    i1 = jnp.min(jnp.where(pe == p1, ei, float(E_PER_GROUP)), axis=0, keepdims=True)
    pe_rest = jnp.where(ei == i1, -1.0, pe)
    p2 = jnp.max(pe_rest, axis=0, keepdims=True)
    i2 = jnp.min(jnp.where(pe_rest == p2, ei, float(E_PER_GROUP)), axis=0, keepdims=True)
    den = p1 + p2
    wt_ref[0:1, :] = pg_top * (p1 / den)
    wt_ref[1:2, :] = pg_top * (p2 / den)
    e1 = g_idx * E_PER_GROUP + i1
    e2 = g_idx * E_PER_GROUP + i2

    erow = lax.broadcasted_iota(I32, (N_EXPERTS, t), 0).astype(F32)
    oh1 = (erow == e1).astype(F32)
    oh2 = (erow == e2).astype(F32)
    cnt = oh1 + oh2
    carry = jnp.zeros((N_EXPERTS, 1), F32)
    ranks = []
    for b in range(t // CUM_BLK):
        blk = cnt[:, b * CUM_BLK:(b + 1) * CUM_BLK]
        ranks.append(jnp.dot(blk.astype(BF16), tri_ref[...], preferred_element_type=F32) + carry)
        carry = carry + jnp.sum(blk, axis=1, keepdims=True)
    rank = jnp.concatenate(ranks, axis=1)

    erow_l = lax.broadcasted_iota(I32, (N_EXPERTS, LANES), 0)

    def excl_scan(v):
        inc = v
        for s in (1, 2, 4, 8):
            inc = inc + jnp.where(erow_l >= s, pltpu.roll(inc, s, 0), 0.0)
        return inc - v

    total = jnp.broadcast_to(carry, (N_EXPERTS, LANES))
    offs = excl_scan(total)
    pos_ref[0:1, :] = jnp.sum(oh1 * (rank + offs[:, 0:1]), axis=0, keepdims=True).astype(I32)
    pos_ref[1:2, :] = jnp.sum(oh2 * (rank + offs[:, 0:1]), axis=0, keepdims=True).astype(I32)

    shift = int(math.log2(MOE_TM))
    offs_i = offs.astype(I32)
    total_i = total.astype(I32)
    first_tile = lax.shift_right_logical(offs_i, shift)
    last_tile = lax.shift_right_logical(offs_i + total_i - 1, shift)
    n_item = jnp.where(total_i > 0, last_tile - first_tile + 1, 0).astype(F32)
    item_start = excl_scan(n_item)
    item_end = item_start + n_item
    kk = lax.broadcasted_iota(I32, (N_EXPERTS, LANES), 1).astype(F32)
    item_e = jnp.minimum(jnp.sum((item_end <= kk).astype(F32), axis=0, keepdims=True),
                         float(N_EXPERTS - 1))
    sel = (erow_l.astype(F32) == item_e).astype(F32)
    pick = lambda v: jnp.sum(sel * v, axis=0, keepdims=True)
    item_ref[0:1, :] = (kk[0:1] + pick(first_tile.astype(F32) - item_start)).astype(I32)
    item_ref[1:2, :] = item_e.astype(I32)
    item_ref[2:3, :] = pick(offs).astype(I32)
    item_ref[3:4, :] = pick(offs + total).astype(I32)
    item_ref[4:5, :] = item_end[N_EXPERTS - 1:N_EXPERTS, :].astype(I32)
    item_ref[5:8, :] = jnp.zeros((3, LANES), I32)


def _route(lt_all, tri):
    full = lambda shape: pl.BlockSpec(shape, lambda i: (0,) * len(shape))
    return pl.pallas_call(
        _route_kernel,
        grid=(1,),
        in_specs=[full((ROUTE_ROWS, T_ALL)), full((CUM_BLK, CUM_BLK))],
        out_specs=[full((2, T_ALL)), full((2, T_ALL)), full((8, LANES))],
        out_shape=[jax.ShapeDtypeStruct((2, T_ALL), I32), jax.ShapeDtypeStruct((2, T_ALL), F32),
                   jax.ShapeDtypeStruct((8, LANES), I32)],
        compiler_params=_cparams(("arbitrary",)),
        name="moe_route",
    )(lt_all, tri)


N_TOK_TILES = T_ALL // MOE_TM
CTX_TILES = T_CTX // MOE_TM


def _tok_row(i):
    return jnp.where(i < CTX_TILES, 0, 1 + (i - CTX_TILES) // (DEC_SEQ // MOE_TM))


N_CHUNK = D_MODEL // LANES
MOE_PITCH = N_CHUNK + 1
SLAB_ROWS = MOE_TM * MOE_PITCH
ISSUE_UNROLL = 4


def _slab_copy(src, src_tok, dst, dst_tok, sem, pitch):
    return pltpu.make_async_copy(src.at[pl.ds(src_tok * pitch, pitch)],
                                 dst.at[pl.ds(dst_tok * pitch, pitch)], sem)


def _to_slabs(ref, value):
    for c in range(N_CHUNK):
        ref[pl.ds(c, MOE_TM, stride=MOE_PITCH), :] = value[:, c * LANES:(c + 1) * LANES]
    ref[pl.ds(N_CHUNK, MOE_TM, stride=MOE_PITCH), :] = jnp.zeros((MOE_TM, LANES), F32)


def _slab_chunk(ref, c):
    return ref[pl.ds(c, MOE_TM, stride=MOE_PITCH), :]


def _dispatch_kernel(pos_ref, xc_ref, xl_ref, sh_ref, sc_ref, ng_ref, xs_ref, h_ref, sem):
    i = pl.program_id(0)
    slot = i % 2
    buf = h_ref.at[slot]

    def drain(s):
        def body(r, carry):
            _slab_copy(h_ref.at[s], 0, xs_ref, 0, sem.at[s], MOE_PITCH).wait()
            return carry
        lax.fori_loop(0, 2 * MOE_TM, body, 0, unroll=ISSUE_UNROLL)

    @pl.when(i >= 2)
    def _():
        drain(slot)

    @pl.when(i < CTX_TILES)
    def _():
        _to_slabs(buf, _modulate(xc_ref[...], ng_ref[...], sh_ref[...], sc_ref[...]))

    @pl.when(i >= CTX_TILES)
    def _():
        _to_slabs(buf, _modulate(xl_ref[...], ng_ref[...], sh_ref[...], sc_ref[...]))

    def issue(r, carry):
        tok = i * MOE_TM + r
        _slab_copy(buf, r, xs_ref, pos_ref[tok], sem.at[slot], MOE_PITCH).start(priority=0)
        _slab_copy(buf, r, xs_ref, pos_ref[T_ALL + tok], sem.at[slot], MOE_PITCH).start(priority=1)
        return carry

    lax.fori_loop(0, MOE_TM, issue, 0, unroll=ISSUE_UNROLL)

    @pl.when(i == N_TOK_TILES - 1)
    def _():
        drain(1 - slot)
        drain(slot)


def _dispatch(pos_flat, x1c, x1l, mod8, norm_g, layer):
    grid_spec = pltpu.PrefetchScalarGridSpec(
        num_scalar_prefetch=1,
        grid=(N_TOK_TILES,),
        in_specs=[
            pl.BlockSpec((MOE_TM, D_MODEL), lambda i, *_: (jnp.minimum(i, CTX_TILES - 1), 0)),
            pl.BlockSpec((MOE_TM, D_MODEL), lambda i, *_: (jnp.maximum(i - CTX_TILES, 0), 0)),
            _mod_spec(layer, 3, _tok_row),
            _mod_spec(layer, 4, _tok_row),
            pl.BlockSpec((None, 1, D_MODEL), lambda i, *_: (layer, 0, 0)),
        ],
        out_specs=pl.BlockSpec(memory_space=pl.ANY),
        scratch_shapes=[pltpu.VMEM((2, SLAB_ROWS, LANES), F32), pltpu.SemaphoreType.DMA((2,))],
    )
    return pl.pallas_call(
        _dispatch_kernel,
        grid_spec=grid_spec,
        out_shape=jax.ShapeDtypeStruct((2 * T_ALL * MOE_PITCH, LANES), F32),
        compiler_params=_cparams(("arbitrary",)),
        name="moe_dispatch",
    )(pos_flat, x1c, x1l, mod8, mod8, norm_g)


N_ROW_TILES = 2 * T_ALL // MOE_TM
X_SLOTS = 3
Y_SLOTS = 2


def _expert_kernel(tile_ref, exp_ref, lo_ref, hi_ref, n_ref, xs_hbm, wg_hbm, wu_hbm, wd_hbm, ys_hbm,
                   x_buf, y_buf, wg_buf, wu_buf, wd_buf, wg_bf, wu_bf, wd_bf, slot_ref, sem, x_sem, y_sem,
                   *, layer):
    k = pl.program_id(0)
    n = n_ref[0]

    def weight_copies(e, s):
        return (pltpu.make_async_copy(wg_hbm.at[layer, e], wg_buf.at[s], sem.at[s, 0]),
                pltpu.make_async_copy(wu_hbm.at[layer, e], wu_buf.at[s], sem.at[s, 1]),
                pltpu.make_async_copy(wd_hbm.at[layer, e], wd_buf.at[s], sem.at[s, 2]))

    def x_copy(t):
        s = t % X_SLOTS
        return pltpu.make_async_copy(xs_hbm.at[pl.ds(t * SLAB_ROWS, SLAB_ROWS)], x_buf.at[s], x_sem.at[s])

    def y_copy(t):
        s = t % Y_SLOTS
        return pltpu.make_async_copy(y_buf.at[s], ys_hbm.at[pl.ds(t * SLAB_ROWS, SLAB_ROWS)], y_sem.at[s])

    @pl.when(k == 0)
    def _():
        slot_ref[0] = 0
        for cp in weight_copies(exp_ref[0], 0):
            cp.start()
        for t in range(X_SLOTS - 1):
            x_copy(t).start()

    @pl.when(k < n)
    def _():
        e = exp_ref[k]
        tile = tile_ref[k]
        prev_tile = tile_ref[jnp.maximum(k - 1, 0)]
        new_expert = jnp.logical_or(k == 0, exp_ref[jnp.maximum(k - 1, 0)] != e)

        @pl.when(jnp.logical_or(k == 0, tile != prev_tile))
        def _():
            x_copy(tile).wait()

            @pl.when(tile + (X_SLOTS - 1) < N_ROW_TILES)
            def _():
                x_copy(tile + (X_SLOTS - 1)).start()

            @pl.when(k > 0)
            def _():
                y_copy(prev_tile).start()

            @pl.when(tile >= Y_SLOTS)
            def _():
                y_copy(tile - Y_SLOTS).wait()

        xs_ref = x_buf.at[tile % X_SLOTS]
        ys_ref = y_buf.at[tile % Y_SLOTS]

        @pl.when(new_expert)
        def _():
            s = slot_ref[0]
            for cp in weight_copies(e, s):
                cp.wait()
            wg_bf[...] = wg_buf[s].astype(BF16)
            wu_bf[...] = wu_buf[s].astype(BF16)
            wd_bf[...] = wd_buf[s].astype(BF16)
            nxt = lax.while_loop(lambda j: jnp.logical_and(j < n, exp_ref[jnp.minimum(j, n - 1)] == e),
                                 lambda j: j + 1, k + 1)

            @pl.when(nxt < n)
            def _():
                for cp in weight_copies(exp_ref[jnp.minimum(nxt, n - 1)], 1 - s):
                    cp.start()
            slot_ref[0] = 1 - s

        x = jnp.concatenate([_slab_chunk(xs_ref, c) for c in range(N_CHUNK)], axis=1).astype(BF16)
        gate = jnp.dot(x, wg_bf[...], preferred_element_type=F32)
        up = jnp.dot(x, wu_bf[...], preferred_element_type=F32)
        hid = (_silu(gate) * up).astype(BF16)
        y = jnp.dot(hid, wd_bf[...], preferred_element_type=F32)
        row0 = tile * MOE_TM
        row = row0 + lax.broadcasted_iota(I32, (MOE_TM, 1), 0)
        mine = jnp.logical_and(row >= lo_ref[k], row < hi_ref[k])
        first = lo_ref[k] <= row0

        @pl.when(first)
        def _():
            _to_slabs(ys_ref, jnp.where(mine, y, 0.0))

        @pl.when(jnp.logical_not(first))
        def _():
            for c in range(N_CHUNK):
                old = _slab_chunk(ys_ref, c)
                ys_ref[pl.ds(c, MOE_TM, stride=MOE_PITCH), :] = jnp.where(
                    mine, y[:, c * LANES:(c + 1) * LANES], old)

        @pl.when(k == n - 1)
        def _():
            y_copy(tile).start()
            y_copy(tile).wait()

            @pl.when(tile >= 1)
            def _():
                y_copy(tile - 1).wait()


def _experts(items, xs, w_gate, w_up, w_down, layer):
    item_tile, item_exp, item_lo, item_hi, n_items = items
    hbm = pl.BlockSpec(memory_space=pl.ANY)
    grid_spec = pltpu.PrefetchScalarGridSpec(
        num_scalar_prefetch=5, grid=(MOE_ITEMS,),
        in_specs=[hbm, hbm, hbm, hbm], out_specs=hbm,
        scratch_shapes=[
            pltpu.VMEM((X_SLOTS, SLAB_ROWS, LANES), F32), pltpu.VMEM((Y_SLOTS, SLAB_ROWS, LANES), F32),
            pltpu.VMEM((2, D_MODEL, D_FF), F32), pltpu.VMEM((2, D_MODEL, D_FF), F32),
            pltpu.VMEM((2, D_FF, D_MODEL), F32),
            pltpu.VMEM((D_MODEL, D_FF), BF16), pltpu.VMEM((D_MODEL, D_FF), BF16),
            pltpu.VMEM((D_FF, D_MODEL), BF16),
            pltpu.SMEM((1,), I32), pltpu.SemaphoreType.DMA((2, 3)),
            pltpu.SemaphoreType.DMA((X_SLOTS,)), pltpu.SemaphoreType.DMA((Y_SLOTS,)),
        ])
    return pl.pallas_call(
        functools.partial(_expert_kernel, layer=layer),
        grid_spec=grid_spec,
        out_shape=jax.ShapeDtypeStruct((2 * T_ALL * MOE_PITCH, LANES), F32),
        compiler_params=_cparams(("arbitrary",)),
        name="moe_experts",
    )(item_tile, item_exp, item_lo, item_hi, n_items, xs, w_gate, w_up, w_down)


def _combine_kernel(pos_ref, xc_ref, xl_ref, wt_ref, g2_ref, ys_ref, oc_ref, ol_ref, y_ref, sem):
    i = pl.program_id(0)
    slot = i % 2

    def gather(tile, s):
        def body(r, carry):
            tok = tile * MOE_TM + r
            _slab_copy(ys_ref, pos_ref[tok], y_ref.at[s, 0], r, sem.at[s], MOE_PITCH).start(priority=0)
            _slab_copy(ys_ref, pos_ref[T_ALL + tok], y_ref.at[s, 1], r, sem.at[s], MOE_PITCH).start(priority=1)
            return carry
        lax.fori_loop(0, MOE_TM, body, 0, unroll=ISSUE_UNROLL)

    @pl.when(i == 0)
    def _():
        gather(0, 0)

    @pl.when(i + 1 < N_TOK_TILES)
    def _():
        gather(i + 1, 1 - slot)

    def drain(r, carry):
        _slab_copy(ys_ref, 0, y_ref.at[slot, 0], 0, sem.at[slot], MOE_PITCH).wait()
        return carry

    lax.fori_loop(0, 2 * MOE_TM, drain, 0, unroll=ISSUE_UNROLL)

    def write(x_ref, o_ref):
        w0 = wt_ref[:, 0:1]
        w1 = wt_ref[:, 1:2]
        for c in range(N_CHUNK):
            sl = slice(c * LANES, (c + 1) * LANES)
            moe = w0 * _slab_chunk(y_ref.at[slot, 0], c) + w1 * _slab_chunk(y_ref.at[slot, 1], c)
            o_ref[:, sl] = x_ref[:, sl] + g2_ref[:, sl] * moe

    pl.when(i < CTX_TILES)(lambda: write(xc_ref, oc_ref))
    pl.when(i >= CTX_TILES)(lambda: write(xl_ref, ol_ref))


def _combine(pos_flat, x1c, x1l, wts_t, mod8, ys, layer):
    cspec = pl.BlockSpec((MOE_TM, D_MODEL), lambda i, *_: (jnp.minimum(i, CTX_TILES - 1), 0))
    lspec = pl.BlockSpec((MOE_TM, D_MODEL), lambda i, *_: (jnp.maximum(i - CTX_TILES, 0), 0))
    grid_spec = pltpu.PrefetchScalarGridSpec(
        num_scalar_prefetch=1,
        grid=(N_TOK_TILES,),
        in_specs=[
            cspec, lspec,
            pl.BlockSpec((MOE_TM, 2), lambda i, *_: (i, 0)),
            _mod_spec(layer, 5, _tok_row),
            pl.BlockSpec(memory_space=pl.ANY),
        ],
        out_specs=[cspec, lspec],
        scratch_shapes=[pltpu.VMEM((2, 2, SLAB_ROWS, LANES), F32), pltpu.SemaphoreType.DMA((2,))],
    )
    return pl.pallas_call(
        _combine_kernel,
        grid_spec=grid_spec,
        out_shape=[jax.ShapeDtypeStruct((T_CTX, D_MODEL), F32),
                   jax.ShapeDtypeStruct((T_LAT, D_MODEL), F32)],
        compiler_params=_cparams(("arbitrary",)),
        name="moe_combine",
    )(pos_flat, x1c, x1l, wts_t, mod8, ys)


def _norm_tables():
    mean_blocks = lambda w: np.kron(np.eye(HEADS_WBLK // w, dtype=np.float32),
                                    np.full((w, w), 1.0 / w, np.float32))
    return jnp.asarray(mean_blocks(DQK), BF16), jnp.asarray(mean_blocks(HEAD_DIM), BF16)


def _rope_tables():
    half = DQK // 2
    inv = 1.0 / (ROPE_BASE ** (np.arange(0, half, 2, dtype=np.float32) / half))
    t = np.arange(DEC_SEQ)
    ang_r = (t // GRID_W).astype(np.float32)[:, None] * inv
    ang_c = (t % GRID_W).astype(np.float32)[:, None] * inv
    ang = np.concatenate([ang_r, ang_r, ang_c, ang_c], axis=-1).astype(np.float32)
    ang = np.concatenate([ang, ang], axis=-1)
    first = (np.arange(LANES) % 32) < 16
    cos, sin = np.cos(ang), np.sin(ang)
    return (jnp.asarray(cos, F32), jnp.asarray(np.where(first, -sin, 0.0), F32),
            jnp.asarray(np.where(first, 0.0, sin), F32))


def _na_mask():
    qc = np.arange(GRID_W)[:, None]
    kc = (np.arange(LANES) % GRID_W)[None, :]
    ws = np.clip(qc - NA_KC // 2, 0, GRID_W - NA_KC)
    ok = (kc >= ws) & (kc < ws + NA_KC)
    first = (np.arange(LANES) < GRID_W)[None, :]
    masks = [ok, ok & first, ok & ~first]
    return jnp.asarray(np.stack([np.where(m, 0.0, NEG_BIG) for m in masks]), F32)


def kernel(x_prompt, x_sample, cache_diff_k, cache_diff_v, cache_na_k, cache_na_v, c, c_ctx, norm_mix_g, norm_ffn_g, w_ada, b_ada, w_in, w_out, conv_w, conv_b, conv_ln_g, conv_ln_b, diff_qn_g, diff_kn_g, diff_lam_q1, diff_lam_k1, diff_lam_q2, diff_lam_k2, diff_subln_g, na_qn_g, na_kn_g, na_rpb, moe_wr_g, moe_br_g, moe_wr_e, moe_br_e, moe_w_gate, moe_w_up, moe_w_down):
    cvec = jnp.zeros((MOD_ROWS, D_MODEL), F32).at[0].set(c_ctx).at[1:1 + DEC_BATCH].set(c)
    ones = jnp.ones((DEPTH, SEG_W), F32)
    gains = jnp.stack(
        [jnp.tile(diff_qn_g, (1, 2 * N_HEADS)), jnp.tile(diff_kn_g, (1, 2 * N_HEADS)), ones,
         jnp.tile(na_qn_g, (1, N_HEADS)), jnp.tile(na_kn_g, (1, N_HEADS)), ones],
        axis=1).reshape(DEPTH, 6, 1, SEG_W)
    a64, a128 = _norm_tables()
    rope_tabs = _rope_tables()
    na_mask = _na_mask()
    lam_params = jnp.stack([diff_lam_q1, diff_lam_k1, diff_lam_q2, diff_lam_k2], axis=1)
    rpb_pad = jnp.pad(na_rpb, ((0, 0), (0, 0), (0, 0), (0, DQK - na_rpb.shape[-1])))
    rpb_src = jnp.concatenate([rpb_pad[:, :, :-1], rpb_pad[:, :, 1:]], axis=-1)
    rpb_src = jnp.pad(rpb_src, ((0, 0), (0, 0), (0, 16 - NA_PAIRS), (0, 0)))
    wr_t = jnp.concatenate([moe_wr_g, moe_wr_e.reshape(DEPTH, D_MODEL, N_EXPERTS)], axis=2)
    wr_t = jnp.pad(wr_t, ((0, 0), (0, 0), (0, LANES - N_GROUPS - N_EXPERTS)))
    br = jnp.concatenate([moe_br_g, moe_br_e.reshape(DEPTH, N_EXPERTS)], axis=1)
    br = jnp.pad(br, ((0, 0), (0, ROUTE_ROWS - N_GROUPS - N_EXPERTS))).reshape(DEPTH, ROUTE_ROWS, 1)
    tri = jnp.asarray(np.triu(np.ones((CUM_BLK, CUM_BLK), np.float32), 1), BF16)
    norm_mix = norm_mix_g.reshape(DEPTH, 1, D_MODEL)
    norm_ffn = norm_ffn_g.reshape(DEPTH, 1, D_MODEL)
    head_major = lambda cache: jnp.transpose(cache, (0, 1, 3, 2, 4))
    ck_diff, cv_diff = head_major(cache_diff_k), head_major(cache_diff_v)
    ck_na, cv_na = head_major(cache_na_k), head_major(cache_na_v)

    mod8 = _ada_modulation(cvec, w_ada, b_ada).reshape(DEPTH * MOD_ROWS, 1, N_ADA)
    w_out_bf = _cast_bf16(w_out)

    ctx_row = lambda i: 0
    lat_row_pre = lambda i: 1 + i * PRE_TM // DEC_SEQ
    lat_row_mix = lambda i: 1 + i * MIX_TM // DEC_SEQ

    xc = x_prompt.reshape(T_CTX, D_MODEL)
    xl = x_sample.reshape(T_LAT, D_MODEL)
    ctx_segs = []
    for layer in range(DEPTH):
        hc, uc = _pre_project(xc, mod8, norm_mix, w_in, layer, ctx_row, "pre_ctx")
        hl, ul = _pre_project(xl, mod8, norm_mix, w_in, layer, lat_row_pre, "pre_lat")
        sc = _head_project(hc, w_in, gains, a64, a128, None, layer, BATCH, SEQ, "heads_ctx")
        sl = _head_project(hl, w_in, gains, a64, a128, rope_tabs, layer, DEC_BATCH, DEC_SEQ, "heads_lat")
        ctx_segs.append(sc)

        last = layer == DEPTH - 1
        dc, nc, *caches = _ctx_attention(sc, ctx_segs[:-1] if last else [], lam_params, diff_subln_g, layer)
        dl = _diff_attention_lat(sl, lam_params, diff_subln_g, ck_diff, cv_diff, layer, 256)
        nl = _na_attention(sl, rpb_src, na_mask, ck_na, cv_na, layer)

        conv_params = (conv_w, conv_b, conv_ln_g, conv_ln_b)
        x1c, ltc = _mix(uc, conv_params, dc, nc, w_out_bf, xc, mod8, norm_ffn, wr_t, br, layer, ctx_row, SEQ)
        x1l, ltl = _mix(ul, conv_params, dl, nl, w_out_bf, xl, mod8, norm_ffn, wr_t, br, layer,
                        lat_row_mix, DEC_SEQ)

        pos, wts, items = _route(jnp.concatenate([ltc, ltl], axis=1), tri)
        pos_flat = pos.reshape(2 * T_ALL)
        items = [items[r, :MOE_ITEMS] for r in range(4)] + [items[4, :1]]
        xs = _dispatch(pos_flat, x1c, x1l, mod8, norm_ffn, layer)
        ys = _experts(items, xs, moe_w_gate, moe_w_up, moe_w_down, layer)
        xc, xl = _combine(pos_flat, x1c, x1l, wts.T, mod8, ys, layer)

    outs = [jnp.transpose(cache, (0, 1, 3, 2, 4)) for cache in caches]
    return (xc.reshape(BATCH, SEQ, D_MODEL), xl.reshape(DEC_BATCH, DEC_SEQ, D_MODEL), *outs)
```

```python
import functools
import math

import numpy as np
import jax
import jax.numpy as jnp
from jax import lax
from jax.experimental import pallas as pl
from jax.experimental.pallas import tpu as pltpu

F32 = jnp.float32
BF16 = jnp.bfloat16
I32 = jnp.int32

D_MODEL = 2048
BATCH = 16
SEQ = 256
DEPTH = 2
DEC_BATCH = 2
DEC_SEQ = 1024
PAST_LEN = 256
GRID_W = 64
GRID_ROWS = DEC_SEQ // GRID_W
HEAD_DIM = 128
C_CONV = 512
CONV_K = 31
N_HEADS = 6
DQK = 64
NA_KR = 8
NA_KC = 16
ROPE_BASE = 10000.0
N_GROUPS = 4
E_PER_GROUP = 4
N_EXPERTS = 16
D_FF = 512
N_ADA = 6 * D_MODEL
W_IN_COLS = 2 * C_CONV + 6 * N_HEADS * HEAD_DIM
T_CTX = BATCH * SEQ
T_LAT = DEC_BATCH * DEC_SEQ
T_ALL = T_CTX + T_LAT

LANES = 128
MOD_ROWS = 8
PRE_TM = 512
HEADS_TM = 1024
MIX_TM = 512
DIFF_TQ = 512
MOE_TM = 256
MOE_ITEMS = 2 * T_ALL // MOE_TM + N_EXPERTS
ROUTE_ROWS = 32
CUM_BLK = 512
VMEM_LIMIT = 56 * 1024 * 1024
NEG_BIG = -1e30


def _cparams(sem):
    return pltpu.CompilerParams(dimension_semantics=sem, vmem_limit_bytes=VMEM_LIMIT)


def _sigmoid(x):
    return 1.0 / (1.0 + jnp.exp(-x))


def _silu(x):
    return x * _sigmoid(x)


def _lambda_init(layer):
    return 0.8 - 0.6 * math.exp(-0.3 * layer)


def _ada_kernel(c_ref, w_ref, b_ref, o_ref):
    c = c_ref[...]
    s = _silu(c).astype(BF16)
    o_ref[...] = jnp.dot(s, w_ref[...].astype(BF16), preferred_element_type=F32) + b_ref[...]


def _ada_modulation(cvec, w_ada, b_ada):
    tn = 1024
    return pl.pallas_call(
        _ada_kernel,
        grid=(DEPTH, N_ADA // tn),
        in_specs=[
            pl.BlockSpec((MOD_ROWS, D_MODEL), lambda l, j: (0, 0)),
            pl.BlockSpec((None, D_MODEL, tn), lambda l, j: (l, 0, j)),
            pl.BlockSpec((None, 1, tn), lambda l, j: (l, 0, j)),
        ],
        out_specs=pl.BlockSpec((None, MOD_ROWS, tn), lambda l, j: (l, 0, j)),
        out_shape=jax.ShapeDtypeStruct((DEPTH, MOD_ROWS, N_ADA), F32),
        compiler_params=_cparams(("arbitrary", "arbitrary")),
        name="ada_modulation",
    )(cvec, w_ada, b_ada.reshape(DEPTH, 1, N_ADA))


def _mod_spec(layer, chunk, row_of_block):
    return pl.BlockSpec((None, 1, D_MODEL),
                        lambda i, *_: (layer * MOD_ROWS + row_of_block(i), 0, chunk))


def _modulate(x, g, shift, scale):
    ms = jnp.mean(x * x, axis=-1, keepdims=True)
    return x * lax.rsqrt(ms + 1e-6) * (g * (1.0 + scale)) + shift


def _pre_kernel(x_ref, sh_ref, sc_ref, g_ref, w_ref, h_ref, u_ref, wbf_ref):
    @pl.when(pl.program_id(0) == 0)
    def _():
        wbf_ref[...] = w_ref[...].astype(BF16)

    h = _modulate(x_ref[...], g_ref[...], sh_ref[...], sc_ref[...]).astype(BF16)
    h_ref[...] = h
    y = jnp.dot(h, wbf_ref[...], preferred_element_type=F32)
    u_ref[...] = y[:, :C_CONV] * _sigmoid(y[:, C_CONV:])


def _pre_project(x, mod8, norm_g, w_in, layer, row_of_block, name):
    t = x.shape[0]
    return pl.pallas_call(
        _pre_kernel,
        grid=(t // PRE_TM,),
        in_specs=[
            pl.BlockSpec((PRE_TM, D_MODEL), lambda i: (i, 0)),
            _mod_spec(layer, 0, row_of_block),
            _mod_spec(layer, 1, row_of_block),
            pl.BlockSpec((None, 1, D_MODEL), lambda i: (layer, 0, 0)),
            pl.BlockSpec((None, D_MODEL, 2 * C_CONV), lambda i: (layer, 0, 0)),
        ],
        out_specs=[pl.BlockSpec((PRE_TM, D_MODEL), lambda i: (i, 0)),
                   pl.BlockSpec((PRE_TM, C_CONV), lambda i: (i, 0))],
        out_shape=[jax.ShapeDtypeStruct((t, D_MODEL), BF16), jax.ShapeDtypeStruct((t, C_CONV), F32)],
        scratch_shapes=[pltpu.VMEM((D_MODEL, 2 * C_CONV), BF16)],
        compiler_params=_cparams(("arbitrary",)),
        name=name,
    )(x, mod8, mod8, norm_g, w_in)


SEG_DQ, SEG_DK, SEG_DV, SEG_NQ, SEG_NK, SEG_NV = range(6)
SEG_W = N_HEADS * HEAD_DIM
HEADS_WBLK = 256
HEADS_RC = 256


CONV_SPLIT = ((0, 1), (2, 3), (4,), (5,), (6,), (7,))


def _heads_kernel(h_ref, w0_ref, w1_ref, w2_ref, gain_ref, u_ref, cw_ref, cb_ref, clg_ref, clb_ref, *rest,
                  rope, per_batch, seq):
    if rope:
        cos_ref, sa_ref, sb_ref, o_ref, a_ref, win_ref, y_ref = rest
    else:
        o_ref, a_ref, win_ref, y_ref = rest
    j = pl.program_id(1)

    def finish(seg):
        for chunk in CONV_SPLIT[seg]:
            _conv_chunk(chunk, u_ref, (cw_ref, cb_ref, clg_ref, clb_ref), win_ref, y_ref, a_ref, seq)
        n_g = HEADS_WBLK // LANES
        for c, w_ref in enumerate((w0_ref, w1_ref, w2_ref)):
            w = w_ref[...].astype(BF16)
            gain = gain_ref[:, c * HEADS_WBLK:(c + 1) * HEADS_WBLK]
            for r0 in range(0, HEADS_TM, HEADS_RC):
                y = jnp.dot(h_ref[r0:r0 + HEADS_RC, :], w, preferred_element_type=F32)
                for g in range(n_g):
                    hd = c * n_g + g
                    yg = y[:, g * LANES:(g + 1) * LANES]
                    if seg in (SEG_NQ, SEG_NK):
                        ms = jnp.mean(yg * yg, axis=-1, keepdims=True)
                        yg = yg * lax.rsqrt(ms + 1e-6) * gain[:, g * LANES:(g + 1) * LANES]
                    elif seg in (SEG_DQ, SEG_DK):
                        sq = yg * yg
                        low = lax.broadcasted_iota(I32, sq.shape, 1) < DQK
                        s_all = jnp.sum(sq, axis=-1, keepdims=True)
                        s_lo = jnp.sum(jnp.where(low, sq, 0.0), axis=-1, keepdims=True)
                        ms = jnp.where(low, s_lo, s_all - s_lo) * (1.0 / DQK)
                        yg = yg * lax.rsqrt(ms + 1e-6) * gain[:, g * LANES:(g + 1) * LANES]
                    if rope and seg in (SEG_DQ, SEG_DK):
                        rs = slice(r0, r0 + HEADS_RC)
                        yg = (yg * cos_ref[rs, :] + pltpu.roll(yg, LANES - 16, 1) * sa_ref[rs, :]
                              + pltpu.roll(yg, 16, 1) * sb_ref[rs, :])
                    yg = yg.astype(o_ref.dtype)
                    if per_batch == 1:
                        o_ref[hd, r0:r0 + HEADS_RC, :] = yg
                    else:
                        rows = HEADS_TM // per_batch
                        for q in range(HEADS_RC // rows):
                            o_ref[r0 // rows + q, hd] = yg[q * rows:(q + 1) * rows]

    for seg in range(6):
        pl.when(j == seg)(functools.partial(finish, seg))


def _head_project(h, u, w_in, gains, conv_params, rope_tabs, layer, batch, seq, name):
    t = h.shape[0]
    rope = rope_tabs is not None
    per_batch = HEADS_TM // seq
    col0 = 2 * C_CONV // HEADS_WBLK
    assert sum(len(s) for s in CONV_SPLIT) * CONV_CH == HEADS_TM
    wspec = lambda c: pl.BlockSpec((None, D_MODEL, HEADS_WBLK),
                                   lambda i, j: (layer, 0, col0 + (SEG_W // HEADS_WBLK) * j + c))
    cvec = lambda: pl.BlockSpec((None, 1, C_CONV), lambda i, j: (layer, 0, 0))
    conv_w, conv_b, ln_g, ln_b = conv_params
    in_specs = [
        pl.BlockSpec((HEADS_TM, D_MODEL), lambda i, j: (i, 0)),
        wspec(0), wspec(1), wspec(2),
        pl.BlockSpec((None, None, 1, SEG_W), lambda i, j: (layer, j, 0, 0)),
        pl.BlockSpec((HEADS_TM, C_CONV), lambda i, j: (i, 0)),
        pl.BlockSpec((None, CONV_K, C_CONV), lambda i, j: (layer, 0, 0)),
        cvec(), cvec(), cvec(),
    ]
    args = [h, w_in, w_in, w_in, gains, u, conv_w, conv_b.reshape(DEPTH, 1, C_CONV),
            ln_g.reshape(DEPTH, 1, C_CONV), ln_b.reshape(DEPTH, 1, C_CONV)]
    if rope:
        in_specs += [pl.BlockSpec((DEC_SEQ, LANES), lambda i, j: (0, 0))] * 3
        args += list(rope_tabs)
    if per_batch == 1:
        seg_spec = pl.BlockSpec((None, None, N_HEADS, seq, HEAD_DIM), lambda i, j: (j, i, 0, 0, 0))
    else:
        seg_spec = pl.BlockSpec((None, per_batch, N_HEADS, seq, HEAD_DIM), lambda i, j: (j, i, 0, 0, 0))
    return pl.pallas_call(
        functools.partial(_heads_kernel, rope=rope, per_batch=per_batch, seq=seq),
        grid=(t // HEADS_TM, 6),
        in_specs=in_specs,
        out_specs=[seg_spec, pl.BlockSpec((HEADS_TM, C_CONV), lambda i, j: (i, 0))],
        out_shape=[jax.ShapeDtypeStruct((6, batch, N_HEADS, seq, HEAD_DIM), BF16 if rope else F32),
                   jax.ShapeDtypeStruct((t, C_CONV), BF16)],
        scratch_shapes=[pltpu.VMEM((CONV_CH + 2 * CONV_HALO, C_CONV), F32),
                        pltpu.VMEM((CONV_CH, C_CONV), F32)],
        compiler_params=_cparams(("arbitrary", "arbitrary")),
        name=name,
    )(*args)


CONV_CH = 128
CONV_HALO = 16


def _conv_window(win_ref, y_ref, w_ref, b_ref, lg_ref, lb_ref):
    off = CONV_HALO - CONV_K // 2
    sub = 8
    win_rows = CONV_CH + 2 * CONV_HALO
    for g in range(C_CONV // LANES):
        sl = slice(g * LANES, (g + 1) * LANES)
        window = win_ref[:, sl]
        acc = jnp.zeros((CONV_CH, LANES), F32) + b_ref[:, sl]
        for phase in range(sub):
            taps = [k for k in range(CONV_K) if (off + k) % sub == phase]
            if not taps:
                continue
            shifted = window if phase == 0 else pltpu.roll(window, win_rows - phase, 0)
            for k in taps:
                a = (off + k) // sub * sub
                acc = acc + shifted[a:a + CONV_CH, :] * w_ref[k:k + 1, sl]
        y_ref[:, sl] = acc

    y = y_ref[...]
    mu = jnp.mean(y, axis=-1, keepdims=True)
    yc = y - mu
    var = jnp.mean(yc * yc, axis=-1, keepdims=True)
    z = yc * lax.rsqrt(var + 1e-5) * lg_ref[...] + lb_ref[...]
    return _silu(z)


def _conv_chunk(c, u_ref, conv_refs, win_ref, y_ref, a_ref, seq):
    assert u_ref.shape[0] % seq == 0
    r0 = c * CONV_CH
    zeros = jnp.zeros((CONV_HALO, C_CONV), F32)
    win_ref[0:CONV_HALO, :] = zeros if r0 % seq == 0 else u_ref[r0 - CONV_HALO:r0, :]
    win_ref[CONV_HALO:CONV_HALO + CONV_CH, :] = u_ref[r0:r0 + CONV_CH, :]
    win_ref[CONV_HALO + CONV_CH:, :] = (zeros if (r0 + CONV_CH) % seq == 0
                                        else u_ref[r0 + CONV_CH:r0 + CONV_CH + CONV_HALO, :])
    a_ref[r0:r0 + CONV_CH, :] = _conv_window(win_ref, y_ref, *conv_refs).astype(a_ref.dtype)


_NT = (((1,), (1,)), ((), ()))


LOG2E = 1.4426950408889634


def _softmax_rows(s):
    m = jnp.max(s, axis=-1, keepdims=True)
    e = jnp.exp(s - m)
    return e / jnp.sum(e, axis=-1, keepdims=True)


def _exp2_rows(s2):
    e = jnp.exp2(s2 - jnp.max(s2, axis=-1, keepdims=True))
    return e, 1.0 / jnp.sum(e, axis=-1, keepdims=True)


def _diff_lambda(lam_ref, lam_init):
    lv = lam_ref[...]
    return (jnp.exp(jnp.sum(lv[0:1] * lv[1:2], axis=-1, keepdims=True))
            - jnp.exp(jnp.sum(lv[2:3] * lv[3:4], axis=-1, keepdims=True)) + lam_init)


def _diff_head(q, k, v, lam, g, lam_init, long_keys):
    lane = lax.broadcasted_iota(I32, q.shape, 1)
    if long_keys:
        qs = q * (DQK ** -0.5 * LOG2E)
        q0 = jnp.where(lane < DQK, qs, 0.0).astype(BF16)
        q1 = jnp.where(lane >= DQK, qs, 0.0).astype(BF16)
        e0, r0 = _exp2_rows(lax.dot_general(q0, k, _NT, preferred_element_type=F32))
        e1, r1 = _exp2_rows(lax.dot_general(q1, k, _NT, preferred_element_type=F32))
        o = (jnp.dot(e0.astype(BF16), v, preferred_element_type=F32) * r0
             - jnp.dot(e1.astype(BF16), v, preferred_element_type=F32) * (lam * r1))
    else:
        q0 = jnp.where(lane < DQK, q, 0.0).astype(BF16)
        q1 = jnp.where(lane >= DQK, q, 0.0).astype(BF16)
        scale = DQK ** -0.5
        p0 = _softmax_rows(lax.dot_general(q0, k, _NT, preferred_element_type=F32) * scale)
        p1 = _softmax_rows(lax.dot_general(q1, k, _NT, preferred_element_type=F32) * scale)
        o = jnp.dot((p0 - lam * p1).astype(BF16), v, preferred_element_type=F32)
    ms = jnp.mean(o * o, axis=-1, keepdims=True)
    return o * lax.rsqrt(ms + 1e-5) * g * (1.0 - lam_init)


def _seg_spec(seg, rows, index):
    def imap(*ids):
        b, h, r = index(*ids)
        return (seg, b, h, r, 0)
    return pl.BlockSpec((None, None, None, rows, HEAD_DIM), imap)


def _diff_attn_lat_kernel(lam_ref, g_ref, q_ref, k_ref, v_ref, ck_ref, cv_ref, o_ref, *, lam_init):
    k = jnp.concatenate([k_ref[...].astype(BF16), ck_ref[...].astype(BF16)], axis=0)
    v = jnp.concatenate([v_ref[...].astype(BF16), cv_ref[...].astype(BF16)], axis=0)
    lam = _diff_lambda(lam_ref, lam_init)
    o_ref[...] = _diff_head(q_ref[...].astype(F32), k, v, lam, g_ref[...], lam_init,
                            True).astype(o_ref.dtype)


def _diff_attention_lat(segs, lam_params, subln_g, cache_k, cache_v, layer, tq):
    nq = DEC_SEQ // tq
    cspec = pl.BlockSpec((None, None, None, PAST_LEN, HEAD_DIM), lambda b, h, qi: (b, layer, h, 0, 0))
    return pl.pallas_call(
        functools.partial(_diff_attn_lat_kernel, lam_init=_lambda_init(layer)),
        grid=(DEC_BATCH, N_HEADS, nq),
        in_specs=[
            pl.BlockSpec((None, 4, DQK), lambda b, h, qi: (layer, 0, 0)),
            pl.BlockSpec((None, 1, HEAD_DIM), lambda b, h, qi: (layer, 0, 0)),
            _seg_spec(SEG_DQ, tq, lambda b, h, qi: (b, h, qi)),
            _seg_spec(SEG_DK, DEC_SEQ, lambda b, h, qi: (b, h, 0)),
            _seg_spec(SEG_DV, DEC_SEQ, lambda b, h, qi: (b, h, 0)),
            cspec, cspec,
        ],
        out_specs=pl.BlockSpec((tq, HEAD_DIM), lambda b, h, qi: (b * nq + qi, h)),
        out_shape=jax.ShapeDtypeStruct((T_LAT, SEG_W), BF16),
        compiler_params=_cparams(("arbitrary", "arbitrary", "arbitrary")),
        name="diff_attn_lat",
    )(lam_params, subln_g.reshape(DEPTH, 1, HEAD_DIM), segs, segs, segs, cache_k, cache_v)


CACHE_SEGS = (SEG_DK, SEG_DV, SEG_NK, SEG_NV)


def _ctx_attn_kernel(lam_ref, g_ref, *refs, lam_init, n_prev):
    seg_refs, refs = refs[:6], refs[6:]
    dq_ref, dk_ref, dv_ref, nq_ref, nk_ref, nv_ref = seg_refs
    prev_refs, refs = refs[:n_prev * len(CACHE_SEGS)], refs[n_prev * len(CACHE_SEGS):]
    d_ref, n_ref = refs[:2]
    cache_refs = refs[2:]
    lam = _diff_lambda(lam_ref, lam_init)
    for h in range(N_HEADS):
        sl = slice(h * HEAD_DIM, (h + 1) * HEAD_DIM)
        d_ref[:, sl] = _diff_head(dq_ref[h], dk_ref[h].astype(BF16), dv_ref[h].astype(BF16), lam,
                                  g_ref[...], lam_init, False).astype(d_ref.dtype)
        s = lax.dot_general(nq_ref[h].astype(BF16), nk_ref[h].astype(BF16), _NT,
                            preferred_element_type=F32) * (HEAD_DIM ** -0.5)
        n_ref[:, sl] = jnp.dot(_softmax_rows(s).astype(BF16), nv_ref[h].astype(BF16),
                               preferred_element_type=F32).astype(n_ref.dtype)
    for n, (seg, c_ref) in enumerate(zip(CACHE_SEGS, cache_refs)):
        for layer in range(n_prev):
            c_ref[layer] = prev_refs[layer * len(CACHE_SEGS) + n][...]
        c_ref[n_prev] = seg_refs[seg][...]


def _ctx_attention(segs, prev_segs, lam_params, subln_g, layer):
    seg = lambda s: pl.BlockSpec((None, None, N_HEADS, SEQ, HEAD_DIM), lambda b: (s, b, 0, 0, 0))
    out = pl.BlockSpec((SEQ, SEG_W), lambda b: (b, 0))
    in_specs = [pl.BlockSpec((None, 4, DQK), lambda b: (layer, 0, 0)),
                pl.BlockSpec((None, 1, HEAD_DIM), lambda b: (layer, 0, 0))] + [seg(s) for s in range(6)]
    args = [lam_params, subln_g.reshape(DEPTH, 1, HEAD_DIM)] + [segs] * 6
    out_specs = [out, out]
    out_shape = [jax.ShapeDtypeStruct((T_CTX, SEG_W), BF16)] * 2
    for prev in prev_segs:
        in_specs += [seg(s) for s in CACHE_SEGS]
        args += [prev] * len(CACHE_SEGS)
    if prev_segs:
        assert len(prev_segs) == DEPTH - 1
        cache = pl.BlockSpec((None, DEPTH, N_HEADS, SEQ, HEAD_DIM), lambda b: (b, 0, 0, 0, 0))
        out_specs += [cache] * len(CACHE_SEGS)
        out_shape += [jax.ShapeDtypeStruct((BATCH, DEPTH, N_HEADS, SEQ, HEAD_DIM), F32)] * len(CACHE_SEGS)
    return pl.pallas_call(
        functools.partial(_ctx_attn_kernel, lam_init=_lambda_init(layer), n_prev=len(prev_segs)),
        grid=(BATCH,),
        in_specs=in_specs,
        out_specs=out_specs,
        out_shape=out_shape,
        compiler_params=_cparams(("arbitrary",)),
        name="ctx_attn_caches" if prev_segs else "ctx_attn",
    )(*args)


NA_PAIRS = 2 * NA_KR - 2
NA_QROWS = 4
NA_KROWS = 12


def _na_key_block(chunk):
    first = min(max(chunk * NA_QROWS - NA_KR // 2, 0), GRID_ROWS - NA_KR)
    return min(first - first % 2, GRID_ROWS - NA_KROWS)


def _na_attn_kernel(src_ref, mask_ref, q_ref, k_ref, v_ref, ck_ref, cv_ref, o_ref, bias_ref):
    for d in range(NA_PAIRS):
        row = jnp.broadcast_to(src_ref[d:d + 1, :] * LOG2E, (GRID_W, LANES))
        tile = pltpu.roll(row, LANES - (NA_KC - 1), 1, stride=1, stride_axis=0)
        for v in range(3):
            bias_ref[v, d] = tile + mask_ref[v]
    dead = jnp.full((GRID_W, LANES), NEG_BIG, F32)
    ck = ck_ref[...].astype(BF16)
    cv = cv_ref[...].astype(BF16)
    scale = HEAD_DIM ** -0.5 * LOG2E
    for chunk in range(GRID_ROWS // NA_QROWS):
        kb = _na_key_block(chunk)
        rows = []
        for qr in range(chunk * NA_QROWS, (chunk + 1) * NA_QROWS):
            start = min(max(qr - NA_KR // 2, 0), GRID_ROWS - NA_KR)
            assert kb <= start and start + NA_KR <= kb + NA_KROWS
            tiles = []
            for m in range(NA_KROWS // 2):
                r0 = kb + 2 * m
                live0 = start <= r0 < start + NA_KR
                live1 = start <= r0 + 1 < start + NA_KR
                d = r0 - qr + NA_KR - 1
                assert not (live0 or live1) or 0 <= d < NA_PAIRS
                if live0 and live1:
                    tiles.append(bias_ref[0, d])
                elif live0:
                    tiles.append(bias_ref[1, d])
                elif live1:
                    tiles.append(bias_ref[2, d])
                else:
                    tiles.append(dead)
            rows.append(jnp.concatenate(tiles, axis=1))
        bias = jnp.concatenate(rows, axis=0)
        qs = slice(chunk * NA_QROWS * GRID_W, (chunk + 1) * NA_QROWS * GRID_W)
        ks = slice(kb * GRID_W, (kb + NA_KROWS) * GRID_W)
        q = (q_ref[qs, :].astype(F32) * scale).astype(BF16)
        kl = k_ref[ks, :].astype(BF16)
        vl = v_ref[ks, :].astype(BF16)
        s_loc = lax.dot_general(q, kl, _NT, preferred_element_type=F32) + bias
        s_ctx = lax.dot_general(q, ck, _NT, preferred_element_type=F32)
        m = jnp.maximum(jnp.max(s_loc, axis=-1, keepdims=True), jnp.max(s_ctx, axis=-1, keepdims=True))
        e_loc = jnp.exp2(s_loc - m)
        e_ctx = jnp.exp2(s_ctx - m)
        r = 1.0 / (jnp.sum(e_loc, axis=-1, keepdims=True) + jnp.sum(e_ctx, axis=-1, keepdims=True))
        o_ref[qs, :] = ((jnp.dot(e_loc.astype(BF16), vl, preferred_element_type=F32)
                         + jnp.dot(e_ctx.astype(BF16), cv, preferred_element_type=F32)) * r
                        ).astype(o_ref.dtype)


def _na_attention(segs, rpb_src, na_mask, cache_k, cache_v, layer):
    blk = lambda s: _seg_spec(s, DEC_SEQ, lambda b, h: (b, h, 0))
    cspec = pl.BlockSpec((None, None, None, PAST_LEN, HEAD_DIM), lambda b, h: (b, layer, h, 0, 0))
    return pl.pallas_call(
        _na_attn_kernel,
        grid=(DEC_BATCH, N_HEADS),
        in_specs=[
            pl.BlockSpec((None, None, 16, LANES), lambda b, h: (layer, h, 0, 0)),
            pl.BlockSpec((3, GRID_W, LANES), lambda b, h: (0, 0, 0)),
            blk(SEG_NQ), blk(SEG_NK), blk(SEG_NV), cspec, cspec,
        ],
        out_specs=pl.BlockSpec((DEC_SEQ, HEAD_DIM), lambda b, h: (b, h)),
        out_shape=jax.ShapeDtypeStruct((T_LAT, SEG_W), BF16),
        scratch_shapes=[pltpu.VMEM((3, NA_PAIRS, GRID_W, LANES), F32)],
        compiler_params=_cparams(("arbitrary", "arbitrary")),
        name="na_attn_lat",
    )(rpb_src, na_mask, segs, segs, segs, cache_k, cache_v)


def _split_f32(x, n):
    terms = []
    for _ in range(n):
        t = x.astype(BF16).astype(F32)
        terms.append(t)
        x = x - t
    return terms


def _mix_kernel(a_ref, d_ref, n_ref, w_ref, x_ref, g1_ref, sh_ref, sc_ref, ng_ref, wr_ref, br_ref,
                x1_ref, lt_ref):
    k0, k1 = C_CONV, C_CONV + N_HEADS * HEAD_DIM
    mixed = (jnp.dot(a_ref[...].astype(BF16), w_ref[0:k0, :], preferred_element_type=F32)
             + jnp.dot(d_ref[...].astype(BF16), w_ref[k0:k1, :], preferred_element_type=F32)
             + jnp.dot(n_ref[...].astype(BF16), w_ref[k1:, :], preferred_element_type=F32))
    x1 = x_ref[...] + g1_ref[...] * mixed
    x1_ref[...] = x1
    h2 = _modulate(x1, ng_ref[...], sh_ref[...], sc_ref[...])
    wh, wm, wl = _split_f32(wr_ref[...], 3)
    w_cat = (wh + pltpu.roll(wm, ROUTE_ROWS, 1) + pltpu.roll(wl, 2 * ROUTE_ROWS, 1)).astype(BF16)
    hh, hm = _split_f32(h2, 2)
    r = jnp.dot(jnp.concatenate([hh, hm], axis=0).astype(BF16), w_cat, preferred_element_type=F32)
    rh, rm = r[:MIX_TM], r[MIX_TM:]
    back = lambda v, groups: pltpu.roll(v, LANES - groups * ROUTE_ROWS, 1)
    small = (back(rm, 1) + back(rh, 2)) + (back(rh, 1) + rm)
    logits = (small + rh).T[:ROUTE_ROWS, :]
    lt_ref[...] = logits + br_ref[...]


def _mix(a_out, d_out, n_out, w_out_bf, x, mod8, norm_g, wr_t, br, layer, row_of_block):
    t = x.shape[0]
    rows = lambda width: pl.BlockSpec((MIX_TM, width), lambda i: (i, 0))
    return pl.pallas_call(
        _mix_kernel,
        grid=(t // MIX_TM,),
        in_specs=[
            rows(C_CONV), rows(N_HEADS * HEAD_DIM), rows(N_HEADS * HEAD_DIM),
            pl.BlockSpec((None, D_MODEL, D_MODEL), lambda i: (layer, 0, 0)),
            rows(D_MODEL),
            _mod_spec(layer, 2, row_of_block),
            _mod_spec(layer, 3, row_of_block),
            _mod_spec(layer, 4, row_of_block),
            pl.BlockSpec((None, 1, D_MODEL), lambda i: (layer, 0, 0)),
            pl.BlockSpec((None, D_MODEL, LANES), lambda i: (layer, 0, 0)),
            pl.BlockSpec((None, ROUTE_ROWS, 1), lambda i: (layer, 0, 0)),
        ],
        out_specs=[rows(D_MODEL), pl.BlockSpec((ROUTE_ROWS, MIX_TM), lambda i: (0, i))],
        out_shape=[jax.ShapeDtypeStruct((t, D_MODEL), F32),
                   jax.ShapeDtypeStruct((ROUTE_ROWS, t), F32)],
        compiler_params=_cparams(("arbitrary",)),
        name="mix_%d" % t,
    )(a_out, d_out, n_out, w_out_bf, x, mod8, mod8, mod8, norm_g, wr_t, br)


def _cast_kernel(x_ref, o_ref):
    o_ref[...] = x_ref[...].astype(BF16)


def _cast_bf16(w):
    tm = 512
    spec = pl.BlockSpec((None, tm, D_MODEL), lambda l, i: (l, i, 0))
    return pl.pallas_call(
        _cast_kernel, grid=(DEPTH, D_MODEL // tm), in_specs=[spec], out_specs=spec,
        out_shape=jax.ShapeDtypeStruct(w.shape, BF16),
        compiler_params=_cparams(("arbitrary", "arbitrary")), name="cast_w_out",
    )(w)


def _route_kernel(lt_ref, tri_ref, pos_ref, wt_ref, item_ref):
    t = T_ALL
    lg = lt_ref[0:N_GROUPS, :]
    eg = jnp.exp(lg - jnp.max(lg, axis=0, keepdims=True))
    pg = eg / jnp.sum(eg, axis=0, keepdims=True)
    pg_top = jnp.max(pg, axis=0, keepdims=True)
    gi = lax.broadcasted_iota(I32, pg.shape, 0).astype(F32)
    g_idx = jnp.min(jnp.where(pg == pg_top, gi, float(N_GROUPS)), axis=0, keepdims=True)

    le = jnp.zeros((E_PER_GROUP, t), F32)
    for g in range(N_GROUPS):
        rows = lt_ref[N_GROUPS + g * E_PER_GROUP:N_GROUPS + (g + 1) * E_PER_GROUP, :]
        le = jnp.where(g_idx == float(g), rows, le)
    ee = jnp.exp(le - jnp.max(le, axis=0, keepdims=True))
    pe = ee / jnp.sum(ee, axis=0, keepdims=True)
    ei = lax.broadcasted_iota(I32, pe.shape, 0).astype(F32)
    p1 = jnp.max(pe, axis=0, keepdims=True)
    i1 = jnp.min(jnp.where(pe == p1, ei, float(E_PER_GROUP)), axis=0, keepdims=True)
    pe_rest = jnp.where(ei == i1, -1.0, pe)
    p2 = jnp.max(pe_rest, axis=0, keepdims=True)
    i2 = jnp.min(jnp.where(pe_rest == p2, ei, float(E_PER_GROUP)), axis=0, keepdims=True)
    den = p1 + p2
    wt_ref[0:1, :] = pg_top * (p1 / den)
    wt_ref[1:2, :] = pg_top * (p2 / den)
    e1 = g_idx * E_PER_GROUP + i1
    e2 = g_idx * E_PER_GROUP + i2

    erow = lax.broadcasted_iota(I32, (N_EXPERTS, t), 0).astype(F32)
    oh1 = (erow == e1).astype(F32)
    oh2 = (erow == e2).astype(F32)
    cnt = oh1 + oh2
    carry = jnp.zeros((N_EXPERTS, 1), F32)
    ranks = []
    for b in range(t // CUM_BLK):
        blk = cnt[:, b * CUM_BLK:(b + 1) * CUM_BLK]
        ranks.append(jnp.dot(blk.astype(BF16), tri_ref[...], preferred_element_type=F32) + carry)
        carry = carry + jnp.sum(blk, axis=1, keepdims=True)
    rank = jnp.concatenate(ranks, axis=1)

    erow_l = lax.broadcasted_iota(I32, (N_EXPERTS, LANES), 0)

    def excl_scan(v):
        inc = v
        for s in (1, 2, 4, 8):
            inc = inc + jnp.where(erow_l >= s, pltpu.roll(inc, s, 0), 0.0)
        return inc - v

    total = jnp.broadcast_to(carry, (N_EXPERTS, LANES))
    offs = excl_scan(total)
    pos_ref[0:1, :] = jnp.sum(oh1 * (rank + offs[:, 0:1]), axis=0, keepdims=True).astype(I32)
    pos_ref[1:2, :] = jnp.sum(oh2 * (rank + offs[:, 0:1]), axis=0, keepdims=True).astype(I32)

    shift = int(math.log2(MOE_TM))
    offs_i = offs.astype(I32)
    total_i = total.astype(I32)
    first_tile = lax.shift_right_logical(offs_i, shift)
    last_tile = lax.shift_right_logical(offs_i + total_i - 1, shift)
    n_item = jnp.where(total_i > 0, last_tile - first_tile + 1, 0).astype(F32)
    item_start = excl_scan(n_item)
    item_end = item_start + n_item
    kk = lax.broadcasted_iota(I32, (N_EXPERTS, LANES), 1).astype(F32)
    item_e = jnp.minimum(jnp.sum((item_end <= kk).astype(F32), axis=0, keepdims=True),
                         float(N_EXPERTS - 1))
    sel = (erow_l.astype(F32) == item_e).astype(F32)
    pick = lambda v: jnp.sum(sel * v, axis=0, keepdims=True)
    item_ref[0:1, :] = (kk[0:1] + pick(first_tile.astype(F32) - item_start)).astype(I32)
    item_ref[1:2, :] = item_e.astype(I32)
    item_ref[2:3, :] = pick(offs).astype(I32)
    item_ref[3:4, :] = pick(offs + total).astype(I32)
    item_ref[4:5, :] = item_end[N_EXPERTS - 1:N_EXPERTS, :].astype(I32)
    item_ref[5:8, :] = jnp.zeros((3, LANES), I32)


def _route(lt_all, tri):
    full = lambda shape: pl.BlockSpec(shape, lambda i: (0,) * len(shape))
    return pl.pallas_call(
        _route_kernel,
        grid=(1,),
        in_specs=[full((ROUTE_ROWS, T_ALL)), full((CUM_BLK, CUM_BLK))],
        out_specs=[full((2, T_ALL)), full((2, T_ALL)), full((8, LANES))],
        out_shape=[jax.ShapeDtypeStruct((2, T_ALL), I32), jax.ShapeDtypeStruct((2, T_ALL), F32),
                   jax.ShapeDtypeStruct((8, LANES), I32)],
        compiler_params=_cparams(("arbitrary",)),
        name="moe_route",
    )(lt_all, tri)


N_TOK_TILES = T_ALL // MOE_TM
CTX_TILES = T_CTX // MOE_TM


def _tok_row(i):
    return jnp.where(i < CTX_TILES, 0, 1 + (i - CTX_TILES) // (DEC_SEQ // MOE_TM))


N_CHUNK = D_MODEL // LANES
MOE_PITCH = N_CHUNK + 1
SLAB_ROWS = MOE_TM * MOE_PITCH
ISSUE_UNROLL = 4


def _slab_copy(src, src_tok, dst, dst_tok, sem, pitch):
    return pltpu.make_async_copy(src.at[pl.ds(src_tok * pitch, pitch)],
                                 dst.at[pl.ds(dst_tok * pitch, pitch)], sem)


def _to_slabs(ref, value):
    for c in range(N_CHUNK):
        ref[pl.ds(c, MOE_TM, stride=MOE_PITCH), :] = value[:, c * LANES:(c + 1) * LANES]
    ref[pl.ds(N_CHUNK, MOE_TM, stride=MOE_PITCH), :] = jnp.zeros((MOE_TM, LANES), F32)


def _slab_chunk(ref, c):
    return ref[pl.ds(c, MOE_TM, stride=MOE_PITCH), :]


def _dispatch_kernel(pos_ref, xc_ref, xl_ref, sh_ref, sc_ref, ng_ref, xs_ref, h_ref, sem):
    i = pl.program_id(0)
    slot = i % 2
    buf = h_ref.at[slot]

    def drain(s):
        def body(r, carry):
            _slab_copy(h_ref.at[s], 0, xs_ref, 0, sem.at[s], MOE_PITCH).wait()
            return carry
        lax.fori_loop(0, 2 * MOE_TM, body, 0, unroll=ISSUE_UNROLL)

    @pl.when(i >= 2)
    def _():
        drain(slot)

    @pl.when(i < CTX_TILES)
    def _():
        _to_slabs(buf, _modulate(xc_ref[...], ng_ref[...], sh_ref[...], sc_ref[...]))

    @pl.when(i >= CTX_TILES)
    def _():
        _to_slabs(buf, _modulate(xl_ref[...], ng_ref[...], sh_ref[...], sc_ref[...]))

    def issue(r, carry):
        tok = i * MOE_TM + r
        _slab_copy(buf, r, xs_ref, pos_ref[tok], sem.at[slot], MOE_PITCH).start(priority=0)
        _slab_copy(buf, r, xs_ref, pos_ref[T_ALL + tok], sem.at[slot], MOE_PITCH).start(priority=1)
        return carry

    lax.fori_loop(0, MOE_TM, issue, 0, unroll=ISSUE_UNROLL)

    @pl.when(i == N_TOK_TILES - 1)
    def _():
        drain(1 - slot)
        drain(slot)


def _dispatch(pos_flat, x1c, x1l, mod8, norm_g, layer):
    grid_spec = pltpu.PrefetchScalarGridSpec(
        num_scalar_prefetch=1,
        grid=(N_TOK_TILES,),
        in_specs=[
            pl.BlockSpec((MOE_TM, D_MODEL), lambda i, *_: (jnp.minimum(i, CTX_TILES - 1), 0)),
            pl.BlockSpec((MOE_TM, D_MODEL), lambda i, *_: (jnp.maximum(i - CTX_TILES, 0), 0)),
            _mod_spec(layer, 3, _tok_row),
            _mod_spec(layer, 4, _tok_row),
            pl.BlockSpec((None, 1, D_MODEL), lambda i, *_: (layer, 0, 0)),
        ],
        out_specs=pl.BlockSpec(memory_space=pl.ANY),
        scratch_shapes=[pltpu.VMEM((2, SLAB_ROWS, LANES), F32), pltpu.SemaphoreType.DMA((2,))],
    )
    return pl.pallas_call(
        _dispatch_kernel,
        grid_spec=grid_spec,
        out_shape=jax.ShapeDtypeStruct((2 * T_ALL * MOE_PITCH, LANES), F32),
        compiler_params=_cparams(("arbitrary",)),
        name="moe_dispatch",
    )(pos_flat, x1c, x1l, mod8, mod8, norm_g)


N_ROW_TILES = 2 * T_ALL // MOE_TM
X_SLOTS = 3
Y_SLOTS = 2


def _expert_kernel(tile_ref, exp_ref, lo_ref, hi_ref, n_ref, xs_hbm, wg_hbm, wu_hbm, wd_hbm, ys_hbm,
                   x_buf, y_buf, wg_buf, wu_buf, wd_buf, wg_bf, wu_bf, wd_bf, slot_ref, sem, x_sem, y_sem,
                   *, layer):
    k = pl.program_id(0)
    n = n_ref[0]

    def weight_copies(e, s):
        return (pltpu.make_async_copy(wg_hbm.at[layer, e], wg_buf.at[s], sem.at[s, 0]),
                pltpu.make_async_copy(wu_hbm.at[layer, e], wu_buf.at[s], sem.at[s, 1]),
                pltpu.make_async_copy(wd_hbm.at[layer, e], wd_buf.at[s], sem.at[s, 2]))

    def x_copy(t):
        s = t % X_SLOTS
        return pltpu.make_async_copy(xs_hbm.at[pl.ds(t * SLAB_ROWS, SLAB_ROWS)], x_buf.at[s], x_sem.at[s])

    def y_copy(t):
        s = t % Y_SLOTS
        return pltpu.make_async_copy(y_buf.at[s], ys_hbm.at[pl.ds(t * SLAB_ROWS, SLAB_ROWS)], y_sem.at[s])

    @pl.when(k == 0)
    def _():
        slot_ref[0] = 0
        for cp in weight_copies(exp_ref[0], 0):
            cp.start()
        for t in range(X_SLOTS - 1):
            x_copy(t).start()

    @pl.when(k < n)
    def _():
        e = exp_ref[k]
        tile = tile_ref[k]
        prev_tile = tile_ref[jnp.maximum(k - 1, 0)]
        new_expert = jnp.logical_or(k == 0, exp_ref[jnp.maximum(k - 1, 0)] != e)

        @pl.when(jnp.logical_or(k == 0, tile != prev_tile))
        def _():
            x_copy(tile).wait()

            @pl.when(tile + (X_SLOTS - 1) < N_ROW_TILES)
            def _():
                x_copy(tile + (X_SLOTS - 1)).start()

            @pl.when(k > 0)
            def _():
                y_copy(prev_tile).start()

            @pl.when(tile >= Y_SLOTS)
            def _():
                y_copy(tile - Y_SLOTS).wait()

        xs_ref = x_buf.at[tile % X_SLOTS]
        ys_ref = y_buf.at[tile % Y_SLOTS]

        @pl.when(new_expert)
        def _():
            s = slot_ref[0]
            for cp in weight_copies(e, s):
                cp.wait()
            wg_bf[...] = wg_buf[s].astype(BF16)
            wu_bf[...] = wu_buf[s].astype(BF16)
            wd_bf[...] = wd_buf[s].astype(BF16)
            nxt = lax.while_loop(lambda j: jnp.logical_and(j < n, exp_ref[jnp.minimum(j, n - 1)] == e),
                                 lambda j: j + 1, k + 1)

            @pl.when(nxt < n)
            def _():
                for cp in weight_copies(exp_ref[jnp.minimum(nxt, n - 1)], 1 - s):
                    cp.start()
            slot_ref[0] = 1 - s

        x = jnp.concatenate([_slab_chunk(xs_ref, c) for c in range(N_CHUNK)], axis=1).astype(BF16)
        gate = jnp.dot(x, wg_bf[...], preferred_element_type=F32)
        up = jnp.dot(x, wu_bf[...], preferred_element_type=F32)
        hid = (_silu(gate) * up).astype(BF16)
        y = jnp.dot(hid, wd_bf[...], preferred_element_type=F32)
        row0 = tile * MOE_TM
        row = row0 + lax.broadcasted_iota(I32, (MOE_TM, 1), 0)
        mine = jnp.logical_and(row >= lo_ref[k], row < hi_ref[k])
        first = lo_ref[k] <= row0

        @pl.when(first)
        def _():
            _to_slabs(ys_ref, jnp.where(mine, y, 0.0))

        @pl.when(jnp.logical_not(first))
        def _():
            for c in range(N_CHUNK):
                old = _slab_chunk(ys_ref, c)
                ys_ref[pl.ds(c, MOE_TM, stride=MOE_PITCH), :] = jnp.where(
                    mine, y[:, c * LANES:(c + 1) * LANES], old)

        @pl.when(k == n - 1)
        def _():
            y_copy(tile).start()
            y_copy(tile).wait()

            @pl.when(tile >= 1)
            def _():
                y_copy(tile - 1).wait()


def _experts(items, xs, w_gate, w_up, w_down, layer):
    item_tile, item_exp, item_lo, item_hi, n_items = items
    hbm = pl.BlockSpec(memory_space=pl.ANY)
    grid_spec = pltpu.PrefetchScalarGridSpec(
        num_scalar_prefetch=5, grid=(MOE_ITEMS,),
        in_specs=[hbm, hbm, hbm, hbm], out_specs=hbm,
        scratch_shapes=[
            pltpu.VMEM((X_SLOTS, SLAB_ROWS, LANES), F32), pltpu.VMEM((Y_SLOTS, SLAB_ROWS, LANES), F32),
            pltpu.VMEM((2, D_MODEL, D_FF), F32), pltpu.VMEM((2, D_MODEL, D_FF), F32),
            pltpu.VMEM((2, D_FF, D_MODEL), F32),
            pltpu.VMEM((D_MODEL, D_FF), BF16), pltpu.VMEM((D_MODEL, D_FF), BF16),
            pltpu.VMEM((D_FF, D_MODEL), BF16),
            pltpu.SMEM((1,), I32), pltpu.SemaphoreType.DMA((2, 3)),
            pltpu.SemaphoreType.DMA((X_SLOTS,)), pltpu.SemaphoreType.DMA((Y_SLOTS,)),
        ])
    return pl.pallas_call(
        functools.partial(_expert_kernel, layer=layer),
        grid_spec=grid_spec,
        out_shape=jax.ShapeDtypeStruct((2 * T_ALL * MOE_PITCH, LANES), F32),
        compiler_params=_cparams(("arbitrary",)),
        name="moe_experts",
    )(item_tile, item_exp, item_lo, item_hi, n_items, xs, w_gate, w_up, w_down)


def _combine_kernel(pos_ref, xc_ref, xl_ref, wt_ref, g2_ref, ys_ref, oc_ref, ol_ref, y_ref, sem):
    i = pl.program_id(0)
    slot = i % 2

    def gather(tile, s):
        def body(r, carry):
            tok = tile * MOE_TM + r
            _slab_copy(ys_ref, pos_ref[tok], y_ref.at[s, 0], r, sem.at[s], MOE_PITCH).start(priority=0)
            _slab_copy(ys_ref, pos_ref[T_ALL + tok], y_ref.at[s, 1], r, sem.at[s], MOE_PITCH).start(priority=1)
            return carry
        lax.fori_loop(0, MOE_TM, body, 0, unroll=ISSUE_UNROLL)

    @pl.when(i == 0)
    def _():
        gather(0, 0)

    @pl.when(i + 1 < N_TOK_TILES)
    def _():
        gather(i + 1, 1 - slot)

    def drain(r, carry):
        _slab_copy(ys_ref, 0, y_ref.at[slot, 0], 0, sem.at[slot], MOE_PITCH).wait()
        return carry

    lax.fori_loop(0, 2 * MOE_TM, drain, 0, unroll=ISSUE_UNROLL)

    def write(x_ref, o_ref):
        w0 = wt_ref[:, 0:1]
        w1 = wt_ref[:, 1:2]
        for c in range(N_CHUNK):
            sl = slice(c * LANES, (c + 1) * LANES)
            moe = w0 * _slab_chunk(y_ref.at[slot, 0], c) + w1 * _slab_chunk(y_ref.at[slot, 1], c)
            o_ref[:, sl] = x_ref[:, sl] + g2_ref[:, sl] * moe

    pl.when(i < CTX_TILES)(lambda: write(xc_ref, oc_ref))
    pl.when(i >= CTX_TILES)(lambda: write(xl_ref, ol_ref))


def _combine(pos_flat, x1c, x1l, wts_t, mod8, ys, layer):
    cspec = pl.BlockSpec((MOE_TM, D_MODEL), lambda i, *_: (jnp.minimum(i, CTX_TILES - 1), 0))
    lspec = pl.BlockSpec((MOE_TM, D_MODEL), lambda i, *_: (jnp.maximum(i - CTX_TILES, 0), 0))
    grid_spec = pltpu.PrefetchScalarGridSpec(
        num_scalar_prefetch=1,
        grid=(N_TOK_TILES,),
        in_specs=[
            cspec, lspec,
            pl.BlockSpec((MOE_TM, 2), lambda i, *_: (i, 0)),
            _mod_spec(layer, 5, _tok_row),
            pl.BlockSpec(memory_space=pl.ANY),
        ],
        out_specs=[cspec, lspec],
        scratch_shapes=[pltpu.VMEM((2, 2, SLAB_ROWS, LANES), F32), pltpu.SemaphoreType.DMA((2,))],
    )
    return pl.pallas_call(
        _combine_kernel,
        grid_spec=grid_spec,
        out_shape=[jax.ShapeDtypeStruct((T_CTX, D_MODEL), F32),
                   jax.ShapeDtypeStruct((T_LAT, D_MODEL), F32)],
        compiler_params=_cparams(("arbitrary",)),
        name="moe_combine",
    )(pos_flat, x1c, x1l, wts_t, mod8, ys)


def _rope_tables():
    half = DQK // 2
    inv = 1.0 / (ROPE_BASE ** (np.arange(0, half, 2, dtype=np.float32) / half))
    t = np.arange(DEC_SEQ)
    ang_r = (t // GRID_W).astype(np.float32)[:, None] * inv
    ang_c = (t % GRID_W).astype(np.float32)[:, None] * inv
    ang = np.concatenate([ang_r, ang_r, ang_c, ang_c], axis=-1).astype(np.float32)
    ang = np.concatenate([ang, ang], axis=-1)
    first = (np.arange(LANES) % 32) < 16
    cos, sin = np.cos(ang), np.sin(ang)
    return (jnp.asarray(cos, F32), jnp.asarray(np.where(first, -sin, 0.0), F32),
            jnp.asarray(np.where(first, 0.0, sin), F32))


def _na_mask():
    qc = np.arange(GRID_W)[:, None]
    kc = (np.arange(LANES) % GRID_W)[None, :]
    ws = np.clip(qc - NA_KC // 2, 0, GRID_W - NA_KC)
    ok = (kc >= ws) & (kc < ws + NA_KC)
    first = (np.arange(LANES) < GRID_W)[None, :]
    masks = [ok, ok & first, ok & ~first]
    return jnp.asarray(np.stack([np.where(m, 0.0, NEG_BIG) for m in masks]), F32)


def kernel(x_prompt, x_sample, cache_diff_k, cache_diff_v, cache_na_k, cache_na_v, c, c_ctx, norm_mix_g, norm_ffn_g, w_ada, b_ada, w_in, w_out, conv_w, conv_b, conv_ln_g, conv_ln_b, diff_qn_g, diff_kn_g, diff_lam_q1, diff_lam_k1, diff_lam_q2, diff_lam_k2, diff_subln_g, na_qn_g, na_kn_g, na_rpb, moe_wr_g, moe_br_g, moe_wr_e, moe_br_e, moe_w_gate, moe_w_up, moe_w_down):
    cvec = jnp.zeros((MOD_ROWS, D_MODEL), F32).at[0].set(c_ctx).at[1:1 + DEC_BATCH].set(c)
    ones = jnp.ones((DEPTH, SEG_W), F32)
    gains = jnp.stack(
        [jnp.tile(diff_qn_g, (1, 2 * N_HEADS)), jnp.tile(diff_kn_g, (1, 2 * N_HEADS)), ones,
         jnp.tile(na_qn_g, (1, N_HEADS)), jnp.tile(na_kn_g, (1, N_HEADS)), ones],
        axis=1).reshape(DEPTH, 6, 1, SEG_W)
    rope_tabs = _rope_tables()
    na_mask = _na_mask()
    lam_params = jnp.stack([diff_lam_q1, diff_lam_k1, diff_lam_q2, diff_lam_k2], axis=1)
    rpb_pad = jnp.pad(na_rpb, ((0, 0), (0, 0), (0, 0), (0, DQK - na_rpb.shape[-1])))
    rpb_src = jnp.concatenate([rpb_pad[:, :, :-1], rpb_pad[:, :, 1:]], axis=-1)
    rpb_src = jnp.pad(rpb_src, ((0, 0), (0, 0), (0, 16 - NA_PAIRS), (0, 0)))
    wr_t = jnp.concatenate([moe_wr_g, moe_wr_e.reshape(DEPTH, D_MODEL, N_EXPERTS)], axis=2)
    wr_t = jnp.pad(wr_t, ((0, 0), (0, 0), (0, LANES - N_GROUPS - N_EXPERTS)))
    br = jnp.concatenate([moe_br_g, moe_br_e.reshape(DEPTH, N_EXPERTS)], axis=1)
    br = jnp.pad(br, ((0, 0), (0, ROUTE_ROWS - N_GROUPS - N_EXPERTS))).reshape(DEPTH, ROUTE_ROWS, 1)
    tri = jnp.asarray(np.triu(np.ones((CUM_BLK, CUM_BLK), np.float32), 1), BF16)
    norm_mix = norm_mix_g.reshape(DEPTH, 1, D_MODEL)
    norm_ffn = norm_ffn_g.reshape(DEPTH, 1, D_MODEL)
    head_major = lambda cache: jnp.transpose(cache, (0, 1, 3, 2, 4))
    ck_diff, cv_diff = head_major(cache_diff_k), head_major(cache_diff_v)
    ck_na, cv_na = head_major(cache_na_k), head_major(cache_na_v)

    mod8 = _ada_modulation(cvec, w_ada, b_ada).reshape(DEPTH * MOD_ROWS, 1, N_ADA)
    w_out_bf = _cast_bf16(w_out)

    ctx_row = lambda i: 0
    lat_row_pre = lambda i: 1 + i * PRE_TM // DEC_SEQ
    lat_row_mix = lambda i: 1 + i * MIX_TM // DEC_SEQ

    xc = x_prompt.reshape(T_CTX, D_MODEL)
    xl = x_sample.reshape(T_LAT, D_MODEL)
    ctx_segs = []
    for layer in range(DEPTH):
        hc, uc = _pre_project(xc, mod8, norm_mix, w_in, layer, ctx_row, "pre_ctx")
        hl, ul = _pre_project(xl, mod8, norm_mix, w_in, layer, lat_row_pre, "pre_lat")
        conv_params = (conv_w, conv_b, conv_ln_g, conv_ln_b)
        sc, ac = _head_project(hc, uc, w_in, gains, conv_params, None, layer, BATCH, SEQ, "heads_ctx")
        sl, al = _head_project(hl, ul, w_in, gains, conv_params, rope_tabs, layer, DEC_BATCH, DEC_SEQ,
                               "heads_lat")
        ctx_segs.append(sc)

        last = layer == DEPTH - 1
        dc, nc, *caches = _ctx_attention(sc, ctx_segs[:-1] if last else [], lam_params, diff_subln_g, layer)
        dl = _diff_attention_lat(sl, lam_params, diff_subln_g, ck_diff, cv_diff, layer, DIFF_TQ)
        nl = _na_attention(sl, rpb_src, na_mask, ck_na, cv_na, layer)

        x1c, ltc = _mix(ac, dc, nc, w_out_bf, xc, mod8, norm_ffn, wr_t, br, layer, ctx_row)
        x1l, ltl = _mix(al, dl, nl, w_out_bf, xl, mod8, norm_ffn, wr_t, br, layer, lat_row_mix)

        pos, wts, items = _route(jnp.concatenate([ltc, ltl], axis=1), tri)
        pos_flat = pos.reshape(2 * T_ALL)
        items = [items[r, :MOE_ITEMS] for r in range(4)] + [items[4, :1]]
        xs = _dispatch(pos_flat, x1c, x1l, mod8, norm_ffn, layer)
        ys = _experts(items, xs, moe_w_gate, moe_w_up, moe_w_down, layer)
        xc, xl = _combine(pos_flat, x1c, x1l, wts.T, mod8, ys, layer)

    outs = [jnp.transpose(cache, (0, 1, 3, 2, 4)) for cache in caches]
    return (xc.reshape(BATCH, SEQ, D_MODEL), xl.reshape(DEC_BATCH, DEC_SEQ, D_MODEL), *outs)
```

```python
import functools
import math

import numpy as np
import jax
import jax.numpy as jnp
from jax import lax
from jax.experimental import pallas as pl
from jax.experimental.pallas import tpu as pltpu

F32 = jnp.float32
BF16 = jnp.bfloat16
I32 = jnp.int32

D_MODEL = 2048
BATCH = 16
SEQ = 256
DEPTH = 2
DEC_BATCH = 2
DEC_SEQ = 1024
PAST_LEN = 256
GRID_W = 64
GRID_ROWS = DEC_SEQ // GRID_W
HEAD_DIM = 128
C_CONV = 512
CONV_K = 31
N_HEADS = 6
DQK = 64
NA_KR = 8
NA_KC = 16
ROPE_BASE = 10000.0
N_GROUPS = 4
E_PER_GROUP = 4
N_EXPERTS = 16
D_FF = 512
N_ADA = 6 * D_MODEL
W_IN_COLS = 2 * C_CONV + 6 * N_HEADS * HEAD_DIM
T_CTX = BATCH * SEQ
T_LAT = DEC_BATCH * DEC_SEQ
T_ALL = T_CTX + T_LAT

LANES = 128
MOD_ROWS = 8
PRE_TM = 512
HEADS_TM = 1024
MIX_TM = 512
DIFF_TQ = 1024
MOE_TM = 256
MOE_ITEMS = 2 * T_ALL // MOE_TM + N_EXPERTS
ROUTE_ROWS = 32
CUM_BLK = 512
VMEM_LIMIT = 56 * 1024 * 1024
NEG_BIG = -1e30


def _cparams(sem):
    return pltpu.CompilerParams(dimension_semantics=sem, vmem_limit_bytes=VMEM_LIMIT)


def _sigmoid(x):
    return 1.0 / (1.0 + jnp.exp(-x))


def _silu(x):
    return x * _sigmoid(x)


def _lambda_init(layer):
    return 0.8 - 0.6 * math.exp(-0.3 * layer)


def _ada_kernel(c_ref, w_ref, b_ref, o_ref):
    c = c_ref[...]
    s = _silu(c).astype(BF16)
    o_ref[...] = jnp.dot(s, w_ref[...].astype(BF16), preferred_element_type=F32) + b_ref[...]


def _ada_modulation(cvec, w_ada, b_ada):
    tn = 1024
    return pl.pallas_call(
        _ada_kernel,
        grid=(DEPTH, N_ADA // tn),
        in_specs=[
            pl.BlockSpec((MOD_ROWS, D_MODEL), lambda l, j: (0, 0)),
            pl.BlockSpec((None, D_MODEL, tn), lambda l, j: (l, 0, j)),
            pl.BlockSpec((None, 1, tn), lambda l, j: (l, 0, j)),
        ],
        out_specs=pl.BlockSpec((None, MOD_ROWS, tn), lambda l, j: (l, 0, j)),
        out_shape=jax.ShapeDtypeStruct((DEPTH, MOD_ROWS, N_ADA), F32),
        compiler_params=_cparams(("arbitrary", "arbitrary")),
        name="ada_modulation",
    )(cvec, w_ada, b_ada.reshape(DEPTH, 1, N_ADA))


def _mod_spec(layer, chunk, row_of_block):
    return pl.BlockSpec((None, 1, D_MODEL),
                        lambda i, *_: (layer * MOD_ROWS + row_of_block(i), 0, chunk))


def _modulate(x, g, shift, scale):
    ms = jnp.mean(x * x, axis=-1, keepdims=True)
    return x * lax.rsqrt(ms + 1e-6) * (g * (1.0 + scale)) + shift


def _pre_kernel(x_ref, sh_ref, sc_ref, g_ref, w_ref, h_ref, u_ref, wbf_ref):
    @pl.when(pl.program_id(0) == 0)
    def _():
        wbf_ref[...] = w_ref[...].astype(BF16)

    h = _modulate(x_ref[...], g_ref[...], sh_ref[...], sc_ref[...]).astype(BF16)
    h_ref[...] = h
    y = jnp.dot(h, wbf_ref[...], preferred_element_type=F32)
    u_ref[...] = y[:, :C_CONV] * _sigmoid(y[:, C_CONV:])


def _pre_project(x, mod8, norm_g, w_in, layer, row_of_block, name):
    t = x.shape[0]
    return pl.pallas_call(
        _pre_kernel,
        grid=(t // PRE_TM,),
        in_specs=[
            pl.BlockSpec((PRE_TM, D_MODEL), lambda i: (i, 0)),
            _mod_spec(layer, 0, row_of_block),
            _mod_spec(layer, 1, row_of_block),
            pl.BlockSpec((None, 1, D_MODEL), lambda i: (layer, 0, 0)),
            pl.BlockSpec((None, D_MODEL, 2 * C_CONV), lambda i: (layer, 0, 0)),
        ],
        out_specs=[pl.BlockSpec((PRE_TM, D_MODEL), lambda i: (i, 0)),
                   pl.BlockSpec((PRE_TM, C_CONV), lambda i: (i, 0))],
        out_shape=[jax.ShapeDtypeStruct((t, D_MODEL), BF16), jax.ShapeDtypeStruct((t, C_CONV), F32)],
        scratch_shapes=[pltpu.VMEM((D_MODEL, 2 * C_CONV), BF16)],
        compiler_params=_cparams(("arbitrary",)),
        name=name,
    )(x, mod8, mod8, norm_g, w_in)


SEG_DQ, SEG_DK, SEG_DV, SEG_NQ, SEG_NK, SEG_NV = range(6)
SEG_W = N_HEADS * HEAD_DIM
HEADS_WBLK = 256
HEADS_RC = 256


def _heads_kernel(h_ref, w0_ref, w1_ref, w2_ref, gain_ref, *rest, rope, per_batch):
    if rope:
        cos_ref, sa_ref, sb_ref, o_ref = rest
    else:
        (o_ref,) = rest
    j = pl.program_id(0)

    def finish(seg):
        n_g = HEADS_WBLK // LANES
        for c, w_ref in enumerate((w0_ref, w1_ref, w2_ref)):
            w = w_ref[...].astype(BF16)
            gain = gain_ref[:, c * HEADS_WBLK:(c + 1) * HEADS_WBLK]
            for r0 in range(0, HEADS_TM, HEADS_RC):
                y = jnp.dot(h_ref[r0:r0 + HEADS_RC, :], w, preferred_element_type=F32)
                for g in range(n_g):
                    hd = c * n_g + g
                    yg = y[:, g * LANES:(g + 1) * LANES]
                    if seg in (SEG_NQ, SEG_NK):
                        ms = jnp.mean(yg * yg, axis=-1, keepdims=True)
                        yg = yg * lax.rsqrt(ms + 1e-6) * gain[:, g * LANES:(g + 1) * LANES]
                    elif seg in (SEG_DQ, SEG_DK):
                        sq = yg * yg
                        low = lax.broadcasted_iota(I32, sq.shape, 1) < DQK
                        s_all = jnp.sum(sq, axis=-1, keepdims=True)
                        s_lo = jnp.sum(jnp.where(low, sq, 0.0), axis=-1, keepdims=True)
                        ms = jnp.where(low, s_lo, s_all - s_lo) * (1.0 / DQK)
                        yg = yg * lax.rsqrt(ms + 1e-6) * gain[:, g * LANES:(g + 1) * LANES]
                    if rope and seg in (SEG_DQ, SEG_DK):
                        rs = slice(r0, r0 + HEADS_RC)
                        yg = (yg * cos_ref[rs, :] + pltpu.roll(yg, LANES - 16, 1) * sa_ref[rs, :]
                              + pltpu.roll(yg, 16, 1) * sb_ref[rs, :])
                    yg = yg.astype(o_ref.dtype)
                    if per_batch == 1:
                        o_ref[hd, r0:r0 + HEADS_RC, :] = yg
                    else:
                        rows = HEADS_TM // per_batch
                        for q in range(HEADS_RC // rows):
                            o_ref[r0 // rows + q, hd] = yg[q * rows:(q + 1) * rows]

    for seg in range(6):
        pl.when(j == seg)(functools.partial(finish, seg))


def _head_project(h, w_in, gains, rope_tabs, layer, batch, seq, name):
    t = h.shape[0]
    rope = rope_tabs is not None
    per_batch = HEADS_TM // seq
    col0 = 2 * C_CONV // HEADS_WBLK
    wspec = lambda c: pl.BlockSpec((None, D_MODEL, HEADS_WBLK),
                                   lambda j, i: (layer, 0, col0 + (SEG_W // HEADS_WBLK) * j + c))
    in_specs = [
        pl.BlockSpec((HEADS_TM, D_MODEL), lambda j, i: (i, 0)),
        wspec(0), wspec(1), wspec(2),
        pl.BlockSpec((None, None, 1, SEG_W), lambda j, i: (layer, j, 0, 0)),
    ]
    args = [h, w_in, w_in, w_in, gains]
    if rope:
        in_specs += [pl.BlockSpec((DEC_SEQ, LANES), lambda j, i: (0, 0))] * 3
        args += list(rope_tabs)
    if per_batch == 1:
        out_spec = pl.BlockSpec((None, None, N_HEADS, seq, HEAD_DIM), lambda j, i: (j, i, 0, 0, 0))
    else:
        out_spec = pl.BlockSpec((None, per_batch, N_HEADS, seq, HEAD_DIM), lambda j, i: (j, i, 0, 0, 0))
    return pl.pallas_call(
        functools.partial(_heads_kernel, rope=rope, per_batch=per_batch),
        grid=(6, t // HEADS_TM),
        in_specs=in_specs,
        out_specs=out_spec,
        out_shape=jax.ShapeDtypeStruct((6, batch, N_HEADS, seq, HEAD_DIM), BF16 if rope else F32),
        compiler_params=_cparams(("arbitrary", "arbitrary")),
        name=name,
    )(*args)


CONV_CH = 128
CONV_HALO = 16


def _conv_window(win_ref, y_ref, w_ref, b_ref, lg_ref, lb_ref):
    off = CONV_HALO - CONV_K // 2
    sub = 8
    win_rows = CONV_CH + 2 * CONV_HALO
    for g in range(C_CONV // LANES):
        sl = slice(g * LANES, (g + 1) * LANES)
        window = win_ref[:, sl]
        acc = jnp.zeros((CONV_CH, LANES), F32) + b_ref[:, sl]
        for phase in range(sub):
            taps = [k for k in range(CONV_K) if (off + k) % sub == phase]
            if not taps:
                continue
            shifted = window if phase == 0 else pltpu.roll(window, win_rows - phase, 0)
            for k in taps:
                a = (off + k) // sub * sub
                acc = acc + shifted[a:a + CONV_CH, :] * w_ref[k:k + 1, sl]
        y_ref[:, sl] = acc

    y = y_ref[...]
    mu = jnp.mean(y, axis=-1, keepdims=True)
    yc = y - mu
    var = jnp.mean(yc * yc, axis=-1, keepdims=True)
    z = yc * lax.rsqrt(var + 1e-5) * lg_ref[...] + lb_ref[...]
    return _silu(z)


def _conv_block(i, u_ref, ulo_ref, uhi_ref, conv_refs, win_ref, y_ref, a_ref, seq):
    rows = u_ref.shape[0]
    for c in range(rows // CONV_CH):
        r0 = c * CONV_CH
        first = i * rows + r0
        lo_keep = ((first % seq) != 0).astype(F32)
        hi_keep = (((first + CONV_CH) % seq) != 0).astype(F32)
        lo = u_ref[r0 - CONV_HALO:r0, :] if c > 0 else ulo_ref[...]
        hi = u_ref[r0 + CONV_CH:r0 + CONV_CH + CONV_HALO, :] if r0 + CONV_CH < rows else uhi_ref[...]
        win_ref[0:CONV_HALO, :] = lo * lo_keep
        win_ref[CONV_HALO:CONV_HALO + CONV_CH, :] = u_ref[r0:r0 + CONV_CH, :]
        win_ref[CONV_HALO + CONV_CH:, :] = hi * hi_keep
        a_ref[r0:r0 + CONV_CH, :] = _conv_window(win_ref, y_ref, *conv_refs).astype(a_ref.dtype)


_NT = (((1,), (1,)), ((), ()))


LOG2E = 1.4426950408889634


def _softmax_rows(s):
    m = jnp.max(s, axis=-1, keepdims=True)
    e = jnp.exp(s - m)
    return e / jnp.sum(e, axis=-1, keepdims=True)


def _exp2_rows(s2):
    e = jnp.exp2(s2 - jnp.max(s2, axis=-1, keepdims=True))
    return e, 1.0 / jnp.sum(e, axis=-1, keepdims=True)


def _diff_lambda(lam_ref, lam_init):
    lv = lam_ref[...]
    return (jnp.exp(jnp.sum(lv[0:1] * lv[1:2], axis=-1, keepdims=True))
            - jnp.exp(jnp.sum(lv[2:3] * lv[3:4], axis=-1, keepdims=True)) + lam_init)


def _diff_head(q, k, v, lam, g, lam_init, long_keys):
    lane = lax.broadcasted_iota(I32, q.shape, 1)
    if long_keys:
        qs = q * (DQK ** -0.5 * LOG2E)
        q0 = jnp.where(lane < DQK, qs, 0.0).astype(BF16)
        q1 = jnp.where(lane >= DQK, qs, 0.0).astype(BF16)
        e0, r0 = _exp2_rows(lax.dot_general(q0, k, _NT, preferred_element_type=F32))
        e1, r1 = _exp2_rows(lax.dot_general(q1, k, _NT, preferred_element_type=F32))
        o = (jnp.dot(e0.astype(BF16), v, preferred_element_type=F32) * r0
             - jnp.dot(e1.astype(BF16), v, preferred_element_type=F32) * (lam * r1))
    else:
        q0 = jnp.where(lane < DQK, q, 0.0).astype(BF16)
        q1 = jnp.where(lane >= DQK, q, 0.0).astype(BF16)
        scale = DQK ** -0.5
        p0 = _softmax_rows(lax.dot_general(q0, k, _NT, preferred_element_type=F32) * scale)
        p1 = _softmax_rows(lax.dot_general(q1, k, _NT, preferred_element_type=F32) * scale)
        o = jnp.dot((p0 - lam * p1).astype(BF16), v, preferred_element_type=F32)
    ms = jnp.mean(o * o, axis=-1, keepdims=True)
    return o * lax.rsqrt(ms + 1e-5) * g * (1.0 - lam_init)


def _seg_spec(seg, rows, index):
    def imap(*ids):
        b, h, r = index(*ids)
        return (seg, b, h, r, 0)
    return pl.BlockSpec((None, None, None, rows, HEAD_DIM), imap)


def _diff_attn_lat_kernel(lam_ref, g_ref, q_ref, k_ref, v_ref, ck_ref, cv_ref, o_ref, *, lam_init):
    k = jnp.concatenate([k_ref[...].astype(BF16), ck_ref[...].astype(BF16)], axis=0)
    v = jnp.concatenate([v_ref[...].astype(BF16), cv_ref[...].astype(BF16)], axis=0)
    lam = _diff_lambda(lam_ref, lam_init)
    o_ref[...] = _diff_head(q_ref[...].astype(F32), k, v, lam, g_ref[...], lam_init,
                            True).astype(o_ref.dtype)


def _diff_attention_lat(segs, lam_params, subln_g, cache_k, cache_v, layer, tq):
    nq = DEC_SEQ // tq
    cspec = pl.BlockSpec((None, None, None, PAST_LEN, HEAD_DIM), lambda b, h, qi: (b, layer, h, 0, 0))
    return pl.pallas_call(
        functools.partial(_diff_attn_lat_kernel, lam_init=_lambda_init(layer)),
        grid=(DEC_BATCH, N_HEADS, nq),
        in_specs=[
            pl.BlockSpec((None, 4, DQK), lambda b, h, qi: (layer, 0, 0)),
            pl.BlockSpec((None, 1, HEAD_DIM), lambda b, h, qi: (layer, 0, 0)),
            _seg_spec(SEG_DQ, tq, lambda b, h, qi: (b, h, qi)),
            _seg_spec(SEG_DK, DEC_SEQ, lambda b, h, qi: (b, h, 0)),
            _seg_spec(SEG_DV, DEC_SEQ, lambda b, h, qi: (b, h, 0)),
            cspec, cspec,
        ],
        out_specs=pl.BlockSpec((tq, HEAD_DIM), lambda b, h, qi: (b * nq + qi, h)),
        out_shape=jax.ShapeDtypeStruct((T_LAT, SEG_W), BF16),
        compiler_params=_cparams(("arbitrary", "arbitrary", "arbitrary")),
        name="diff_attn_lat",
    )(lam_params, subln_g.reshape(DEPTH, 1, HEAD_DIM), segs, segs, segs, cache_k, cache_v)


CACHE_SEGS = (SEG_DK, SEG_DV, SEG_NK, SEG_NV)


def _ctx_attn_kernel(lam_ref, g_ref, *refs, lam_init, n_prev):
    seg_refs, refs = refs[:6], refs[6:]
    dq_ref, dk_ref, dv_ref, nq_ref, nk_ref, nv_ref = seg_refs
    prev_refs, refs = refs[:n_prev * len(CACHE_SEGS)], refs[n_prev * len(CACHE_SEGS):]
    d_ref, n_ref = refs[:2]
    cache_refs = refs[2:]
    lam = _diff_lambda(lam_ref, lam_init)
    for h in range(N_HEADS):
        sl = slice(h * HEAD_DIM, (h + 1) * HEAD_DIM)
        d_ref[:, sl] = _diff_head(dq_ref[h], dk_ref[h].astype(BF16), dv_ref[h].astype(BF16), lam,
                                  g_ref[...], lam_init, False).astype(d_ref.dtype)
        s = lax.dot_general(nq_ref[h].astype(BF16), nk_ref[h].astype(BF16), _NT,
                            preferred_element_type=F32) * (HEAD_DIM ** -0.5)
        n_ref[:, sl] = jnp.dot(_softmax_rows(s).astype(BF16), nv_ref[h].astype(BF16),
                               preferred_element_type=F32).astype(n_ref.dtype)
    for n, (seg, c_ref) in enumerate(zip(CACHE_SEGS, cache_refs)):
        for layer in range(n_prev):
            c_ref[layer] = prev_refs[layer * len(CACHE_SEGS) + n][...]
        c_ref[n_prev] = seg_refs[seg][...]


def _ctx_attention(segs, prev_segs, lam_params, subln_g, layer):
    seg = lambda s: pl.BlockSpec((None, None, N_HEADS, SEQ, HEAD_DIM), lambda b: (s, b, 0, 0, 0))
    out = pl.BlockSpec((SEQ, SEG_W), lambda b: (b, 0))
    in_specs = [pl.BlockSpec((None, 4, DQK), lambda b: (layer, 0, 0)),
                pl.BlockSpec((None, 1, HEAD_DIM), lambda b: (layer, 0, 0))] + [seg(s) for s in range(6)]
    args = [lam_params, subln_g.reshape(DEPTH, 1, HEAD_DIM)] + [segs] * 6
    out_specs = [out, out]
    out_shape = [jax.ShapeDtypeStruct((T_CTX, SEG_W), BF16)] * 2
    for prev in prev_segs:
        in_specs += [seg(s) for s in CACHE_SEGS]
        args += [prev] * len(CACHE_SEGS)
    if prev_segs:
        assert len(prev_segs) == DEPTH - 1
        cache = pl.BlockSpec((None, DEPTH, N_HEADS, SEQ, HEAD_DIM), lambda b: (b, 0, 0, 0, 0))
        out_specs += [cache] * len(CACHE_SEGS)
        out_shape += [jax.ShapeDtypeStruct((BATCH, DEPTH, N_HEADS, SEQ, HEAD_DIM), F32)] * len(CACHE_SEGS)
    return pl.pallas_call(
        functools.partial(_ctx_attn_kernel, lam_init=_lambda_init(layer), n_prev=len(prev_segs)),
        grid=(BATCH,),
        in_specs=in_specs,
        out_specs=out_specs,
        out_shape=out_shape,
        compiler_params=_cparams(("arbitrary",)),
        name="ctx_attn_caches" if prev_segs else "ctx_attn",
    )(*args)


NA_PAIRS = 2 * NA_KR - 2
NA_QROWS = 4
NA_KROWS = 12


def _na_key_block(chunk):
    first = min(max(chunk * NA_QROWS - NA_KR // 2, 0), GRID_ROWS - NA_KR)
    return min(first - first % 2, GRID_ROWS - NA_KROWS)


def _na_attn_kernel(src_ref, mask_ref, q_ref, k_ref, v_ref, ck_ref, cv_ref, o_ref, bias_ref):
    for d in range(NA_PAIRS):
        row = jnp.broadcast_to(src_ref[d:d + 1, :] * LOG2E, (GRID_W, LANES))
        tile = pltpu.roll(row, LANES - (NA_KC - 1), 1, stride=1, stride_axis=0)
        for v in range(3):
            bias_ref[v, d] = tile + mask_ref[v]
    dead = jnp.full((GRID_W, LANES), NEG_BIG, F32)
    ck = ck_ref[...].astype(BF16)
    cv = cv_ref[...].astype(BF16)
    scale = HEAD_DIM ** -0.5 * LOG2E
    for chunk in range(GRID_ROWS // NA_QROWS):
        kb = _na_key_block(chunk)
        rows = []
        for qr in range(chunk * NA_QROWS, (chunk + 1) * NA_QROWS):
            start = min(max(qr - NA_KR // 2, 0), GRID_ROWS - NA_KR)
            assert kb <= start and start + NA_KR <= kb + NA_KROWS
            tiles = []
            for m in range(NA_KROWS // 2):
                r0 = kb + 2 * m
                live0 = start <= r0 < start + NA_KR
                live1 = start <= r0 + 1 < start + NA_KR
                d = r0 - qr + NA_KR - 1
                assert not (live0 or live1) or 0 <= d < NA_PAIRS
                if live0 and live1:
                    tiles.append(bias_ref[0, d])
                elif live0:
                    tiles.append(bias_ref[1, d])
                elif live1:
                    tiles.append(bias_ref[2, d])
                else:
                    tiles.append(dead)
            rows.append(jnp.concatenate(tiles, axis=1))
        bias = jnp.concatenate(rows, axis=0)
        qs = slice(chunk * NA_QROWS * GRID_W, (chunk + 1) * NA_QROWS * GRID_W)
        ks = slice(kb * GRID_W, (kb + NA_KROWS) * GRID_W)
        q = (q_ref[qs, :].astype(F32) * scale).astype(BF16)
        kl = k_ref[ks, :].astype(BF16)
        vl = v_ref[ks, :].astype(BF16)
        s_loc = lax.dot_general(q, kl, _NT, preferred_element_type=F32) + bias
        s_ctx = lax.dot_general(q, ck, _NT, preferred_element_type=F32)
        m = jnp.maximum(jnp.max(s_loc, axis=-1, keepdims=True), jnp.max(s_ctx, axis=-1, keepdims=True))
        e_loc = jnp.exp2(s_loc - m)
        e_ctx = jnp.exp2(s_ctx - m)
        r = 1.0 / (jnp.sum(e_loc, axis=-1, keepdims=True) + jnp.sum(e_ctx, axis=-1, keepdims=True))
        o_ref[qs, :] = ((jnp.dot(e_loc.astype(BF16), vl, preferred_element_type=F32)
                         + jnp.dot(e_ctx.astype(BF16), cv, preferred_element_type=F32)) * r
                        ).astype(o_ref.dtype)


def _na_attention(segs, rpb_src, na_mask, cache_k, cache_v, layer):
    blk = lambda s: _seg_spec(s, DEC_SEQ, lambda b, h: (b, h, 0))
    cspec = pl.BlockSpec((None, None, None, PAST_LEN, HEAD_DIM), lambda b, h: (b, layer, h, 0, 0))
    return pl.pallas_call(
        _na_attn_kernel,
        grid=(DEC_BATCH, N_HEADS),
        in_specs=[
            pl.BlockSpec((None, None, 16, LANES), lambda b, h: (layer, h, 0, 0)),
            pl.BlockSpec((3, GRID_W, LANES), lambda b, h: (0, 0, 0)),
            blk(SEG_NQ), blk(SEG_NK), blk(SEG_NV), cspec, cspec,
        ],
        out_specs=pl.BlockSpec((DEC_SEQ, HEAD_DIM), lambda b, h: (b, h)),
        out_shape=jax.ShapeDtypeStruct((T_LAT, SEG_W), BF16),
        scratch_shapes=[pltpu.VMEM((3, NA_PAIRS, GRID_W, LANES), F32)],
        compiler_params=_cparams(("arbitrary", "arbitrary")),
        name="na_attn_lat",
    )(rpb_src, na_mask, segs, segs, segs, cache_k, cache_v)


def _split_f32(x, n):
    terms = []
    for _ in range(n):
        t = x.astype(BF16).astype(F32)
        terms.append(t)
        x = x - t
    return terms


def _mix_kernel(u_ref, ulo_ref, uhi_ref, cw_ref, cb_ref, clg_ref, clb_ref, d_ref, n_ref, w_ref, x_ref,
                g1_ref, sh_ref, sc_ref, ng_ref, wr_ref, br_ref, x1_ref, lt_ref, win_ref, y_ref, a_ref, *, seq):
    k0, k1 = C_CONV, C_CONV + N_HEADS * HEAD_DIM
    _conv_block(pl.program_id(0), u_ref, ulo_ref, uhi_ref, (cw_ref, cb_ref, clg_ref, clb_ref),
                win_ref, y_ref, a_ref, seq)
    mixed = (jnp.dot(d_ref[...].astype(BF16), w_ref[k0:k1, :], preferred_element_type=F32)
             + jnp.dot(n_ref[...].astype(BF16), w_ref[k1:, :], preferred_element_type=F32)
             + jnp.dot(a_ref[...], w_ref[0:k0, :], preferred_element_type=F32))
    x1 = x_ref[...] + g1_ref[...] * mixed
    x1_ref[...] = x1
    h2 = _modulate(x1, ng_ref[...], sh_ref[...], sc_ref[...])
    wh, wm, wl = _split_f32(wr_ref[...], 3)
    w_cat = (wh + pltpu.roll(wm, ROUTE_ROWS, 1) + pltpu.roll(wl, 2 * ROUTE_ROWS, 1)).astype(BF16)
    hh, hm = _split_f32(h2, 2)
    r = jnp.dot(jnp.concatenate([hh, hm], axis=0).astype(BF16), w_cat, preferred_element_type=F32)
    rh, rm = r[:MIX_TM], r[MIX_TM:]
    back = lambda v, groups: pltpu.roll(v, LANES - groups * ROUTE_ROWS, 1)
    small = (back(rm, 1) + back(rh, 2)) + (back(rh, 1) + rm)
    logits = (small + rh).T[:ROUTE_ROWS, :]
    lt_ref[...] = logits + br_ref[...]


def _mix(u, conv_params, d_out, n_out, w_out_bf, x, mod8, norm_g, wr_t, br, layer, row_of_block, seq):
    t = x.shape[0]
    rows = lambda width: pl.BlockSpec((MIX_TM, width), lambda i: (i, 0))
    halo_blocks = MIX_TM // CONV_HALO
    cvec = lambda: pl.BlockSpec((None, 1, C_CONV), lambda i: (layer, 0, 0))
    conv_w, conv_b, ln_g, ln_b = conv_params
    return pl.pallas_call(
        functools.partial(_mix_kernel, seq=seq),
        grid=(t // MIX_TM,),
        in_specs=[
            rows(C_CONV),
            pl.BlockSpec((CONV_HALO, C_CONV), lambda i: (jnp.maximum(i * halo_blocks - 1, 0), 0)),
            pl.BlockSpec((CONV_HALO, C_CONV),
                         lambda i: (jnp.minimum((i + 1) * halo_blocks, t // CONV_HALO - 1), 0)),
            pl.BlockSpec((None, CONV_K, C_CONV), lambda i: (layer, 0, 0)),
            cvec(), cvec(), cvec(),
            rows(N_HEADS * HEAD_DIM), rows(N_HEADS * HEAD_DIM),
            pl.BlockSpec((None, D_MODEL, D_MODEL), lambda i: (layer, 0, 0)),
            rows(D_MODEL),
            _mod_spec(layer, 2, row_of_block),
            _mod_spec(layer, 3, row_of_block),
            _mod_spec(layer, 4, row_of_block),
            pl.BlockSpec((None, 1, D_MODEL), lambda i: (layer, 0, 0)),
            pl.BlockSpec((None, D_MODEL, LANES), lambda i: (layer, 0, 0)),
            pl.BlockSpec((None, ROUTE_ROWS, 1), lambda i: (layer, 0, 0)),
        ],
        out_specs=[rows(D_MODEL), pl.BlockSpec((ROUTE_ROWS, MIX_TM), lambda i: (0, i))],
        out_shape=[jax.ShapeDtypeStruct((t, D_MODEL), F32),
                   jax.ShapeDtypeStruct((ROUTE_ROWS, t), F32)],
        scratch_shapes=[pltpu.VMEM((CONV_CH + 2 * CONV_HALO, C_CONV), F32),
                        pltpu.VMEM((CONV_CH, C_CONV), F32),
                        pltpu.VMEM((MIX_TM, C_CONV), BF16)],
        compiler_params=_cparams(("arbitrary",)),
        name="mix_%d" % t,
    )(u, u, u, conv_w, conv_b.reshape(DEPTH, 1, C_CONV), ln_g.reshape(DEPTH, 1, C_CONV),
      ln_b.reshape(DEPTH, 1, C_CONV), d_out, n_out, w_out_bf, x, mod8, mod8, mod8, norm_g, wr_t, br)


def _cast_kernel(x_ref, o_ref):
    o_ref[...] = x_ref[...].astype(BF16)


def _cast_bf16(w):
    tm = 512
    spec = pl.BlockSpec((None, tm, D_MODEL), lambda l, i: (l, i, 0))
    return pl.pallas_call(
        _cast_kernel, grid=(DEPTH, D_MODEL // tm), in_specs=[spec], out_specs=spec,
        out_shape=jax.ShapeDtypeStruct(w.shape, BF16),
        compiler_params=_cparams(("arbitrary", "arbitrary")), name="cast_w_out",
    )(w)


def _route_kernel(lt_ref, tri_ref, pos_ref, wt_ref, item_ref):
    t = T_ALL
    lg = lt_ref[0:N_GROUPS, :]
    eg = jnp.exp(lg - jnp.max(lg, axis=0, keepdims=True))
    pg = eg / jnp.sum(eg, axis=0, keepdims=True)
    pg_top = jnp.max(pg, axis=0, keepdims=True)
    gi = lax.broadcasted_iota(I32, pg.shape, 0).astype(F32)
    g_idx = jnp.min(jnp.where(pg == pg_top, gi, float(N_GROUPS)), axis=0, keepdims=True)

    le = jnp.zeros((E_PER_GROUP, t), F32)
    for g in range(N_GROUPS):
        rows = lt_ref[N_GROUPS + g * E_PER_GROUP:N_GROUPS + (g + 1) * E_PER_GROUP, :]
        le = jnp.where(g_idx == float(g), rows, le)
    ee = jnp.exp(le - jnp.max(le, axis=0, keepdims=True))
    pe = ee / jnp.sum(ee, axis=0, keepdims=True)
    ei = lax.broadcasted_iota(I32, pe.shape, 0).astype(F32)
    p1 = jnp.max(pe, axis=0, keepdims=True)
    i1 = jnp.min(jnp.where(pe == p1, ei, float(E_PER_GROUP)), axis=0, keepdims=True)
    pe_rest = jnp.where(ei == i1, -1.0, pe)
    p2 = jnp.max(pe_rest, axis=0, keepdims=True)
    i2 = jnp.min(jnp.where(pe_rest == p2, ei, float(E_PER_GROUP)), axis=0, keepdims=True)
    den = p1 + p2
    wt_ref[0:1, :] = pg_top * (p1 / den)
    wt_ref[1:2, :] = pg_top * (p2 / den)
    e1 = g_idx * E_PER_GROUP + i1
    e2 = g_idx * E_PER_GROUP + i2

    erow = lax.broadcasted_iota(I32, (N_EXPERTS, t), 0).astype(F32)
    oh1 = (erow == e1).astype(F32)
    oh2 = (erow == e2).astype(F32)
    cnt = oh1 + oh2
    carry = jnp.zeros((N_EXPERTS, 1), F32)
    ranks = []
    for b in range(t // CUM_BLK):
        blk = cnt[:, b * CUM_BLK:(b + 1) * CUM_BLK]
        ranks.append(jnp.dot(blk.astype(BF16), tri_ref[...], preferred_element_type=F32) + carry)
        carry = carry + jnp.sum(blk, axis=1, keepdims=True)
    rank = jnp.concatenate(ranks, axis=1)

    erow_l = lax.broadcasted_iota(I32, (N_EXPERTS, LANES), 0)

    def excl_scan(v):
        inc = v
        for s in (1, 2, 4, 8):
            inc = inc + jnp.where(erow_l >= s, pltpu.roll(inc, s, 0), 0.0)
        return inc - v

    total = jnp.broadcast_to(carry, (N_EXPERTS, LANES))
    offs = excl_scan(total)
    pos_ref[0:1, :] = jnp.sum(oh1 * (rank + offs[:, 0:1]), axis=0, keepdims=True).astype(I32)
    pos_ref[1:2, :] = jnp.sum(oh2 * (rank + offs[:, 0:1]), axis=0, keepdims=True).astype(I32)

    shift = int(math.log2(MOE_TM))
    offs_i = offs.astype(I32)
    total_i = total.astype(I32)
    first_tile = lax.shift_right_logical(offs_i, shift)
    last_tile = lax.shift_right_logical(offs_i + total_i - 1, shift)
    n_item = jnp.where(total_i > 0, last_tile - first_tile + 1, 0).astype(F32)
    item_start = excl_scan(n_item)
    item_end = item_start + n_item
    kk = lax.broadcasted_iota(I32, (N_EXPERTS, LANES), 1).astype(F32)
    item_e = jnp.minimum(jnp.sum((item_end <= kk).astype(F32), axis=0, keepdims=True),
                         float(N_EXPERTS - 1))
    sel = (erow_l.astype(F32) == item_e).astype(F32)
    pick = lambda v: jnp.sum(sel * v, axis=0, keepdims=True)
    item_ref[0:1, :] = (kk[0:1] + pick(first_tile.astype(F32) - item_start)).astype(I32)
    item_ref[1:2, :] = item_e.astype(I32)
    item_ref[2:3, :] = pick(offs).astype(I32)
    item_ref[3:4, :] = pick(offs + total).astype(I32)
    item_ref[4:5, :] = item_end[N_EXPERTS - 1:N_EXPERTS, :].astype(I32)
    item_ref[5:8, :] = jnp.zeros((3, LANES), I32)


def _route(lt_all, tri):
    full = lambda shape: pl.BlockSpec(shape, lambda i: (0,) * len(shape))
    return pl.pallas_call(
        _route_kernel,
        grid=(1,),
        in_specs=[full((ROUTE_ROWS, T_ALL)), full((CUM_BLK, CUM_BLK))],
        out_specs=[full((2, T_ALL)), full((2, T_ALL)), full((8, LANES))],
        out_shape=[jax.ShapeDtypeStruct((2, T_ALL), I32), jax.ShapeDtypeStruct((2, T_ALL), F32),
                   jax.ShapeDtypeStruct((8, LANES), I32)],
        compiler_params=_cparams(("arbitrary",)),
        name="moe_route",
    )(lt_all, tri)


N_TOK_TILES = T_ALL // MOE_TM
CTX_TILES = T_CTX // MOE_TM


def _tok_row(i):
    return jnp.where(i < CTX_TILES, 0, 1 + (i - CTX_TILES) // (DEC_SEQ // MOE_TM))


N_CHUNK = D_MODEL // LANES
MOE_PITCH = N_CHUNK + 1
SLAB_ROWS = MOE_TM * MOE_PITCH
ISSUE_UNROLL = 4


def _slab_copy(src, src_tok, dst, dst_tok, sem, pitch):
    return pltpu.make_async_copy(src.at[pl.ds(src_tok * pitch, pitch)],
                                 dst.at[pl.ds(dst_tok * pitch, pitch)], sem)


def _to_slabs(ref, value):
    for c in range(N_CHUNK):
        ref[pl.ds(c, MOE_TM, stride=MOE_PITCH), :] = value[:, c * LANES:(c + 1) * LANES]
    ref[pl.ds(N_CHUNK, MOE_TM, stride=MOE_PITCH), :] = jnp.zeros((MOE_TM, LANES), F32)


def _slab_chunk(ref, c):
    return ref[pl.ds(c, MOE_TM, stride=MOE_PITCH), :]


def _dispatch_kernel(pos_ref, xc_ref, xl_ref, sh_ref, sc_ref, ng_ref, xs_ref, h_ref, sem):
    i = pl.program_id(0)
    slot = i % 2
    buf = h_ref.at[slot]

    def drain(s):
        def body(r, carry):
            _slab_copy(h_ref.at[s], 0, xs_ref, 0, sem.at[s], MOE_PITCH).wait()
            return carry
        lax.fori_loop(0, 2 * MOE_TM, body, 0, unroll=ISSUE_UNROLL)

    @pl.when(i >= 2)
    def _():
        drain(slot)

    @pl.when(i < CTX_TILES)
    def _():
        _to_slabs(buf, _modulate(xc_ref[...], ng_ref[...], sh_ref[...], sc_ref[...]))

    @pl.when(i >= CTX_TILES)
    def _():
        _to_slabs(buf, _modulate(xl_ref[...], ng_ref[...], sh_ref[...], sc_ref[...]))

    def issue(r, carry):
        tok = i * MOE_TM + r
        _slab_copy(buf, r, xs_ref, pos_ref[tok], sem.at[slot], MOE_PITCH).start(priority=0)
        _slab_copy(buf, r, xs_ref, pos_ref[T_ALL + tok], sem.at[slot], MOE_PITCH).start(priority=1)
        return carry

    lax.fori_loop(0, MOE_TM, issue, 0, unroll=ISSUE_UNROLL)

    @pl.when(i == N_TOK_TILES - 1)
    def _():
        drain(1 - slot)
        drain(slot)


def _dispatch(pos_flat, x1c, x1l, mod8, norm_g, layer):
    grid_spec = pltpu.PrefetchScalarGridSpec(
        num_scalar_prefetch=1,
        grid=(N_TOK_TILES,),
        in_specs=[
            pl.BlockSpec((MOE_TM, D_MODEL), lambda i, *_: (jnp.minimum(i, CTX_TILES - 1), 0)),
            pl.BlockSpec((MOE_TM, D_MODEL), lambda i, *_: (jnp.maximum(i - CTX_TILES, 0), 0)),
            _mod_spec(layer, 3, _tok_row),
            _mod_spec(layer, 4, _tok_row),
            pl.BlockSpec((None, 1, D_MODEL), lambda i, *_: (layer, 0, 0)),
        ],
        out_specs=pl.BlockSpec(memory_space=pl.ANY),
        scratch_shapes=[pltpu.VMEM((2, SLAB_ROWS, LANES), F32), pltpu.SemaphoreType.DMA((2,))],
    )
    return pl.pallas_call(
        _dispatch_kernel,
        grid_spec=grid_spec,
        out_shape=jax.ShapeDtypeStruct((2 * T_ALL * MOE_PITCH, LANES), F32),
        compiler_params=_cparams(("arbitrary",)),
        name="moe_dispatch",
    )(pos_flat, x1c, x1l, mod8, mod8, norm_g)


N_ROW_TILES = 2 * T_ALL // MOE_TM
X_SLOTS = 3
Y_SLOTS = 2


def _expert_kernel(tile_ref, exp_ref, lo_ref, hi_ref, n_ref, xs_hbm, wg_hbm, wu_hbm, wd_hbm, ys_hbm,
                   x_buf, y_buf, wg_buf, wu_buf, wd_buf, wg_bf, wu_bf, wd_bf, slot_ref, sem, x_sem, y_sem,
                   *, layer):
    k = pl.program_id(0)
    n = n_ref[0]

    def weight_copies(e, s):
        return (pltpu.make_async_copy(wg_hbm.at[layer, e], wg_buf.at[s], sem.at[s, 0]),
                pltpu.make_async_copy(wu_hbm.at[layer, e], wu_buf.at[s], sem.at[s, 1]),
                pltpu.make_async_copy(wd_hbm.at[layer, e], wd_buf.at[s], sem.at[s, 2]))

    def x_copy(t):
        s = t % X_SLOTS
        return pltpu.make_async_copy(xs_hbm.at[pl.ds(t * SLAB_ROWS, SLAB_ROWS)], x_buf.at[s], x_sem.at[s])

    def y_copy(t):
        s = t % Y_SLOTS
        return pltpu.make_async_copy(y_buf.at[s], ys_hbm.at[pl.ds(t * SLAB_ROWS, SLAB_ROWS)], y_sem.at[s])

    @pl.when(k == 0)
    def _():
        slot_ref[0] = 0
        for cp in weight_copies(exp_ref[0], 0):
            cp.start()
        for t in range(X_SLOTS - 1):
            x_copy(t).start()

    @pl.when(k < n)
    def _():
        e = exp_ref[k]
        tile = tile_ref[k]
        prev_tile = tile_ref[jnp.maximum(k - 1, 0)]
        new_expert = jnp.logical_or(k == 0, exp_ref[jnp.maximum(k - 1, 0)] != e)

        @pl.when(jnp.logical_or(k == 0, tile != prev_tile))
        def _():
            x_copy(tile).wait()

            @pl.when(tile + (X_SLOTS - 1) < N_ROW_TILES)
            def _():
                x_copy(tile + (X_SLOTS - 1)).start()

            @pl.when(k > 0)
            def _():
                y_copy(prev_tile).start()

            @pl.when(tile >= Y_SLOTS)
            def _():
                y_copy(tile - Y_SLOTS).wait()

        xs_ref = x_buf.at[tile % X_SLOTS]
        ys_ref = y_buf.at[tile % Y_SLOTS]

        @pl.when(new_expert)
        def _():
            s = slot_ref[0]
            for cp in weight_copies(e, s):
                cp.wait()
            wg_bf[...] = wg_buf[s].astype(BF16)
            wu_bf[...] = wu_buf[s].astype(BF16)
            wd_bf[...] = wd_buf[s].astype(BF16)
            nxt = lax.while_loop(lambda j: jnp.logical_and(j < n, exp_ref[jnp.minimum(j, n - 1)] == e),
                                 lambda j: j + 1, k + 1)

            @pl.when(nxt < n)
            def _():
                for cp in weight_copies(exp_ref[jnp.minimum(nxt, n - 1)], 1 - s):
                    cp.start()
            slot_ref[0] = 1 - s

        x = jnp.concatenate([_slab_chunk(xs_ref, c) for c in range(N_CHUNK)], axis=1).astype(BF16)
        gate = jnp.dot(x, wg_bf[...], preferred_element_type=F32)
        up = jnp.dot(x, wu_bf[...], preferred_element_type=F32)
        hid = (_silu(gate) * up).astype(BF16)
        y = jnp.dot(hid, wd_bf[...], preferred_element_type=F32)
        row0 = tile * MOE_TM
        row = row0 + lax.broadcasted_iota(I32, (MOE_TM, 1), 0)
        mine = jnp.logical_and(row >= lo_ref[k], row < hi_ref[k])
        first = lo_ref[k] <= row0

        @pl.when(first)
        def _():
            _to_slabs(ys_ref, jnp.where(mine, y, 0.0))

        @pl.when(jnp.logical_not(first))
        def _():
            for c in range(N_CHUNK):
                old = _slab_chunk(ys_ref, c)
                ys_ref[pl.ds(c, MOE_TM, stride=MOE_PITCH), :] = jnp.where(
                    mine, y[:, c * LANES:(c + 1) * LANES], old)

        @pl.when(k == n - 1)
        def _():
            y_copy(tile).start()
            y_copy(tile).wait()

            @pl.when(tile >= 1)
            def _():
                y_copy(tile - 1).wait()


def _experts(items, xs, w_gate, w_up, w_down, layer):
    item_tile, item_exp, item_lo, item_hi, n_items = items
    hbm = pl.BlockSpec(memory_space=pl.ANY)
    grid_spec = pltpu.PrefetchScalarGridSpec(
        num_scalar_prefetch=5, grid=(MOE_ITEMS,),
        in_specs=[hbm, hbm, hbm, hbm], out_specs=hbm,
        scratch_shapes=[
            pltpu.VMEM((X_SLOTS, SLAB_ROWS, LANES), F32), pltpu.VMEM((Y_SLOTS, SLAB_ROWS, LANES), F32),
            pltpu.VMEM((2, D_MODEL, D_FF), F32), pltpu.VMEM((2, D_MODEL, D_FF), F32),
            pltpu.VMEM((2, D_FF, D_MODEL), F32),
            pltpu.VMEM((D_MODEL, D_FF), BF16), pltpu.VMEM((D_MODEL, D_FF), BF16),
            pltpu.VMEM((D_FF, D_MODEL), BF16),
            pltpu.SMEM((1,), I32), pltpu.SemaphoreType.DMA((2, 3)),
            pltpu.SemaphoreType.DMA((X_SLOTS,)), pltpu.SemaphoreType.DMA((Y_SLOTS,)),
        ])
    return pl.pallas_call(
        functools.partial(_expert_kernel, layer=layer),
        grid_spec=grid_spec,
        out_shape=jax.ShapeDtypeStruct((2 * T_ALL * MOE_PITCH, LANES), F32),
        compiler_params=_cparams(("arbitrary",)),
        name="moe_experts",
    )(item_tile, item_exp, item_lo, item_hi, n_items, xs, w_gate, w_up, w_down)


def _combine_kernel(pos_ref, xc_ref, xl_ref, wt_ref, g2_ref, ys_ref, oc_ref, ol_ref, y_ref, sem):
    i = pl.program_id(0)
    slot = i % 2

    def gather(tile, s):
        def body(r, carry):
            tok = tile * MOE_TM + r
            _slab_copy(ys_ref, pos_ref[tok], y_ref.at[s, 0], r, sem.at[s], MOE_PITCH).start(priority=0)
            _slab_copy(ys_ref, pos_ref[T_ALL + tok], y_ref.at[s, 1], r, sem.at[s], MOE_PITCH).start(priority=1)
            return carry
        lax.fori_loop(0, MOE_TM, body, 0, unroll=ISSUE_UNROLL)

    @pl.when(i == 0)
    def _():
        gather(0, 0)

    @pl.when(i + 1 < N_TOK_TILES)
    def _():
        gather(i + 1, 1 - slot)

    def drain(r, carry):
        _slab_copy(ys_ref, 0, y_ref.at[slot, 0], 0, sem.at[slot], MOE_PITCH).wait()
        return carry

    lax.fori_loop(0, 2 * MOE_TM, drain, 0, unroll=ISSUE_UNROLL)

    def write(x_ref, o_ref):
        w0 = wt_ref[:, 0:1]
        w1 = wt_ref[:, 1:2]
        for c in range(N_CHUNK):
            sl = slice(c * LANES, (c + 1) * LANES)
            moe = w0 * _slab_chunk(y_ref.at[slot, 0], c) + w1 * _slab_chunk(y_ref.at[slot, 1], c)
            o_ref[:, sl] = x_ref[:, sl] + g2_ref[:, sl] * moe

    pl.when(i < CTX_TILES)(lambda: write(xc_ref, oc_ref))
    pl.when(i >= CTX_TILES)(lambda: write(xl_ref, ol_ref))


def _combine(pos_flat, x1c, x1l, wts_t, mod8, ys, layer):
    cspec = pl.BlockSpec((MOE_TM, D_MODEL), lambda i, *_: (jnp.minimum(i, CTX_TILES - 1), 0))
    lspec = pl.BlockSpec((MOE_TM, D_MODEL), lambda i, *_: (jnp.maximum(i - CTX_TILES, 0), 0))
    grid_spec = pltpu.PrefetchScalarGridSpec(
        num_scalar_prefetch=1,
        grid=(N_TOK_TILES,),
        in_specs=[
            cspec, lspec,
            pl.BlockSpec((MOE_TM, 2), lambda i, *_: (i, 0)),
            _mod_spec(layer, 5, _tok_row),
            pl.BlockSpec(memory_space=pl.ANY),
        ],
        out_specs=[cspec, lspec],
        scratch_shapes=[pltpu.VMEM((2, 2, SLAB_ROWS, LANES), F32), pltpu.SemaphoreType.DMA((2,))],
    )
    return pl.pallas_call(
        _combine_kernel,
        grid_spec=grid_spec,
        out_shape=[jax.ShapeDtypeStruct((T_CTX, D_MODEL), F32),
                   jax.ShapeDtypeStruct((T_LAT, D_MODEL), F32)],
        compiler_params=_cparams(("arbitrary",)),
        name="moe_combine",
    )(pos_flat, x1c, x1l, wts_t, mod8, ys)


def _rope_tables():
    half = DQK // 2
    inv = 1.0 / (ROPE_BASE ** (np.arange(0, half, 2, dtype=np.float32) / half))
    t = np.arange(DEC_SEQ)
    ang_r = (t // GRID_W).astype(np.float32)[:, None] * inv
    ang_c = (t % GRID_W).astype(np.float32)[:, None] * inv
    ang = np.concatenate([ang_r, ang_r, ang_c, ang_c], axis=-1).astype(np.float32)
    ang = np.concatenate([ang, ang], axis=-1)
    first = (np.arange(LANES) % 32) < 16
    cos, sin = np.cos(ang), np.sin(ang)
    return (jnp.asarray(cos, F32), jnp.asarray(np.where(first, -sin, 0.0), F32),
            jnp.asarray(np.where(first, 0.0, sin), F32))


def _na_mask():
    qc = np.arange(GRID_W)[:, None]
    kc = (np.arange(LANES) % GRID_W)[None, :]
    ws = np.clip(qc - NA_KC // 2, 0, GRID_W - NA_KC)
    ok = (kc >= ws) & (kc < ws + NA_KC)
    first = (np.arange(LANES) < GRID_W)[None, :]
    masks = [ok, ok & first, ok & ~first]
    return jnp.asarray(np.stack([np.where(m, 0.0, NEG_BIG) for m in masks]), F32)


def kernel(x_prompt, x_sample, cache_diff_k, cache_diff_v, cache_na_k, cache_na_v, c, c_ctx, norm_mix_g, norm_ffn_g, w_ada, b_ada, w_in, w_out, conv_w, conv_b, conv_ln_g, conv_ln_b, diff_qn_g, diff_kn_g, diff_lam_q1, diff_lam_k1, diff_lam_q2, diff_lam_k2, diff_subln_g, na_qn_g, na_kn_g, na_rpb, moe_wr_g, moe_br_g, moe_wr_e, moe_br_e, moe_w_gate, moe_w_up, moe_w_down):
    cvec = jnp.zeros((MOD_ROWS, D_MODEL), F32).at[0].set(c_ctx).at[1:1 + DEC_BATCH].set(c)
    ones = jnp.ones((DEPTH, SEG_W), F32)
    gains = jnp.stack(
        [jnp.tile(diff_qn_g, (1, 2 * N_HEADS)), jnp.tile(diff_kn_g, (1, 2 * N_HEADS)), ones,
         jnp.tile(na_qn_g, (1, N_HEADS)), jnp.tile(na_kn_g, (1, N_HEADS)), ones],
        axis=1).reshape(DEPTH, 6, 1, SEG_W)
    rope_tabs = _rope_tables()
    na_mask = _na_mask()
    lam_params = jnp.stack([diff_lam_q1, diff_lam_k1, diff_lam_q2, diff_lam_k2], axis=1)
    rpb_pad = jnp.pad(na_rpb, ((0, 0), (0, 0), (0, 0), (0, DQK - na_rpb.shape[-1])))
    rpb_src = jnp.concatenate([rpb_pad[:, :, :-1], rpb_pad[:, :, 1:]], axis=-1)
    rpb_src = jnp.pad(rpb_src, ((0, 0), (0, 0), (0, 16 - NA_PAIRS), (0, 0)))
    wr_t = jnp.concatenate([moe_wr_g, moe_wr_e.reshape(DEPTH, D_MODEL, N_EXPERTS)], axis=2)
    wr_t = jnp.pad(wr_t, ((0, 0), (0, 0), (0, LANES - N_GROUPS - N_EXPERTS)))
    br = jnp.concatenate([moe_br_g, moe_br_e.reshape(DEPTH, N_EXPERTS)], axis=1)
    br = jnp.pad(br, ((0, 0), (0, ROUTE_ROWS - N_GROUPS - N_EXPERTS))).reshape(DEPTH, ROUTE_ROWS, 1)
    tri = jnp.asarray(np.triu(np.ones((CUM_BLK, CUM_BLK), np.float32), 1), BF16)
    norm_mix = norm_mix_g.reshape(DEPTH, 1, D_MODEL)
    norm_ffn = norm_ffn_g.reshape(DEPTH, 1, D_MODEL)
    head_major = lambda cache: jnp.transpose(cache, (0, 1, 3, 2, 4))
    ck_diff, cv_diff = head_major(cache_diff_k), head_major(cache_diff_v)
    ck_na, cv_na = head_major(cache_na_k), head_major(cache_na_v)

    mod8 = _ada_modulation(cvec, w_ada, b_ada).reshape(DEPTH * MOD_ROWS, 1, N_ADA)
    w_out_bf = _cast_bf16(w_out)

    ctx_row = lambda i: 0
    lat_row_pre = lambda i: 1 + i * PRE_TM // DEC_SEQ
    lat_row_mix = lambda i: 1 + i * MIX_TM // DEC_SEQ

    xc = x_prompt.reshape(T_CTX, D_MODEL)
    xl = x_sample.reshape(T_LAT, D_MODEL)
    ctx_segs = []
    for layer in range(DEPTH):
        hc, uc = _pre_project(xc, mod8, norm_mix, w_in, layer, ctx_row, "pre_ctx")
        hl, ul = _pre_project(xl, mod8, norm_mix, w_in, layer, lat_row_pre, "pre_lat")
        sc = _head_project(hc, w_in, gains, None, layer, BATCH, SEQ, "heads_ctx")
        sl = _head_project(hl, w_in, gains, rope_tabs, layer, DEC_BATCH, DEC_SEQ, "heads_lat")
        ctx_segs.append(sc)

        last = layer == DEPTH - 1
        dc, nc, *caches = _ctx_attention(sc, ctx_segs[:-1] if last else [], lam_params, diff_subln_g, layer)
        dl = _diff_attention_lat(sl, lam_params, diff_subln_g, ck_diff, cv_diff, layer, DIFF_TQ)
        nl = _na_attention(sl, rpb_src, na_mask, ck_na, cv_na, layer)

        conv_params = (conv_w, conv_b, conv_ln_g, conv_ln_b)
        x1c, ltc = _mix(uc, conv_params, dc, nc, w_out_bf, xc, mod8, norm_ffn, wr_t, br, layer, ctx_row, SEQ)
        x1l, ltl = _mix(ul, conv_params, dl, nl, w_out_bf, xl, mod8, norm_ffn, wr_t, br, layer,
                        lat_row_mix, DEC_SEQ)

        pos, wts, items = _route(jnp.concatenate([ltc, ltl], axis=1), tri)
        pos_flat = pos.reshape(2 * T_ALL)
        items = [items[r, :MOE_ITEMS] for r in range(4)] + [items[4, :1]]
        xs = _dispatch(pos_flat, x1c, x1l, mod8, norm_ffn, layer)
        ys = _experts(items, xs, moe_w_gate, moe_w_up, moe_w_down, layer)
        xc, xl = _combine(pos_flat, x1c, x1l, wts.T, mod8, ys, layer)

    outs = [jnp.transpose(cache, (0, 1, 3, 2, 4)) for cache in caches]
    return (xc.reshape(BATCH, SEQ, D_MODEL), xl.reshape(DEC_BATCH, DEC_SEQ, D_MODEL), *outs)
```

```python
import functools
import math

import numpy as np
import jax
import jax.numpy as jnp
from jax import lax
from jax.experimental import pallas as pl
from jax.experimental.pallas import tpu as pltpu

F32 = jnp.float32
BF16 = jnp.bfloat16
I32 = jnp.int32

D_MODEL = 2048
BATCH = 16
SEQ = 256
DEPTH = 2
DEC_BATCH = 2
DEC_SEQ = 1024
PAST_LEN = 256
GRID_W = 64
GRID_ROWS = DEC_SEQ // GRID_W
HEAD_DIM = 128
C_CONV = 512
CONV_K = 31
N_HEADS = 6
DQK = 64
NA_KR = 8
NA_KC = 16
ROPE_BASE = 10000.0
N_GROUPS = 4
E_PER_GROUP = 4
N_EXPERTS = 16
D_FF = 512
N_ADA = 6 * D_MODEL
W_IN_COLS = 2 * C_CONV + 6 * N_HEADS * HEAD_DIM
T_CTX = BATCH * SEQ
T_LAT = DEC_BATCH * DEC_SEQ
T_ALL = T_CTX + T_LAT

LANES = 128
MOD_ROWS = 8
PRE_TM = 512
HEADS_TM = 1024
MIX_TM = 512
DIFF_TQ = 1024
MOE_TM = 256
MOE_ITEMS = 2 * T_ALL // MOE_TM + N_EXPERTS
ROUTE_ROWS = 32
CUM_BLK = 512
VMEM_LIMIT = 56 * 1024 * 1024
NEG_BIG = -1e30


def _cparams(sem):
    return pltpu.CompilerParams(dimension_semantics=sem, vmem_limit_bytes=VMEM_LIMIT)


def _sigmoid(x):
    return 1.0 / (1.0 + jnp.exp(-x))


def _silu(x):
    return x * _sigmoid(x)


def _lambda_init(layer):
    return 0.8 - 0.6 * math.exp(-0.3 * layer)


def _ada_kernel(c_ref, w_ref, b_ref, o_ref):
    c = c_ref[...]
    s = _silu(c).astype(BF16)
    o_ref[...] = jnp.dot(s, w_ref[...].astype(BF16), preferred_element_type=F32) + b_ref[...]


def _ada_modulation(cvec, w_ada, b_ada):
    tn = 1024
    return pl.pallas_call(
        _ada_kernel,
        grid=(DEPTH, N_ADA // tn),
        in_specs=[
            pl.BlockSpec((MOD_ROWS, D_MODEL), lambda l, j: (0, 0)),
            pl.BlockSpec((None, D_MODEL, tn), lambda l, j: (l, 0, j)),
            pl.BlockSpec((None, 1, tn), lambda l, j: (l, 0, j)),
        ],
        out_specs=pl.BlockSpec((None, MOD_ROWS, tn), lambda l, j: (l, 0, j)),
        out_shape=jax.ShapeDtypeStruct((DEPTH, MOD_ROWS, N_ADA), F32),
        compiler_params=_cparams(("arbitrary", "arbitrary")),
        name="ada_modulation",
    )(cvec, w_ada, b_ada.reshape(DEPTH, 1, N_ADA))


def _mod_spec(layer, chunk, row_of_block):
    return pl.BlockSpec((None, 1, D_MODEL),
                        lambda i, *_: (layer * MOD_ROWS + row_of_block(i), 0, chunk))


def _modulate(x, g, shift, scale):
    ms = jnp.mean(x * x, axis=-1, keepdims=True)
    return x * lax.rsqrt(ms + 1e-6) * (g * (1.0 + scale)) + shift


def _pre_kernel(x_ref, sh_ref, sc_ref, g_ref, w_ref, h_ref, u_ref, wbf_ref):
    @pl.when(pl.program_id(0) == 0)
    def _():
        wbf_ref[...] = w_ref[...].astype(BF16)

    h = _modulate(x_ref[...], g_ref[...], sh_ref[...], sc_ref[...]).astype(BF16)
    h_ref[...] = h
    y = jnp.dot(h, wbf_ref[...], preferred_element_type=F32)
    u_ref[...] = y[:, :C_CONV] * _sigmoid(y[:, C_CONV:])


def _pre_project(x, mod8, norm_g, w_in, layer, row_of_block, name):
    t = x.shape[0]
    return pl.pallas_call(
        _pre_kernel,
        grid=(t // PRE_TM,),
        in_specs=[
            pl.BlockSpec((PRE_TM, D_MODEL), lambda i: (i, 0)),
            _mod_spec(layer, 0, row_of_block),
            _mod_spec(layer, 1, row_of_block),
            pl.BlockSpec((None, 1, D_MODEL), lambda i: (layer, 0, 0)),
            pl.BlockSpec((None, D_MODEL, 2 * C_CONV), lambda i: (layer, 0, 0)),
        ],
        out_specs=[pl.BlockSpec((PRE_TM, D_MODEL), lambda i: (i, 0)),
                   pl.BlockSpec((PRE_TM, C_CONV), lambda i: (i, 0))],
        out_shape=[jax.ShapeDtypeStruct((t, D_MODEL), BF16), jax.ShapeDtypeStruct((t, C_CONV), F32)],
        scratch_shapes=[pltpu.VMEM((D_MODEL, 2 * C_CONV), BF16)],
        compiler_params=_cparams(("arbitrary",)),
        name=name,
    )(x, mod8, mod8, norm_g, w_in)


SEG_DQ, SEG_DK, SEG_DV, SEG_NQ, SEG_NK, SEG_NV = range(6)
SEG_W = N_HEADS * HEAD_DIM
HEADS_WBLK = 256
HEADS_RC = 256


def _heads_kernel(h_ref, w0_ref, w1_ref, w2_ref, gain_ref, *rest, rope, per_batch):
    if rope:
        cos_ref, sa_ref, sb_ref, o_ref = rest
    else:
        (o_ref,) = rest
    j = pl.program_id(0)

    def finish(seg):
        n_g = HEADS_WBLK // LANES
        for c, w_ref in enumerate((w0_ref, w1_ref, w2_ref)):
            w = w_ref[...].astype(BF16)
            gain = gain_ref[:, c * HEADS_WBLK:(c + 1) * HEADS_WBLK]
            for r0 in range(0, HEADS_TM, HEADS_RC):
                y = jnp.dot(h_ref[r0:r0 + HEADS_RC, :], w, preferred_element_type=F32)
                for g in range(n_g):
                    hd = c * n_g + g
                    yg = y[:, g * LANES:(g + 1) * LANES]
                    if seg in (SEG_NQ, SEG_NK):
                        ms = jnp.mean(yg * yg, axis=-1, keepdims=True)
                        yg = yg * lax.rsqrt(ms + 1e-6) * gain[:, g * LANES:(g + 1) * LANES]
                    elif seg in (SEG_DQ, SEG_DK):
                        sq = yg * yg
                        low = lax.broadcasted_iota(I32, sq.shape, 1) < DQK
                        s_all = jnp.sum(sq, axis=-1, keepdims=True)
                        s_lo = jnp.sum(jnp.where(low, sq, 0.0), axis=-1, keepdims=True)
                        ms = jnp.where(low, s_lo, s_all - s_lo) * (1.0 / DQK)
                        yg = yg * lax.rsqrt(ms + 1e-6) * gain[:, g * LANES:(g + 1) * LANES]
                    if rope and seg in (SEG_DQ, SEG_DK):
                        rs = slice(r0, r0 + HEADS_RC)
                        yg = (yg * cos_ref[rs, :] + pltpu.roll(yg, LANES - 16, 1) * sa_ref[rs, :]
                              + pltpu.roll(yg, 16, 1) * sb_ref[rs, :])
                    yg = yg.astype(o_ref.dtype)
                    if per_batch == 1:
                        o_ref[hd, r0:r0 + HEADS_RC, :] = yg
                    else:
                        rows = HEADS_TM // per_batch
                        for q in range(HEADS_RC // rows):
                            o_ref[r0 // rows + q, hd] = yg[q * rows:(q + 1) * rows]

    for seg in range(6):
        pl.when(j == seg)(functools.partial(finish, seg))


def _head_project(h, w_in, gains, rope_tabs, layer, batch, seq, name):
    t = h.shape[0]
    rope = rope_tabs is not None
    per_batch = HEADS_TM // seq
    col0 = 2 * C_CONV // HEADS_WBLK
    wspec = lambda c: pl.BlockSpec((None, D_MODEL, HEADS_WBLK),
                                   lambda j, i: (layer, 0, col0 + (SEG_W // HEADS_WBLK) * j + c))
    in_specs = [
        pl.BlockSpec((HEADS_TM, D_MODEL), lambda j, i: (i, 0)),
        wspec(0), wspec(1), wspec(2),
        pl.BlockSpec((None, None, 1, SEG_W), lambda j, i: (layer, j, 0, 0)),
    ]
    args = [h, w_in, w_in, w_in, gains]
    if rope:
        in_specs += [pl.BlockSpec((DEC_SEQ, LANES), lambda j, i: (0, 0))] * 3
        args += list(rope_tabs)
    if per_batch == 1:
        out_spec = pl.BlockSpec((None, None, N_HEADS, seq, HEAD_DIM), lambda j, i: (j, i, 0, 0, 0))
    else:
        out_spec = pl.BlockSpec((None, per_batch, N_HEADS, seq, HEAD_DIM), lambda j, i: (j, i, 0, 0, 0))
    return pl.pallas_call(
        functools.partial(_heads_kernel, rope=rope, per_batch=per_batch),
        grid=(6, t // HEADS_TM),
        in_specs=in_specs,
        out_specs=out_spec,
        out_shape=jax.ShapeDtypeStruct((6, batch, N_HEADS, seq, HEAD_DIM), BF16 if rope else F32),
        compiler_params=_cparams(("arbitrary", "arbitrary")),
        name=name,
    )(*args)


CONV_CH = 128
CONV_HALO = 16


def _conv_window(win_ref, y_ref, w_ref, b_ref, lg_ref, lb_ref):
    off = CONV_HALO - CONV_K // 2
    sub = 8
    win_rows = CONV_CH + 2 * CONV_HALO
    for g in range(C_CONV // LANES):
        sl = slice(g * LANES, (g + 1) * LANES)
        window = win_ref[:, sl]
        acc = jnp.zeros((CONV_CH, LANES), F32) + b_ref[:, sl]
        for phase in range(sub):
            taps = [k for k in range(CONV_K) if (off + k) % sub == phase]
            if not taps:
                continue
            shifted = window if phase == 0 else pltpu.roll(window, win_rows - phase, 0)
            for k in taps:
                a = (off + k) // sub * sub
                acc = acc + shifted[a:a + CONV_CH, :] * w_ref[k:k + 1, sl]
        y_ref[:, sl] = acc

    y = y_ref[...]
    mu = jnp.mean(y, axis=-1, keepdims=True)
    yc = y - mu
    var = jnp.mean(yc * yc, axis=-1, keepdims=True)
    z = yc * lax.rsqrt(var + 1e-5) * lg_ref[...] + lb_ref[...]
    return _silu(z)


def _conv_block(i, u_ref, ulo_ref, uhi_ref, conv_refs, win_ref, y_ref, a_ref, seq):
    rows = u_ref.shape[0]
    for c in range(rows // CONV_CH):
        r0 = c * CONV_CH
        first = i * rows + r0
        lo_keep = ((first % seq) != 0).astype(F32)
        hi_keep = (((first + CONV_CH) % seq) != 0).astype(F32)
        lo = u_ref[r0 - CONV_HALO:r0, :] if c > 0 else ulo_ref[...]
        hi = u_ref[r0 + CONV_CH:r0 + CONV_CH + CONV_HALO, :] if r0 + CONV_CH < rows else uhi_ref[...]
        win_ref[0:CONV_HALO, :] = lo * lo_keep
        win_ref[CONV_HALO:CONV_HALO + CONV_CH, :] = u_ref[r0:r0 + CONV_CH, :]
        win_ref[CONV_HALO + CONV_CH:, :] = hi * hi_keep
        a_ref[r0:r0 + CONV_CH, :] = _conv_window(win_ref, y_ref, *conv_refs).astype(a_ref.dtype)


_NT = (((1,), (1,)), ((), ()))


LOG2E = 1.4426950408889634


def _softmax_rows(s):
    m = jnp.max(s, axis=-1, keepdims=True)
    e = jnp.exp(s - m)
    return e / jnp.sum(e, axis=-1, keepdims=True)


def _exp2_rows(s2):
    e = jnp.exp2(s2 - jnp.max(s2, axis=-1, keepdims=True))
    return e, 1.0 / jnp.sum(e, axis=-1, keepdims=True)


def _diff_lambda(lam_ref, lam_init):
    lv = lam_ref[...]
    return (jnp.exp(jnp.sum(lv[0:1] * lv[1:2], axis=-1, keepdims=True))
            - jnp.exp(jnp.sum(lv[2:3] * lv[3:4], axis=-1, keepdims=True)) + lam_init)


def _diff_head(q, k, v, lam, g, lam_init, long_keys):
    lane = lax.broadcasted_iota(I32, q.shape, 1)
    if long_keys:
        qs = q * (DQK ** -0.5 * LOG2E)
        q0 = jnp.where(lane < DQK, qs, 0.0).astype(BF16)
        q1 = jnp.where(lane >= DQK, qs, 0.0).astype(BF16)
        e0, r0 = _exp2_rows(lax.dot_general(q0, k, _NT, preferred_element_type=F32))
        e1, r1 = _exp2_rows(lax.dot_general(q1, k, _NT, preferred_element_type=F32))
        o = (jnp.dot(e0.astype(BF16), v, preferred_element_type=F32) * r0
             - jnp.dot(e1.astype(BF16), v, preferred_element_type=F32) * (lam * r1))
    else:
        q0 = jnp.where(lane < DQK, q, 0.0).astype(BF16)
        q1 = jnp.where(lane >= DQK, q, 0.0).astype(BF16)
        scale = DQK ** -0.5
        p0 = _softmax_rows(lax.dot_general(q0, k, _NT, preferred_element_type=F32) * scale)
        p1 = _softmax_rows(lax.dot_general(q1, k, _NT, preferred_element_type=F32) * scale)
        o = jnp.dot((p0 - lam * p1).astype(BF16), v, preferred_element_type=F32)
    ms = jnp.mean(o * o, axis=-1, keepdims=True)
    return o * lax.rsqrt(ms + 1e-5) * g * (1.0 - lam_init)


def _seg_spec(seg, rows, index):
    def imap(*ids):
        b, h, r = index(*ids)
        return (seg, b, h, r, 0)
    return pl.BlockSpec((None, None, None, rows, HEAD_DIM), imap)


def _diff_attn_lat_kernel(lam_ref, g_ref, q_ref, k_ref, v_ref, ck_ref, cv_ref, o_ref, *, lam_init):
    k = jnp.concatenate([k_ref[...].astype(BF16), ck_ref[...].astype(BF16)], axis=0)
    v = jnp.concatenate([v_ref[...].astype(BF16), cv_ref[...].astype(BF16)], axis=0)
    lam = _diff_lambda(lam_ref, lam_init)
    o_ref[...] = _diff_head(q_ref[...].astype(F32), k, v, lam, g_ref[...], lam_init,
                            True).astype(o_ref.dtype)


def _diff_attention_lat(segs, lam_params, subln_g, cache_k, cache_v, layer, tq):
    nq = DEC_SEQ // tq
    cspec = pl.BlockSpec((None, None, None, PAST_LEN, HEAD_DIM), lambda b, h, qi: (b, layer, h, 0, 0))
    return pl.pallas_call(
        functools.partial(_diff_attn_lat_kernel, lam_init=_lambda_init(layer)),
        grid=(DEC_BATCH, N_HEADS, nq),
        in_specs=[
            pl.BlockSpec((None, 4, DQK), lambda b, h, qi: (layer, 0, 0)),
            pl.BlockSpec((None, 1, HEAD_DIM), lambda b, h, qi: (layer, 0, 0)),
            _seg_spec(SEG_DQ, tq, lambda b, h, qi: (b, h, qi)),
            _seg_spec(SEG_DK, DEC_SEQ, lambda b, h, qi: (b, h, 0)),
            _seg_spec(SEG_DV, DEC_SEQ, lambda b, h, qi: (b, h, 0)),
            cspec, cspec,
        ],
        out_specs=pl.BlockSpec((tq, HEAD_DIM), lambda b, h, qi: (b * nq + qi, h)),
        out_shape=jax.ShapeDtypeStruct((T_LAT, SEG_W), BF16),
        compiler_params=_cparams(("arbitrary", "arbitrary", "arbitrary")),
        name="diff_attn_lat",
    )(lam_params, subln_g.reshape(DEPTH, 1, HEAD_DIM), segs, segs, segs, cache_k, cache_v)


CACHE_SEGS = (SEG_DK, SEG_DV, SEG_NK, SEG_NV)


def _ctx_attn_kernel(lam_ref, g_ref, *refs, lam_init, n_prev):
    seg_refs, refs = refs[:6], refs[6:]
    dq_ref, dk_ref, dv_ref, nq_ref, nk_ref, nv_ref = seg_refs
    prev_refs, refs = refs[:n_prev * len(CACHE_SEGS)], refs[n_prev * len(CACHE_SEGS):]
    d_ref, n_ref = refs[:2]
    cache_refs = refs[2:]
    lam = _diff_lambda(lam_ref, lam_init)
    for h in range(N_HEADS):
        sl = slice(h * HEAD_DIM, (h + 1) * HEAD_DIM)
        d_ref[:, sl] = _diff_head(dq_ref[h], dk_ref[h].astype(BF16), dv_ref[h].astype(BF16), lam,
                                  g_ref[...], lam_init, False).astype(d_ref.dtype)
        s = lax.dot_general(nq_ref[h].astype(BF16), nk_ref[h].astype(BF16), _NT,
                            preferred_element_type=F32) * (HEAD_DIM ** -0.5)
        n_ref[:, sl] = jnp.dot(_softmax_rows(s).astype(BF16), nv_ref[h].astype(BF16),
                               preferred_element_type=F32).astype(n_ref.dtype)
    for n, (seg, c_ref) in enumerate(zip(CACHE_SEGS, cache_refs)):
        for layer in range(n_prev):
            c_ref[layer] = prev_refs[layer * len(CACHE_SEGS) + n][...]
        c_ref[n_prev] = seg_refs[seg][...]


def _ctx_attention(segs, prev_segs, lam_params, subln_g, layer):
    seg = lambda s: pl.BlockSpec((None, None, N_HEADS, SEQ, HEAD_DIM), lambda b: (s, b, 0, 0, 0))
    out = pl.BlockSpec((SEQ, SEG_W), lambda b: (b, 0))
    in_specs = [pl.BlockSpec((None, 4, DQK), lambda b: (layer, 0, 0)),
                pl.BlockSpec((None, 1, HEAD_DIM), lambda b: (layer, 0, 0))] + [seg(s) for s in range(6)]
    args = [lam_params, subln_g.reshape(DEPTH, 1, HEAD_DIM)] + [segs] * 6
    out_specs = [out, out]
    out_shape = [jax.ShapeDtypeStruct((T_CTX, SEG_W), BF16)] * 2
    for prev in prev_segs:
        in_specs += [seg(s) for s in CACHE_SEGS]
        args += [prev] * len(CACHE_SEGS)
    if prev_segs:
        assert len(prev_segs) == DEPTH - 1
        cache = pl.BlockSpec((None, DEPTH, N_HEADS, SEQ, HEAD_DIM), lambda b: (b, 0, 0, 0, 0))
        out_specs += [cache] * len(CACHE_SEGS)
        out_shape += [jax.ShapeDtypeStruct((BATCH, DEPTH, N_HEADS, SEQ, HEAD_DIM), F32)] * len(CACHE_SEGS)
    return pl.pallas_call(
        functools.partial(_ctx_attn_kernel, lam_init=_lambda_init(layer), n_prev=len(prev_segs)),
        grid=(BATCH,),
        in_specs=in_specs,
        out_specs=out_specs,
        out_shape=out_shape,
        compiler_params=_cparams(("arbitrary",)),
        name="ctx_attn_caches" if prev_segs else "ctx_attn",
    )(*args)


NA_PAIRS = 2 * NA_KR - 2
NA_QROWS = 4
NA_KROWS = 12


def _na_key_block(chunk):
    first = min(max(chunk * NA_QROWS - NA_KR // 2, 0), GRID_ROWS - NA_KR)
    return min(first - first % 2, GRID_ROWS - NA_KROWS)


def _na_attn_kernel(src_ref, mask_ref, q_ref, k_ref, v_ref, ck_ref, cv_ref, o_ref, bias_ref):
    for d in range(NA_PAIRS):
        row = jnp.broadcast_to(src_ref[d:d + 1, :] * LOG2E, (GRID_W, LANES))
        tile = pltpu.roll(row, LANES - (NA_KC - 1), 1, stride=1, stride_axis=0)
        for v in range(3):
            bias_ref[v, d] = tile + mask_ref[v]
    dead = jnp.full((GRID_W, LANES), NEG_BIG, F32)
    ck = ck_ref[...].astype(BF16)
    cv = cv_ref[...].astype(BF16)
    scale = HEAD_DIM ** -0.5 * LOG2E
    for chunk in range(GRID_ROWS // NA_QROWS):
        kb = _na_key_block(chunk)
        rows = []
        for qr in range(chunk * NA_QROWS, (chunk + 1) * NA_QROWS):
            start = min(max(qr - NA_KR // 2, 0), GRID_ROWS - NA_KR)
            assert kb <= start and start + NA_KR <= kb + NA_KROWS
            tiles = []
            for m in range(NA_KROWS // 2):
                r0 = kb + 2 * m
                live0 = start <= r0 < start + NA_KR
                live1 = start <= r0 + 1 < start + NA_KR
                d = r0 - qr + NA_KR - 1
                assert not (live0 or live1) or 0 <= d < NA_PAIRS
                if live0 and live1:
                    tiles.append(bias_ref[0, d])
                elif live0:
                    tiles.append(bias_ref[1, d])
                elif live1:
                    tiles.append(bias_ref[2, d])
                else:
                    tiles.append(dead)
            rows.append(jnp.concatenate(tiles, axis=1))
        bias = jnp.concatenate(rows, axis=0)
        qs = slice(chunk * NA_QROWS * GRID_W, (chunk + 1) * NA_QROWS * GRID_W)
        ks = slice(kb * GRID_W, (kb + NA_KROWS) * GRID_W)
        q = (q_ref[qs, :].astype(F32) * scale).astype(BF16)
        kl = k_ref[ks, :].astype(BF16)
        vl = v_ref[ks, :].astype(BF16)
        s_loc = lax.dot_general(q, kl, _NT, preferred_element_type=F32) + bias
        s_ctx = lax.dot_general(q, ck, _NT, preferred_element_type=F32)
        m = jnp.maximum(jnp.max(s_loc, axis=-1, keepdims=True), jnp.max(s_ctx, axis=-1, keepdims=True))
        e_loc = jnp.exp2(s_loc - m)
        e_ctx = jnp.exp2(s_ctx - m)
        r = 1.0 / (jnp.sum(e_loc, axis=-1, keepdims=True) + jnp.sum(e_ctx, axis=-1, keepdims=True))
        o_ref[qs, :] = ((jnp.dot(e_loc.astype(BF16), vl, preferred_element_type=F32)
                         + jnp.dot(e_ctx.astype(BF16), cv, preferred_element_type=F32)) * r
                        ).astype(o_ref.dtype)


def _na_attention(segs, rpb_src, na_mask, cache_k, cache_v, layer):
    blk = lambda s: _seg_spec(s, DEC_SEQ, lambda b, h: (b, h, 0))
    cspec = pl.BlockSpec((None, None, None, PAST_LEN, HEAD_DIM), lambda b, h: (b, layer, h, 0, 0))
    return pl.pallas_call(
        _na_attn_kernel,
        grid=(DEC_BATCH, N_HEADS),
        in_specs=[
            pl.BlockSpec((None, None, 16, LANES), lambda b, h: (layer, h, 0, 0)),
            pl.BlockSpec((3, GRID_W, LANES), lambda b, h: (0, 0, 0)),
            blk(SEG_NQ), blk(SEG_NK), blk(SEG_NV), cspec, cspec,
        ],
        out_specs=pl.BlockSpec((DEC_SEQ, HEAD_DIM), lambda b, h: (b, h)),
        out_shape=jax.ShapeDtypeStruct((T_LAT, SEG_W), BF16),
        scratch_shapes=[pltpu.VMEM((3, NA_PAIRS, GRID_W, LANES), F32)],
        compiler_params=_cparams(("arbitrary", "arbitrary")),
        name="na_attn_lat",
    )(rpb_src, na_mask, segs, segs, segs, cache_k, cache_v)


def _split_f32(x, n):
    terms = []
    for _ in range(n):
        t = x.astype(BF16).astype(F32)
        terms.append(t)
        x = x - t
    return terms


def _mix_kernel(u_ref, ulo_ref, uhi_ref, cw_ref, cb_ref, clg_ref, clb_ref, d_ref, n_ref, w_ref, x_ref,
                g1_ref, sh_ref, sc_ref, ng_ref, wr_ref, br_ref, x1_ref, lt_ref, win_ref, y_ref, a_ref, *, seq):
    k0, k1 = C_CONV, C_CONV + N_HEADS * HEAD_DIM
    _conv_block(pl.program_id(0), u_ref, ulo_ref, uhi_ref, (cw_ref, cb_ref, clg_ref, clb_ref),
                win_ref, y_ref, a_ref, seq)
    mixed = (jnp.dot(d_ref[...].astype(BF16), w_ref[k0:k1, :], preferred_element_type=F32)
             + jnp.dot(n_ref[...].astype(BF16), w_ref[k1:, :], preferred_element_type=F32)
             + jnp.dot(a_ref[...], w_ref[0:k0, :], preferred_element_type=F32))
    x1 = x_ref[...] + g1_ref[...] * mixed
    x1_ref[...] = x1
    h2 = _modulate(x1, ng_ref[...], sh_ref[...], sc_ref[...])
    wh, wm, wl = _split_f32(wr_ref[...], 3)
    w_cat = (wh + pltpu.roll(wm, ROUTE_ROWS, 1) + pltpu.roll(wl, 2 * ROUTE_ROWS, 1)).astype(BF16)
    hh, hm = _split_f32(h2, 2)
    r = jnp.dot(jnp.concatenate([hh, hm], axis=0).astype(BF16), w_cat, preferred_element_type=F32)
    rh, rm = r[:MIX_TM], r[MIX_TM:]
    back = lambda v, groups: pltpu.roll(v, LANES - groups * ROUTE_ROWS, 1)
    small = (back(rm, 1) + back(rh, 2)) + (back(rh, 1) + rm)
    logits = (small + rh).T[:ROUTE_ROWS, :]
    lt_ref[...] = logits + br_ref[...]


def _mix(u, conv_params, d_out, n_out, w_out_bf, x, mod8, norm_g, wr_t, br, layer, row_of_block, seq):
    t = x.shape[0]
    rows = lambda width: pl.BlockSpec((MIX_TM, width), lambda i: (i, 0))
    halo_blocks = MIX_TM // CONV_HALO
    cvec = lambda: pl.BlockSpec((None, 1, C_CONV), lambda i: (layer, 0, 0))
    conv_w, conv_b, ln_g, ln_b = conv_params
    return pl.pallas_call(
        functools.partial(_mix_kernel, seq=seq),
        grid=(t // MIX_TM,),
        in_specs=[
            rows(C_CONV),
            pl.BlockSpec((CONV_HALO, C_CONV), lambda i: (jnp.maximum(i * halo_blocks - 1, 0), 0)),
            pl.BlockSpec((CONV_HALO, C_CONV),
                         lambda i: (jnp.minimum((i + 1) * halo_blocks, t // CONV_HALO - 1), 0)),
            pl.BlockSpec((None, CONV_K, C_CONV), lambda i: (layer, 0, 0)),
            cvec(), cvec(), cvec(),
            rows(N_HEADS * HEAD_DIM), rows(N_HEADS * HEAD_DIM),
            pl.BlockSpec((None, D_MODEL, D_MODEL), lambda i: (layer, 0, 0)),
            rows(D_MODEL),
            _mod_spec(layer, 2, row_of_block),
            _mod_spec(layer, 3, row_of_block),
            _mod_spec(layer, 4, row_of_block),
            pl.BlockSpec((None, 1, D_MODEL), lambda i: (layer, 0, 0)),
            pl.BlockSpec((None, D_MODEL, LANES), lambda i: (layer, 0, 0)),
            pl.BlockSpec((None, ROUTE_ROWS, 1), lambda i: (layer, 0, 0)),
        ],
        out_specs=[rows(D_MODEL), pl.BlockSpec((ROUTE_ROWS, MIX_TM), lambda i: (0, i))],
        out_shape=[jax.ShapeDtypeStruct((t, D_MODEL), F32),
                   jax.ShapeDtypeStruct((ROUTE_ROWS, t), F32)],
        scratch_shapes=[pltpu.VMEM((CONV_CH + 2 * CONV_HALO, C_CONV), F32),
                        pltpu.VMEM((CONV_CH, C_CONV), F32),
                        pltpu.VMEM((MIX_TM, C_CONV), BF16)],
        compiler_params=_cparams(("arbitrary",)),
        name="mix_%d" % t,
    )(u, u, u, conv_w, conv_b.reshape(DEPTH, 1, C_CONV), ln_g.reshape(DEPTH, 1, C_CONV),
      ln_b.reshape(DEPTH, 1, C_CONV), d_out, n_out, w_out_bf, x, mod8, mod8, mod8, norm_g, wr_t, br)


def _cast_kernel(x_ref, o_ref):
    o_ref[...] = x_ref[...].astype(BF16)


def _cast_bf16(w):
    tm = 512
    spec = pl.BlockSpec((None, tm, D_MODEL), lambda l, i: (l, i, 0))
    return pl.pallas_call(
        _cast_kernel, grid=(DEPTH, D_MODEL // tm), in_specs=[spec], out_specs=spec,
        out_shape=jax.ShapeDtypeStruct(w.shape, BF16),
        compiler_params=_cparams(("arbitrary", "arbitrary")), name="cast_w_out",
    )(w)


def _route_kernel(lt_ref, tri_ref, pos_ref, wt_ref, item_ref):
    t = T_ALL
    lg = lt_ref[0:N_GROUPS, :]
    eg = jnp.exp(lg - jnp.max(lg, axis=0, keepdims=True))
    pg = eg / jnp.sum(eg, axis=0, keepdims=True)
    pg_top = jnp.max(pg, axis=0, keepdims=True)
    gi = lax.broadcasted_iota(I32, pg.shape, 0).astype(F32)
    g_idx = jnp.min(jnp.where(pg == pg_top, gi, float(N_GROUPS)), axis=0, keepdims=True)

    le = jnp.zeros((E_PER_GROUP, t), F32)
    for g in range(N_GROUPS):
        rows = lt_ref[N_GROUPS + g * E_PER_GROUP:N_GROUPS + (g + 1) * E_PER_GROUP, :]
        le = jnp.where(g_idx == float(g), rows, le)
    ee = jnp.exp(le - jnp.max(le, axis=0, keepdims=True))
    pe = ee / jnp.sum(ee, axis=0, keepdims=True)
    ei = lax.broadcasted_iota(I32, pe.shape, 0).astype(F32)
    p1 = jnp.max(pe, axis=0, keepdims=True)
    i1 = jnp.min(jnp.where(pe == p1, ei, float(E_PER_GROUP)), axis=0, keepdims=True)
    pe_rest = jnp.where(ei == i1, -1.0, pe)
    p2 = jnp.max(pe_rest, axis=0, keepdims=True)
    i2 = jnp.min(jnp.where(pe_rest == p2, ei, float(E_PER_GROUP)), axis=0, keepdims=True)
    den = p1 + p2
    wt_ref[0:1, :] = pg_top * (p1 / den)
    wt_ref[1:2, :] = pg_top * (p2 / den)
    e1 = g_idx * E_PER_GROUP + i1
    e2 = g_idx * E_PER_GROUP + i2

    erow = lax.broadcasted_iota(I32, (N_EXPERTS, t), 0).astype(F32)
    oh1 = (erow == e1).astype(F32)
    oh2 = (erow == e2).astype(F32)
    cnt = oh1 + oh2
    carry = jnp.zeros((N_EXPERTS, 1), F32)
    ranks = []
    for b in range(t // CUM_BLK):
        blk = cnt[:, b * CUM_BLK:(b + 1) * CUM_BLK]
        ranks.append(jnp.dot(blk.astype(BF16), tri_ref[...], preferred_element_type=F32) + carry)
        carry = carry + jnp.sum(blk, axis=1, keepdims=True)
    rank = jnp.concatenate(ranks, axis=1)

    erow_l = lax.broadcasted_iota(I32, (N_EXPERTS, LANES), 0)

    def excl_scan(v):
        inc = v
        for s in (1, 2, 4, 8):
            inc = inc + jnp.where(erow_l >= s, pltpu.roll(inc, s, 0), 0.0)
        return inc - v

    total = jnp.broadcast_to(carry, (N_EXPERTS, LANES))
    offs = excl_scan(total)
    pos_ref[0:1, :] = jnp.sum(oh1 * (rank + offs[:, 0:1]), axis=0, keepdims=True).astype(I32)
    pos_ref[1:2, :] = jnp.sum(oh2 * (rank + offs[:, 0:1]), axis=0, keepdims=True).astype(I32)

    shift = int(math.log2(MOE_TM))
    offs_i = offs.astype(I32)
    total_i = total.astype(I32)
    first_tile = lax.shift_right_logical(offs_i, shift)
    last_tile = lax.shift_right_logical(offs_i + total_i - 1, shift)
    n_item = jnp.where(total_i > 0, last_tile - first_tile + 1, 0).astype(F32)
    item_start = excl_scan(n_item)
    item_end = item_start + n_item
    kk = lax.broadcasted_iota(I32, (N_EXPERTS, LANES), 1).astype(F32)
    item_e = jnp.minimum(jnp.sum((item_end <= kk).astype(F32), axis=0, keepdims=True),
                         float(N_EXPERTS - 1))
    sel = (erow_l.astype(F32) == item_e).astype(F32)
    pick = lambda v: jnp.sum(sel * v, axis=0, keepdims=True)
    item_ref[0:1, :] = (kk[0:1] + pick(first_tile.astype(F32) - item_start)).astype(I32)
    item_ref[1:2, :] = item_e.astype(I32)
    item_ref[2:3, :] = pick(offs).astype(I32)
    item_ref[3:4, :] = pick(offs + total).astype(I32)
    item_ref[4:5, :] = item_end[N_EXPERTS - 1:N_EXPERTS, :].astype(I32)
    item_ref[5:8, :] = jnp.zeros((3, LANES), I32)


def _route(lt_all, tri):
    full = lambda shape: pl.BlockSpec(shape, lambda i: (0,) * len(shape))
    return pl.pallas_call(
        _route_kernel,
        grid=(1,),
        in_specs=[full((ROUTE_ROWS, T_ALL)), full((CUM_BLK, CUM_BLK))],
        out_specs=[full((2, T_ALL)), full((2, T_ALL)), full((8, LANES))],
        out_shape=[jax.ShapeDtypeStruct((2, T_ALL), I32), jax.ShapeDtypeStruct((2, T_ALL), F32),
                   jax.ShapeDtypeStruct((8, LANES), I32)],
        compiler_params=_cparams(("arbitrary",)),
        name="moe_route",
    )(lt_all, tri)


N_TOK_TILES = T_ALL // MOE_TM
CTX_TILES = T_CTX // MOE_TM


def _tok_row(i):
    return jnp.where(i < CTX_TILES, 0, 1 + (i - CTX_TILES) // (DEC_SEQ // MOE_TM))


N_CHUNK = D_MODEL // LANES
MOE_PITCH = N_CHUNK + 1
SLAB_ROWS = MOE_TM * MOE_PITCH
ISSUE_UNROLL = 8


def _slab_copy(src, src_tok, dst, dst_tok, sem, pitch):
    return pltpu.make_async_copy(src.at[pl.ds(src_tok * pitch, pitch)],
                                 dst.at[pl.ds(dst_tok * pitch, pitch)], sem)


def _to_slabs(ref, value):
    for c in range(N_CHUNK):
        ref[pl.ds(c, MOE_TM, stride=MOE_PITCH), :] = value[:, c * LANES:(c + 1) * LANES]
    ref[pl.ds(N_CHUNK, MOE_TM, stride=MOE_PITCH), :] = jnp.zeros((MOE_TM, LANES), F32)


def _slab_chunk(ref, c):
    return ref[pl.ds(c, MOE_TM, stride=MOE_PITCH), :]


def _dispatch_kernel(pos_ref, xc_ref, xl_ref, sh_ref, sc_ref, ng_ref, xs_ref, h_ref, sem):
    i = pl.program_id(0)
    slot = i % 2
    buf = h_ref.at[slot]

    def drain(s):
        for _ in range(2):
            pltpu.make_async_copy(h_ref.at[s], xs_ref.at[pl.ds(0, SLAB_ROWS)], sem.at[s]).wait()

    @pl.when(i >= 2)
    def _():
        drain(slot)

    @pl.when(i < CTX_TILES)
    def _():
        _to_slabs(buf, _modulate(xc_ref[...], ng_ref[...], sh_ref[...], sc_ref[...]))

    @pl.when(i >= CTX_TILES)
    def _():
        _to_slabs(buf, _modulate(xl_ref[...], ng_ref[...], sh_ref[...], sc_ref[...]))

    def issue(r, carry):
        tok = i * MOE_TM + r
        _slab_copy(buf, r, xs_ref, pos_ref[tok], sem.at[slot], MOE_PITCH).start(priority=0)
        _slab_copy(buf, r, xs_ref, pos_ref[T_ALL + tok], sem.at[slot], MOE_PITCH).start(priority=1)
        return carry

    lax.fori_loop(0, MOE_TM, issue, 0, unroll=ISSUE_UNROLL)

    @pl.when(i == N_TOK_TILES - 1)
    def _():
        drain(1 - slot)
        drain(slot)


def _dispatch(pos_flat, x1c, x1l, mod8, norm_g, layer):
    grid_spec = pltpu.PrefetchScalarGridSpec(
        num_scalar_prefetch=1,
        grid=(N_TOK_TILES,),
        in_specs=[
            pl.BlockSpec((MOE_TM, D_MODEL), lambda i, *_: (jnp.minimum(i, CTX_TILES - 1), 0)),
            pl.BlockSpec((MOE_TM, D_MODEL), lambda i, *_: (jnp.maximum(i - CTX_TILES, 0), 0)),
            _mod_spec(layer, 3, _tok_row),
            _mod_spec(layer, 4, _tok_row),
            pl.BlockSpec((None, 1, D_MODEL), lambda i, *_: (layer, 0, 0)),
        ],
        out_specs=pl.BlockSpec(memory_space=pl.ANY),
        scratch_shapes=[pltpu.VMEM((2, SLAB_ROWS, LANES), F32), pltpu.SemaphoreType.DMA((2,))],
    )
    return pl.pallas_call(
        _dispatch_kernel,
        grid_spec=grid_spec,
        out_shape=jax.ShapeDtypeStruct((2 * T_ALL * MOE_PITCH, LANES), F32),
        compiler_params=_cparams(("arbitrary",)),
        name="moe_dispatch",
    )(pos_flat, x1c, x1l, mod8, mod8, norm_g)


N_ROW_TILES = 2 * T_ALL // MOE_TM
X_SLOTS = 3
Y_SLOTS = 2


def _expert_kernel(tile_ref, exp_ref, lo_ref, hi_ref, n_ref, xs_hbm, wg_hbm, wu_hbm, wd_hbm, ys_hbm,
                   x_buf, y_buf, wg_buf, wu_buf, wd_buf, wg_bf, wu_bf, wd_bf, slot_ref, sem, x_sem, y_sem,
                   *, layer):
    k = pl.program_id(0)
    n = n_ref[0]

    def weight_copies(e, s):
        return (pltpu.make_async_copy(wg_hbm.at[layer, e], wg_buf.at[s], sem.at[s, 0]),
                pltpu.make_async_copy(wu_hbm.at[layer, e], wu_buf.at[s], sem.at[s, 1]),
                pltpu.make_async_copy(wd_hbm.at[layer, e], wd_buf.at[s], sem.at[s, 2]))

    def x_copy(t):
        s = t % X_SLOTS
        return pltpu.make_async_copy(xs_hbm.at[pl.ds(t * SLAB_ROWS, SLAB_ROWS)], x_buf.at[s], x_sem.at[s])

    def y_copy(t):
        s = t % Y_SLOTS
        return pltpu.make_async_copy(y_buf.at[s], ys_hbm.at[pl.ds(t * SLAB_ROWS, SLAB_ROWS)], y_sem.at[s])

    @pl.when(k == 0)
    def _():
        slot_ref[0] = 0
        for cp in weight_copies(exp_ref[0], 0):
            cp.start()
        for t in range(X_SLOTS - 1):
            x_copy(t).start()

    @pl.when(k < n)
    def _():
        e = exp_ref[k]
        tile = tile_ref[k]
        prev_tile = tile_ref[jnp.maximum(k - 1, 0)]
        new_expert = jnp.logical_or(k == 0, exp_ref[jnp.maximum(k - 1, 0)] != e)

        @pl.when(jnp.logical_or(k == 0, tile != prev_tile))
        def _():
            x_copy(tile).wait()

            @pl.when(tile + (X_SLOTS - 1) < N_ROW_TILES)
            def _():
                x_copy(tile + (X_SLOTS - 1)).start()

            @pl.when(k > 0)
            def _():
                y_copy(prev_tile).start()

            @pl.when(tile >= Y_SLOTS)
            def _():
                y_copy(tile - Y_SLOTS).wait()

        xs_ref = x_buf.at[tile % X_SLOTS]
        ys_ref = y_buf.at[tile % Y_SLOTS]

        @pl.when(new_expert)
        def _():
            s = slot_ref[0]
            for cp in weight_copies(e, s):
                cp.wait()
            wg_bf[...] = wg_buf[s].astype(BF16)
            wu_bf[...] = wu_buf[s].astype(BF16)
            wd_bf[...] = wd_buf[s].astype(BF16)
            nxt = lax.while_loop(lambda j: jnp.logical_and(j < n, exp_ref[jnp.minimum(j, n - 1)] == e),
                                 lambda j: j + 1, k + 1)

            @pl.when(nxt < n)
            def _():
                for cp in weight_copies(exp_ref[jnp.minimum(nxt, n - 1)], 1 - s):
                    cp.start()
            slot_ref[0] = 1 - s

        x = jnp.concatenate([_slab_chunk(xs_ref, c) for c in range(N_CHUNK)], axis=1).astype(BF16)
        gate = jnp.dot(x, wg_bf[...], preferred_element_type=F32)
        up = jnp.dot(x, wu_bf[...], preferred_element_type=F32)
        hid = (_silu(gate) * up).astype(BF16)
        y = jnp.dot(hid, wd_bf[...], preferred_element_type=F32)
        row0 = tile * MOE_TM
        row = row0 + lax.broadcasted_iota(I32, (MOE_TM, 1), 0)
        mine = jnp.logical_and(row >= lo_ref[k], row < hi_ref[k])
        first = lo_ref[k] <= row0

        @pl.when(first)
        def _():
            _to_slabs(ys_ref, jnp.where(mine, y, 0.0))

        @pl.when(jnp.logical_not(first))
        def _():
            for c in range(N_CHUNK):
                old = _slab_chunk(ys_ref, c)
                ys_ref[pl.ds(c, MOE_TM, stride=MOE_PITCH), :] = jnp.where(
                    mine, y[:, c * LANES:(c + 1) * LANES], old)

        @pl.when(k == n - 1)
        def _():
            y_copy(tile).start()
            y_copy(tile).wait()

            @pl.when(tile >= 1)
            def _():
                y_copy(tile - 1).wait()


def _experts(items, xs, w_gate, w_up, w_down, layer):
    item_tile, item_exp, item_lo, item_hi, n_items = items
    hbm = pl.BlockSpec(memory_space=pl.ANY)
    grid_spec = pltpu.PrefetchScalarGridSpec(
        num_scalar_prefetch=5, grid=(MOE_ITEMS,),
        in_specs=[hbm, hbm, hbm, hbm], out_specs=hbm,
        scratch_shapes=[
            pltpu.VMEM((X_SLOTS, SLAB_ROWS, LANES), F32), pltpu.VMEM((Y_SLOTS, SLAB_ROWS, LANES), F32),
            pltpu.VMEM((2, D_MODEL, D_FF), F32), pltpu.VMEM((2, D_MODEL, D_FF), F32),
            pltpu.VMEM((2, D_FF, D_MODEL), F32),
            pltpu.VMEM((D_MODEL, D_FF), BF16), pltpu.VMEM((D_MODEL, D_FF), BF16),
            pltpu.VMEM((D_FF, D_MODEL), BF16),
            pltpu.SMEM((1,), I32), pltpu.SemaphoreType.DMA((2, 3)),
            pltpu.SemaphoreType.DMA((X_SLOTS,)), pltpu.SemaphoreType.DMA((Y_SLOTS,)),
        ])
    return pl.pallas_call(
        functools.partial(_expert_kernel, layer=layer),
        grid_spec=grid_spec,
        out_shape=jax.ShapeDtypeStruct((2 * T_ALL * MOE_PITCH, LANES), F32),
        compiler_params=_cparams(("arbitrary",)),
        name="moe_experts",
    )(item_tile, item_exp, item_lo, item_hi, n_items, xs, w_gate, w_up, w_down)


def _combine_kernel(pos_ref, xc_ref, xl_ref, wt_ref, g2_ref, ys_ref, oc_ref, ol_ref, y_ref, sem):
    i = pl.program_id(0)
    slot = i % 2

    def gather(tile, s):
        def body(r, carry):
            tok = tile * MOE_TM + r
            _slab_copy(ys_ref, pos_ref[tok], y_ref.at[s, 0], r, sem.at[s], MOE_PITCH).start(priority=0)
            _slab_copy(ys_ref, pos_ref[T_ALL + tok], y_ref.at[s, 1], r, sem.at[s], MOE_PITCH).start(priority=1)
            return carry
        lax.fori_loop(0, MOE_TM, body, 0, unroll=ISSUE_UNROLL)

    @pl.when(i == 0)
    def _():
        gather(0, 0)

    @pl.when(i + 1 < N_TOK_TILES)
    def _():
        gather(i + 1, 1 - slot)

    for j in range(2):
        pltpu.make_async_copy(ys_ref.at[pl.ds(0, SLAB_ROWS)], y_ref.at[slot, j], sem.at[slot]).wait()

    def write(x_ref, o_ref):
        w0 = wt_ref[:, 0:1]
        w1 = wt_ref[:, 1:2]
        for c in range(N_CHUNK):
            sl = slice(c * LANES, (c + 1) * LANES)
            moe = w0 * _slab_chunk(y_ref.at[slot, 0], c) + w1 * _slab_chunk(y_ref.at[slot, 1], c)
            o_ref[:, sl] = x_ref[:, sl] + g2_ref[:, sl] * moe

    pl.when(i < CTX_TILES)(lambda: write(xc_ref, oc_ref))
    pl.when(i >= CTX_TILES)(lambda: write(xl_ref, ol_ref))


def _combine(pos_flat, x1c, x1l, wts_t, mod8, ys, layer):
    cspec = pl.BlockSpec((MOE_TM, D_MODEL), lambda i, *_: (jnp.minimum(i, CTX_TILES - 1), 0))
    lspec = pl.BlockSpec((MOE_TM, D_MODEL), lambda i, *_: (jnp.maximum(i - CTX_TILES, 0), 0))
    grid_spec = pltpu.PrefetchScalarGridSpec(
        num_scalar_prefetch=1,
        grid=(N_TOK_TILES,),
        in_specs=[
            cspec, lspec,
            pl.BlockSpec((MOE_TM, 2), lambda i, *_: (i, 0)),
            _mod_spec(layer, 5, _tok_row),
            pl.BlockSpec(memory_space=pl.ANY),
        ],
        out_specs=[cspec, lspec],
        scratch_shapes=[pltpu.VMEM((2, 2, SLAB_ROWS, LANES), F32), pltpu.SemaphoreType.DMA((2,))],
    )
    return pl.pallas_call(
        _combine_kernel,
        grid_spec=grid_spec,
        out_shape=[jax.ShapeDtypeStruct((T_CTX, D_MODEL), F32),
                   jax.ShapeDtypeStruct((T_LAT, D_MODEL), F32)],
        compiler_params=_cparams(("arbitrary",)),
        name="moe_combine",
    )(pos_flat, x1c, x1l, wts_t, mod8, ys)


def _rope_tables():
    half = DQK // 2
    inv = 1.0 / (ROPE_BASE ** (np.arange(0, half, 2, dtype=np.float32) / half))
    t = np.arange(DEC_SEQ)
    ang_r = (t // GRID_W).astype(np.float32)[:, None] * inv
    ang_c = (t % GRID_W).astype(np.float32)[:, None] * inv
    ang = np.concatenate([ang_r, ang_r, ang_c, ang_c], axis=-1).astype(np.float32)
    ang = np.concatenate([ang, ang], axis=-1)
    first = (np.arange(LANES) % 32) < 16
    cos, sin = np.cos(ang), np.sin(ang)
    return (jnp.asarray(cos, F32), jnp.asarray(np.where(first, -sin, 0.0), F32),
            jnp.asarray(np.where(first, 0.0, sin), F32))


def _na_mask():
    qc = np.arange(GRID_W)[:, None]
    kc = (np.arange(LANES) % GRID_W)[None, :]
    ws = np.clip(qc - NA_KC // 2, 0, GRID_W - NA_KC)
    ok = (kc >= ws) & (kc < ws + NA_KC)
    first = (np.arange(LANES) < GRID_W)[None, :]
    masks = [ok, ok & first, ok & ~first]
    return jnp.asarray(np.stack([np.where(m, 0.0, NEG_BIG) for m in masks]), F32)


def kernel(x_prompt, x_sample, cache_diff_k, cache_diff_v, cache_na_k, cache_na_v, c, c_ctx, norm_mix_g, norm_ffn_g, w_ada, b_ada, w_in, w_out, conv_w, conv_b, conv_ln_g, conv_ln_b, diff_qn_g, diff_kn_g, diff_lam_q1, diff_lam_k1, diff_lam_q2, diff_lam_k2, diff_subln_g, na_qn_g, na_kn_g, na_rpb, moe_wr_g, moe_br_g, moe_wr_e, moe_br_e, moe_w_gate, moe_w_up, moe_w_down):
    cvec = jnp.zeros((MOD_ROWS, D_MODEL), F32).at[0].set(c_ctx).at[1:1 + DEC_BATCH].set(c)
    ones = jnp.ones((DEPTH, SEG_W), F32)
    gains = jnp.stack(
        [jnp.tile(diff_qn_g, (1, 2 * N_HEADS)), jnp.tile(diff_kn_g, (1, 2 * N_HEADS)), ones,
         jnp.tile(na_qn_g, (1, N_HEADS)), jnp.tile(na_kn_g, (1, N_HEADS)), ones],
        axis=1).reshape(DEPTH, 6, 1, SEG_W)
    rope_tabs = _rope_tables()
    na_mask = _na_mask()
    lam_params = jnp.stack([diff_lam_q1, diff_lam_k1, diff_lam_q2, diff_lam_k2], axis=1)
    rpb_pad = jnp.pad(na_rpb, ((0, 0), (0, 0), (0, 0), (0, DQK - na_rpb.shape[-1])))
    rpb_src = jnp.concatenate([rpb_pad[:, :, :-1], rpb_pad[:, :, 1:]], axis=-1)
    rpb_src = jnp.pad(rpb_src, ((0, 0), (0, 0), (0, 16 - NA_PAIRS), (0, 0)))
    wr_t = jnp.concatenate([moe_wr_g, moe_wr_e.reshape(DEPTH, D_MODEL, N_EXPERTS)], axis=2)
    wr_t = jnp.pad(wr_t, ((0, 0), (0, 0), (0, LANES - N_GROUPS - N_EXPERTS)))
    br = jnp.concatenate([moe_br_g, moe_br_e.reshape(DEPTH, N_EXPERTS)], axis=1)
    br = jnp.pad(br, ((0, 0), (0, ROUTE_ROWS - N_GROUPS - N_EXPERTS))).reshape(DEPTH, ROUTE_ROWS, 1)
    tri = jnp.asarray(np.triu(np.ones((CUM_BLK, CUM_BLK), np.float32), 1), BF16)
    norm_mix = norm_mix_g.reshape(DEPTH, 1, D_MODEL)
    norm_ffn = norm_ffn_g.reshape(DEPTH, 1, D_MODEL)
    head_major = lambda cache: jnp.transpose(cache, (0, 1, 3, 2, 4))
    ck_diff, cv_diff = head_major(cache_diff_k), head_major(cache_diff_v)
    ck_na, cv_na = head_major(cache_na_k), head_major(cache_na_v)

    mod8 = _ada_modulation(cvec, w_ada, b_ada).reshape(DEPTH * MOD_ROWS, 1, N_ADA)
    w_out_bf = _cast_bf16(w_out)

    ctx_row = lambda i: 0
    lat_row_pre = lambda i: 1 + i * PRE_TM // DEC_SEQ
    lat_row_mix = lambda i: 1 + i * MIX_TM // DEC_SEQ

    xc = x_prompt.reshape(T_CTX, D_MODEL)
    xl = x_sample.reshape(T_LAT, D_MODEL)
    ctx_segs = []
    for layer in range(DEPTH):
        hc, uc = _pre_project(xc, mod8, norm_mix, w_in, layer, ctx_row, "pre_ctx")
        hl, ul = _pre_project(xl, mod8, norm_mix, w_in, layer, lat_row_pre, "pre_lat")
        sc = _head_project(hc, w_in, gains, None, layer, BATCH, SEQ, "heads_ctx")
        sl = _head_project(hl, w_in, gains, rope_tabs, layer, DEC_BATCH, DEC_SEQ, "heads_lat")
        ctx_segs.append(sc)

        last = layer == DEPTH - 1
        dc, nc, *caches = _ctx_attention(sc, ctx_segs[:-1] if last else [], lam_params, diff_subln_g, layer)
        dl = _diff_attention_lat(sl, lam_params, diff_subln_g, ck_diff, cv_diff, layer, DIFF_TQ)
        nl = _na_attention(sl, rpb_src, na_mask, ck_na, cv_na, layer)

        conv_params = (conv_w, conv_b, conv_ln_g, conv_ln_b)
        x1c, ltc = _mix(uc, conv_params, dc, nc, w_out_bf, xc, mod8, norm_ffn, wr_t, br, layer, ctx_row, SEQ)
        x1l, ltl = _mix(ul, conv_params, dl, nl, w_out_bf, xl, mod8, norm_ffn, wr_t, br, layer,
                        lat_row_mix, DEC_SEQ)

        pos, wts, items = _route(jnp.concatenate([ltc, ltl], axis=1), tri)
        pos_flat = pos.reshape(2 * T_ALL)
        items = [items[r, :MOE_ITEMS] for r in range(4)] + [items[4, :1]]
        xs = _dispatch(pos_flat, x1c, x1l, mod8, norm_ffn, layer)
        ys = _experts(items, xs, moe_w_gate, moe_w_up, moe_w_down, layer)
        xc, xl = _combine(pos_flat, x1c, x1l, wts.T, mod8, ys, layer)

    outs = [jnp.transpose(cache, (0, 1, 3, 2, 4)) for cache in caches]
    return (xc.reshape(BATCH, SEQ, D_MODEL), xl.reshape(DEC_BATCH, DEC_SEQ, D_MODEL), *outs)
```

```python
import functools
import math

import numpy as np
import jax
import jax.numpy as jnp
from jax import lax
from jax.experimental import pallas as pl
from jax.experimental.pallas import tpu as pltpu

F32 = jnp.float32
BF16 = jnp.bfloat16
I32 = jnp.int32

D_MODEL = 2048
BATCH = 16
SEQ = 256
DEPTH = 2
DEC_BATCH = 2
DEC_SEQ = 1024
PAST_LEN = 256
GRID_W = 64
GRID_ROWS = DEC_SEQ // GRID_W
HEAD_DIM = 128
C_CONV = 512
CONV_K = 31
N_HEADS = 6
DQK = 64
NA_KR = 8
NA_KC = 16
ROPE_BASE = 10000.0
N_GROUPS = 4
E_PER_GROUP = 4
N_EXPERTS = 16
D_FF = 512
N_ADA = 6 * D_MODEL
W_IN_COLS = 2 * C_CONV + 6 * N_HEADS * HEAD_DIM
T_CTX = BATCH * SEQ
T_LAT = DEC_BATCH * DEC_SEQ
T_ALL = T_CTX + T_LAT

LANES = 128
MOD_ROWS = 8
PRE_TM = 512
HEADS_TM = 1024
MIX_TM = 512
DIFF_TQ = 1024
MOE_TM = 256
MOE_ITEMS = 2 * T_ALL // MOE_TM + N_EXPERTS
ROUTE_ROWS = 32
CUM_BLK = 512
VMEM_LIMIT = 56 * 1024 * 1024
NEG_BIG = -1e30


def _cparams(sem):
    return pltpu.CompilerParams(dimension_semantics=sem, vmem_limit_bytes=VMEM_LIMIT)


def _sigmoid(x):
    return 1.0 / (1.0 + jnp.exp(-x))


def _silu(x):
    return x * _sigmoid(x)


def _lambda_init(layer):
    return 0.8 - 0.6 * math.exp(-0.3 * layer)


def _ada_kernel(c_ref, w_ref, b_ref, o_ref):
    c = c_ref[...]
    s = _silu(c).astype(BF16)
    o_ref[...] = jnp.dot(s, w_ref[...].astype(BF16), preferred_element_type=F32) + b_ref[...]


def _ada_modulation(cvec, w_ada, b_ada):
    tn = 1024
    return pl.pallas_call(
        _ada_kernel,
        grid=(DEPTH, N_ADA // tn),
        in_specs=[
            pl.BlockSpec((MOD_ROWS, D_MODEL), lambda l, j: (0, 0)),
            pl.BlockSpec((None, D_MODEL, tn), lambda l, j: (l, 0, j)),
            pl.BlockSpec((None, 1, tn), lambda l, j: (l, 0, j)),
        ],
        out_specs=pl.BlockSpec((None, MOD_ROWS, tn), lambda l, j: (l, 0, j)),
        out_shape=jax.ShapeDtypeStruct((DEPTH, MOD_ROWS, N_ADA), F32),
        compiler_params=_cparams(("arbitrary", "arbitrary")),
        name="ada_modulation",
    )(cvec, w_ada, b_ada.reshape(DEPTH, 1, N_ADA))


def _mod_spec(layer, chunk, row_of_block):
    return pl.BlockSpec((None, 1, D_MODEL),
                        lambda i, *_: (layer * MOD_ROWS + row_of_block(i), 0, chunk))


def _modulate(x, g, shift, scale):
    ms = jnp.mean(x * x, axis=-1, keepdims=True)
    return x * lax.rsqrt(ms + 1e-6) * (g * (1.0 + scale)) + shift


def _pre_kernel(x_ref, sh_ref, sc_ref, g_ref, w_ref, h_ref, u_ref, wbf_ref):
    @pl.when(pl.program_id(0) == 0)
    def _():
        wbf_ref[...] = w_ref[...].astype(BF16)

    h = _modulate(x_ref[...], g_ref[...], sh_ref[...], sc_ref[...]).astype(BF16)
    h_ref[...] = h
    y = jnp.dot(h, wbf_ref[...], preferred_element_type=F32)
    u_ref[...] = y[:, :C_CONV] * _sigmoid(y[:, C_CONV:])


def _pre_project(x, mod8, norm_g, w_in, layer, row_of_block, name):
    t = x.shape[0]
    return pl.pallas_call(
        _pre_kernel,
        grid=(t // PRE_TM,),
        in_specs=[
            pl.BlockSpec((PRE_TM, D_MODEL), lambda i: (i, 0)),
            _mod_spec(layer, 0, row_of_block),
            _mod_spec(layer, 1, row_of_block),
            pl.BlockSpec((None, 1, D_MODEL), lambda i: (layer, 0, 0)),
            pl.BlockSpec((None, D_MODEL, 2 * C_CONV), lambda i: (layer, 0, 0)),
        ],
        out_specs=[pl.BlockSpec((PRE_TM, D_MODEL), lambda i: (i, 0)),
                   pl.BlockSpec((PRE_TM, C_CONV), lambda i: (i, 0))],
        out_shape=[jax.ShapeDtypeStruct((t, D_MODEL), BF16), jax.ShapeDtypeStruct((t, C_CONV), F32)],
        scratch_shapes=[pltpu.VMEM((D_MODEL, 2 * C_CONV), BF16)],
        compiler_params=_cparams(("arbitrary",)),
        name=name,
    )(x, mod8, mod8, norm_g, w_in)


SEG_DQ, SEG_DK, SEG_DV, SEG_NQ, SEG_NK, SEG_NV = range(6)
SEG_W = N_HEADS * HEAD_DIM
HEADS_WBLK = 256
HEADS_RC = 256


def _heads_kernel(h_ref, w0_ref, w1_ref, w2_ref, gain_ref, *rest, rope, per_batch):
    if rope:
        cos_ref, sa_ref, sb_ref, o_ref = rest
    else:
        (o_ref,) = rest
    j = pl.program_id(0)

    def finish(seg):
        n_g = HEADS_WBLK // LANES
        for c, w_ref in enumerate((w0_ref, w1_ref, w2_ref)):
            w = w_ref[...].astype(BF16)
            gain = gain_ref[:, c * HEADS_WBLK:(c + 1) * HEADS_WBLK]
            for r0 in range(0, HEADS_TM, HEADS_RC):
                y = jnp.dot(h_ref[r0:r0 + HEADS_RC, :], w, preferred_element_type=F32)
                for g in range(n_g):
                    hd = c * n_g + g
                    yg = y[:, g * LANES:(g + 1) * LANES]
                    if seg in (SEG_NQ, SEG_NK):
                        ms = jnp.mean(yg * yg, axis=-1, keepdims=True)
                        yg = yg * lax.rsqrt(ms + 1e-6) * gain[:, g * LANES:(g + 1) * LANES]
                    elif seg in (SEG_DQ, SEG_DK):
                        sq = yg * yg
                        low = lax.broadcasted_iota(I32, sq.shape, 1) < DQK
                        s_all = jnp.sum(sq, axis=-1, keepdims=True)
                        s_lo = jnp.sum(jnp.where(low, sq, 0.0), axis=-1, keepdims=True)
                        ms = jnp.where(low, s_lo, s_all - s_lo) * (1.0 / DQK)
                        yg = yg * lax.rsqrt(ms + 1e-6) * gain[:, g * LANES:(g + 1) * LANES]
                    if rope and seg in (SEG_DQ, SEG_DK):
                        rs = slice(r0, r0 + HEADS_RC)
                        yg = (yg * cos_ref[rs, :] + pltpu.roll(yg, LANES - 16, 1) * sa_ref[rs, :]
                              + pltpu.roll(yg, 16, 1) * sb_ref[rs, :])
                    yg = yg.astype(o_ref.dtype)
                    if per_batch == 1:
                        o_ref[hd, r0:r0 + HEADS_RC, :] = yg
                    else:
                        rows = HEADS_TM // per_batch
                        for q in range(HEADS_RC // rows):
                            o_ref[r0 // rows + q, hd] = yg[q * rows:(q + 1) * rows]

    for seg in range(6):
        pl.when(j == seg)(functools.partial(finish, seg))


def _head_project(h, w_in, gains, rope_tabs, layer, batch, seq, name):
    t = h.shape[0]
    rope = rope_tabs is not None
    per_batch = HEADS_TM // seq
    col0 = 2 * C_CONV // HEADS_WBLK
    wspec = lambda c: pl.BlockSpec((None, D_MODEL, HEADS_WBLK),
                                   lambda j, i: (layer, 0, col0 + (SEG_W // HEADS_WBLK) * j + c))
    in_specs = [
        pl.BlockSpec((HEADS_TM, D_MODEL), lambda j, i: (i, 0)),
        wspec(0), wspec(1), wspec(2),
        pl.BlockSpec((None, None, 1, SEG_W), lambda j, i: (layer, j, 0, 0)),
    ]
    args = [h, w_in, w_in, w_in, gains]
    if rope:
        in_specs += [pl.BlockSpec((DEC_SEQ, LANES), lambda j, i: (0, 0))] * 3
        args += list(rope_tabs)
    if per_batch == 1:
        out_spec = pl.BlockSpec((None, None, N_HEADS, seq, HEAD_DIM), lambda j, i: (j, i, 0, 0, 0))
    else:
        out_spec = pl.BlockSpec((None, per_batch, N_HEADS, seq, HEAD_DIM), lambda j, i: (j, i, 0, 0, 0))
    return pl.pallas_call(
        functools.partial(_heads_kernel, rope=rope, per_batch=per_batch),
        grid=(6, t // HEADS_TM),
        in_specs=in_specs,
        out_specs=out_spec,
        out_shape=jax.ShapeDtypeStruct((6, batch, N_HEADS, seq, HEAD_DIM), BF16 if rope else F32),
        compiler_params=_cparams(("arbitrary", "arbitrary")),
        name=name,
    )(*args)


CONV_CH = 128
CONV_HALO = 16


def _conv_window(win_ref, y_ref, w_ref, b_ref, lg_ref, lb_ref):
    off = CONV_HALO - CONV_K // 2
    sub = 8
    win_rows = CONV_CH + 2 * CONV_HALO
    for g in range(C_CONV // LANES):
        sl = slice(g * LANES, (g + 1) * LANES)
        window = win_ref[:, sl]
        acc = jnp.zeros((CONV_CH, LANES), F32) + b_ref[:, sl]
        for phase in range(sub):
            taps = [k for k in range(CONV_K) if (off + k) % sub == phase]
            if not taps:
                continue
            shifted = window if phase == 0 else pltpu.roll(window, win_rows - phase, 0)
            for k in taps:
                a = (off + k) // sub * sub
                acc = acc + shifted[a:a + CONV_CH, :] * w_ref[k:k + 1, sl]
        y_ref[:, sl] = acc

    y = y_ref[...]
    mu = jnp.mean(y, axis=-1, keepdims=True)
    yc = y - mu
    var = jnp.mean(yc * yc, axis=-1, keepdims=True)
    z = yc * lax.rsqrt(var + 1e-5) * lg_ref[...] + lb_ref[...]
    return _silu(z)


def _conv_block(i, u_ref, ulo_ref, uhi_ref, conv_refs, win_ref, y_ref, a_ref, seq):
    rows = u_ref.shape[0]
    for c in range(rows // CONV_CH):
        r0 = c * CONV_CH
        first = i * rows + r0
        lo_keep = ((first % seq) != 0).astype(F32)
        hi_keep = (((first + CONV_CH) % seq) != 0).astype(F32)
        lo = u_ref[r0 - CONV_HALO:r0, :] if c > 0 else ulo_ref[...]
        hi = u_ref[r0 + CONV_CH:r0 + CONV_CH + CONV_HALO, :] if r0 + CONV_CH < rows else uhi_ref[...]
        win_ref[0:CONV_HALO, :] = lo * lo_keep
        win_ref[CONV_HALO:CONV_HALO + CONV_CH, :] = u_ref[r0:r0 + CONV_CH, :]
        win_ref[CONV_HALO + CONV_CH:, :] = hi * hi_keep
        a_ref[r0:r0 + CONV_CH, :] = _conv_window(win_ref, y_ref, *conv_refs).astype(a_ref.dtype)


_NT = (((1,), (1,)), ((), ()))


LOG2E = 1.4426950408889634


def _softmax_rows(s):
    m = jnp.max(s, axis=-1, keepdims=True)
    e = jnp.exp(s - m)
    return e / jnp.sum(e, axis=-1, keepdims=True)


def _exp2_rows(s2):
    e = jnp.exp2(s2 - jnp.max(s2, axis=-1, keepdims=True))
    return e, 1.0 / jnp.sum(e, axis=-1, keepdims=True)


def _diff_lambda(lam_ref, lam_init):
    lv = lam_ref[...]
    return (jnp.exp(jnp.sum(lv[0:1] * lv[1:2], axis=-1, keepdims=True))
            - jnp.exp(jnp.sum(lv[2:3] * lv[3:4], axis=-1, keepdims=True)) + lam_init)


def _diff_head(q, k, v, lam, g, lam_init, long_keys):
    lane = lax.broadcasted_iota(I32, q.shape, 1)
    if long_keys:
        qs = q * (DQK ** -0.5 * LOG2E)
        q0 = jnp.where(lane < DQK, qs, 0.0).astype(BF16)
        q1 = jnp.where(lane >= DQK, qs, 0.0).astype(BF16)
        e0, r0 = _exp2_rows(lax.dot_general(q0, k, _NT, preferred_element_type=F32))
        e1, r1 = _exp2_rows(lax.dot_general(q1, k, _NT, preferred_element_type=F32))
        o = (jnp.dot(e0.astype(BF16), v, preferred_element_type=F32) * r0
             - jnp.dot(e1.astype(BF16), v, preferred_element_type=F32) * (lam * r1))
    else:
        q0 = jnp.where(lane < DQK, q, 0.0).astype(BF16)
        q1 = jnp.where(lane >= DQK, q, 0.0).astype(BF16)
        scale = DQK ** -0.5
        p0 = _softmax_rows(lax.dot_general(q0, k, _NT, preferred_element_type=F32) * scale)
        p1 = _softmax_rows(lax.dot_general(q1, k, _NT, preferred_element_type=F32) * scale)
        o = jnp.dot((p0 - lam * p1).astype(BF16), v, preferred_element_type=F32)
    ms = jnp.mean(o * o, axis=-1, keepdims=True)
    return o * lax.rsqrt(ms + 1e-5) * g * (1.0 - lam_init)


def _seg_spec(seg, rows, index):
    def imap(*ids):
        b, h, r = index(*ids)
        return (seg, b, h, r, 0)
    return pl.BlockSpec((None, None, None, rows, HEAD_DIM), imap)


def _diff_attn_lat_kernel(lam_ref, g_ref, q_ref, k_ref, v_ref, ck_ref, cv_ref, o_ref, *, lam_init):
    k = jnp.concatenate([k_ref[...].astype(BF16), ck_ref[...].astype(BF16)], axis=0)
    v = jnp.concatenate([v_ref[...].astype(BF16), cv_ref[...].astype(BF16)], axis=0)
    lam = _diff_lambda(lam_ref, lam_init)
    o_ref[...] = _diff_head(q_ref[...].astype(F32), k, v, lam, g_ref[...], lam_init,
                            True).astype(o_ref.dtype)


def _diff_attention_lat(segs, lam_params, subln_g, cache_k, cache_v, layer, tq):
    nq = DEC_SEQ // tq
    cspec = pl.BlockSpec((None, None, None, PAST_LEN, HEAD_DIM), lambda b, h, qi: (b, layer, h, 0, 0))
    return pl.pallas_call(
        functools.partial(_diff_attn_lat_kernel, lam_init=_lambda_init(layer)),
        grid=(DEC_BATCH, N_HEADS, nq),
        in_specs=[
            pl.BlockSpec((None, 4, DQK), lambda b, h, qi: (layer, 0, 0)),
            pl.BlockSpec((None, 1, HEAD_DIM), lambda b, h, qi: (layer, 0, 0)),
            _seg_spec(SEG_DQ, tq, lambda b, h, qi: (b, h, qi)),
            _seg_spec(SEG_DK, DEC_SEQ, lambda b, h, qi: (b, h, 0)),
            _seg_spec(SEG_DV, DEC_SEQ, lambda b, h, qi: (b, h, 0)),
            cspec, cspec,
        ],
        out_specs=pl.BlockSpec((tq, HEAD_DIM), lambda b, h, qi: (b * nq + qi, h)),
        out_shape=jax.ShapeDtypeStruct((T_LAT, SEG_W), BF16),
        compiler_params=_cparams(("arbitrary", "arbitrary", "arbitrary")),
        name="diff_attn_lat",
    )(lam_params, subln_g.reshape(DEPTH, 1, HEAD_DIM), segs, segs, segs, cache_k, cache_v)


CACHE_SEGS = (SEG_DK, SEG_DV, SEG_NK, SEG_NV)


def _ctx_attn_kernel(lam_ref, g_ref, *refs, lam_init, n_prev):
    seg_refs, refs = refs[:6], refs[6:]
    dq_ref, dk_ref, dv_ref, nq_ref, nk_ref, nv_ref = seg_refs
    prev_refs, refs = refs[:n_prev * len(CACHE_SEGS)], refs[n_prev * len(CACHE_SEGS):]
    d_ref, n_ref = refs[:2]
    cache_refs = refs[2:]
    lam = _diff_lambda(lam_ref, lam_init)
    for h in range(N_HEADS):
        sl = slice(h * HEAD_DIM, (h + 1) * HEAD_DIM)
        d_ref[:, sl] = _diff_head(dq_ref[h], dk_ref[h].astype(BF16), dv_ref[h].astype(BF16), lam,
                                  g_ref[...], lam_init, False).astype(d_ref.dtype)
        s = lax.dot_general(nq_ref[h].astype(BF16), nk_ref[h].astype(BF16), _NT,
                            preferred_element_type=F32) * (HEAD_DIM ** -0.5)
        n_ref[:, sl] = jnp.dot(_softmax_rows(s).astype(BF16), nv_ref[h].astype(BF16),
                               preferred_element_type=F32).astype(n_ref.dtype)
    for n, (seg, c_ref) in enumerate(zip(CACHE_SEGS, cache_refs)):
        for layer in range(n_prev):
            c_ref[layer] = prev_refs[layer * len(CACHE_SEGS) + n][...]
        c_ref[n_prev] = seg_refs[seg][...]


def _ctx_attention(segs, prev_segs, lam_params, subln_g, layer):
    seg = lambda s: pl.BlockSpec((None, None, N_HEADS, SEQ, HEAD_DIM), lambda b: (s, b, 0, 0, 0))
    out = pl.BlockSpec((SEQ, SEG_W), lambda b: (b, 0))
    in_specs = [pl.BlockSpec((None, 4, DQK), lambda b: (layer, 0, 0)),
                pl.BlockSpec((None, 1, HEAD_DIM), lambda b: (layer, 0, 0))] + [seg(s) for s in range(6)]
    args = [lam_params, subln_g.reshape(DEPTH, 1, HEAD_DIM)] + [segs] * 6
    out_specs = [out, out]
    out_shape = [jax.ShapeDtypeStruct((T_CTX, SEG_W), BF16)] * 2
    for prev in prev_segs:
        in_specs += [seg(s) for s in CACHE_SEGS]
        args += [prev] * len(CACHE_SEGS)
    if prev_segs:
        assert len(prev_segs) == DEPTH - 1
        cache = pl.BlockSpec((None, DEPTH, N_HEADS, SEQ, HEAD_DIM), lambda b: (b, 0, 0, 0, 0))
        out_specs += [cache] * len(CACHE_SEGS)
        out_shape += [jax.ShapeDtypeStruct((BATCH, DEPTH, N_HEADS, SEQ, HEAD_DIM), F32)] * len(CACHE_SEGS)
    return pl.pallas_call(
        functools.partial(_ctx_attn_kernel, lam_init=_lambda_init(layer), n_prev=len(prev_segs)),
        grid=(BATCH,),
        in_specs=in_specs,
        out_specs=out_specs,
        out_shape=out_shape,
        compiler_params=_cparams(("arbitrary",)),
        name="ctx_attn_caches" if prev_segs else "ctx_attn",
    )(*args)


NA_PAIRS = 2 * NA_KR - 2
NA_QROWS = 4
NA_KROWS = 12


def _na_key_block(chunk):
    first = min(max(chunk * NA_QROWS - NA_KR // 2, 0), GRID_ROWS - NA_KR)
    return min(first - first % 2, GRID_ROWS - NA_KROWS)


def _na_attn_kernel(src_ref, mask_ref, q_ref, k_ref, v_ref, ck_ref, cv_ref, o_ref, bias_ref):
    for d in range(NA_PAIRS):
        row = jnp.broadcast_to(src_ref[d:d + 1, :] * LOG2E, (GRID_W, LANES))
        tile = pltpu.roll(row, LANES - (NA_KC - 1), 1, stride=1, stride_axis=0)
        for v in range(3):
            bias_ref[v, d] = tile + mask_ref[v]
    dead = jnp.full((GRID_W, LANES), NEG_BIG, F32)
    ck = ck_ref[...].astype(BF16)
    cv = cv_ref[...].astype(BF16)
    scale = HEAD_DIM ** -0.5 * LOG2E
    for chunk in range(GRID_ROWS // NA_QROWS):
        kb = _na_key_block(chunk)
        rows = []
        for qr in range(chunk * NA_QROWS, (chunk + 1) * NA_QROWS):
            start = min(max(qr - NA_KR // 2, 0), GRID_ROWS - NA_KR)
            assert kb <= start and start + NA_KR <= kb + NA_KROWS
            tiles = []
            for m in range(NA_KROWS // 2):
                r0 = kb + 2 * m
                live0 = start <= r0 < start + NA_KR
                live1 = start <= r0 + 1 < start + NA_KR
                d = r0 - qr + NA_KR - 1
                assert not (live0 or live1) or 0 <= d < NA_PAIRS
                if live0 and live1:
                    tiles.append(bias_ref[0, d])
                elif live0:
                    tiles.append(bias_ref[1, d])
                elif live1:
                    tiles.append(bias_ref[2, d])
                else:
                    tiles.append(dead)
            rows.append(jnp.concatenate(tiles, axis=1))
        bias = jnp.concatenate(rows, axis=0)
        qs = slice(chunk * NA_QROWS * GRID_W, (chunk + 1) * NA_QROWS * GRID_W)
        ks = slice(kb * GRID_W, (kb + NA_KROWS) * GRID_W)
        q = (q_ref[qs, :].astype(F32) * scale).astype(BF16)
        kl = k_ref[ks, :].astype(BF16)
        vl = v_ref[ks, :].astype(BF16)
        s_loc = lax.dot_general(q, kl, _NT, preferred_element_type=F32) + bias
        s_ctx = lax.dot_general(q, ck, _NT, preferred_element_type=F32)
        m = jnp.maximum(jnp.max(s_loc, axis=-1, keepdims=True), jnp.max(s_ctx, axis=-1, keepdims=True))
        e_loc = jnp.exp2(s_loc - m)
        e_ctx = jnp.exp2(s_ctx - m)
        r = 1.0 / (jnp.sum(e_loc, axis=-1, keepdims=True) + jnp.sum(e_ctx, axis=-1, keepdims=True))
        o_ref[qs, :] = ((jnp.dot(e_loc.astype(BF16), vl, preferred_element_type=F32)
                         + jnp.dot(e_ctx.astype(BF16), cv, preferred_element_type=F32)) * r
                        ).astype(o_ref.dtype)


def _na_attention(segs, rpb_src, na_mask, cache_k, cache_v, layer):
    blk = lambda s: _seg_spec(s, DEC_SEQ, lambda b, h: (b, h, 0))
    cspec = pl.BlockSpec((None, None, None, PAST_LEN, HEAD_DIM), lambda b, h: (b, layer, h, 0, 0))
    return pl.pallas_call(
        _na_attn_kernel,
        grid=(DEC_BATCH, N_HEADS),
        in_specs=[
            pl.BlockSpec((None, None, 16, LANES), lambda b, h: (layer, h, 0, 0)),
            pl.BlockSpec((3, GRID_W, LANES), lambda b, h: (0, 0, 0)),
            blk(SEG_NQ), blk(SEG_NK), blk(SEG_NV), cspec, cspec,
        ],
        out_specs=pl.BlockSpec((DEC_SEQ, HEAD_DIM), lambda b, h: (b, h)),
        out_shape=jax.ShapeDtypeStruct((T_LAT, SEG_W), BF16),
        scratch_shapes=[pltpu.VMEM((3, NA_PAIRS, GRID_W, LANES), F32)],
        compiler_params=_cparams(("arbitrary", "arbitrary")),
        name="na_attn_lat",
    )(rpb_src, na_mask, segs, segs, segs, cache_k, cache_v)


def _split_f32(x, n):
    terms = []
    for _ in range(n):
        t = x.astype(BF16).astype(F32)
        terms.append(t)
        x = x - t
    return terms


def _mix_kernel(u_ref, ulo_ref, uhi_ref, cw_ref, cb_ref, clg_ref, clb_ref, d_ref, n_ref, w_ref, x_ref,
                g1_ref, sh_ref, sc_ref, ng_ref, wr_ref, br_ref, x1_ref, lt_ref, win_ref, y_ref, a_ref, *, seq):
    k0, k1 = C_CONV, C_CONV + N_HEADS * HEAD_DIM
    _conv_block(pl.program_id(0), u_ref, ulo_ref, uhi_ref, (cw_ref, cb_ref, clg_ref, clb_ref),
                win_ref, y_ref, a_ref, seq)
    mixed = (jnp.dot(d_ref[...].astype(BF16), w_ref[k0:k1, :], preferred_element_type=F32)
             + jnp.dot(n_ref[...].astype(BF16), w_ref[k1:, :], preferred_element_type=F32)
             + jnp.dot(a_ref[...], w_ref[0:k0, :], preferred_element_type=F32))
    x1 = x_ref[...] + g1_ref[...] * mixed
    x1_ref[...] = x1
    h2 = _modulate(x1, ng_ref[...], sh_ref[...], sc_ref[...])
    wh, wm, wl = _split_f32(wr_ref[...], 3)
    w_cat = (wh + pltpu.roll(wm, ROUTE_ROWS, 1) + pltpu.roll(wl, 2 * ROUTE_ROWS, 1)).astype(BF16)
    hh, hm = _split_f32(h2, 2)
    r = jnp.dot(jnp.concatenate([hh, hm], axis=0).astype(BF16), w_cat, preferred_element_type=F32)
    rh, rm = r[:MIX_TM], r[MIX_TM:]
    back = lambda v, groups: pltpu.roll(v, LANES - groups * ROUTE_ROWS, 1)
    small = (back(rm, 1) + back(rh, 2)) + (back(rh, 1) + rm)
    logits = (small + rh).T[:ROUTE_ROWS, :]
    lt_ref[...] = logits + br_ref[...]


def _mix(u, conv_params, d_out, n_out, w_out_bf, x, mod8, norm_g, wr_t, br, layer, row_of_block, seq):
    t = x.shape[0]
    rows = lambda width: pl.BlockSpec((MIX_TM, width), lambda i: (i, 0))
    halo_blocks = MIX_TM // CONV_HALO
    cvec = lambda: pl.BlockSpec((None, 1, C_CONV), lambda i: (layer, 0, 0))
    conv_w, conv_b, ln_g, ln_b = conv_params
    return pl.pallas_call(
        functools.partial(_mix_kernel, seq=seq),
        grid=(t // MIX_TM,),
        in_specs=[
            rows(C_CONV),
            pl.BlockSpec((CONV_HALO, C_CONV), lambda i: (jnp.maximum(i * halo_blocks - 1, 0), 0)),
            pl.BlockSpec((CONV_HALO, C_CONV),
                         lambda i: (jnp.minimum((i + 1) * halo_blocks, t // CONV_HALO - 1), 0)),
            pl.BlockSpec((None, CONV_K, C_CONV), lambda i: (layer, 0, 0)),
            cvec(), cvec(), cvec(),
            rows(N_HEADS * HEAD_DIM), rows(N_HEADS * HEAD_DIM),
            pl.BlockSpec((None, D_MODEL, D_MODEL), lambda i: (layer, 0, 0)),
            rows(D_MODEL),
            _mod_spec(layer, 2, row_of_block),
            _mod_spec(layer, 3, row_of_block),
            _mod_spec(layer, 4, row_of_block),
            pl.BlockSpec((None, 1, D_MODEL), lambda i: (layer, 0, 0)),
            pl.BlockSpec((None, D_MODEL, LANES), lambda i: (layer, 0, 0)),
            pl.BlockSpec((None, ROUTE_ROWS, 1), lambda i: (layer, 0, 0)),
        ],
        out_specs=[rows(D_MODEL), pl.BlockSpec((ROUTE_ROWS, MIX_TM), lambda i: (0, i))],
        out_shape=[jax.ShapeDtypeStruct((t, D_MODEL), F32),
                   jax.ShapeDtypeStruct((ROUTE_ROWS, t), F32)],
        scratch_shapes=[pltpu.VMEM((CONV_CH + 2 * CONV_HALO, C_CONV), F32),
                        pltpu.VMEM((CONV_CH, C_CONV), F32),
                        pltpu.VMEM((MIX_TM, C_CONV), BF16)],
        compiler_params=_cparams(("arbitrary",)),
        name="mix_%d" % t,
    )(u, u, u, conv_w, conv_b.reshape(DEPTH, 1, C_CONV), ln_g.reshape(DEPTH, 1, C_CONV),
      ln_b.reshape(DEPTH, 1, C_CONV), d_out, n_out, w_out_bf, x, mod8, mod8, mod8, norm_g, wr_t, br)


def _cast_kernel(x_ref, o_ref):
    o_ref[...] = x_ref[...].astype(BF16)


def _cast_bf16(w):
    tm = 512
    spec = pl.BlockSpec((None, tm, D_MODEL), lambda l, i: (l, i, 0))
    return pl.pallas_call(
        _cast_kernel, grid=(DEPTH, D_MODEL // tm), in_specs=[spec], out_specs=spec,
        out_shape=jax.ShapeDtypeStruct(w.shape, BF16),
        compiler_params=_cparams(("arbitrary", "arbitrary")), name="cast_w_out",
    )(w)


def _route_kernel(lt_ref, tri_ref, pos_ref, wt_ref, item_ref):
    t = T_ALL
    lg = lt_ref[0:N_GROUPS, :]
    eg = jnp.exp(lg - jnp.max(lg, axis=0, keepdims=True))
    pg = eg / jnp.sum(eg, axis=0, keepdims=True)
    pg_top = jnp.max(pg, axis=0, keepdims=True)
    gi = lax.broadcasted_iota(I32, pg.shape, 0).astype(F32)
    g_idx = jnp.min(jnp.where(pg == pg_top, gi, float(N_GROUPS)), axis=0, keepdims=True)

    le = jnp.zeros((E_PER_GROUP, t), F32)
    for g in range(N_GROUPS):
        rows = lt_ref[N_GROUPS + g * E_PER_GROUP:N_GROUPS + (g + 1) * E_PER_GROUP, :]
        le = jnp.where(g_idx == float(g), rows, le)
    ee = jnp.exp(le - jnp.max(le, axis=0, keepdims=True))
    pe = ee / jnp.sum(ee, axis=0, keepdims=True)
    ei = lax.broadcasted_iota(I32, pe.shape, 0).astype(F32)
    p1 = jnp.max(pe, axis=0, keepdims=True)
    i1 = jnp.min(jnp.where(pe == p1, ei, float(E_PER_GROUP)), axis=0, keepdims=True)
    pe_rest = jnp.where(ei == i1, -1.0, pe)
    p2 = jnp.max(pe_rest, axis=0, keepdims=True)
    i2 = jnp.min(jnp.where(pe_rest == p2, ei, float(E_PER_GROUP)), axis=0, keepdims=True)
    den = p1 + p2
    wt_ref[0:1, :] = pg_top * (p1 / den)
    wt_ref[1:2, :] = pg_top * (p2 / den)
    e1 = g_idx * E_PER_GROUP + i1
    e2 = g_idx * E_PER_GROUP + i2

    erow = lax.broadcasted_iota(I32, (N_EXPERTS, t), 0).astype(F32)
    oh1 = (erow == e1).astype(F32)
    oh2 = (erow == e2).astype(F32)
    cnt = oh1 + oh2
    carry = jnp.zeros((N_EXPERTS, 1), F32)
    ranks = []
    for b in range(t // CUM_BLK):
        blk = cnt[:, b * CUM_BLK:(b + 1) * CUM_BLK]
        ranks.append(jnp.dot(blk.astype(BF16), tri_ref[...], preferred_element_type=F32) + carry)
        carry = carry + jnp.sum(blk, axis=1, keepdims=True)
    rank = jnp.concatenate(ranks, axis=1)

    erow_l = lax.broadcasted_iota(I32, (N_EXPERTS, LANES), 0)

    def excl_scan(v):
        inc = v
        for s in (1, 2, 4, 8):
            inc = inc + jnp.where(erow_l >= s, pltpu.roll(inc, s, 0), 0.0)
        return inc - v

    total = jnp.broadcast_to(carry, (N_EXPERTS, LANES))
    offs = excl_scan(total)
    pos_ref[0:1, :] = jnp.sum(oh1 * (rank + offs[:, 0:1]), axis=0, keepdims=True).astype(I32)
    pos_ref[1:2, :] = jnp.sum(oh2 * (rank + offs[:, 0:1]), axis=0, keepdims=True).astype(I32)

    shift = int(math.log2(MOE_TM))
    offs_i = offs.astype(I32)
    total_i = total.astype(I32)
    first_tile = lax.shift_right_logical(offs_i, shift)
    last_tile = lax.shift_right_logical(offs_i + total_i - 1, shift)
    n_item = jnp.where(total_i > 0, last_tile - first_tile + 1, 0).astype(F32)
    item_start = excl_scan(n_item)
    item_end = item_start + n_item
    kk = lax.broadcasted_iota(I32, (N_EXPERTS, LANES), 1).astype(F32)
    item_e = jnp.minimum(jnp.sum((item_end <= kk).astype(F32), axis=0, keepdims=True),
                         float(N_EXPERTS - 1))
    sel = (erow_l.astype(F32) == item_e).astype(F32)
    pick = lambda v: jnp.sum(sel * v, axis=0, keepdims=True)
    item_ref[0:1, :] = (kk[0:1] + pick(first_tile.astype(F32) - item_start)).astype(I32)
    item_ref[1:2, :] = item_e.astype(I32)
    item_ref[2:3, :] = pick(offs).astype(I32)
    item_ref[3:4, :] = pick(offs + total).astype(I32)
    item_ref[4:5, :] = item_end[N_EXPERTS - 1:N_EXPERTS, :].astype(I32)
    item_ref[5:8, :] = jnp.zeros((3, LANES), I32)


def _route(lt_all, tri):
    full = lambda shape: pl.BlockSpec(shape, lambda i: (0,) * len(shape))
    return pl.pallas_call(
        _route_kernel,
        grid=(1,),
        in_specs=[full((ROUTE_ROWS, T_ALL)), full((CUM_BLK, CUM_BLK))],
        out_specs=[full((2, T_ALL)), full((2, T_ALL)), full((8, LANES))],
        out_shape=[jax.ShapeDtypeStruct((2, T_ALL), I32), jax.ShapeDtypeStruct((2, T_ALL), F32),
                   jax.ShapeDtypeStruct((8, LANES), I32)],
        compiler_params=_cparams(("arbitrary",)),
        name="moe_route",
    )(lt_all, tri)


N_TOK_TILES = T_ALL // MOE_TM
CTX_TILES = T_CTX // MOE_TM


def _tok_row(i):
    return jnp.where(i < CTX_TILES, 0, 1 + (i - CTX_TILES) // (DEC_SEQ // MOE_TM))


N_CHUNK = D_MODEL // LANES
MOE_PITCH = N_CHUNK + 1
SLAB_ROWS = MOE_TM * MOE_PITCH
ISSUE_UNROLL = 8


def _slab_copy(src, src_tok, dst, dst_tok, sem, pitch):
    return pltpu.make_async_copy(src.at[pl.ds(src_tok * pitch, pitch)],
                                 dst.at[pl.ds(dst_tok * pitch, pitch)], sem)


def _to_slabs(ref, value):
    for c in range(N_CHUNK):
        ref[pl.ds(c, MOE_TM, stride=MOE_PITCH), :] = value[:, c * LANES:(c + 1) * LANES]
    ref[pl.ds(N_CHUNK, MOE_TM, stride=MOE_PITCH), :] = jnp.zeros((MOE_TM, LANES), F32)


def _slab_chunk(ref, c):
    return ref[pl.ds(c, MOE_TM, stride=MOE_PITCH), :]


def _dispatch_kernel(pos_ref, xc_ref, xl_ref, sh_ref, sc_ref, ng_ref, xs_ref, h_ref, sem):
    i = pl.program_id(0)
    slot = i % 2
    buf = h_ref.at[slot]

    def drain(s):
        for _ in range(2):
            pltpu.make_async_copy(h_ref.at[s], xs_ref.at[pl.ds(0, SLAB_ROWS)], sem.at[s]).wait()

    @pl.when(i >= 2)
    def _():
        drain(slot)

    @pl.when(i < CTX_TILES)
    def _():
        _to_slabs(buf, _modulate(xc_ref[...], ng_ref[...], sh_ref[...], sc_ref[...]))

    @pl.when(i >= CTX_TILES)
    def _():
        _to_slabs(buf, _modulate(xl_ref[...], ng_ref[...], sh_ref[...], sc_ref[...]))

    def issue(r, carry):
        tok = i * MOE_TM + r
        _slab_copy(buf, r, xs_ref, pos_ref[tok], sem.at[slot], MOE_PITCH).start(priority=0)
        _slab_copy(buf, r, xs_ref, pos_ref[T_ALL + tok], sem.at[slot], MOE_PITCH).start(priority=1)
        return carry

    lax.fori_loop(0, MOE_TM, issue, 0, unroll=ISSUE_UNROLL)

    @pl.when(i == N_TOK_TILES - 1)
    def _():
        drain(1 - slot)
        drain(slot)


def _dispatch(pos_flat, x1c, x1l, mod8, norm_g, layer):
    grid_spec = pltpu.PrefetchScalarGridSpec(
        num_scalar_prefetch=1,
        grid=(N_TOK_TILES,),
        in_specs=[
            pl.BlockSpec((MOE_TM, D_MODEL), lambda i, *_: (jnp.minimum(i, CTX_TILES - 1), 0)),
            pl.BlockSpec((MOE_TM, D_MODEL), lambda i, *_: (jnp.maximum(i - CTX_TILES, 0), 0)),
            _mod_spec(layer, 3, _tok_row),
            _mod_spec(layer, 4, _tok_row),
            pl.BlockSpec((None, 1, D_MODEL), lambda i, *_: (layer, 0, 0)),
        ],
        out_specs=pl.BlockSpec(memory_space=pl.ANY),
        scratch_shapes=[pltpu.VMEM((2, SLAB_ROWS, LANES), F32), pltpu.SemaphoreType.DMA((2,))],
    )
    return pl.pallas_call(
        _dispatch_kernel,
        grid_spec=grid_spec,
        out_shape=jax.ShapeDtypeStruct((2 * T_ALL * MOE_PITCH, LANES), F32),
        compiler_params=_cparams(("arbitrary",)),
        name="moe_dispatch",
    )(pos_flat, x1c, x1l, mod8, mod8, norm_g)


N_ROW_TILES = 2 * T_ALL // MOE_TM
X_SLOTS = 4
Y_SLOTS = 3


def _expert_kernel(tile_ref, exp_ref, lo_ref, hi_ref, n_ref, xs_hbm, wg_hbm, wu_hbm, wd_hbm, ys_hbm,
                   x_buf, y_buf, wg_buf, wu_buf, wd_buf, wg_bf, wu_bf, wd_bf, slot_ref, sem, x_sem, y_sem,
                   *, layer):
    k = pl.program_id(0)
    n = n_ref[0]

    def weight_copies(e, s):
        return (pltpu.make_async_copy(wg_hbm.at[layer, e], wg_buf.at[s], sem.at[s, 0]),
                pltpu.make_async_copy(wu_hbm.at[layer, e], wu_buf.at[s], sem.at[s, 1]),
                pltpu.make_async_copy(wd_hbm.at[layer, e], wd_buf.at[s], sem.at[s, 2]))

    def x_copy(t):
        s = t % X_SLOTS
        return pltpu.make_async_copy(xs_hbm.at[pl.ds(t * SLAB_ROWS, SLAB_ROWS)], x_buf.at[s], x_sem.at[s])

    def y_copy(t):
        s = t % Y_SLOTS
        return pltpu.make_async_copy(y_buf.at[s], ys_hbm.at[pl.ds(t * SLAB_ROWS, SLAB_ROWS)], y_sem.at[s])

    @pl.when(k == 0)
    def _():
        slot_ref[0] = 0
        for cp in weight_copies(exp_ref[0], 0):
            cp.start()
        for t in range(X_SLOTS - 1):
            x_copy(t).start()

    @pl.when(k < n)
    def _():
        e = exp_ref[k]
        tile = tile_ref[k]
        prev_tile = tile_ref[jnp.maximum(k - 1, 0)]
        new_expert = jnp.logical_or(k == 0, exp_ref[jnp.maximum(k - 1, 0)] != e)

        @pl.when(jnp.logical_or(k == 0, tile != prev_tile))
        def _():
            x_copy(tile).wait()

            @pl.when(tile + (X_SLOTS - 1) < N_ROW_TILES)
            def _():
                x_copy(tile + (X_SLOTS - 1)).start()

            @pl.when(k > 0)
            def _():
                y_copy(prev_tile).start()

            @pl.when(tile >= Y_SLOTS)
            def _():
                y_copy(tile - Y_SLOTS).wait()

        xs_ref = x_buf.at[tile % X_SLOTS]
        ys_ref = y_buf.at[tile % Y_SLOTS]

        @pl.when(new_expert)
        def _():
            s = slot_ref[0]
            for cp in weight_copies(e, s):
                cp.wait()
            wg_bf[...] = wg_buf[s].astype(BF16)
            wu_bf[...] = wu_buf[s].astype(BF16)
            wd_bf[...] = wd_buf[s].astype(BF16)
            nxt = lax.while_loop(lambda j: jnp.logical_and(j < n, exp_ref[jnp.minimum(j, n - 1)] == e),
                                 lambda j: j + 1, k + 1)

            @pl.when(nxt < n)
            def _():
                for cp in weight_copies(exp_ref[jnp.minimum(nxt, n - 1)], 1 - s):
                    cp.start()
            slot_ref[0] = 1 - s

        x = jnp.concatenate([_slab_chunk(xs_ref, c) for c in range(N_CHUNK)], axis=1).astype(BF16)
        gate = jnp.dot(x, wg_bf[...], preferred_element_type=F32)
        up = jnp.dot(x, wu_bf[...], preferred_element_type=F32)
        hid = (_silu(gate) * up).astype(BF16)
        y = jnp.dot(hid, wd_bf[...], preferred_element_type=F32)
        row0 = tile * MOE_TM
        row = row0 + lax.broadcasted_iota(I32, (MOE_TM, 1), 0)
        mine = jnp.logical_and(row >= lo_ref[k], row < hi_ref[k])
        first = lo_ref[k] <= row0

        @pl.when(first)
        def _():
            _to_slabs(ys_ref, jnp.where(mine, y, 0.0))

        @pl.when(jnp.logical_not(first))
        def _():
            for c in range(N_CHUNK):
                old = _slab_chunk(ys_ref, c)
                ys_ref[pl.ds(c, MOE_TM, stride=MOE_PITCH), :] = jnp.where(
                    mine, y[:, c * LANES:(c + 1) * LANES], old)

        @pl.when(k == n - 1)
        def _():
            y_copy(tile).start()
            y_copy(tile).wait()
            for back in range(1, Y_SLOTS):
                @pl.when(tile >= back)
                def _(back=back):
                    y_copy(tile - back).wait()


def _experts(items, xs, w_gate, w_up, w_down, layer):
    item_tile, item_exp, item_lo, item_hi, n_items = items
    hbm = pl.BlockSpec(memory_space=pl.ANY)
    grid_spec = pltpu.PrefetchScalarGridSpec(
        num_scalar_prefetch=5, grid=(MOE_ITEMS,),
        in_specs=[hbm, hbm, hbm, hbm], out_specs=hbm,
        scratch_shapes=[
            pltpu.VMEM((X_SLOTS, SLAB_ROWS, LANES), F32), pltpu.VMEM((Y_SLOTS, SLAB_ROWS, LANES), F32),
            pltpu.VMEM((2, D_MODEL, D_FF), F32), pltpu.VMEM((2, D_MODEL, D_FF), F32),
            pltpu.VMEM((2, D_FF, D_MODEL), F32),
            pltpu.VMEM((D_MODEL, D_FF), BF16), pltpu.VMEM((D_MODEL, D_FF), BF16),
            pltpu.VMEM((D_FF, D_MODEL), BF16),
            pltpu.SMEM((1,), I32), pltpu.SemaphoreType.DMA((2, 3)),
            pltpu.SemaphoreType.DMA((X_SLOTS,)), pltpu.SemaphoreType.DMA((Y_SLOTS,)),
        ])
    return pl.pallas_call(
        functools.partial(_expert_kernel, layer=layer),
        grid_spec=grid_spec,
        out_shape=jax.ShapeDtypeStruct((2 * T_ALL * MOE_PITCH, LANES), F32),
        compiler_params=_cparams(("arbitrary",)),
        name="moe_experts",
    )(item_tile, item_exp, item_lo, item_hi, n_items, xs, w_gate, w_up, w_down)


def _combine_kernel(pos_ref, xc_ref, xl_ref, wt_ref, g2_ref, ys_ref, oc_ref, ol_ref, y_ref, sem):
    i = pl.program_id(0)
    slot = i % 2

    def gather(tile, s):
        def body(r, carry):
            tok = tile * MOE_TM + r
            _slab_copy(ys_ref, pos_ref[tok], y_ref.at[s, 0], r, sem.at[s], MOE_PITCH).start(priority=0)
            _slab_copy(ys_ref, pos_ref[T_ALL + tok], y_ref.at[s, 1], r, sem.at[s], MOE_PITCH).start(priority=1)
            return carry
        lax.fori_loop(0, MOE_TM, body, 0, unroll=ISSUE_UNROLL)

    @pl.when(i == 0)
    def _():
        gather(0, 0)

    @pl.when(i + 1 < N_TOK_TILES)
    def _():
        gather(i + 1, 1 - slot)

    for j in range(2):
        pltpu.make_async_copy(ys_ref.at[pl.ds(0, SLAB_ROWS)], y_ref.at[slot, j], sem.at[slot]).wait()

    def write(x_ref, o_ref):
        w0 = wt_ref[:, 0:1]
        w1 = wt_ref[:, 1:2]
        for c in range(N_CHUNK):
            sl = slice(c * LANES, (c + 1) * LANES)
            moe = w0 * _slab_chunk(y_ref.at[slot, 0], c) + w1 * _slab_chunk(y_ref.at[slot, 1], c)
            o_ref[:, sl] = x_ref[:, sl] + g2_ref[:, sl] * moe

    pl.when(i < CTX_TILES)(lambda: write(xc_ref, oc_ref))
    pl.when(i >= CTX_TILES)(lambda: write(xl_ref, ol_ref))


def _combine(pos_flat, x1c, x1l, wts_t, mod8, ys, layer):
    cspec = pl.BlockSpec((MOE_TM, D_MODEL), lambda i, *_: (jnp.minimum(i, CTX_TILES - 1), 0))
    lspec = pl.BlockSpec((MOE_TM, D_MODEL), lambda i, *_: (jnp.maximum(i - CTX_TILES, 0), 0))
    grid_spec = pltpu.PrefetchScalarGridSpec(
        num_scalar_prefetch=1,
        grid=(N_TOK_TILES,),
        in_specs=[
            cspec, lspec,
            pl.BlockSpec((MOE_TM, 2), lambda i, *_: (i, 0)),
            _mod_spec(layer, 5, _tok_row),
            pl.BlockSpec(memory_space=pl.ANY),
        ],
        out_specs=[cspec, lspec],
        scratch_shapes=[pltpu.VMEM((2, 2, SLAB_ROWS, LANES), F32), pltpu.SemaphoreType.DMA((2,))],
    )
    return pl.pallas_call(
        _combine_kernel,
        grid_spec=grid_spec,
        out_shape=[jax.ShapeDtypeStruct((T_CTX, D_MODEL), F32),
                   jax.ShapeDtypeStruct((T_LAT, D_MODEL), F32)],
        compiler_params=_cparams(("arbitrary",)),
        name="moe_combine",
    )(pos_flat, x1c, x1l, wts_t, mod8, ys)


def _rope_tables():
    half = DQK // 2
    inv = 1.0 / (ROPE_BASE ** (np.arange(0, half, 2, dtype=np.float32) / half))
    t = np.arange(DEC_SEQ)
    ang_r = (t // GRID_W).astype(np.float32)[:, None] * inv
    ang_c = (t % GRID_W).astype(np.float32)[:, None] * inv
    ang = np.concatenate([ang_r, ang_r, ang_c, ang_c], axis=-1).astype(np.float32)
    ang = np.concatenate([ang, ang], axis=-1)
    first = (np.arange(LANES) % 32) < 16
    cos, sin = np.cos(ang), np.sin(ang)
    return (jnp.asarray(cos, F32), jnp.asarray(np.where(first, -sin, 0.0), F32),
            jnp.asarray(np.where(first, 0.0, sin), F32))


def _na_mask():
    qc = np.arange(GRID_W)[:, None]
    kc = (np.arange(LANES) % GRID_W)[None, :]
    ws = np.clip(qc - NA_KC // 2, 0, GRID_W - NA_KC)
    ok = (kc >= ws) & (kc < ws + NA_KC)
    first = (np.arange(LANES) < GRID_W)[None, :]
    masks = [ok, ok & first, ok & ~first]
    return jnp.asarray(np.stack([np.where(m, 0.0, NEG_BIG) for m in masks]), F32)


def kernel(x_prompt, x_sample, cache_diff_k, cache_diff_v, cache_na_k, cache_na_v, c, c_ctx, norm_mix_g, norm_ffn_g, w_ada, b_ada, w_in, w_out, conv_w, conv_b, conv_ln_g, conv_ln_b, diff_qn_g, diff_kn_g, diff_lam_q1, diff_lam_k1, diff_lam_q2, diff_lam_k2, diff_subln_g, na_qn_g, na_kn_g, na_rpb, moe_wr_g, moe_br_g, moe_wr_e, moe_br_e, moe_w_gate, moe_w_up, moe_w_down):
    cvec = jnp.zeros((MOD_ROWS, D_MODEL), F32).at[0].set(c_ctx).at[1:1 + DEC_BATCH].set(c)
    ones = jnp.ones((DEPTH, SEG_W), F32)
    gains = jnp.stack(
        [jnp.tile(diff_qn_g, (1, 2 * N_HEADS)), jnp.tile(diff_kn_g, (1, 2 * N_HEADS)), ones,
         jnp.tile(na_qn_g, (1, N_HEADS)), jnp.tile(na_kn_g, (1, N_HEADS)), ones],
        axis=1).reshape(DEPTH, 6, 1, SEG_W)
    rope_tabs = _rope_tables()
    na_mask = _na_mask()
    lam_params = jnp.stack([diff_lam_q1, diff_lam_k1, diff_lam_q2, diff_lam_k2], axis=1)
    rpb_pad = jnp.pad(na_rpb, ((0, 0), (0, 0), (0, 0), (0, DQK - na_rpb.shape[-1])))
    rpb_src = jnp.concatenate([rpb_pad[:, :, :-1], rpb_pad[:, :, 1:]], axis=-1)
    rpb_src = jnp.pad(rpb_src, ((0, 0), (0, 0), (0, 16 - NA_PAIRS), (0, 0)))
    wr_t = jnp.concatenate([moe_wr_g, moe_wr_e.reshape(DEPTH, D_MODEL, N_EXPERTS)], axis=2)
    wr_t = jnp.pad(wr_t, ((0, 0), (0, 0), (0, LANES - N_GROUPS - N_EXPERTS)))
    br = jnp.concatenate([moe_br_g, moe_br_e.reshape(DEPTH, N_EXPERTS)], axis=1)
    br = jnp.pad(br, ((0, 0), (0, ROUTE_ROWS - N_GROUPS - N_EXPERTS))).reshape(DEPTH, ROUTE_ROWS, 1)
    tri = jnp.asarray(np.triu(np.ones((CUM_BLK, CUM_BLK), np.float32), 1), BF16)
    norm_mix = norm_mix_g.reshape(DEPTH, 1, D_MODEL)
    norm_ffn = norm_ffn_g.reshape(DEPTH, 1, D_MODEL)
    head_major = lambda cache: jnp.transpose(cache, (0, 1, 3, 2, 4))
    ck_diff, cv_diff = head_major(cache_diff_k), head_major(cache_diff_v)
    ck_na, cv_na = head_major(cache_na_k), head_major(cache_na_v)

    mod8 = _ada_modulation(cvec, w_ada, b_ada).reshape(DEPTH * MOD_ROWS, 1, N_ADA)
    w_out_bf = _cast_bf16(w_out)

    ctx_row = lambda i: 0
    lat_row_pre = lambda i: 1 + i * PRE_TM // DEC_SEQ
    lat_row_mix = lambda i: 1 + i * MIX_TM // DEC_SEQ

    xc = x_prompt.reshape(T_CTX, D_MODEL)
    xl = x_sample.reshape(T_LAT, D_MODEL)
    ctx_segs = []
    for layer in range(DEPTH):
        hc, uc = _pre_project(xc, mod8, norm_mix, w_in, layer, ctx_row, "pre_ctx")
        hl, ul = _pre_project(xl, mod8, norm_mix, w_in, layer, lat_row_pre, "pre_lat")
        sc = _head_project(hc, w_in, gains, None, layer, BATCH, SEQ, "heads_ctx")
        sl = _head_project(hl, w_in, gains, rope_tabs, layer, DEC_BATCH, DEC_SEQ, "heads_lat")
        ctx_segs.append(sc)

        last = layer == DEPTH - 1
        dc, nc, *caches = _ctx_attention(sc, ctx_segs[:-1] if last else [], lam_params, diff_subln_g, layer)
        dl = _diff_attention_lat(sl, lam_params, diff_subln_g, ck_diff, cv_diff, layer, DIFF_TQ)
        nl = _na_attention(sl, rpb_src, na_mask, ck_na, cv_na, layer)

        conv_params = (conv_w, conv_b, conv_ln_g, conv_ln_b)
        x1c, ltc = _mix(uc, conv_params, dc, nc, w_out_bf, xc, mod8, norm_ffn, wr_t, br, layer, ctx_row, SEQ)
        x1l, ltl = _mix(ul, conv_params, dl, nl, w_out_bf, xl, mod8, norm_ffn, wr_t, br, layer,
                        lat_row_mix, DEC_SEQ)

        pos, wts, items = _route(jnp.concatenate([ltc, ltl], axis=1), tri)
        pos_flat = pos.reshape(2 * T_ALL)
        items = [items[r, :MOE_ITEMS] for r in range(4)] + [items[4, :1]]
        xs = _dispatch(pos_flat, x1c, x1l, mod8, norm_ffn, layer)
        ys = _experts(items, xs, moe_w_gate, moe_w_up, moe_w_down, layer)
        xc, xl = _combine(pos_flat, x1c, x1l, wts.T, mod8, ys, layer)

    outs = [jnp.transpose(cache, (0, 1, 3, 2, 4)) for cache in caches]
    return (xc.reshape(BATCH, SEQ, D_MODEL), xl.reshape(DEC_BATCH, DEC_SEQ, D_MODEL), *outs)
```
